```python
import jax, jax.numpy as jnp
from jax import lax
import numpy as np

D_MODEL = 2048
BATCH = 16
SEQ = 2048
DEPTH = 2

GRID_W = 64
CTX_LEN = 256
HEAD_DIM = 64
BRANCH_WIDTH = D_MODEL // 2
N_BRANCHES = 3
RWKV_WIDTH = BRANCH_WIDTH
RWKV_HEADS = RWKV_WIDTH // HEAD_DIM
DECAY_RANK = 96
ICLR_RANK = 96
VRES_RANK = 64
GATE_RANK = 256
GN_EPS = 64e-5
CONV_WIDTH = BRANCH_WIDTH
CONV_K = 3
ATTN_HEADS = BRANCH_WIDTH // HEAD_DIM
ATTN_KV_HEADS = ATTN_HEADS // 4
ATTN_WIDTH = ATTN_HEADS * HEAD_DIM
KV_WIDTH = ATTN_KV_HEADS * HEAD_DIM
WINDOW = 128
BLOCK = 128
ROPE_BASE = 10000.0
NEG_INF = -1e30
N_EXPERTS = 16
EXPERT_FF = 2048
CAPACITY_FACTOR = 2
NORM_EPS = 1e-6

RWKV_COLS = 3 * RWKV_WIDTH + 2 * DECAY_RANK + 2 * ICLR_RANK + GATE_RANK
RWKV_SPLITS = (RWKV_WIDTH, 2 * RWKV_WIDTH, 3 * RWKV_WIDTH,
               3 * RWKV_WIDTH + 2 * DECAY_RANK, 3 * RWKV_WIDTH + 2 * DECAY_RANK + 2 * ICLR_RANK)
CTX_COLS = RWKV_COLS + 2 * KV_WIDTH
Q_END = CTX_COLS + ATTN_WIDTH
CONV_END = Q_END + 3 * CONV_WIDTH
IN_COLS = CONV_END + N_BRANCHES * D_MODEL

kernel_name = "hybrid_rwkv7_shortconv_swa_ec_dit"


def rms_norm(x, gain):
    xf = x.astype(jnp.float32)
    y = xf * lax.rsqrt(jnp.mean(xf * xf, axis=-1, keepdims=True) + NORM_EPS)
    return y.astype(x.dtype) * gain


def modulate(h, shift, scale):
    return h * (1 + scale) + shift


def neighbours(u):
    up = jnp.pad(u, ((0, 0), (1, 1), (0, 0)))
    return up[:, :-2], up[:, 2:]


def depthwise_conv3(u, w):
    prev, nxt = neighbours(u)
    return w[0] * prev + w[1] * u + w[2] * nxt


def rope_2d(x, rows, cols):
    half = HEAD_DIM // 2
    quarter = half // 2
    inv = ROPE_BASE ** (-jnp.arange(quarter, dtype=jnp.float32) / quarter)

    def rot(xp, pos):
        ang = pos.astype(jnp.float32)[:, None] * inv[None, :]
        cos = jnp.cos(ang)[:, None, :].astype(x.dtype)
        sin = jnp.sin(ang)[:, None, :].astype(x.dtype)
        x1, x2 = xp[..., :quarter], xp[..., quarter:]
        return jnp.concatenate([x1 * cos - x2 * sin, x2 * cos + x1 * sin], axis=-1)

    return jnp.concatenate([rot(x[..., :half], rows), rot(x[..., half:], cols)], axis=-1)


def rwkv_streams(p, mu, decay_up, decay_bias, iclr_up, iclr_bias, k_k, k_a, v_first, vres):
    B, T, _ = p.shape
    prev, nxt = neighbours(p)
    p = p + mu * (0.5 * (prev + nxt) - p)
    r, k, v, wd, ad, gd = jnp.split(p, RWKV_SPLITS, axis=-1)
    wd = wd.reshape(B, T, 2, DECAY_RANK)
    ad = ad.reshape(B, T, 2, ICLR_RANK)
    w_logit = decay_bias + jnp.einsum('btdr,drc->btdc', jnp.tanh(wd), decay_up)
    decay = jnp.exp(-jnp.exp(-jax.nn.softplus(-w_logit) - 0.5))
    a = jax.nn.sigmoid(iclr_bias + jnp.einsum('btdr,drc->btdc', ad, iclr_up))
    if vres is None:
        v_first = v
    else:
        vd, vu, vb = vres
        v = v + (v_first - v) * jax.nn.sigmoid(vb + (v @ vd) @ vu)
    kh = (k * k_k).reshape(B, T, RWKV_HEADS, HEAD_DIM).astype(jnp.float32)
    kk = kh * lax.rsqrt(jnp.maximum(jnp.sum(kh * kh, axis=-1, keepdims=True), 1e-24))
    kk = kk.reshape(B, T, RWKV_WIDTH).astype(k.dtype)
    kd = k[:, :, None, :] * (1 + (a - 1) * k_a)
    return (r, kd, v, kk, decay, a, gd), v_first


def to_scan_layout(z):
    if z.ndim == 3:
        z = jnp.broadcast_to(z[:, :, None, :], z.shape[:2] + (2,) + z.shape[2:])
    z = jnp.stack([z[:, :, 0], jnp.flip(z[:, :, 1], axis=1)], axis=0)
    d, b, t, _ = z.shape
    return jnp.moveaxis(z.reshape(d, b, t, RWKV_HEADS, HEAD_DIM), 2, 0).astype(jnp.float32)


def rwkv_update(S, w, k, v, kk, a):
    sa = jnp.einsum('dbhij,dbhj->dbhi', S, -kk)
    return S * w[..., None, :] + sa[..., :, None] * (kk * a)[..., None, :] + v[..., :, None] * k[..., None, :]


def rwkv_scan(S0, r, decay, kd, v, kk, a, with_output):
    w_s, k_s, v_s, kk_s, a_s = (to_scan_layout(decay), to_scan_layout(kd), to_scan_layout(v),
                                to_scan_layout(kk), to_scan_layout(a))
    if with_output:
        def step(S, inp):
            rr, w, k, vv, kq, aa = inp
            S = rwkv_update(S, w, k, vv, kq, aa)
            return S, jnp.einsum('dbhij,dbhj->dbhi', S, rr)
        S, y = lax.scan(step, S0, (to_scan_layout(r), w_s, k_s, v_s, kk_s, a_s))
        y = y[:, 0] + jnp.flip(y[:, 1], axis=0)
        return S, jnp.moveaxis(y, 0, 1)

    def step_state(S, inp):
        w, k, vv, kq, aa = inp
        return rwkv_update(S, w, k, vv, kq, aa), None
    S, _ = lax.scan(step_state, S0, (w_s, k_s, v_s, kk_s, a_s))
    return S, None


def rwkv_output(y, r, kd, v, gd, gate_up, r_k, gn_w, gn_b):
    B, T, _ = r.shape
    mean = jnp.mean(y, axis=-1, keepdims=True)
    var = jnp.mean(jnp.square(y - mean), axis=-1, keepdims=True)
    yn = ((y - mean) * lax.rsqrt(var + GN_EPS)).reshape(B, T, RWKV_WIDTH).astype(r.dtype) * gn_w + gn_b
    rh = r.reshape(B, T, RWKV_HEADS, HEAD_DIM)
    kh = kd.reshape(B, T, 2, RWKV_HEADS, HEAD_DIM)
    vh = v.reshape(B, T, RWKV_HEADS, HEAD_DIM)
    bonus = jnp.einsum('bthn,btdhn,hn->bth', rh, kh, r_k)[..., None] * vh
    g = jax.nn.sigmoid(gd) @ gate_up
    return (yn + bonus.reshape(B, T, RWKV_WIDTH)) * g


def short_conv_mixer(p, conv_w):
    b_gate, c_gate, u = jnp.split(p, 3, axis=-1)
    return b_gate * depthwise_conv3(c_gate * u, conv_w)


def latent_attention(q, k, v, kc, vc, sink):
    B, S, H, Dh = q.shape
    nb = S // BLOCK
    G = H // ATTN_KV_HEADS
    scale = Dh ** -0.5
    qb = q.reshape(B, nb, BLOCK, ATTN_KV_HEADS, G, Dh)

    def band(z):
        zp = jnp.pad(z, ((0, 0), (BLOCK, BLOCK), (0, 0), (0, 0))).reshape(B, nb + 2, BLOCK, ATTN_KV_HEADS, Dh)
        return jnp.concatenate([zp[:, :-2], zp[:, 1:-1], zp[:, 2:]], axis=2)

    kb, vb = band(k), band(v)
    qpos = jnp.arange(BLOCK)[:, None]
    kpos = jnp.arange(3 * BLOCK)[None, :] - BLOCK
    kabs = jnp.arange(nb)[:, None, None] * BLOCK + kpos[None]
    mask = (jnp.abs(qpos - kpos) <= WINDOW)[None] & (kabs >= 0) & (kabs < S)
    s_band = jnp.einsum('bnqkgd,bnskd->bnkgqs', qb, kb).astype(jnp.float32) * scale
    s_band = jnp.where(mask[None, :, None, None], s_band, NEG_INF)
    s_ctx = jnp.einsum('bnqkgd,bckd->bnkgqc', qb, kc).astype(jnp.float32) * scale
    s_sink = jnp.broadcast_to(sink.astype(jnp.float32).reshape(1, 1, ATTN_KV_HEADS, G, 1, 1),
                              s_ctx.shape[:-1] + (1,))
    probs = jax.nn.softmax(jnp.concatenate([s_band, s_ctx, s_sink], axis=-1), axis=-1).astype(v.dtype)
    nband = 3 * BLOCK
    L = kc.shape[1]
    out = (jnp.einsum('bnkgqs,bnskd->bnqkgd', probs[..., :nband], vb)
           + jnp.einsum('bnkgqc,bckd->bnqkgd', probs[..., nband:nband + L], vc))
    return out.reshape(B, S, H * Dh)


def context_attention(q, kc, vc, sink):
    B, L, H, Dh = q.shape
    G = H // ATTN_KV_HEADS
    qg = q.reshape(B, L, ATTN_KV_HEADS, G, Dh)
    s = jnp.einsum('bqkgd,bckd->bkgqc', qg, kc).astype(jnp.float32) * Dh ** -0.5
    s_sink = jnp.broadcast_to(sink.astype(jnp.float32).reshape(1, ATTN_KV_HEADS, G, 1, 1), s.shape[:-1] + (1,))
    probs = jax.nn.softmax(jnp.concatenate([s, s_sink], axis=-1), axis=-1)[..., :L].astype(vc.dtype)
    return jnp.einsum('bkgqc,bckd->bqkgd', probs, vc).reshape(B, L, H * Dh)


def merge_branches(branches, gate_cols, w_branch, w_out):
    B, T, _ = gate_cols.shape
    gates = jax.nn.sigmoid(gate_cols.reshape(B, T, N_BRANCHES, D_MODEL))
    proj = jnp.einsum('btiw,iwd->btid', branches, w_branch)
    return jnp.einsum('btd,de->bte', jnp.sum(gates * proj, axis=2), w_out)


def expert_choice_ffn(h, w_router, w_gate, w_up, w_down):
    B, n, _ = h.shape
    cap = CAPACITY_FACTOR * n // N_EXPERTS
    aff = jax.nn.softmax(jnp.einsum('bnd,de->bne', h, w_router).astype(jnp.float32), axis=-1)
    g, idx = lax.top_k(jnp.swapaxes(aff, 1, 2), cap)
    bidx = jnp.arange(B)[:, None, None]
    xs = h[bidx, idx]
    hid = jax.nn.silu(jnp.einsum('becd,edf->becf', xs, w_gate)) * jnp.einsum('becd,edf->becf', xs, w_up)
    ys = jnp.einsum('becf,efd->becd', hid, w_down) * g[..., None].astype(h.dtype)
    return jnp.zeros_like(h).at[bidx, idx].add(ys)


def setup_inputs(seed: int = 0) -> dict:
    key = jax.random.key(seed)
    ks = jax.random.split(key, 40)
    f32 = jnp.float32

    def nrm(k, shape, scale):
        return jax.random.normal(k, shape, f32) * scale

    D, L1 = D_MODEL, DEPTH - 1
    return {
        "x": nrm(ks[0], (BATCH, SEQ, D), 1.0),
        "c": nrm(ks[1], (BATCH, D), 1.0),
        "ctx": nrm(ks[2], (BATCH, CTX_LEN, D), 1.0),
        "c_ctx": nrm(ks[3], (D,), 1.0),
        "w_mod": nrm(ks[4], (DEPTH, D, 6 * D), 0.5 * D ** -0.5),
        "b_mod": nrm(ks[5], (DEPTH, 6 * D), 0.02),
        "norm_mix": 1.0 + nrm(ks[6], (DEPTH, D), 0.02),
        "norm_ffn": 1.0 + nrm(ks[7], (DEPTH, D), 0.02),
        "w_in": nrm(ks[8], (DEPTH, D, IN_COLS), D ** -0.5),
        "shift_mu": jax.random.uniform(ks[9], (DEPTH, RWKV_COLS), f32),
        "decay_up": nrm(ks[10], (DEPTH, 2, DECAY_RANK, RWKV_WIDTH), 0.5 * DECAY_RANK ** -0.5),
        "decay_bias": nrm(ks[11], (DEPTH, 2, RWKV_WIDTH), 0.5),
        "iclr_up": nrm(ks[12], (DEPTH, 2, ICLR_RANK, RWKV_WIDTH), 0.5 * ICLR_RANK ** -0.5),
        "iclr_bias": nrm(ks[13], (DEPTH, 2, RWKV_WIDTH), 0.5),
        "gate_up": nrm(ks[14], (DEPTH, GATE_RANK, RWKV_WIDTH), GATE_RANK ** -0.5),
        "vres_down": nrm(ks[15], (L1, RWKV_WIDTH, VRES_RANK), RWKV_WIDTH ** -0.5),
        "vres_up": nrm(ks[16], (L1, VRES_RANK, RWKV_WIDTH), 0.5 * VRES_RANK ** -0.5),
        "vres_bias": nrm(ks[17], (L1, RWKV_WIDTH), 0.5),
        "k_k": 0.85 + nrm(ks[18], (DEPTH, RWKV_WIDTH), 0.05),
        "k_a": 1.0 + nrm(ks[19], (DEPTH, RWKV_WIDTH), 0.05),
        "r_k": nrm(ks[20], (DEPTH, RWKV_HEADS, HEAD_DIM), 0.1),
        "gn_w": 1.0 + nrm(ks[21], (DEPTH, RWKV_WIDTH), 0.02),
        "gn_b": nrm(ks[22], (DEPTH, RWKV_WIDTH), 0.02),
        "conv_w": nrm(ks[23], (DEPTH, CONV_K, CONV_WIDTH), CONV_K ** -0.5),
        "attn_sink": nrm(ks[24], (DEPTH, ATTN_HEADS), 1.0),
        "w_branch": nrm(ks[25], (DEPTH, N_BRANCHES, BRANCH_WIDTH, D), BRANCH_WIDTH ** -0.5),
        "w_out": nrm(ks[26], (DEPTH, D, D), D ** -0.5),
        "w_router": nrm(ks[27], (DEPTH, D, N_EXPERTS), D ** -0.5),
        "w_exp_gate": nrm(ks[28], (DEPTH, N_EXPERTS, D, EXPERT_FF), D ** -0.5),
        "w_exp_up": nrm(ks[29], (DEPTH, N_EXPERTS, D, EXPERT_FF), D ** -0.5),
        "w_exp_down": nrm(ks[30], (DEPTH, N_EXPERTS, EXPERT_FF, D), EXPERT_FF ** -0.5),
        "norm_final": 1.0 + nrm(ks[31], (D,), 0.02),
    }


def reference(x, c, ctx, c_ctx, w_mod, b_mod, norm_mix, norm_ffn, w_in, shift_mu, decay_up, decay_bias,
              iclr_up, iclr_bias, gate_up, vres_down, vres_up, vres_bias, k_k, k_a, r_k, gn_w, gn_b,
              conv_w, attn_sink, w_branch, w_out, w_router, w_exp_gate, w_exp_up, w_exp_down, norm_final):
    B, S, _ = x.shape
    L = ctx.shape[1]
    n_rows = S // GRID_W
    rows = jnp.repeat(jnp.arange(n_rows), GRID_W)
    cols = jnp.tile(jnp.arange(GRID_W), n_rows)
    silu_c = jax.nn.silu(c)
    silu_cc = jax.nn.silu(c_ctx)
    x_l, x_c = x, ctx
    v_first_l = v_first_c = None

    for l in range(DEPTH):
        last = l == DEPTH - 1
        mod_l = (silu_c @ w_mod[l] + b_mod[l]).reshape(B, 6, 1, D_MODEL)
        mod_c = (silu_cc @ w_mod[l] + b_mod[l]).reshape(6, 1, D_MODEL)
        h_l = modulate(rms_norm(x_l, norm_mix[l]), mod_l[:, 0], mod_l[:, 1])
        h_c = modulate(rms_norm(x_c, norm_mix[l]), mod_c[0], mod_c[1])
        p_l = h_l @ w_in[l]
        p_c = h_c @ (w_in[l][:, :CTX_COLS] if last else w_in[l])

        vres = None if l == 0 else (vres_down[l - 1], vres_up[l - 1], vres_bias[l - 1])
        (r_c, kd_c, v_c, kk_c, dec_c, a_c, gd_c), v_first_c = rwkv_streams(
            p_c[..., :RWKV_COLS], shift_mu[l], decay_up[l], decay_bias[l], iclr_up[l], iclr_bias[l],
            k_k[l], k_a[l], v_first_c, vres)
        (r_l, kd_l, v_l, kk_l, dec_l, a_l, gd_l), v_first_l = rwkv_streams(
            p_l[..., :RWKV_COLS], shift_mu[l], decay_up[l], decay_bias[l], iclr_up[l], iclr_bias[l],
            k_k[l], k_a[l], v_first_l, vres)
        s0 = jnp.zeros((2, B, RWKV_HEADS, HEAD_DIM, HEAD_DIM), jnp.float32)
        state_c, y_c = rwkv_scan(s0, r_c, dec_c, kd_c, v_c, kk_c, a_c, not last)
        _, y_l = rwkv_scan(state_c, r_l, dec_l, kd_l, v_l, kk_l, a_l, True)
        br_rwkv_l = rwkv_output(y_l, r_l, kd_l, v_l, gd_l, gate_up[l], r_k[l], gn_w[l], gn_b[l])

        k_ctx = p_c[..., RWKV_COLS:RWKV_COLS + KV_WIDTH].reshape(B, L, ATTN_KV_HEADS, HEAD_DIM)
        v_ctx = p_c[..., RWKV_COLS + KV_WIDTH:CTX_COLS].reshape(B, L, ATTN_KV_HEADS, HEAD_DIM)
        q_lat = rope_2d(p_l[..., CTX_COLS:Q_END].reshape(B, S, ATTN_HEADS, HEAD_DIM), rows, cols)
        k_lat = rope_2d(p_l[..., RWKV_COLS:RWKV_COLS + KV_WIDTH].reshape(B, S, ATTN_KV_HEADS, HEAD_DIM), rows, cols)
        v_lat = p_l[..., RWKV_COLS + KV_WIDTH:CTX_COLS].reshape(B, S, ATTN_KV_HEADS, HEAD_DIM)
        br_attn_l = latent_attention(q_lat, k_lat, v_lat, k_ctx, v_ctx, attn_sink[l])

        br_conv_l = short_conv_mixer(p_l[..., Q_END:CONV_END], conv_w[l])

        out_l = merge_branches(jnp.stack([br_rwkv_l, br_conv_l, br_attn_l], axis=2),
                               p_l[..., CONV_END:], w_branch[l], w_out[l])
        x_l = x_l + mod_l[:, 2] * out_l
        f_l = expert_choice_ffn(modulate(rms_norm(x_l, norm_ffn[l]), mod_l[:, 3], mod_l[:, 4]),
                                w_router[l], w_exp_gate[l], w_exp_up[l], w_exp_down[l])
        x_l = x_l + mod_l[:, 5] * f_l

        if not last:
            br_rwkv_c = rwkv_output(y_c, r_c, kd_c, v_c, gd_c, gate_up[l], r_k[l], gn_w[l], gn_b[l])
            q_ctx = p_c[..., CTX_COLS:Q_END].reshape(B, L, ATTN_HEADS, HEAD_DIM)
            br_attn_c = context_attention(q_ctx, k_ctx, v_ctx, attn_sink[l])
            br_conv_c = short_conv_mixer(p_c[..., Q_END:CONV_END], conv_w[l])
            out_c = merge_branches(jnp.stack([br_rwkv_c, br_conv_c, br_attn_c], axis=2),
                                   p_c[..., CONV_END:], w_branch[l], w_out[l])
            x_c = x_c + mod_c[2] * out_c
            f_c = expert_choice_ffn(modulate(rms_norm(x_c, norm_ffn[l]), mod_c[3], mod_c[4]),
                                    w_router[l], w_exp_gate[l], w_exp_up[l], w_exp_down[l])
            x_c = x_c + mod_c[5] * f_c

    return rms_norm(x_l, norm_final)
```

```python
import functools

import jax
import jax.numpy as jnp
from jax import lax
from jax.experimental import pallas as pl
from jax.experimental.pallas import tpu as pltpu

F32 = jnp.float32
BF16 = jnp.bfloat16
HIGHEST = lax.Precision.HIGHEST

D = 2048
HD = 64
BW = 1024
NH = BW // HD
KVH = 4
GQ = NH // KVH
DECAY_RANK = 96
ICLR_RANK = 96
GATE_RANK = 256
VRES_RANK = 64
GN_EPS = 64e-5
NORM_EPS = 1e-6
WINDOW = 128
BLK = 128
GRID_W = 64
ROPE_BASE = 10000.0
NEG_INF = -1e30
N_EXPERTS = 16
EXPERT_FF = 2048
CAPACITY_FACTOR = 2
CHUNK = 64

C_R, C_K, C_V = 0, 1024, 2048
C_WD = (3072, 3200)
C_AD = (3328, 3456)
C_GD = 3584
RW_PAD = 4096
C_Q = 4096
C_CONV = 5120
C_GATE = 8192
C_AK = 14336
C_AV = 14592
NP = 14848
TN_IN = 512

VMEM_LIMIT = 56 * 1024 * 1024


def _cp(*sem, vmem=VMEM_LIMIT):
    return pltpu.CompilerParams(dimension_semantics=tuple(sem), vmem_limit_bytes=vmem)


def _dot(a, b):
    return jnp.dot(a, b, preferred_element_type=F32)


def _dot_nt(a, b):
    return lax.dot_general(a, b, (((1,), (1,)), ((), ())), preferred_element_type=F32)


def _dot_tn(a, b):
    return lax.dot_general(a, b, (((0,), (0,)), ((), ())), preferred_element_type=F32)


def _sigmoid(x):
    return 1.0 / (1.0 + jnp.exp(-x))


def _seg_sum(x, e_ref):
    hi = x.astype(BF16)
    lo = (x - hi.astype(F32)).astype(BF16)
    e = e_ref[...]
    return _dot(hi, e) + _dot(lo, e)


def _mod_kernel(c_ref, w_ref, b_ref, o_ref):
    c = c_ref[...]
    sc = c * _sigmoid(c)
    o_ref[0] = _dot(sc.astype(BF16), w_ref[0].astype(BF16)) + b_ref[0]


def _mod_call(cc, w_mod, b_mod):
    depth, _, n6 = w_mod.shape
    rows = cc.shape[0]
    tn = 1024
    return pl.pallas_call(
        _mod_kernel,
        grid=(depth, n6 // tn),
        in_specs=[pl.BlockSpec((rows, D), lambda l, j: (0, 0)),
                  pl.BlockSpec((1, D, tn), lambda l, j: (l, 0, j)),
                  pl.BlockSpec((1, 1, tn), lambda l, j: (l, 0, j))],
        out_specs=pl.BlockSpec((1, rows, tn), lambda l, j: (l, 0, j)),
        out_shape=jax.ShapeDtypeStruct((depth, rows, n6), F32),
        compiler_params=_cp("arbitrary", "arbitrary"),
        name="mod_proj",
    )(cc, w_mod, b_mod.reshape(depth, 1, n6))


def _norm_mod(x, gain, shift, scale):
    ms = jnp.mean(x * x, axis=-1, keepdims=True)
    y = x * lax.rsqrt(ms + NORM_EPS)
    return (y * gain) * (1.0 + scale) + shift


def _inproj_kernel(x_ref, gain_ref, shift_ref, scale_ref, w_ref, o_ref, h_ref):
    @pl.when(pl.program_id(1) == 0)
    def _():
        h_ref[...] = _norm_mod(x_ref[...], gain_ref[...], shift_ref[0], scale_ref[0]).astype(BF16)

    o_ref[...] = _dot(h_ref[...], w_ref[...]).astype(BF16)


def _inproj_call(x2, gain, mod3, mod_row, w_p, tm):
    n_tok = x2.shape[0]
    ncols = w_p.shape[1]
    return pl.pallas_call(
        _inproj_kernel,
        grid=(n_tok // tm, ncols // TN_IN),
        in_specs=[pl.BlockSpec((tm, D), lambda i, j: (i, 0)),
                  pl.BlockSpec((1, D), lambda i, j: (0, 0)),
                  pl.BlockSpec((1, 1, D), lambda i, j: (mod_row(i) * 6 + 0, 0, 0)),
                  pl.BlockSpec((1, 1, D), lambda i, j: (mod_row(i) * 6 + 1, 0, 0)),
                  pl.BlockSpec((D, TN_IN), lambda i, j: (0, j))],
        out_specs=pl.BlockSpec((tm, TN_IN), lambda i, j: (i, j)),
        out_shape=jax.ShapeDtypeStruct((n_tok, ncols), BF16),
        scratch_shapes=[pltpu.VMEM((tm, D), BF16)],
        compiler_params=_cp("arbitrary", "arbitrary"),
        name="in_proj",
    )(x2, gain.reshape(1, D), mod3, mod3, w_p)


HALO = 16


def _shifted(cur, prev_blk, next_blk, first, last):
    tt = cur.shape[0]
    row = lax.broadcasted_iota(jnp.int32, cur.shape, 0)
    p_row = jnp.where(first, 0.0, prev_blk[HALO - 1:HALO, :].astype(F32))
    n_row = jnp.where(last, 0.0, next_blk[0:1, :].astype(F32))
    prev = jnp.where(row == 0, p_row, pltpu.roll(cur, 1, axis=0))
    nxt = jnp.where(row == tt - 1, n_row, pltpu.roll(cur, tt - 1, axis=0))
    return prev, nxt


def _halo_specs(tt, width, col_blk, n_tok):
    nb = n_tok // HALO
    r = tt // HALO
    return [pl.BlockSpec((tt, width), lambda i: (i, col_blk)),
            pl.BlockSpec((HALO, width), lambda i: (jnp.maximum(i * r - 1, 0), col_blk)),
            pl.BlockSpec((HALO, width), lambda i: (jnp.minimum((i + 1) * r, nb - 1), col_blk))]


def _streams_kernel(*refs, tiles_per_seq, has_vres):
    if has_vres:
        (p_ref, pp_ref, pn_ref, mu_ref, dup_ref, dbias_ref, iup_ref, ibias_ref, gup_ref, kk_ref_, e_ref,
         vd_ref, vu_ref, vb_ref, vf_ref,
         r_o, k_o, v_o, kk_o, g_o, a_o, lw_o) = refs
    else:
        (p_ref, pp_ref, pn_ref, mu_ref, dup_ref, dbias_ref, iup_ref, ibias_ref, gup_ref, kk_ref_, e_ref,
         r_o, k_o, v_o, kk_o, g_o, a_o, lw_o) = refs
    j = pl.program_id(0) % tiles_per_seq
    cur = p_ref[...].astype(F32)
    prev, nxt = _shifted(cur, pp_ref[...], pn_ref[...], j == 0, j == tiles_per_seq - 1)
    ps = cur + mu_ref[...] * (0.5 * (prev + nxt) - cur)
    r = ps[:, C_R:C_R + BW]
    k = ps[:, C_K:C_K + BW]
    v = ps[:, C_V:C_V + BW]
    gd = ps[:, C_GD:C_GD + GATE_RANK]
    if has_vres:
        low = _dot(v.astype(BF16), vd_ref[...])
        mix = _sigmoid(vb_ref[...] + _dot(low.astype(BF16), vu_ref[...]))
        v = v + (vf_ref[...].astype(F32) - v) * mix
    for d in range(2):
        wd = ps[:, C_WD[d]:C_WD[d] + 128]
        ad = ps[:, C_AD[d]:C_AD[d] + 128]
        w_logit = dbias_ref[d] + _dot(jnp.tanh(wd).astype(BF16), dup_ref[d])
        lw_o[d] = -jnp.exp(-0.5) * _sigmoid(w_logit)
        a_o[d] = _sigmoid(ibias_ref[d] + _dot(ad.astype(BF16), iup_ref[d])).astype(BF16)
    kh = k * kk_ref_[...]
    ss = _seg_sum(kh * kh, e_ref)
    kk = kh * lax.rsqrt(jnp.maximum(ss, 1e-24))
    r_o[...] = r.astype(BF16)
    k_o[...] = k.astype(BF16)
    v_o[...] = v.astype(BF16)
    kk_o[...] = kk.astype(BF16)
    g_o[...] = _dot(_sigmoid(gd).astype(BF16), gup_ref[...]).astype(BF16)


def _streams_call(p, seq_len, lp, v_first):
    n_tok = p.shape[0]
    tt = 256
    has_vres = v_first is not None
    full = lambda *s: pl.BlockSpec(s, lambda i: (0,) * len(s))
    tok = pl.BlockSpec((tt, BW), lambda i: (i, 0))
    tok2 = pl.BlockSpec((2, tt, BW), lambda i: (0, i, 0))
    in_specs = _halo_specs(tt, RW_PAD, 0, n_tok) + [
        full(1, RW_PAD), full(2, 128, BW), full(2, 1, BW), full(2, 128, BW), full(2, 1, BW),
        full(GATE_RANK, BW), full(1, BW), full(BW, BW)]
    args = [p, p, p, lp["mu"], lp["decay_up"], lp["decay_bias"], lp["iclr_up"], lp["iclr_bias"],
            lp["gate_up"], lp["k_k"], lp["eblk"]]
    if has_vres:
        in_specs += [full(BW, 128), full(128, BW), full(1, BW), tok]
        args += [lp["vres_down"], lp["vres_up"], lp["vres_bias"], v_first]
    sd = lambda dt: jax.ShapeDtypeStruct((n_tok, BW), dt)
    sd2 = lambda dt: jax.ShapeDtypeStruct((2, n_tok, BW), dt)
    return pl.pallas_call(
        functools.partial(_streams_kernel, tiles_per_seq=seq_len // tt, has_vres=has_vres),
        grid=(n_tok // tt,),
        in_specs=in_specs,
        out_specs=[tok, tok, tok, tok, tok, tok2, tok2],
        out_shape=[sd(BF16), sd(BF16), sd(BF16), sd(BF16), sd(BF16), sd2(BF16), sd2(F32)],
        compiler_params=_cp("arbitrary"),
        name="rwkv_streams",
    )(*args)


def _scan_kernel(*refs, with_output):
    if with_output:
        r_ref, k_ref, v_ref, kk_ref, a_ref, lw_ref, ka_ref, s0_ref, y_ref, sout_ref, st_ref = refs
    else:
        r_ref, k_ref, v_ref, kk_ref, a_ref, lw_ref, ka_ref, s0_ref, sout_ref, st_ref = refs
    d = pl.program_id(0)
    s = pl.program_id(2)
    c = CHUNK

    @pl.when(s == 0)
    def _():
        st_ref[...] = s0_ref[0, 0]

    sgn = 1 - 2 * d
    ri = lax.broadcasted_iota(jnp.int32, (c, c), 0)
    ci = lax.broadcasted_iota(jnp.int32, (c, c), 1)
    diff = (ri - ci) * sgn
    incl = diff >= 0
    strict = diff > 0
    eye = (ri == ci).astype(F32)

    lw = lw_ref[0]
    tri = incl.astype(BF16)
    lw_hi = lw.astype(BF16)
    lw_lo = (lw - lw_hi.astype(F32)).astype(BF16)
    b = _dot(tri, lw_hi) + _dot(tri, lw_lo)
    b_tot = jnp.where(d == 0, b[c - 1:c, :], b[0:1, :])

    a = a_ref[0].astype(F32)
    kf = k_ref[...].astype(F32)
    kk = kk_ref[...].astype(F32)
    kd = kf * (1.0 + (a - 1.0) * ka_ref[...])
    kka = kk * a
    enb = jnp.exp(-b)
    etail = jnp.exp(b_tot - b)
    rt = (r_ref[...].astype(F32) * jnp.exp(b)).astype(BF16)
    bt = (-kk * jnp.exp(b - lw)).astype(BF16)
    kt = (kd * enb).astype(BF16)
    at = (kka * enb).astype(BF16)
    kh = (kd * etail).astype(BF16)
    ah = (kka * etail).astype(BF16)
    vv = v_ref[...]
    e_tot = jnp.exp(b_tot)

    for h in range(NH):
        sl = slice(h * HD, (h + 1) * HD)
        rb = jnp.concatenate([rt[:, sl], bt[:, sl]], axis=0)
        ka = jnp.concatenate([kt[:, sl], at[:, sl]], axis=0)
        g = _dot_nt(rb, ka)
        a_ba = jnp.where(strict, g[c:, c:], 0.0)
        a_bk = jnp.where(strict, g[c:, :c], 0.0)
        st = st_ref[:, sl]
        rbs = _dot_nt(rb, st.astype(BF16))
        vh = vv[:, sl]
        if with_output:
            a_rk = jnp.where(incl, g[:c, :c], 0.0)
            a_ra = jnp.where(incl, g[:c, c:], 0.0)
            av = _dot(jnp.concatenate([a_rk, a_bk], axis=0).astype(BF16), vh)
            w = rbs[c:] + av[c:]
        else:
            w = rbs[c:] + _dot(a_bk.astype(BF16), vh)
        x = a_ba.astype(BF16)
        t = eye + a_ba
        for _ in range(5):
            x2 = _dot(x, x)
            x = x2.astype(BF16)
            t = t + _dot(x, t.astype(BF16))
        u = _dot(t.astype(BF16), w.astype(BF16))
        ub = u.astype(BF16)
        if with_output:
            y_ref[0, :, sl] = rbs[:c] + av[:c] + _dot(a_ra.astype(BF16), ub)
        vu = jnp.concatenate([vh, ub], axis=0)
        kah = jnp.concatenate([kh[:, sl], ah[:, sl]], axis=0)
        st_ref[:, sl] = st * e_tot[:, sl] + _dot_tn(vu, kah)

    @pl.when(s == pl.num_programs(2) - 1)
    def _():
        sout_ref[0, 0] = st_ref[...]


def _scan_call(streams, k_a, s0, bsz, seq_len, with_output):
    r, k, v, kk, _, a2, lw2 = streams
    n_tok = r.shape[0]
    nc = seq_len // CHUNK

    def row(d, b, s):
        return b * nc + jnp.where(d == 0, s, nc - 1 - s)

    tok = pl.BlockSpec((CHUNK, BW), lambda d, b, s: (row(d, b, s), 0))
    tok2 = pl.BlockSpec((1, CHUNK, BW), lambda d, b, s: (d, row(d, b, s), 0))
    st_spec = pl.BlockSpec((1, 1, HD, BW), lambda d, b, s: (d, b, 0, 0))
    out_specs = [st_spec]
    out_shape = [jax.ShapeDtypeStruct((2, bsz, HD, BW), F32)]
    if with_output:
        out_specs = [tok2] + out_specs
        out_shape = [jax.ShapeDtypeStruct((2, n_tok, BW), F32)] + out_shape
    res = pl.pallas_call(
        functools.partial(_scan_kernel, with_output=with_output),
        grid=(2, bsz, nc),
        in_specs=[tok, tok, tok, tok, tok2, tok2, pl.BlockSpec((1, BW), lambda d, b, s: (0, 0)), st_spec],
        out_specs=out_specs,
        out_shape=out_shape,
        scratch_shapes=[pltpu.VMEM((HD, BW), F32)],
        compiler_params=_cp("arbitrary", "arbitrary", "arbitrary"),
        name="rwkv_scan",
    )(r, k, v, kk, a2, lw2, k_a, s0)
    if with_output:
        return res[1], res[0]
    return res[0], None


def _rwkv_out_kernel(y_ref, r_ref, k_ref, v_ref, a_ref, g_ref, ka_ref, rk_ref, gnw_ref, gnb_ref, e_ref, o_ref):
    y = y_ref[0] + y_ref[1]
    mean = _seg_sum(y, e_ref) * (1.0 / HD)
    yc = y - mean
    var = _seg_sum(yc * yc, e_ref) * (1.0 / HD)
    yn = yc * lax.rsqrt(var + GN_EPS) * gnw_ref[...] + gnb_ref[...]
    r = r_ref[...].astype(F32)
    k = k_ref[...].astype(F32)
    asum = a_ref[0].astype(F32) + a_ref[1].astype(F32)
    kd_sum = k * (2.0 + (asum - 2.0) * ka_ref[...])
    bonus = _seg_sum(r * kd_sum * rk_ref[...], e_ref) * v_ref[...].astype(F32)
    o_ref[...] = ((yn + bonus) * g_ref[...].astype(F32)).astype(BF16)


def _rwkv_out_call(y2, streams, lp):
    r, k, v, _, g, a2, _ = streams
    n_tok = r.shape[0]
    tt = 256
    tok = pl.BlockSpec((tt, BW), lambda i: (i, 0))
    tok2 = pl.BlockSpec((2, tt, BW), lambda i: (0, i, 0))
    vec = pl.BlockSpec((1, BW), lambda i: (0, 0))
    return pl.pallas_call(
        _rwkv_out_kernel,
        grid=(n_tok // tt,),
        in_specs=[tok2, tok, tok, tok, tok2, tok, vec, vec, vec, vec, pl.BlockSpec((BW, BW), lambda i: (0, 0))],
        out_specs=tok,
        out_shape=jax.ShapeDtypeStruct((n_tok, BW), BF16),
        compiler_params=_cp("arbitrary"),
        name="rwkv_out",
    )(y2, r, k, v, a2, g, lp["k_a"], lp["r_k"], lp["gn_w"], lp["gn_b"], lp["eblk"])


def _conv_kernel(b_ref, c_ref, cp_ref, cn_ref, u_ref, up_ref, un_ref, w_ref, o_ref, *, tiles_per_seq):
    j = pl.program_id(0) % tiles_per_seq
    cu = c_ref[...].astype(F32) * u_ref[...].astype(F32)
    cu_p = cp_ref[...].astype(F32) * up_ref[...].astype(F32)
    cu_n = cn_ref[...].astype(F32) * un_ref[...].astype(F32)
    prev, nxt = _shifted(cu, cu_p, cu_n, j == 0, j == tiles_per_seq - 1)
    w = w_ref[...]
    conv = w[0:1] * prev + w[1:2] * cu + w[2:3] * nxt
    o_ref[...] = (b_ref[...].astype(F32) * conv).astype(BF16)


def _conv_call(p, seq_len, conv_w):
    n_tok = p.shape[0]
    tt = 256
    cb = C_CONV // BW
    return pl.pallas_call(
        functools.partial(_conv_kernel, tiles_per_seq=seq_len // tt),
        grid=(n_tok // tt,),
        in_specs=[pl.BlockSpec((tt, BW), lambda i: (i, cb))] + _halo_specs(tt, BW, cb + 1, n_tok)
        + _halo_specs(tt, BW, cb + 2, n_tok) + [pl.BlockSpec((3, BW), lambda i: (0, 0))],
        out_specs=pl.BlockSpec((tt, BW), lambda i: (i, 0)),
        out_shape=jax.ShapeDtypeStruct((n_tok, BW), BF16),
        compiler_params=_cp("arbitrary"),
        name="short_conv",
    )(p, p, p, p, p, p, p, conv_w)


def _rope(x, cos, sin):
    w = x.shape[1]
    lane = lax.broadcasted_iota(jnp.int32, x.shape, 1)
    partner = jnp.where((lane % 32) < 16, pltpu.roll(x, w - 16, axis=1), pltpu.roll(x, 16, axis=1))
    return x * cos + partner * sin


def _softmax_pv(s, sink_col, v):
    m = jnp.maximum(jnp.max(s, axis=-1, keepdims=True), sink_col)
    p = jnp.exp(s - m)
    den = jnp.sum(p, axis=-1, keepdims=True) + jnp.exp(sink_col - m)
    return _dot(p.astype(BF16), v) / den


def _attn_lat_kernel(q_ref, kp_ref, kc_ref, kn_ref, vp_ref, vc_ref, vn_ref, kx_ref, vx_ref,
                     cos_ref, cosp_ref, cosn_ref, sin_ref, sinp_ref, sinn_ref, sink_ref, o_ref, *, nblk):
    n = pl.program_id(1)
    scale = HD ** -0.5
    kvw = KVH * HD
    q = _rope(q_ref[...].astype(F32), cos_ref[...], sin_ref[...]).astype(BF16)
    kb = jnp.concatenate([
        _rope(kp_ref[...].astype(F32), cosp_ref[:, :kvw], sinp_ref[:, :kvw]),
        _rope(kc_ref[...].astype(F32), cos_ref[:, :kvw], sin_ref[:, :kvw]),
        _rope(kn_ref[...].astype(F32), cosn_ref[:, :kvw], sinn_ref[:, :kvw])], axis=0).astype(BF16)
    k_all = jnp.concatenate([kb, kx_ref[...]], axis=0)
    v_all = jnp.concatenate([vp_ref[...], vc_ref[...], vn_ref[...], vx_ref[...]], axis=0)
    n_keys = k_all.shape[0]
    rows = GQ * BLK
    qpos = lax.broadcasted_iota(jnp.int32, (rows, n_keys), 0) % BLK
    kidx = lax.broadcasted_iota(jnp.int32, (rows, n_keys), 1)
    kpos = kidx - BLK
    kabs = n * BLK + kpos
    band_ok = (jnp.abs(qpos - kpos) <= WINDOW) & (kabs >= 0) & (kabs < nblk * BLK)
    ok = band_ok | (kidx >= 3 * BLK)
    sink = sink_ref[...]
    outs = []
    for gi in range(KVH):
        qg = jnp.concatenate([q[:, (gi * GQ + t) * HD:(gi * GQ + t + 1) * HD] for t in range(GQ)], axis=0)
        sk = jnp.concatenate([jnp.broadcast_to(sink[:, gi * GQ + t:gi * GQ + t + 1], (BLK, 1))
                              for t in range(GQ)], axis=0)
        s = _dot_nt(qg, k_all[:, gi * HD:(gi + 1) * HD]) * scale
        s = jnp.where(ok, s, NEG_INF)
        og = _softmax_pv(s, sk, v_all[:, gi * HD:(gi + 1) * HD])
        outs += [og[t * BLK:(t + 1) * BLK] for t in range(GQ)]
    o_ref[...] = jnp.concatenate(outs, axis=1).astype(BF16)


def _attn_lat_call(p_l, p_c, kv_blk_c, cos, sin, sink, bsz, seq_len, ctx_len):
    nblk = seq_len // BLK
    kvw = KVH * HD
    kb, vb = C_AK // kvw, C_AV // kvw
    rowq = lambda b, n: b * nblk + n
    rowp = lambda b, n: b * nblk + jnp.maximum(n - 1, 0)
    rown = lambda b, n: b * nblk + jnp.minimum(n + 1, nblk - 1)
    tabp = lambda b, n: (jnp.maximum(n - 1, 0), 0)
    tabn = lambda b, n: (jnp.minimum(n + 1, nblk - 1), 0)
    kv = lambda rf, cb: pl.BlockSpec((BLK, kvw), lambda b, n: (rf(b, n), cb))
    tab = lambda f: pl.BlockSpec((BLK, BW), f)
    return pl.pallas_call(
        functools.partial(_attn_lat_kernel, nblk=nblk),
        grid=(bsz, nblk),
        in_specs=[pl.BlockSpec((BLK, BW), lambda b, n: (rowq(b, n), C_Q // BW)),
                  kv(rowp, kb), kv(rowq, kb), kv(rown, kb), kv(rowp, vb), kv(rowq, vb), kv(rown, vb),
                  pl.BlockSpec((ctx_len, kvw), lambda b, n: (b, kv_blk_c)),
                  pl.BlockSpec((ctx_len, kvw), lambda b, n: (b, kv_blk_c + 1)),
                  tab(lambda b, n: (n, 0)), tab(tabp), tab(tabn),
                  tab(lambda b, n: (n, 0)), tab(tabp), tab(tabn),
                  pl.BlockSpec((1, NH), lambda b, n: (0, 0))],
        out_specs=pl.BlockSpec((BLK, BW), lambda b, n: (rowq(b, n), 0)),
        out_shape=jax.ShapeDtypeStruct((bsz * seq_len, BW), BF16),
        compiler_params=_cp("arbitrary", "arbitrary"),
        name="attn_latent",
    )(p_l, p_l, p_l, p_l, p_l, p_l, p_l, p_c, p_c, cos, cos, cos, sin, sin, sin, sink)


def _attn_ctx_kernel(q_ref, kx_ref, vx_ref, sink_ref, o_ref):
    scale = HD ** -0.5
    q = q_ref[...]
    kx = kx_ref[...]
    vx = vx_ref[...]
    sink = sink_ref[...]
    tq = q.shape[0]
    outs = []
    for gi in range(KVH):
        qg = jnp.concatenate([q[:, (gi * GQ + t) * HD:(gi * GQ + t + 1) * HD] for t in range(GQ)], axis=0)
        sk = jnp.concatenate([jnp.broadcast_to(sink[:, gi * GQ + t:gi * GQ + t + 1], (tq, 1))
                              for t in range(GQ)], axis=0)
        s = _dot_nt(qg, kx[:, gi * HD:(gi + 1) * HD]) * scale
        og = _softmax_pv(s, sk, vx[:, gi * HD:(gi + 1) * HD])
        outs += [og[t * tq:(t + 1) * tq] for t in range(GQ)]
    o_ref[...] = jnp.concatenate(outs, axis=1).astype(BF16)


def _attn_ctx_call(p_c, sink, bsz, ctx_len):
    kvw = KVH * HD
    tq = 128
    nq = ctx_len // tq
    return pl.pallas_call(
        _attn_ctx_kernel,
        grid=(bsz, nq),
        in_specs=[pl.BlockSpec((tq, BW), lambda b, n: (b * nq + n, C_Q // BW)),
                  pl.BlockSpec((ctx_len, kvw), lambda b, n: (b, C_AK // kvw)),
                  pl.BlockSpec((ctx_len, kvw), lambda b, n: (b, C_AV // kvw)),
                  pl.BlockSpec((1, NH), lambda b, n: (0, 0))],
        out_specs=pl.BlockSpec((tq, BW), lambda b, n: (b * nq + n, 0)),
        out_shape=jax.ShapeDtypeStruct((bsz * ctx_len, BW), BF16),
        compiler_params=_cp("arbitrary", "arbitrary"),
        name="attn_context",
    )(p_c, p_c, p_c, sink)


def _merge_kernel(b0_ref, b1_ref, b2_ref, g0_ref, g1_ref, g2_ref, w_ref, o_ref):
    acc = _sigmoid(g0_ref[...].astype(F32)) * _dot(b0_ref[...], w_ref[0])
    acc += _sigmoid(g1_ref[...].astype(F32)) * _dot(b1_ref[...], w_ref[1])
    acc += _sigmoid(g2_ref[...].astype(F32)) * _dot(b2_ref[...], w_ref[2])
    o_ref[...] = acc.astype(BF16)


def _merge_call(br_rwkv, br_conv, br_attn, p, w_branch):
    n_tok = p.shape[0]
    tm, tn = 512, 1024
    nn = D // tn
    br = pl.BlockSpec((tm, BW), lambda j, i: (i, 0))
    gate = lambda t: pl.BlockSpec((tm, tn), lambda j, i: (i, (C_GATE + t * D) // tn + j))
    return pl.pallas_call(
        _merge_kernel,
        grid=(nn, n_tok // tm),
        in_specs=[br, br, br, gate(0), gate(1), gate(2), pl.BlockSpec((3, BW, tn), lambda j, i: (0, 0, j))],
        out_specs=pl.BlockSpec((tm, tn), lambda j, i: (i, j)),
        out_shape=jax.ShapeDtypeStruct((n_tok, D), BF16),
        compiler_params=_cp("arbitrary", "arbitrary"),
        name="merge_branches",
    )(br_rwkv, br_conv, br_attn, p, p, p, w_branch)


def _outproj_kernel(m_ref, w_ref, x_ref, gate_ref, o_ref):
    o_ref[...] = x_ref[...] + gate_ref[0] * _dot(m_ref[...], w_ref[...])


def _outproj_call(m, w_out, x2, mod3, mod_row, tm):
    n_tok = x2.shape[0]
    return pl.pallas_call(
        _outproj_kernel,
        grid=(n_tok // tm,),
        in_specs=[pl.BlockSpec((tm, D), lambda i: (i, 0)),
                  pl.BlockSpec((D, D), lambda i: (0, 0)),
                  pl.BlockSpec((tm, D), lambda i: (i, 0)),
                  pl.BlockSpec((1, 1, D), lambda i: (mod_row(i) * 6 + 2, 0, 0))],
        out_specs=pl.BlockSpec((tm, D), lambda i: (i, 0)),
        out_shape=jax.ShapeDtypeStruct((n_tok, D), F32),
        compiler_params=_cp("arbitrary"),
        name="out_proj",
    )(m, w_out, x2, mod3)


def _ffn_prep_kernel(x_ref, gain_ref, shift_ref, scale_ref, wr_ref, h_ref, aff_ref):
    h = _norm_mod(x_ref[...], gain_ref[...], shift_ref[0], scale_ref[0])
    h_ref[...] = h.astype(BF16)
    logits = jnp.dot(h, wr_ref[...], precision=HIGHEST, preferred_element_type=F32)
    lane = lax.broadcasted_iota(jnp.int32, logits.shape, 1)
    logits = jnp.where(lane < N_EXPERTS, logits, NEG_INF)
    m = jnp.max(logits, axis=-1, keepdims=True)
    e = jnp.exp(logits - m)
    aff_ref[...] = e / jnp.sum(e, axis=-1, keepdims=True)


def _ffn_prep_call(x2, gain, mod3, mod_row, wr_pad, tm):
    n_tok = x2.shape[0]
    return pl.pallas_call(
        _ffn_prep_kernel,
        grid=(n_tok // tm,),
        in_specs=[pl.BlockSpec((tm, D), lambda i: (i, 0)),
                  pl.BlockSpec((1, D), lambda i: (0, 0)),
                  pl.BlockSpec((1, 1, D), lambda i: (mod_row(i) * 6 + 3, 0, 0)),
                  pl.BlockSpec((1, 1, D), lambda i: (mod_row(i) * 6 + 4, 0, 0)),
                  pl.BlockSpec((D, 128), lambda i: (0, 0))],
        out_specs=[pl.BlockSpec((tm, D), lambda i: (i, 0)), pl.BlockSpec((tm, 128), lambda i: (i, 0))],
        out_shape=[jax.ShapeDtypeStruct((n_tok, D), BF16), jax.ShapeDtypeStruct((n_tok, 128), F32)],
        compiler_params=_cp("arbitrary"),
        name="ffn_prep",
    )(x2, gain.reshape(1, D), mod3, mod3, wr_pad)


def _select_kernel(aff_ref, slot_ref, *, cap):
    a = aff_ref[0]
    n = a.shape[1]
    bits = lax.bitcast_convert_type(a, jnp.int32)

    def body(i, t):
        cand = t | jnp.left_shift(jnp.int32(1), 30 - i)
        cnt = jnp.sum((bits >= cand).astype(jnp.int32), axis=-1, keepdims=True)
        return jnp.where(cnt >= cap, cand, t)

    thr = lax.fori_loop(0, 31, body, jnp.zeros((a.shape[0], 1), jnp.int32))
    gt = bits > thr
    eq = bits == thr
    n_gt = jnp.sum(gt.astype(jnp.int32), axis=-1, keepdims=True)
    tc = min(n, 512)

    def prefix(mask_bf16):
        cols = []
        for j0 in range(0, n, tc):
            ri = lax.broadcasted_iota(jnp.int32, (n, tc), 0)
            ci = lax.broadcasted_iota(jnp.int32, (n, tc), 1) + j0
            cols.append(_dot(mask_bf16, (ri <= ci).astype(BF16)))
        return jnp.concatenate(cols, axis=1)

    eq_f = eq.astype(BF16)
    excl_eq = prefix(eq_f) - eq_f.astype(F32)
    sel = gt | (eq & (excl_eq < (cap - n_gt).astype(F32)))
    pos = prefix(sel.astype(BF16))
    slot_ref[0] = jnp.where(sel, pos.astype(jnp.int32) - 1, -1)


def _select_call(aff_t, cap):
    bsz, ne, n = aff_t.shape
    return pl.pallas_call(
        functools.partial(_select_kernel, cap=cap),
        grid=(bsz,),
        in_specs=[pl.BlockSpec((1, ne, n), lambda b: (b, 0, 0))],
        out_specs=pl.BlockSpec((1, ne, n), lambda b: (b, 0, 0)),
        out_shape=jax.ShapeDtypeStruct((bsz, ne, n), jnp.int32),
        compiler_params=_cp("arbitrary"),
        name="expert_select",
    )(aff_t)


def _gather_kernel(slot_ref, aff_ref, h_ref, xs_ref, g_ref, *, cap):
    slot = slot_ref[0]
    n = slot.shape[1]
    onehot = lax.broadcasted_iota(jnp.int32, (cap, n), 0) == slot
    xs_ref[0] = _dot(onehot.astype(BF16), h_ref[...]).astype(BF16)
    g_ref[0] = jnp.sum(jnp.where(onehot, aff_ref[0], 0.0), axis=-1, keepdims=True)


def _gather_call(slot, aff_t, h2, cap):
    bsz, ne, n = slot.shape
    return pl.pallas_call(
        functools.partial(_gather_kernel, cap=cap),
        grid=(bsz, ne),
        in_specs=[pl.BlockSpec((1, 1, n), lambda b, e: (b * ne + e, 0, 0)),
                  pl.BlockSpec((1, 1, n), lambda b, e: (b * ne + e, 0, 0)),
                  pl.BlockSpec((n, D), lambda b, e: (b, 0))],
        out_specs=[pl.BlockSpec((1, cap, D), lambda b, e: (e, b, 0)),
                   pl.BlockSpec((1, cap, 1), lambda b, e: (e, b, 0))],
        out_shape=[jax.ShapeDtypeStruct((ne, bsz * cap, D), BF16),
                   jax.ShapeDtypeStruct((ne, bsz * cap, 1), F32)],
        compiler_params=_cp("arbitrary", "arbitrary"),
        name="expert_gather",
    )(slot.reshape(bsz * ne, 1, n), aff_t.reshape(bsz * ne, 1, n), h2)


def _expert_kernel(x_ref, wg_ref, wu_ref, wd_ref, g_ref, o_ref, acc_ref):
    f = pl.program_id(2)

    @pl.when(f == 0)
    def _():
        acc_ref[...] = jnp.zeros_like(acc_ref)

    x = x_ref[0]
    gate = _dot(x, wg_ref[0])
    hid = gate * _sigmoid(gate) * _dot(x, wu_ref[0])
    acc_ref[...] += _dot(hid.astype(BF16), wd_ref[0])

    @pl.when(f == pl.num_programs(2) - 1)
    def _():
        o_ref[0] = (acc_ref[...] * g_ref[0]).astype(BF16)


def _expert_call(xs, g, wg, wu, wd):
    ne, rows, _ = xs.shape
    tm = min(rows, 1024)
    tf = 512
    return pl.pallas_call(
        _expert_kernel,
        grid=(ne, rows // tm, EXPERT_FF // tf),
        in_specs=[pl.BlockSpec((1, tm, D), lambda e, i, f: (e, i, 0)),
                  pl.BlockSpec((1, D, tf), lambda e, i, f: (e, 0, f)),
                  pl.BlockSpec((1, D, tf), lambda e, i, f: (e, 0, f)),
                  pl.BlockSpec((1, tf, D), lambda e, i, f: (e, f, 0)),
                  pl.BlockSpec((1, tm, 1), lambda e, i, f: (e, i, 0))],
        out_specs=pl.BlockSpec((1, tm, D), lambda e, i, f: (e, i, 0)),
        out_shape=jax.ShapeDtypeStruct((ne, rows, D), BF16),
        scratch_shapes=[pltpu.VMEM((tm, D), F32)],
        compiler_params=_cp("arbitrary", "arbitrary", "arbitrary"),
        name="expert_mlp",
    )(xs, wg, wu, wd, g)


def _scatter_kernel(slot_ref, ys_ref, x_ref, gate_ref, o_ref, *, cap):
    slot_t = slot_ref[0]
    n = slot_t.shape[0]
    lane = lax.broadcasted_iota(jnp.int32, (n, cap), 1)
    acc = jnp.zeros(x_ref.shape, F32)
    for e in range(N_EXPERTS):
        onehot = (slot_t[:, e:e + 1] == lane).astype(BF16)
        acc += _dot(onehot, ys_ref[e])
    o_ref[...] = x_ref[...] + gate_ref[0] * acc


def _scatter_call(slot_t, ys, x2, mod3, mod_row_b, cap):
    bsz, n, ne = slot_t.shape
    tn = 512
    return pl.pallas_call(
        functools.partial(_scatter_kernel, cap=cap),
        grid=(bsz, D // tn),
        in_specs=[pl.BlockSpec((1, n, ne), lambda b, j: (b, 0, 0)),
                  pl.BlockSpec((ne, cap, tn), lambda b, j: (0, b, j)),
                  pl.BlockSpec((n, tn), lambda b, j: (b, j)),
                  pl.BlockSpec((1, 1, tn), lambda b, j: (mod_row_b(b) * 6 + 5, 0, j))],
        out_specs=pl.BlockSpec((n, tn), lambda b, j: (b, j)),
        out_shape=jax.ShapeDtypeStruct(x2.shape, F32),
        compiler_params=_cp("arbitrary", "arbitrary"),
        name="expert_scatter",
    )(slot_t, ys, x2, mod3)


def _ffn(x2, bsz, n, lp, mod3, mod_row, mod_row_b, tm):
    cap = CAPACITY_FACTOR * n // N_EXPERTS
    h2, aff = _ffn_prep_call(x2, lp["norm_ffn"], mod3, mod_row, lp["w_router"], tm)
    aff_t = jnp.swapaxes(aff[:, :N_EXPERTS].reshape(bsz, n, N_EXPERTS), 1, 2)
    slot = _select_call(aff_t, cap)
    xs, g = _gather_call(slot, aff_t, h2, cap)
    ys = _expert_call(xs, g, lp["w_exp_gate"], lp["w_exp_up"], lp["w_exp_down"])
    return _scatter_call(jnp.swapaxes(slot, 1, 2), ys, x2, mod3, mod_row_b, cap)


def _final_norm_kernel(x_ref, gain_ref, o_ref):
    x = x_ref[...]
    ms = jnp.mean(x * x, axis=-1, keepdims=True)
    o_ref[...] = (x * lax.rsqrt(ms + NORM_EPS)) * gain_ref[...]


def _final_norm_call(x2, gain):
    n_tok = x2.shape[0]
    tm = 512
    return pl.pallas_call(
        _final_norm_kernel,
        grid=(n_tok // tm,),
        in_specs=[pl.BlockSpec((tm, D), lambda i: (i, 0)), pl.BlockSpec((1, D), lambda i: (0, 0))],
        out_specs=pl.BlockSpec((tm, D), lambda i: (i, 0)),
        out_shape=jax.ShapeDtypeStruct(x2.shape, F32),
        compiler_params=_cp("arbitrary"),
        name="final_norm",
    )(x2, gain.reshape(1, D))


def _pad_cols(parts):
    out = []
    for a, width in parts:
        out.append(a)
        if a.shape[-1] < width:
            out.append(jnp.zeros(a.shape[:-1] + (width - a.shape[-1],), a.dtype))
    return jnp.concatenate(out, axis=-1)


def _rwkv_col_parts(w):
    o = 3 * BW
    parts = [(w[..., :o], o)]
    for _ in range(2):
        parts.append((w[..., o:o + DECAY_RANK], 128))
        o += DECAY_RANK
    for _ in range(2):
        parts.append((w[..., o:o + ICLR_RANK], 128))
        o += ICLR_RANK
    parts.append((w[..., o:o + GATE_RANK], RW_PAD - C_GD))
    return parts, o + GATE_RANK


def _prep_layer(l, depth, w_in, shift_mu, decay_up, decay_bias, iclr_up, iclr_bias, gate_up, vres_down, vres_up,
                vres_bias, k_k, k_a, r_k, gn_w, gn_b, conv_w, attn_sink, w_branch, w_out, w_router,
                w_exp_gate, w_exp_up, w_exp_down, norm_mix, norm_ffn, eblk):
    w = w_in[l]
    parts, rw_end = _rwkv_col_parts(w)
    kvw = KVH * HD
    ak = w[:, rw_end:rw_end + kvw]
    av = w[:, rw_end + kvw:rw_end + 2 * kvw]
    o = rw_end + 2 * kvw
    q = w[:, o:o + BW]
    conv = w[:, o + BW:o + 4 * BW]
    gates = w[:, o + 4 * BW:]
    w_p = jnp.concatenate([_pad_cols(parts), q, conv, gates, ak, av], axis=1).astype(BF16)
    mu_parts, _ = _rwkv_col_parts(shift_mu[l][None, :])
    pad_rank = lambda u: jnp.pad(u, ((0, 0), (0, 128 - u.shape[1]), (0, 0))).astype(BF16)
    lp = dict(
        w_in=w_p, mu=_pad_cols(mu_parts),
        decay_up=pad_rank(decay_up[l]), decay_bias=decay_bias[l].reshape(2, 1, BW),
        iclr_up=pad_rank(iclr_up[l]), iclr_bias=iclr_bias[l].reshape(2, 1, BW),
        gate_up=gate_up[l].astype(BF16), k_k=k_k[l].reshape(1, BW), k_a=k_a[l].reshape(1, BW),
        r_k=r_k[l].reshape(1, BW), gn_w=gn_w[l].reshape(1, BW), gn_b=gn_b[l].reshape(1, BW),
        conv_w=conv_w[l], sink=attn_sink[l].reshape(1, NH),
        w_branch=w_branch[l].astype(BF16), w_out=w_out[l].astype(BF16),
        w_router=jnp.pad(w_router[l], ((0, 0), (0, 128 - N_EXPERTS))),
        w_exp_gate=w_exp_gate[l].astype(BF16), w_exp_up=w_exp_up[l].astype(BF16),
        w_exp_down=w_exp_down[l].astype(BF16),
        norm_mix=norm_mix[l], norm_ffn=norm_ffn[l], eblk=eblk)
    if l > 0:
        lp["vres_down"] = jnp.pad(vres_down[l - 1], ((0, 0), (0, 128 - VRES_RANK))).astype(BF16)
        lp["vres_up"] = jnp.pad(vres_up[l - 1], ((0, 128 - VRES_RANK), (0, 0))).astype(BF16)
        lp["vres_bias"] = vres_bias[l - 1].reshape(1, BW)
    if l == depth - 1:
        lp["w_in_ctx"] = jnp.concatenate([w_p[:, :RW_PAD], w_p[:, C_AK:]], axis=1)
    return lp


def _rope_tables(seq_len):
    quarter = HD // 4
    inv = ROPE_BASE ** (-jnp.arange(quarter, dtype=F32) / quarter)
    pos = jnp.arange(seq_len)
    ang_r = (pos // GRID_W).astype(F32)[:, None] * inv[None, :]
    ang_c = (pos % GRID_W).astype(F32)[:, None] * inv[None, :]
    cos = jnp.concatenate([jnp.cos(ang_r)] * 2 + [jnp.cos(ang_c)] * 2, axis=1)
    sin = jnp.concatenate([-jnp.sin(ang_r), jnp.sin(ang_r), -jnp.sin(ang_c), jnp.sin(ang_c)], axis=1)
    return jnp.tile(cos, (1, NH)), jnp.tile(sin, (1, NH))


def kernel(x, c, ctx, c_ctx, w_mod, b_mod, norm_mix, norm_ffn, w_in, shift_mu, decay_up, decay_bias, iclr_up,
           iclr_bias, gate_up, vres_down, vres_up, vres_bias, k_k, k_a, r_k, gn_w, gn_b, conv_w, attn_sink,
           w_branch, w_out, w_router, w_exp_gate, w_exp_up, w_exp_down, norm_final):
    bsz, seq_len, _ = x.shape
    ctx_len = ctx.shape[1]
    depth = w_in.shape[0]
    mod_rows = -(-(bsz + 1) // 8) * 8
    cc = jnp.concatenate([c, c_ctx[None, :], jnp.zeros((mod_rows - bsz - 1, D), F32)], axis=0)
    mod_all = _mod_call(cc, w_mod, b_mod)

    hd_i = jnp.arange(BW) // HD
    eblk = (hd_i[:, None] == hd_i[None, :]).astype(BF16)
    cos, sin = _rope_tables(seq_len)

    tm_l = min(1024, seq_len)
    tm_c = min(1024, bsz * ctx_len)
    tiles_per_sample = seq_len // tm_l
    row_l = lambda i: i // tiles_per_sample
    row_c = lambda i: bsz
    tm2 = 512
    row_l2 = lambda i: i // (seq_len // tm2)

    x_l = x.reshape(bsz * seq_len, D)
    x_c = ctx.reshape(bsz * ctx_len, D)
    vf_l = vf_c = None
    zero_state = jnp.zeros((2, bsz, HD, BW), F32)

    for l in range(depth):
        last = l == depth - 1
        lp = _prep_layer(l, depth, w_in, shift_mu, decay_up, decay_bias, iclr_up, iclr_bias, gate_up, vres_down,
                         vres_up, vres_bias, k_k, k_a, r_k, gn_w, gn_b, conv_w, attn_sink, w_branch, w_out,
                         w_router, w_exp_gate, w_exp_up, w_exp_down, norm_mix, norm_ffn, eblk)
        mod3 = mod_all[l].reshape(mod_rows * 6, 1, D)

        p_l = _inproj_call(x_l, lp["norm_mix"], mod3, row_l, lp["w_in"], tm_l)
        p_c = _inproj_call(x_c, lp["norm_mix"], mod3, row_c, lp["w_in_ctx"] if last else lp["w_in"], tm_c)
        kv_blk_c = (RW_PAD if last else C_AK) // (KVH * HD)

        st_c = _streams_call(p_c, ctx_len, lp, vf_c)
        st_l = _streams_call(p_l, seq_len, lp, vf_l)
        if l == 0:
            vf_c, vf_l = st_c[2], st_l[2]
        state_c, y_c = _scan_call(st_c, lp["k_a"], zero_state, bsz, ctx_len, not last)
        _, y_l = _scan_call(st_l, lp["k_a"], state_c, bsz, seq_len, True)
        br_rwkv_l = _rwkv_out_call(y_l, st_l, lp)

        br_attn_l = _attn_lat_call(p_l, p_c, kv_blk_c, cos, sin, lp["sink"], bsz, seq_len, ctx_len)
        br_conv_l = _conv_call(p_l, seq_len, lp["conv_w"])

        m_l = _merge_call(br_rwkv_l, br_conv_l, br_attn_l, p_l, lp["w_branch"])
        x_l = _outproj_call(m_l, lp["w_out"], x_l, mod3, row_l2, tm2)
        x_l = _ffn(x_l, bsz, seq_len, lp, mod3, row_l2, lambda b: b, tm2)

        if not last:
            br_rwkv_c = _rwkv_out_call(y_c, st_c, lp)
            br_attn_c = _attn_ctx_call(p_c, lp["sink"], bsz, ctx_len)
            br_conv_c = _conv_call(p_c, ctx_len, lp["conv_w"])
            m_c = _merge_call(br_rwkv_c, br_conv_c, br_attn_c, p_c, lp["w_branch"])
            x_c = _outproj_call(m_c, lp["w_out"], x_c, mod3, row_c, tm2)
            x_c = _ffn(x_c, bsz, ctx_len, lp, mod3, row_c, lambda b: bsz, tm2)

    return _final_norm_call(x_l, norm_final).reshape(bsz, seq_len, D)
```

```python
import functools

import jax
import jax.numpy as jnp
from jax import lax
from jax.experimental import pallas as pl
from jax.experimental.pallas import tpu as pltpu

F32 = jnp.float32
BF16 = jnp.bfloat16
HIGHEST = lax.Precision.HIGHEST

D = 2048
HD = 64
BW = 1024
NH = BW // HD
KVH = 4
GQ = NH // KVH
DECAY_RANK = 96
ICLR_RANK = 96
GATE_RANK = 256
VRES_RANK = 64
GN_EPS = 64e-5
NORM_EPS = 1e-6
WINDOW = 128
BLK = 128
assert WINDOW == BLK
GRID_W = 64
ROPE_BASE = 10000.0
NEG_INF = -1e30
N_EXPERTS = 16
EXPERT_FF = 2048
CAPACITY_FACTOR = 2
CHUNK = 64

C_R, C_K, C_V = 0, 1024, 2048
C_WD = (3072, 3200)
C_AD = (3328, 3456)
C_GD = 3584
RW_PAD = 4096
C_Q = 4096
C_CONV = 5120
C_GATE = 8192
C_AK = 14336
C_AV = 14592
NP = 14848
TN_IN = 512

VMEM_LIMIT = 56 * 1024 * 1024


def _cp(*sem, vmem=VMEM_LIMIT):
    return pltpu.CompilerParams(dimension_semantics=tuple(sem), vmem_limit_bytes=vmem)


def _dot(a, b):
    return jnp.dot(a, b, preferred_element_type=F32)


def _dot_nt(a, b):
    return lax.dot_general(a, b, (((1,), (1,)), ((), ())), preferred_element_type=F32)


def _dot_tn(a, b):
    return lax.dot_general(a, b, (((0,), (0,)), ((), ())), preferred_element_type=F32)


def _sigmoid(x):
    return 1.0 / (1.0 + jnp.exp(-x))


def _seg_sum(x, e_ref):
    e = e_ref[...]

    def split(z):
        hi = z.astype(BF16)
        return hi, (z - hi.astype(F32)).astype(BF16)

    hi, lo = split(x)
    hi2, lo2 = split(_dot(hi, e) + _dot(lo, e))
    return _dot_nt(hi2, e) + _dot_nt(lo2, e)


def _mod_kernel(c_ref, w_ref, b_ref, o_ref):
    c = c_ref[...]
    sc = c * _sigmoid(c)
    o_ref[0] = _dot(sc.astype(BF16), w_ref[0].astype(BF16)) + b_ref[0]


def _mod_call(cc, w_mod, b_mod):
    depth, _, n6 = w_mod.shape
    rows = cc.shape[0]
    tn = 1024
    return pl.pallas_call(
        _mod_kernel,
        grid=(depth, n6 // tn),
        in_specs=[pl.BlockSpec((rows, D), lambda l, j: (0, 0)),
                  pl.BlockSpec((1, D, tn), lambda l, j: (l, 0, j)),
                  pl.BlockSpec((1, 1, tn), lambda l, j: (l, 0, j))],
        out_specs=pl.BlockSpec((1, rows, tn), lambda l, j: (l, 0, j)),
        out_shape=jax.ShapeDtypeStruct((depth, rows, n6), F32),
        compiler_params=_cp("arbitrary", "arbitrary"),
        name="mod_proj",
    )(cc, w_mod, b_mod.reshape(depth, 1, n6))


def _norm_mod(x, gain, shift, scale):
    ms = jnp.mean(x * x, axis=-1, keepdims=True)
    y = x * lax.rsqrt(ms + NORM_EPS)
    return (y * gain) * (1.0 + scale) + shift


def _inproj_kernel(x_ref, gain_ref, shift_ref, scale_ref, w_ref, o_ref, h_ref):
    @pl.when(pl.program_id(1) == 0)
    def _():
        h_ref[...] = _norm_mod(x_ref[...], gain_ref[...], shift_ref[0], scale_ref[0]).astype(BF16)

    o_ref[...] = _dot(h_ref[...], w_ref[...]).astype(BF16)


def _inproj_call(x2, gain, mod3, mod_row, w_p, tm):
    n_tok = x2.shape[0]
    ncols = w_p.shape[1]
    return pl.pallas_call(
        _inproj_kernel,
        grid=(n_tok // tm, ncols // TN_IN),
        in_specs=[pl.BlockSpec((tm, D), lambda i, j: (i, 0)),
                  pl.BlockSpec((1, D), lambda i, j: (0, 0)),
                  pl.BlockSpec((1, 1, D), lambda i, j: (mod_row(i) * 6 + 0, 0, 0)),
                  pl.BlockSpec((1, 1, D), lambda i, j: (mod_row(i) * 6 + 1, 0, 0)),
                  pl.BlockSpec((D, TN_IN), lambda i, j: (0, j))],
        out_specs=pl.BlockSpec((tm, TN_IN), lambda i, j: (i, j)),
        out_shape=jax.ShapeDtypeStruct((n_tok, ncols), BF16),
        scratch_shapes=[pltpu.VMEM((tm, D), BF16)],
        compiler_params=_cp("arbitrary", "arbitrary"),
        name="in_proj",
    )(x2, gain.reshape(1, D), mod3, mod3, w_p)


HALO = 16


def _shifted(cur, prev_blk, next_blk, first, last):
    tt = cur.shape[0]
    row = lax.broadcasted_iota(jnp.int32, cur.shape, 0)
    p_row = jnp.where(first, 0.0, prev_blk[HALO - 1:HALO, :].astype(F32))
    n_row = jnp.where(last, 0.0, next_blk[0:1, :].astype(F32))
    prev = jnp.where(row == 0, p_row, pltpu.roll(cur, 1, axis=0))
    nxt = jnp.where(row == tt - 1, n_row, pltpu.roll(cur, tt - 1, axis=0))
    return prev, nxt


def _halo_specs(tt, width, col_blk, n_tok):
    nb = n_tok // HALO
    r = tt // HALO
    return [pl.BlockSpec((tt, width), lambda i: (i, col_blk)),
            pl.BlockSpec((HALO, width), lambda i: (jnp.maximum(i * r - 1, 0), col_blk)),
            pl.BlockSpec((HALO, width), lambda i: (jnp.minimum((i + 1) * r, nb - 1), col_blk))]


def _streams_kernel(*refs, tiles_per_seq, has_vres):
    if has_vres:
        (p_ref, pp_ref, pn_ref, mu_ref, dup_ref, dbias_ref, iup_ref, ibias_ref, gup_ref, kk_ref_, e_ref,
         vd_ref, vu_ref, vb_ref, vf_ref,
         r_o, k_o, v_o, kk_o, g_o, a_o, lw_o) = refs
    else:
        (p_ref, pp_ref, pn_ref, mu_ref, dup_ref, dbias_ref, iup_ref, ibias_ref, gup_ref, kk_ref_, e_ref,
         r_o, k_o, v_o, kk_o, g_o, a_o, lw_o) = refs
    j = pl.program_id(0) % tiles_per_seq
    cur = p_ref[...].astype(F32)
    prev, nxt = _shifted(cur, pp_ref[...], pn_ref[...], j == 0, j == tiles_per_seq - 1)
    ps = cur + mu_ref[...] * (0.5 * (prev + nxt) - cur)
    r = ps[:, C_R:C_R + BW]
    k = ps[:, C_K:C_K + BW]
    v = ps[:, C_V:C_V + BW]
    gd = ps[:, C_GD:C_GD + GATE_RANK]
    if has_vres:
        low = _dot(v.astype(BF16), vd_ref[...])
        mix = _sigmoid(vb_ref[...] + _dot(low.astype(BF16), vu_ref[...]))
        v = v + (vf_ref[...].astype(F32) - v) * mix
    for d in range(2):
        wd = ps[:, C_WD[d]:C_WD[d] + 128]
        ad = ps[:, C_AD[d]:C_AD[d] + 128]
        w_logit = dbias_ref[d] + _dot(jnp.tanh(wd).astype(BF16), dup_ref[d])
        lw_o[d] = -jnp.exp(-0.5) * _sigmoid(w_logit)
        a_o[d] = _sigmoid(ibias_ref[d] + _dot(ad.astype(BF16), iup_ref[d])).astype(BF16)
    kh = k * kk_ref_[...]
    ss = _seg_sum(kh * kh, e_ref)
    kk = kh * lax.rsqrt(jnp.maximum(ss, 1e-24))
    r_o[...] = r.astype(BF16)
    k_o[...] = k.astype(BF16)
    v_o[...] = v.astype(BF16)
    kk_o[...] = kk.astype(BF16)
    g_o[...] = _dot(_sigmoid(gd).astype(BF16), gup_ref[...]).astype(BF16)


def _streams_call(p, seq_len, lp, v_first):
    n_tok = p.shape[0]
    tt = 256
    has_vres = v_first is not None
    full = lambda *s: pl.BlockSpec(s, lambda i: (0,) * len(s))
    tok = pl.BlockSpec((tt, BW), lambda i: (i, 0))
    tok2 = pl.BlockSpec((2, tt, BW), lambda i: (0, i, 0))
    in_specs = _halo_specs(tt, RW_PAD, 0, n_tok) + [
        full(1, RW_PAD), full(2, 128, BW), full(2, 1, BW), full(2, 128, BW), full(2, 1, BW),
        full(GATE_RANK, BW), full(1, BW), full(BW, 128)]
    args = [p, p, p, lp["mu"], lp["decay_up"], lp["decay_bias"], lp["iclr_up"], lp["iclr_bias"],
            lp["gate_up"], lp["k_k"], lp["eblk"]]
    if has_vres:
        in_specs += [full(BW, 128), full(128, BW), full(1, BW), tok]
        args += [lp["vres_down"], lp["vres_up"], lp["vres_bias"], v_first]
    sd = lambda dt: jax.ShapeDtypeStruct((n_tok, BW), dt)
    sd2 = lambda dt: jax.ShapeDtypeStruct((2, n_tok, BW), dt)
    return pl.pallas_call(
        functools.partial(_streams_kernel, tiles_per_seq=seq_len // tt, has_vres=has_vres),
        grid=(n_tok // tt,),
        in_specs=in_specs,
        out_specs=[tok, tok, tok, tok, tok, tok2, tok2],
        out_shape=[sd(BF16), sd(BF16), sd(BF16), sd(BF16), sd(BF16), sd2(BF16), sd2(F32)],
        compiler_params=_cp("arbitrary"),
        name="rwkv_streams",
    )(*args)


def _scan_prologue(d, r_ref, k_ref, v_ref, kk_ref, a_ref, lw_ref, ka_ref):
    c = CHUNK
    ri = lax.broadcasted_iota(jnp.int32, (c, c), 0)
    ci = lax.broadcasted_iota(jnp.int32, (c, c), 1)
    diff = (ci - ri) if d else (ri - ci)
    incl = diff >= 0
    strict = diff > 0
    lw = lw_ref[0]
    tri = incl.astype(BF16)
    lw_hi = lw.astype(BF16)
    lw_lo = (lw - lw_hi.astype(F32)).astype(BF16)
    b = _dot(tri, lw_hi) + _dot(tri, lw_lo)
    b_tot = b[0:1, :] if d else b[c - 1:c, :]
    a = a_ref[0].astype(F32)
    kk = kk_ref[...].astype(F32)
    kd = k_ref[...].astype(F32) * (1.0 + (a - 1.0) * ka_ref[...])
    kka = kk * a
    enb = jnp.exp(-b)
    etail = jnp.exp(b_tot - b)
    return dict(
        incl=incl, strict=strict,
        rt=(r_ref[...].astype(F32) * jnp.exp(b)).astype(BF16),
        bt=(-kk * jnp.exp(b - lw)).astype(BF16),
        kt=(kd * enb).astype(BF16), at=(kka * enb).astype(BF16),
        kh=(kd * etail).astype(BF16), ah=(kka * etail).astype(BF16),
        vv=v_ref[...], e_tot=jnp.exp(b_tot))


def _scan_kernel(*refs, with_output):
    ins = refs[:14]
    if with_output:
        yf_ref, yb_ref, sout_ref, st_ref = refs[14:]
    else:
        sout_ref, st_ref = refs[14:]
    ka_ref, s0_ref = ins[12], ins[13]
    s = pl.program_id(1)
    c = CHUNK

    @pl.when(s == 0)
    def _():
        st_ref[...] = s0_ref[:, 0]

    pro = [_scan_prologue(d, *ins[6 * d:6 * d + 6], ka_ref) for d in range(2)]
    st_all = [st_ref[d] for d in range(2)]

    units = [(d, slice(h * HD, (h + 1) * HD)) for d in range(2) for h in range(NH)]
    us = range(len(units))
    col = lambda name: [pro[d][name][:, sl] for d, sl in units]
    strict = [pro[d]["strict"] for d, _ in units]
    incl = [pro[d]["incl"] for d, _ in units]
    st = [st_all[d][:, sl] for d, sl in units]
    vh = col("vv")
    rb = [jnp.concatenate(p, axis=0) for p in zip(col("rt"), col("bt"))]
    ka = [jnp.concatenate(p, axis=0) for p in zip(col("kt"), col("at"))]
    g = [_dot_nt(rb[u], ka[u]) for u in us]
    rbs = [_dot_nt(rb[u], st[u].astype(BF16)) for u in us]
    x = [jnp.where(strict[u], g[u][c:, c:], 0.0).astype(BF16) for u in us]
    a_bk = [jnp.where(strict[u], g[u][c:, :c], 0.0) for u in us]
    if with_output:
        a_rk = [jnp.where(incl[u], g[u][:c, :c], 0.0) for u in us]
        a_ra = [jnp.where(incl[u], g[u][:c, c:], 0.0).astype(BF16) for u in us]
        av = [_dot(jnp.concatenate([a_rk[u], a_bk[u]], axis=0).astype(BF16), vh[u]) for u in us]
        w = [rbs[u][c:] + av[u][c:] for u in us]
    else:
        w = [rbs[u][c:] + _dot(a_bk[u].astype(BF16), vh[u]) for u in us]
    for rnd in range(6):
        if rnd < 5:
            m = [_dot(x[u], jnp.concatenate([x[u], w[u].astype(BF16)], axis=1)) for u in us]
            x = [m[u][:, :c].astype(BF16) for u in us]
            w = [w[u] + m[u][:, c:] for u in us]
        else:
            w = [w[u] + _dot(x[u], w[u].astype(BF16)) for u in us]
    ub = [w[u].astype(BF16) for u in us]
    if with_output:
        ys = [rbs[u][:c] + av[u][:c] + _dot(a_ra[u], ub[u]) for u in us]
        yf_ref[...] = jnp.concatenate(ys[:NH], axis=1)
        yb_ref[...] = jnp.concatenate(ys[NH:], axis=1)
    upd = [_dot_tn(jnp.concatenate([vh[u], ub[u]], axis=0), jnp.concatenate(p, axis=0))
           for u, p in zip(us, zip(col("kh"), col("ah")))]
    for d in range(2):
        st_ref[d] = st_all[d] * pro[d]["e_tot"] + jnp.concatenate(upd[d * NH:(d + 1) * NH], axis=1)

    @pl.when(s == pl.num_programs(1) - 1)
    def _():
        sout_ref[:, 0] = st_ref[...]


def _scan_call(streams, k_a, s0, bsz, seq_len, with_output):
    r, k, v, kk, _, a2, lw2 = streams
    n_tok = r.shape[0]
    nc = seq_len // CHUNK

    rows = (lambda b, s: b * nc + s, lambda b, s: b * nc + nc - 1 - s)
    tok = [pl.BlockSpec((CHUNK, BW), lambda b, s, f=f: (f(b, s), 0)) for f in rows]
    tok2 = [pl.BlockSpec((1, CHUNK, BW), lambda b, s, f=f, d=d: (d, f(b, s), 0)) for d, f in enumerate(rows)]
    st_spec = pl.BlockSpec((2, 1, HD, BW), lambda b, s: (0, b, 0, 0))
    in_specs, args = [], []
    for d in range(2):
        in_specs += [tok[d]] * 4 + [tok2[d]] * 2
        args += [r, k, v, kk, a2, lw2]
    in_specs += [pl.BlockSpec((1, BW), lambda b, s: (0, 0)), st_spec]
    out_specs = [st_spec]
    out_shape = [jax.ShapeDtypeStruct((2, bsz, HD, BW), F32)]
    if with_output:
        out_specs = tok + out_specs
        out_shape = [jax.ShapeDtypeStruct((n_tok, BW), F32)] * 2 + out_shape
    res = pl.pallas_call(
        functools.partial(_scan_kernel, with_output=with_output),
        grid=(bsz, nc),
        in_specs=in_specs,
        out_specs=out_specs,
        out_shape=out_shape,
        scratch_shapes=[pltpu.VMEM((2, HD, BW), F32)],
        compiler_params=_cp("arbitrary", "arbitrary"),
        name="rwkv_scan",
    )(*args, k_a, s0)
    if with_output:
        return res[2], (res[0], res[1])
    return res[0], None


def _rwkv_out_kernel(yf_ref, yb_ref, r_ref, k_ref, v_ref, a_ref, g_ref, ka_ref, rk_ref, gnw_ref, gnb_ref, e_ref,
                     o_ref):
    y = yf_ref[...] + yb_ref[...]
    mean = _seg_sum(y, e_ref) * (1.0 / HD)
    yc = y - mean
    var = _seg_sum(yc * yc, e_ref) * (1.0 / HD)
    yn = yc * lax.rsqrt(var + GN_EPS) * gnw_ref[...] + gnb_ref[...]
    r = r_ref[...].astype(F32)
    k = k_ref[...].astype(F32)
    asum = a_ref[0].astype(F32) + a_ref[1].astype(F32)
    kd_sum = k * (2.0 + (asum - 2.0) * ka_ref[...])
    bonus = _seg_sum(r * kd_sum * rk_ref[...], e_ref) * v_ref[...].astype(F32)
    o_ref[...] = ((yn + bonus) * g_ref[...].astype(F32)).astype(BF16)


def _rwkv_out_call(y2, streams, lp):
    r, k, v, _, g, a2, _ = streams
    n_tok = r.shape[0]
    tt = 256
    tok = pl.BlockSpec((tt, BW), lambda i: (i, 0))
    tok2 = pl.BlockSpec((2, tt, BW), lambda i: (0, i, 0))
    vec = pl.BlockSpec((1, BW), lambda i: (0, 0))
    return pl.pallas_call(
        _rwkv_out_kernel,
        grid=(n_tok // tt,),
        in_specs=[tok, tok, tok, tok, tok, tok2, tok, vec, vec, vec, vec, pl.BlockSpec((BW, 128), lambda i: (0, 0))],
        out_specs=tok,
        out_shape=jax.ShapeDtypeStruct((n_tok, BW), BF16),
        compiler_params=_cp("arbitrary"),
        name="rwkv_out",
    )(y2[0], y2[1], r, k, v, a2, g, lp["k_a"], lp["r_k"], lp["gn_w"], lp["gn_b"], lp["eblk"])


def _conv_kernel(b_ref, c_ref, cp_ref, cn_ref, u_ref, up_ref, un_ref, w_ref, o_ref, *, tiles_per_seq):
    j = pl.program_id(0) % tiles_per_seq
    cu = c_ref[...].astype(F32) * u_ref[...].astype(F32)
    cu_p = cp_ref[...].astype(F32) * up_ref[...].astype(F32)
    cu_n = cn_ref[...].astype(F32) * un_ref[...].astype(F32)
    prev, nxt = _shifted(cu, cu_p, cu_n, j == 0, j == tiles_per_seq - 1)
    w = w_ref[...]
    conv = w[0:1] * prev + w[1:2] * cu + w[2:3] * nxt
    o_ref[...] = (b_ref[...].astype(F32) * conv).astype(BF16)


def _conv_call(p, seq_len, conv_w):
    n_tok = p.shape[0]
    tt = 256
    cb = C_CONV // BW
    return pl.pallas_call(
        functools.partial(_conv_kernel, tiles_per_seq=seq_len // tt),
        grid=(n_tok // tt,),
        in_specs=[pl.BlockSpec((tt, BW), lambda i: (i, cb))] + _halo_specs(tt, BW, cb + 1, n_tok)
        + _halo_specs(tt, BW, cb + 2, n_tok) + [pl.BlockSpec((3, BW), lambda i: (0, 0))],
        out_specs=pl.BlockSpec((tt, BW), lambda i: (i, 0)),
        out_shape=jax.ShapeDtypeStruct((n_tok, BW), BF16),
        compiler_params=_cp("arbitrary"),
        name="short_conv",
    )(p, p, p, p, p, p, p, conv_w)


def _rope(x, cos, sin):
    w = x.shape[1]
    lane = lax.broadcasted_iota(jnp.int32, x.shape, 1)
    partner = jnp.where((lane % 32) < 16, pltpu.roll(x, w - 16, axis=1), pltpu.roll(x, 16, axis=1))
    return x * cos + partner * sin


LOG2E = 1.4426950408889634
Q_SCALE = HD ** -0.5 * LOG2E


def _softmax_pv(s2, sink2_col, v):
    m = jnp.maximum(jnp.max(s2, axis=-1, keepdims=True), sink2_col)
    p = jnp.exp2(s2 - m)
    den = jnp.sum(p, axis=-1, keepdims=True) + jnp.exp2(sink2_col - m)
    return _dot(p.astype(BF16), v) / den


def _attn_lat_kernel(q_ref, kp_ref, kc_ref, kn_ref, vp_ref, vc_ref, vn_ref, kx_ref, vx_ref,
                     cos_ref, cosp_ref, cosn_ref, sin_ref, sinp_ref, sinn_ref, sink_ref, o_ref, *, nblk):
    n = pl.program_id(1)
    kvw = KVH * HD
    q = (_rope(q_ref[...].astype(F32), cos_ref[...], sin_ref[...]) * Q_SCALE).astype(BF16)
    kb = jnp.concatenate([
        _rope(kp_ref[...].astype(F32), cosp_ref[:, :kvw], sinp_ref[:, :kvw]),
        _rope(kc_ref[...].astype(F32), cos_ref[:, :kvw], sin_ref[:, :kvw]),
        _rope(kn_ref[...].astype(F32), cosn_ref[:, :kvw], sinn_ref[:, :kvw])], axis=0).astype(BF16)
    k_all = jnp.concatenate([kb, kx_ref[...]], axis=0)
    v_all = jnp.concatenate([vp_ref[...], vc_ref[...], vn_ref[...], vx_ref[...]], axis=0)
    rows = GQ * BLK
    qpos = lax.broadcasted_iota(jnp.int32, (rows, BLK), 0) % BLK
    kidx = lax.broadcasted_iota(jnp.int32, (rows, BLK), 1)
    bias_p = jnp.where((kidx >= qpos) & (n > 0), 0.0, NEG_INF)
    bias_n = jnp.where((kidx <= qpos) & (n < nblk - 1), 0.0, NEG_INF)
    sink = sink_ref[...] * LOG2E
    outs = []
    for gi in range(KVH):
        qg = jnp.concatenate([q[:, (gi * GQ + t) * HD:(gi * GQ + t + 1) * HD] for t in range(GQ)], axis=0)
        sk = jnp.concatenate([jnp.broadcast_to(sink[:, gi * GQ + t:gi * GQ + t + 1], (BLK, 1))
                              for t in range(GQ)], axis=0)
        s = _dot_nt(qg, k_all[:, gi * HD:(gi + 1) * HD])
        s = jnp.concatenate([s[:, :BLK] + bias_p, s[:, BLK:2 * BLK], s[:, 2 * BLK:3 * BLK] + bias_n,
                             s[:, 3 * BLK:]], axis=1)
        og = _softmax_pv(s, sk, v_all[:, gi * HD:(gi + 1) * HD])
        outs += [og[t * BLK:(t + 1) * BLK] for t in range(GQ)]
    o_ref[...] = jnp.concatenate(outs, axis=1).astype(BF16)


def _attn_lat_call(p_l, p_c, kv_blk_c, cos, sin, sink, bsz, seq_len, ctx_len):
    nblk = seq_len // BLK
    kvw = KVH * HD
    kb, vb = C_AK // kvw, C_AV // kvw
    rowq = lambda b, n: b * nblk + n
    rowp = lambda b, n: b * nblk + jnp.maximum(n - 1, 0)
    rown = lambda b, n: b * nblk + jnp.minimum(n + 1, nblk - 1)
    tabp = lambda b, n: (jnp.maximum(n - 1, 0), 0)
    tabn = lambda b, n: (jnp.minimum(n + 1, nblk - 1), 0)
    kv = lambda rf, cb: pl.BlockSpec((BLK, kvw), lambda b, n: (rf(b, n), cb))
    tab = lambda f: pl.BlockSpec((BLK, BW), f)
    return pl.pallas_call(
        functools.partial(_attn_lat_kernel, nblk=nblk),
        grid=(bsz, nblk),
        in_specs=[pl.BlockSpec((BLK, BW), lambda b, n: (rowq(b, n), C_Q // BW)),
                  kv(rowp, kb), kv(rowq, kb), kv(rown, kb), kv(rowp, vb), kv(rowq, vb), kv(rown, vb),
                  pl.BlockSpec((ctx_len, kvw), lambda b, n: (b, kv_blk_c)),
                  pl.BlockSpec((ctx_len, kvw), lambda b, n: (b, kv_blk_c + 1)),
                  tab(lambda b, n: (n, 0)), tab(tabp), tab(tabn),
                  tab(lambda b, n: (n, 0)), tab(tabp), tab(tabn),
                  pl.BlockSpec((1, NH), lambda b, n: (0, 0))],
        out_specs=pl.BlockSpec((BLK, BW), lambda b, n: (rowq(b, n), 0)),
        out_shape=jax.ShapeDtypeStruct((bsz * seq_len, BW), BF16),
        compiler_params=_cp("arbitrary", "arbitrary"),
        name="attn_latent",
    )(p_l, p_l, p_l, p_l, p_l, p_l, p_l, p_c, p_c, cos, cos, cos, sin, sin, sin, sink)


def _attn_ctx_kernel(q_ref, kx_ref, vx_ref, sink_ref, o_ref):
    q = (q_ref[...].astype(F32) * Q_SCALE).astype(BF16)
    kx = kx_ref[...]
    vx = vx_ref[...]
    sink = sink_ref[...] * LOG2E
    tq = q.shape[0]
    outs = []
    for gi in range(KVH):
        qg = jnp.concatenate([q[:, (gi * GQ + t) * HD:(gi * GQ + t + 1) * HD] for t in range(GQ)], axis=0)
        sk = jnp.concatenate([jnp.broadcast_to(sink[:, gi * GQ + t:gi * GQ + t + 1], (tq, 1))
                              for t in range(GQ)], axis=0)
        s = _dot_nt(qg, kx[:, gi * HD:(gi + 1) * HD])
        og = _softmax_pv(s, sk, vx[:, gi * HD:(gi + 1) * HD])
        outs += [og[t * tq:(t + 1) * tq] for t in range(GQ)]
    o_ref[...] = jnp.concatenate(outs, axis=1).astype(BF16)


def _attn_ctx_call(p_c, sink, bsz, ctx_len):
    kvw = KVH * HD
    tq = 128
    nq = ctx_len // tq
    return pl.pallas_call(
        _attn_ctx_kernel,
        grid=(bsz, nq),
        in_specs=[pl.BlockSpec((tq, BW), lambda b, n: (b * nq + n, C_Q // BW)),
                  pl.BlockSpec((ctx_len, kvw), lambda b, n: (b, C_AK // kvw)),
                  pl.BlockSpec((ctx_len, kvw), lambda b, n: (b, C_AV // kvw)),
                  pl.BlockSpec((1, NH), lambda b, n: (0, 0))],
        out_specs=pl.BlockSpec((tq, BW), lambda b, n: (b * nq + n, 0)),
        out_shape=jax.ShapeDtypeStruct((bsz * ctx_len, BW), BF16),
        compiler_params=_cp("arbitrary", "arbitrary"),
        name="attn_context",
    )(p_c, p_c, p_c, sink)


def _merge_kernel(b0_ref, b1_ref, b2_ref, g0_ref, g1_ref, g2_ref, w_ref, o_ref):
    acc = _sigmoid(g0_ref[...].astype(F32)) * _dot(b0_ref[...], w_ref[0])
    acc += _sigmoid(g1_ref[...].astype(F32)) * _dot(b1_ref[...], w_ref[1])
    acc += _sigmoid(g2_ref[...].astype(F32)) * _dot(b2_ref[...], w_ref[2])
    o_ref[...] = acc.astype(BF16)


def _merge_call(br_rwkv, br_conv, br_attn, p, w_branch):
    n_tok = p.shape[0]
    tm, tn = 512, 1024
    nn = D // tn
    br = pl.BlockSpec((tm, BW), lambda j, i: (i, 0))
    gate = lambda t: pl.BlockSpec((tm, tn), lambda j, i: (i, (C_GATE + t * D) // tn + j))
    return pl.pallas_call(
        _merge_kernel,
        grid=(nn, n_tok // tm),
        in_specs=[br, br, br, gate(0), gate(1), gate(2), pl.BlockSpec((3, BW, tn), lambda j, i: (0, 0, j))],
        out_specs=pl.BlockSpec((tm, tn), lambda j, i: (i, j)),
        out_shape=jax.ShapeDtypeStruct((n_tok, D), BF16),
        compiler_params=_cp("arbitrary", "arbitrary"),
        name="merge_branches",
    )(br_rwkv, br_conv, br_attn, p, p, p, w_branch)


def _outproj_kernel(m_ref, w_ref, x_ref, gate_ref, o_ref):
    o_ref[...] = x_ref[...] + gate_ref[0] * _dot(m_ref[...], w_ref[...])


def _outproj_call(m, w_out, x2, mod3, mod_row, tm):
    n_tok = x2.shape[0]
    return pl.pallas_call(
        _outproj_kernel,
        grid=(n_tok // tm,),
        in_specs=[pl.BlockSpec((tm, D), lambda i: (i, 0)),
                  pl.BlockSpec((D, D), lambda i: (0, 0)),
                  pl.BlockSpec((tm, D), lambda i: (i, 0)),
                  pl.BlockSpec((1, 1, D), lambda i: (mod_row(i) * 6 + 2, 0, 0))],
        out_specs=pl.BlockSpec((tm, D), lambda i: (i, 0)),
        out_shape=jax.ShapeDtypeStruct((n_tok, D), F32),
        compiler_params=_cp("arbitrary"),
        name="out_proj",
    )(m, w_out, x2, mod3)


def _ffn_prep_kernel(x_ref, gain_ref, shift_ref, scale_ref, wr_ref, h_ref, aff_ref):
    h = _norm_mod(x_ref[...], gain_ref[...], shift_ref[0], scale_ref[0])
    h_ref[...] = h.astype(BF16)
    logits = jnp.dot(h, wr_ref[...], precision=HIGHEST, preferred_element_type=F32)
    lane = lax.broadcasted_iota(jnp.int32, logits.shape, 1)
    logits = jnp.where(lane < N_EXPERTS, logits, NEG_INF)
    m = jnp.max(logits, axis=-1, keepdims=True)
    e = jnp.exp(logits - m)
    aff_ref[...] = e / jnp.sum(e, axis=-1, keepdims=True)


def _ffn_prep_call(x2, gain, mod3, mod_row, wr_pad, tm):
    n_tok = x2.shape[0]
    return pl.pallas_call(
        _ffn_prep_kernel,
        grid=(n_tok // tm,),
        in_specs=[pl.BlockSpec((tm, D), lambda i: (i, 0)),
                  pl.BlockSpec((1, D), lambda i: (0, 0)),
                  pl.BlockSpec((1, 1, D), lambda i: (mod_row(i) * 6 + 3, 0, 0)),
                  pl.BlockSpec((1, 1, D), lambda i: (mod_row(i) * 6 + 4, 0, 0)),
                  pl.BlockSpec((D, 128), lambda i: (0, 0))],
        out_specs=[pl.BlockSpec((tm, D), lambda i: (i, 0)), pl.BlockSpec((tm, 128), lambda i: (i, 0))],
        out_shape=[jax.ShapeDtypeStruct((n_tok, D), BF16), jax.ShapeDtypeStruct((n_tok, 128), F32)],
        compiler_params=_cp("arbitrary"),
        name="ffn_prep",
    )(x2, gain.reshape(1, D), mod3, mod3, wr_pad)


def _select_kernel(aff_ref, slot_ref, *, cap):
    a = aff_ref[0]
    n = a.shape[1]
    bits = lax.bitcast_convert_type(a, jnp.int32)

    def body(i, t):
        cand = t | jnp.left_shift(jnp.int32(1), 30 - i)
        cnt = jnp.sum((bits >= cand).astype(jnp.int32), axis=-1, keepdims=True)
        return jnp.where(cnt >= cap, cand, t)

    thr = lax.fori_loop(0, 31, body, jnp.zeros((a.shape[0], 1), jnp.int32))
    gt = bits > thr
    eq = bits == thr
    n_gt = jnp.sum(gt.astype(jnp.int32), axis=-1, keepdims=True)
    tc = min(n, 512)

    def prefix(mask_bf16):
        cols = []
        for j0 in range(0, n, tc):
            ri = lax.broadcasted_iota(jnp.int32, (n, tc), 0)
            ci = lax.broadcasted_iota(jnp.int32, (n, tc), 1) + j0
            cols.append(_dot(mask_bf16, (ri <= ci).astype(BF16)))
        return jnp.concatenate(cols, axis=1)

    eq_f = eq.astype(BF16)
    excl_eq = prefix(eq_f) - eq_f.astype(F32)
    sel = gt | (eq & (excl_eq < (cap - n_gt).astype(F32)))
    pos = prefix(sel.astype(BF16))
    slot_ref[0] = jnp.where(sel, pos.astype(jnp.int32) - 1, -1)


def _select_call(aff_t, cap):
    bsz, ne, n = aff_t.shape
    return pl.pallas_call(
        functools.partial(_select_kernel, cap=cap),
        grid=(bsz,),
        in_specs=[pl.BlockSpec((1, ne, n), lambda b: (b, 0, 0))],
        out_specs=pl.BlockSpec((1, ne, n), lambda b: (b, 0, 0)),
        out_shape=jax.ShapeDtypeStruct((bsz, ne, n), jnp.int32),
        compiler_params=_cp("arbitrary"),
        name="expert_select",
    )(aff_t)


def _gather_kernel(slot_ref, aff_ref, h_ref, xs_ref, g_ref, *, cap):
    slot = slot_ref[0]
    n = slot.shape[1]
    onehot = lax.broadcasted_iota(jnp.int32, (cap, n), 0) == slot
    xs_ref[0] = _dot(onehot.astype(BF16), h_ref[...]).astype(BF16)
    g_ref[0] = jnp.sum(jnp.where(onehot, aff_ref[0], 0.0), axis=-1, keepdims=True)


def _gather_call(slot, aff_t, h2, cap):
    bsz, ne, n = slot.shape
    return pl.pallas_call(
        functools.partial(_gather_kernel, cap=cap),
        grid=(bsz, ne),
        in_specs=[pl.BlockSpec((1, 1, n), lambda b, e: (b * ne + e, 0, 0)),
                  pl.BlockSpec((1, 1, n), lambda b, e: (b * ne + e, 0, 0)),
                  pl.BlockSpec((n, D), lambda b, e: (b, 0))],
        out_specs=[pl.BlockSpec((1, cap, D), lambda b, e: (e, b, 0)),
                   pl.BlockSpec((1, cap, 1), lambda b, e: (e, b, 0))],
        out_shape=[jax.ShapeDtypeStruct((ne, bsz * cap, D), BF16),
                   jax.ShapeDtypeStruct((ne, bsz * cap, 1), F32)],
        compiler_params=_cp("arbitrary", "arbitrary"),
        name="expert_gather",
    )(slot.reshape(bsz * ne, 1, n), aff_t.reshape(bsz * ne, 1, n), h2)


def _expert_kernel(x_ref, wg_ref, wu_ref, wd_ref, g_ref, o_ref, acc_ref):
    f = pl.program_id(2)

    @pl.when(f == 0)
    def _():
        acc_ref[...] = jnp.zeros_like(acc_ref)

    x = x_ref[0]
    gate = _dot(x, wg_ref[0])
    hid = gate * _sigmoid(gate) * _dot(x, wu_ref[0])
    acc_ref[...] += _dot(hid.astype(BF16), wd_ref[0])

    @pl.when(f == pl.num_programs(2) - 1)
    def _():
        o_ref[0] = (acc_ref[...] * g_ref[0]).astype(BF16)


def _expert_call(xs, g, wg, wu, wd):
    ne, rows, _ = xs.shape
    tm = min(rows, 1024)
    tf = 512
    return pl.pallas_call(
        _expert_kernel,
        grid=(ne, rows // tm, EXPERT_FF // tf),
        in_specs=[pl.BlockSpec((1, tm, D), lambda e, i, f: (e, i, 0)),
                  pl.BlockSpec((1, D, tf), lambda e, i, f: (e, 0, f)),
                  pl.BlockSpec((1, D, tf), lambda e, i, f: (e, 0, f)),
                  pl.BlockSpec((1, tf, D), lambda e, i, f: (e, f, 0)),
                  pl.BlockSpec((1, tm, 1), lambda e, i, f: (e, i, 0))],
        out_specs=pl.BlockSpec((1, tm, D), lambda e, i, f: (e, i, 0)),
        out_shape=jax.ShapeDtypeStruct((ne, rows, D), BF16),
        scratch_shapes=[pltpu.VMEM((tm, D), F32)],
        compiler_params=_cp("arbitrary", "arbitrary", "arbitrary"),
        name="expert_mlp",
    )(xs, wg, wu, wd, g)


def _scatter_kernel(slot_ref, ys_ref, x_ref, gate_ref, o_ref, *, cap):
    slot_t = slot_ref[0]
    n = slot_t.shape[0]
    lane = lax.broadcasted_iota(jnp.int32, (n, cap), 1)
    acc = jnp.zeros(x_ref.shape, F32)
    for e in range(N_EXPERTS):
        onehot = (slot_t[:, e:e + 1] == lane).astype(BF16)
        acc += _dot(onehot, ys_ref[e])
    o_ref[...] = x_ref[...] + gate_ref[0] * acc


def _scatter_call(slot_t, ys, x2, mod3, mod_row_b, cap):
    bsz, n, ne = slot_t.shape
    tn = 512
    return pl.pallas_call(
        functools.partial(_scatter_kernel, cap=cap),
        grid=(bsz, D // tn),
        in_specs=[pl.BlockSpec((1, n, ne), lambda b, j: (b, 0, 0)),
                  pl.BlockSpec((ne, cap, tn), lambda b, j: (0, b, j)),
                  pl.BlockSpec((n, tn), lambda b, j: (b, j)),
                  pl.BlockSpec((1, 1, tn), lambda b, j: (mod_row_b(b) * 6 + 5, 0, j))],
        out_specs=pl.BlockSpec((n, tn), lambda b, j: (b, j)),
        out_shape=jax.ShapeDtypeStruct(x2.shape, F32),
        compiler_params=_cp("arbitrary", "arbitrary"),
        name="expert_scatter",
    )(slot_t, ys, x2, mod3)


def _ffn(x2, bsz, n, lp, mod3, mod_row, mod_row_b, tm):
    cap = CAPACITY_FACTOR * n // N_EXPERTS
    h2, aff = _ffn_prep_call(x2, lp["norm_ffn"], mod3, mod_row, lp["w_router"], tm)
    aff_t = jnp.swapaxes(aff[:, :N_EXPERTS].reshape(bsz, n, N_EXPERTS), 1, 2)
    slot = _select_call(aff_t, cap)
    xs, g = _gather_call(slot, aff_t, h2, cap)
    ys = _expert_call(xs, g, lp["w_exp_gate"], lp["w_exp_up"], lp["w_exp_down"])
    return _scatter_call(jnp.swapaxes(slot, 1, 2), ys, x2, mod3, mod_row_b, cap)


def _final_norm_kernel(x_ref, gain_ref, o_ref):
    x = x_ref[...]
    ms = jnp.mean(x * x, axis=-1, keepdims=True)
    o_ref[...] = (x * lax.rsqrt(ms + NORM_EPS)) * gain_ref[...]


def _final_norm_call(x2, gain):
    n_tok = x2.shape[0]
    tm = 512
    return pl.pallas_call(
        _final_norm_kernel,
        grid=(n_tok // tm,),
        in_specs=[pl.BlockSpec((tm, D), lambda i: (i, 0)), pl.BlockSpec((1, D), lambda i: (0, 0))],
        out_specs=pl.BlockSpec((tm, D), lambda i: (i, 0)),
        out_shape=jax.ShapeDtypeStruct(x2.shape, F32),
        compiler_params=_cp("arbitrary"),
        name="final_norm",
    )(x2, gain.reshape(1, D))


def _pad_cols(parts):
    out = []
    for a, width in parts:
        out.append(a)
        if a.shape[-1] < width:
            out.append(jnp.zeros(a.shape[:-1] + (width - a.shape[-1],), a.dtype))
    return jnp.concatenate(out, axis=-1)


def _rwkv_col_parts(w):
    o = 3 * BW
    parts = [(w[..., :o], o)]
    for _ in range(2):
        parts.append((w[..., o:o + DECAY_RANK], 128))
        o += DECAY_RANK
    for _ in range(2):
        parts.append((w[..., o:o + ICLR_RANK], 128))
        o += ICLR_RANK
    parts.append((w[..., o:o + GATE_RANK], RW_PAD - C_GD))
    return parts, o + GATE_RANK


def _prep_layer(l, depth, w_in, shift_mu, decay_up, decay_bias, iclr_up, iclr_bias, gate_up, vres_down, vres_up,
                vres_bias, k_k, k_a, r_k, gn_w, gn_b, conv_w, attn_sink, w_branch, w_out, w_router,
                w_exp_gate, w_exp_up, w_exp_down, norm_mix, norm_ffn, eblk):
    w = w_in[l]
    parts, rw_end = _rwkv_col_parts(w)
    kvw = KVH * HD
    ak = w[:, rw_end:rw_end + kvw]
    av = w[:, rw_end + kvw:rw_end + 2 * kvw]
    o = rw_end + 2 * kvw
    q = w[:, o:o + BW]
    conv = w[:, o + BW:o + 4 * BW]
    gates = w[:, o + 4 * BW:]
    w_p = jnp.concatenate([_pad_cols(parts), q, conv, gates, ak, av], axis=1).astype(BF16)
    mu_parts, _ = _rwkv_col_parts(shift_mu[l][None, :])
    pad_rank = lambda u: jnp.pad(u, ((0, 0), (0, 128 - u.shape[1]), (0, 0))).astype(BF16)
    lp = dict(
        w_in=w_p, mu=_pad_cols(mu_parts),
        decay_up=pad_rank(decay_up[l]), decay_bias=decay_bias[l].reshape(2, 1, BW),
        iclr_up=pad_rank(iclr_up[l]), iclr_bias=iclr_bias[l].reshape(2, 1, BW),
        gate_up=gate_up[l].astype(BF16), k_k=k_k[l].reshape(1, BW), k_a=k_a[l].reshape(1, BW),
        r_k=r_k[l].reshape(1, BW), gn_w=gn_w[l].reshape(1, BW), gn_b=gn_b[l].reshape(1, BW),
        conv_w=conv_w[l], sink=attn_sink[l].reshape(1, NH),
        w_branch=w_branch[l].astype(BF16), w_out=w_out[l].astype(BF16),
        w_router=jnp.pad(w_router[l], ((0, 0), (0, 128 - N_EXPERTS))),
        w_exp_gate=w_exp_gate[l].astype(BF16), w_exp_up=w_exp_up[l].astype(BF16),
        w_exp_down=w_exp_down[l].astype(BF16),
        norm_mix=norm_mix[l], norm_ffn=norm_ffn[l], eblk=eblk)
    if l > 0:
        lp["vres_down"] = jnp.pad(vres_down[l - 1], ((0, 0), (0, 128 - VRES_RANK))).astype(BF16)
        lp["vres_up"] = jnp.pad(vres_up[l - 1], ((0, 128 - VRES_RANK), (0, 0))).astype(BF16)
        lp["vres_bias"] = vres_bias[l - 1].reshape(1, BW)
    if l == depth - 1:
        lp["w_in_ctx"] = jnp.concatenate([w_p[:, :RW_PAD], w_p[:, C_AK:]], axis=1)
    return lp


def _rope_tables(seq_len):
    quarter = HD // 4
    inv = ROPE_BASE ** (-jnp.arange(quarter, dtype=F32) / quarter)
    pos = jnp.arange(seq_len)
    ang_r = (pos // GRID_W).astype(F32)[:, None] * inv[None, :]
    ang_c = (pos % GRID_W).astype(F32)[:, None] * inv[None, :]
    cos = jnp.concatenate([jnp.cos(ang_r)] * 2 + [jnp.cos(ang_c)] * 2, axis=1)
    sin = jnp.concatenate([-jnp.sin(ang_r), jnp.sin(ang_r), -jnp.sin(ang_c), jnp.sin(ang_c)], axis=1)
    return jnp.tile(cos, (1, NH)), jnp.tile(sin, (1, NH))


def kernel(x, c, ctx, c_ctx, w_mod, b_mod, norm_mix, norm_ffn, w_in, shift_mu, decay_up, decay_bias, iclr_up,
           iclr_bias, gate_up, vres_down, vres_up, vres_bias, k_k, k_a, r_k, gn_w, gn_b, conv_w, attn_sink,
           w_branch, w_out, w_router, w_exp_gate, w_exp_up, w_exp_down, norm_final):
    bsz, seq_len, _ = x.shape
    ctx_len = ctx.shape[1]
    depth = w_in.shape[0]
    mod_rows = -(-(bsz + 1) // 8) * 8
    cc = jnp.concatenate([c, c_ctx[None, :], jnp.zeros((mod_rows - bsz - 1, D), F32)], axis=0)
    mod_all = _mod_call(cc, w_mod, b_mod)

    hd_i = jnp.arange(BW) // HD
    eblk = (hd_i[:, None] == jnp.arange(128)[None, :]).astype(BF16)
    cos, sin = _rope_tables(seq_len)

    tm_l = min(1024, seq_len)
    tm_c = min(1024, bsz * ctx_len)
    tiles_per_sample = seq_len // tm_l
    row_l = lambda i: i // tiles_per_sample
    row_c = lambda i: bsz
    tm2 = 512
    row_l2 = lambda i: i // (seq_len // tm2)

    x_l = x.reshape(bsz * seq_len, D)
    x_c = ctx.reshape(bsz * ctx_len, D)
    vf_l = vf_c = None
    zero_state = jnp.zeros((2, bsz, HD, BW), F32)

    for l in range(depth):
        last = l == depth - 1
        lp = _prep_layer(l, depth, w_in, shift_mu, decay_up, decay_bias, iclr_up, iclr_bias, gate_up, vres_down,
                         vres_up, vres_bias, k_k, k_a, r_k, gn_w, gn_b, conv_w, attn_sink, w_branch, w_out,
                         w_router, w_exp_gate, w_exp_up, w_exp_down, norm_mix, norm_ffn, eblk)
        mod3 = mod_all[l].reshape(mod_rows * 6, 1, D)

        p_l = _inproj_call(x_l, lp["norm_mix"], mod3, row_l, lp["w_in"], tm_l)
        p_c = _inproj_call(x_c, lp["norm_mix"], mod3, row_c, lp["w_in_ctx"] if last else lp["w_in"], tm_c)
        kv_blk_c = (RW_PAD if last else C_AK) // (KVH * HD)

        st_c = _streams_call(p_c, ctx_len, lp, vf_c)
        st_l = _streams_call(p_l, seq_len, lp, vf_l)
        if l == 0:
            vf_c, vf_l = st_c[2], st_l[2]
        state_c, y_c = _scan_call(st_c, lp["k_a"], zero_state, bsz, ctx_len, not last)
        _, y_l = _scan_call(st_l, lp["k_a"], state_c, bsz, seq_len, True)
        br_rwkv_l = _rwkv_out_call(y_l, st_l, lp)

        br_attn_l = _attn_lat_call(p_l, p_c, kv_blk_c, cos, sin, lp["sink"], bsz, seq_len, ctx_len)
        br_conv_l = _conv_call(p_l, seq_len, lp["conv_w"])

        m_l = _merge_call(br_rwkv_l, br_conv_l, br_attn_l, p_l, lp["w_branch"])
        x_l = _outproj_call(m_l, lp["w_out"], x_l, mod3, row_l2, tm2)
        x_l = _ffn(x_l, bsz, seq_len, lp, mod3, row_l2, lambda b: b, tm2)

        if not last:
            br_rwkv_c = _rwkv_out_call(y_c, st_c, lp)
            br_attn_c = _attn_ctx_call(p_c, lp["sink"], bsz, ctx_len)
            br_conv_c = _conv_call(p_c, ctx_len, lp["conv_w"])
            m_c = _merge_call(br_rwkv_c, br_conv_c, br_attn_c, p_c, lp["w_branch"])
            x_c = _outproj_call(m_c, lp["w_out"], x_c, mod3, row_c, tm2)
            x_c = _ffn(x_c, bsz, ctx_len, lp, mod3, row_c, lambda b: bsz, tm2)

    return _final_norm_call(x_l, norm_final).reshape(bsz, seq_len, D)
```

```python
import functools

import jax
import jax.numpy as jnp
from jax import lax
from jax.experimental import pallas as pl
from jax.experimental.pallas import tpu as pltpu

F32 = jnp.float32
BF16 = jnp.bfloat16
HIGHEST = lax.Precision.HIGHEST

D = 2048
HD = 64
BW = 1024
NH = BW // HD
KVH = 4
GQ = NH // KVH
DECAY_RANK = 96
ICLR_RANK = 96
GATE_RANK = 256
VRES_RANK = 64
GN_EPS = 64e-5
NORM_EPS = 1e-6
WINDOW = 128
BLK = 128
assert WINDOW == BLK
GRID_W = 64
ROPE_BASE = 10000.0
NEG_INF = -1e30
N_EXPERTS = 16
EXPERT_FF = 2048
CAPACITY_FACTOR = 2
CHUNK = 64

C_R, C_K, C_V = 0, 1024, 2048
C_WD = (3072, 3200)
C_AD = (3328, 3456)
C_GD = 3584
RW_PAD = 4096
C_Q = 4096
C_CONV = 5120
C_GATE = 8192
C_AK = 14336
C_AV = 14592
NP = 14848
TN_IN = 512

VMEM_LIMIT = 56 * 1024 * 1024


def _cp(*sem, vmem=VMEM_LIMIT):
    return pltpu.CompilerParams(dimension_semantics=tuple(sem), vmem_limit_bytes=vmem)


def _dot(a, b):
    return jnp.dot(a, b, preferred_element_type=F32)


def _dot_nt(a, b):
    return lax.dot_general(a, b, (((1,), (1,)), ((), ())), preferred_element_type=F32)


def _dot_tn(a, b):
    return lax.dot_general(a, b, (((0,), (0,)), ((), ())), preferred_element_type=F32)


def _sigmoid(x):
    return 1.0 / (1.0 + jnp.exp(-x))


def _seg_sum(x, e_ref):
    e = e_ref[...]

    def split(z):
        hi = z.astype(BF16)
        return hi, (z - hi.astype(F32)).astype(BF16)

    hi, lo = split(x)
    hi2, lo2 = split(_dot(hi, e) + _dot(lo, e))
    return _dot_nt(hi2, e) + _dot_nt(lo2, e)


def _mod_kernel(c_ref, w_ref, b_ref, o_ref):
    c = c_ref[...]
    sc = c * _sigmoid(c)
    o_ref[0] = _dot(sc.astype(BF16), w_ref[0].astype(BF16)) + b_ref[0]


def _mod_call(cc, w_mod, b_mod):
    depth, _, n6 = w_mod.shape
    rows = cc.shape[0]
    tn = 1024
    return pl.pallas_call(
        _mod_kernel,
        grid=(depth, n6 // tn),
        in_specs=[pl.BlockSpec((rows, D), lambda l, j: (0, 0)),
                  pl.BlockSpec((1, D, tn), lambda l, j: (l, 0, j)),
                  pl.BlockSpec((1, 1, tn), lambda l, j: (l, 0, j))],
        out_specs=pl.BlockSpec((1, rows, tn), lambda l, j: (l, 0, j)),
        out_shape=jax.ShapeDtypeStruct((depth, rows, n6), F32),
        compiler_params=_cp("arbitrary", "arbitrary"),
        name="mod_proj",
    )(cc, w_mod, b_mod.reshape(depth, 1, n6))


def _norm_mod(x, gain, shift, scale):
    ms = jnp.mean(x * x, axis=-1, keepdims=True)
    y = x * lax.rsqrt(ms + NORM_EPS)
    return (y * gain) * (1.0 + scale) + shift


def _inproj_kernel(x_ref, gain_ref, shift_ref, scale_ref, w_ref, o_ref, h_ref):
    @pl.when(pl.program_id(1) == 0)
    def _():
        h_ref[...] = _norm_mod(x_ref[...], gain_ref[...], shift_ref[0], scale_ref[0]).astype(BF16)

    o_ref[...] = _dot(h_ref[...], w_ref[...]).astype(BF16)


def _inproj_call(x2, gain, mod3, mod_row, w_p, tm):
    n_tok = x2.shape[0]
    ncols = w_p.shape[1]
    return pl.pallas_call(
        _inproj_kernel,
        grid=(n_tok // tm, ncols // TN_IN),
        in_specs=[pl.BlockSpec((tm, D), lambda i, j: (i, 0)),
                  pl.BlockSpec((1, D), lambda i, j: (0, 0)),
                  pl.BlockSpec((1, 1, D), lambda i, j: (mod_row(i) * 6 + 0, 0, 0)),
                  pl.BlockSpec((1, 1, D), lambda i, j: (mod_row(i) * 6 + 1, 0, 0)),
                  pl.BlockSpec((D, TN_IN), lambda i, j: (0, j))],
        out_specs=pl.BlockSpec((tm, TN_IN), lambda i, j: (i, j)),
        out_shape=jax.ShapeDtypeStruct((n_tok, ncols), BF16),
        scratch_shapes=[pltpu.VMEM((tm, D), BF16)],
        compiler_params=_cp("arbitrary", "arbitrary"),
        name="in_proj",
    )(x2, gain.reshape(1, D), mod3, mod3, w_p)


HALO = 16


def _shifted(cur, prev_blk, next_blk, first, last):
    tt = cur.shape[0]
    row = lax.broadcasted_iota(jnp.int32, cur.shape, 0)
    p_row = jnp.where(first, 0.0, prev_blk[HALO - 1:HALO, :].astype(F32))
    n_row = jnp.where(last, 0.0, next_blk[0:1, :].astype(F32))
    prev = jnp.where(row == 0, p_row, pltpu.roll(cur, 1, axis=0))
    nxt = jnp.where(row == tt - 1, n_row, pltpu.roll(cur, tt - 1, axis=0))
    return prev, nxt


def _halo_specs(tt, width, col_blk, n_tok):
    nb = n_tok // HALO
    r = tt // HALO
    return [pl.BlockSpec((tt, width), lambda i: (i, col_blk)),
            pl.BlockSpec((HALO, width), lambda i: (jnp.maximum(i * r - 1, 0), col_blk)),
            pl.BlockSpec((HALO, width), lambda i: (jnp.minimum((i + 1) * r, nb - 1), col_blk))]


def _streams_kernel(*refs, tiles_per_seq, has_vres):
    if has_vres:
        (p_ref, pp_ref, pn_ref, mu_ref, dup_ref, dbias_ref, iup_ref, ibias_ref, gup_ref, kk_ref_, e_ref,
         vd_ref, vu_ref, vb_ref, vf_ref,
         r_o, k_o, v_o, kk_o, g_o, a_o, lw_o) = refs
    else:
        (p_ref, pp_ref, pn_ref, mu_ref, dup_ref, dbias_ref, iup_ref, ibias_ref, gup_ref, kk_ref_, e_ref,
         r_o, k_o, v_o, kk_o, g_o, a_o, lw_o) = refs
    j = pl.program_id(0) % tiles_per_seq
    cur = p_ref[...].astype(F32)
    prev, nxt = _shifted(cur, pp_ref[...], pn_ref[...], j == 0, j == tiles_per_seq - 1)
    ps = cur + mu_ref[...] * (0.5 * (prev + nxt) - cur)
    r = ps[:, C_R:C_R + BW]
    k = ps[:, C_K:C_K + BW]
    v = ps[:, C_V:C_V + BW]
    gd = ps[:, C_GD:C_GD + GATE_RANK]
    if has_vres:
        low = _dot(v.astype(BF16), vd_ref[...])
        mix = _sigmoid(vb_ref[...] + _dot(low.astype(BF16), vu_ref[...]))
        v = v + (vf_ref[...].astype(F32) - v) * mix
    for d in range(2):
        wd = ps[:, C_WD[d]:C_WD[d] + 128]
        ad = ps[:, C_AD[d]:C_AD[d] + 128]
        w_logit = dbias_ref[d] + _dot(jnp.tanh(wd).astype(BF16), dup_ref[d])
        lw_o[d] = -jnp.exp(-0.5) * _sigmoid(w_logit)
        a_o[d] = _sigmoid(ibias_ref[d] + _dot(ad.astype(BF16), iup_ref[d])).astype(BF16)
    kh = k * kk_ref_[...]
    ss = _seg_sum(kh * kh, e_ref)
    kk = kh * lax.rsqrt(jnp.maximum(ss, 1e-24))
    r_o[...] = r.astype(BF16)
    k_o[...] = k.astype(BF16)
    v_o[...] = v.astype(BF16)
    kk_o[...] = kk.astype(BF16)
    g_o[...] = _dot(_sigmoid(gd).astype(BF16), gup_ref[...]).astype(BF16)


def _streams_call(p, seq_len, lp, v_first):
    n_tok = p.shape[0]
    tt = 256
    has_vres = v_first is not None
    full = lambda *s: pl.BlockSpec(s, lambda i: (0,) * len(s))
    tok = pl.BlockSpec((tt, BW), lambda i: (i, 0))
    tok2 = pl.BlockSpec((2, tt, BW), lambda i: (0, i, 0))
    in_specs = _halo_specs(tt, RW_PAD, 0, n_tok) + [
        full(1, RW_PAD), full(2, 128, BW), full(2, 1, BW), full(2, 128, BW), full(2, 1, BW),
        full(GATE_RANK, BW), full(1, BW), full(BW, 128)]
    args = [p, p, p, lp["mu"], lp["decay_up"], lp["decay_bias"], lp["iclr_up"], lp["iclr_bias"],
            lp["gate_up"], lp["k_k"], lp["eblk"]]
    if has_vres:
        in_specs += [full(BW, 128), full(128, BW), full(1, BW), tok]
        args += [lp["vres_down"], lp["vres_up"], lp["vres_bias"], v_first]
    sd = lambda dt: jax.ShapeDtypeStruct((n_tok, BW), dt)
    sd2 = lambda dt: jax.ShapeDtypeStruct((2, n_tok, BW), dt)
    return pl.pallas_call(
        functools.partial(_streams_kernel, tiles_per_seq=seq_len // tt, has_vres=has_vres),
        grid=(n_tok // tt,),
        in_specs=in_specs,
        out_specs=[tok, tok, tok, tok, tok, tok2, tok2],
        out_shape=[sd(BF16), sd(BF16), sd(BF16), sd(BF16), sd(BF16), sd2(BF16), sd2(F32)],
        compiler_params=_cp("arbitrary"),
        name="rwkv_streams",
    )(*args)


def _scan_prologue(d, r_ref, k_ref, v_ref, kk_ref, a_ref, lw_ref, ka_ref):
    c = CHUNK
    ri = lax.broadcasted_iota(jnp.int32, (c, 2 * c), 0)
    ci = lax.broadcasted_iota(jnp.int32, (c, 2 * c), 1) % c
    diff = (ci - ri) if d else (ri - ci)
    incl = diff >= 0
    strict = diff > 0
    lw = lw_ref[0]
    tri = incl[:, :c].astype(BF16)
    lw_hi = lw.astype(BF16)
    lw_lo = (lw - lw_hi.astype(F32)).astype(BF16)
    b = _dot(tri, lw_hi) + _dot(tri, lw_lo)
    b_tot = b[0:1, :] if d else b[c - 1:c, :]
    a = a_ref[0].astype(F32)
    kk = kk_ref[...].astype(F32)
    kd = k_ref[...].astype(F32) * (1.0 + (a - 1.0) * ka_ref[...])
    kka = kk * a
    enb = jnp.exp(-b)
    etail = jnp.exp(b_tot - b)
    return dict(
        incl=incl, strict=strict,
        rt=(r_ref[...].astype(F32) * jnp.exp(b)).astype(BF16),
        bt=(-kk * jnp.exp(b - lw)).astype(BF16),
        kt=(kd * enb).astype(BF16), at=(kka * enb).astype(BF16),
        kh=(kd * etail).astype(BF16), ah=(kka * etail).astype(BF16),
        vv=v_ref[...], e_tot=jnp.exp(b_tot))


def _scan_kernel(*refs, with_output):
    ins = refs[:14]
    if with_output:
        yf_ref, yb_ref, sout_ref, st_ref = refs[14:]
    else:
        sout_ref, st_ref = refs[14:]
    ka_ref, s0_ref = ins[12], ins[13]
    s = pl.program_id(1)
    c = CHUNK

    @pl.when(s == 0)
    def _():
        st_ref[...] = s0_ref[:, 0]

    pro = [_scan_prologue(d, *ins[6 * d:6 * d + 6], ka_ref) for d in range(2)]
    st_all = [st_ref[d] for d in range(2)]

    npair = NH // 2
    units = [(d, slice(p * 2 * HD, (p + 1) * 2 * HD)) for d in range(2) for p in range(npair)]
    us = range(len(units))
    col = lambda name: [pro[d][name][:, sl] for d, sl in units]
    even1 = lax.broadcasted_iota(jnp.int32, (c, 2 * HD), 1) < HD

    def bd(z):
        zero = jnp.zeros_like(z)
        return jnp.concatenate([jnp.where(even1, z, zero), jnp.where(even1, zero, z)], axis=0)

    strict = [pro[d]["strict"] for d, _ in units]
    incl = [pro[d]["incl"] for d, _ in units]
    st = [st_all[d][:, sl] for d, sl in units]
    vbd = [bd(z) for z in col("vv")]
    rb = [jnp.concatenate(p, axis=0) for p in zip(col("rt"), col("bt"))]
    gk = [_dot_nt(rb[u], bd(z)) for u, z in zip(us, col("kt"))]
    ga = [_dot_nt(rb[u], bd(z)) for u, z in zip(us, col("at"))]
    rbs = [_dot_nt(rb[u], bd(st[u].astype(BF16))) for u in us]
    x = [jnp.where(strict[u], ga[u][c:], 0.0).astype(BF16) for u in us]
    a_bk = [jnp.where(strict[u], gk[u][c:], 0.0) for u in us]
    if with_output:
        a_rk = [jnp.where(incl[u], gk[u][:c], 0.0) for u in us]
        a_ra = [jnp.where(incl[u], ga[u][:c], 0.0).astype(BF16) for u in us]
        av = [_dot(jnp.concatenate([a_rk[u], a_bk[u]], axis=0).astype(BF16), vbd[u]) for u in us]
        w = [rbs[u][c:] + av[u][c:] for u in us]
    else:
        w = [rbs[u][c:] + _dot(a_bk[u].astype(BF16), vbd[u]) for u in us]
    for rnd in range(6):
        if rnd < 5:
            m = [_dot(x[u], jnp.concatenate([bd(x[u]), bd(w[u].astype(BF16))], axis=1)) for u in us]
            x = [m[u][:, :2 * c].astype(BF16) for u in us]
            w = [w[u] + m[u][:, 2 * c:] for u in us]
        else:
            w = [w[u] + _dot(x[u], bd(w[u].astype(BF16))) for u in us]
    ub = [w[u].astype(BF16) for u in us]
    if with_output:
        ys = [rbs[u][:c] + av[u][:c] + _dot(a_ra[u], bd(ub[u])) for u in us]
        yf_ref[...] = jnp.concatenate(ys[:npair], axis=1)
        yb_ref[...] = jnp.concatenate(ys[npair:], axis=1)
    full = [_dot_tn(jnp.concatenate([z, ub[u]], axis=0), jnp.concatenate(p, axis=0))
            for u, z, p in zip(us, col("vv"), zip(col("kh"), col("ah")))]
    upd = [jnp.where(even1, f[:HD], f[HD:]) for f in full]
    for d in range(2):
        st_ref[d] = st_all[d] * pro[d]["e_tot"] + jnp.concatenate(upd[d * npair:(d + 1) * npair], axis=1)

    @pl.when(s == pl.num_programs(1) - 1)
    def _():
        sout_ref[:, 0] = st_ref[...]


def _scan_call(streams, k_a, s0, bsz, seq_len, with_output):
    r, k, v, kk, _, a2, lw2 = streams
    n_tok = r.shape[0]
    nc = seq_len // CHUNK

    rows = (lambda b, s: b * nc + s, lambda b, s: b * nc + nc - 1 - s)
    tok = [pl.BlockSpec((CHUNK, BW), lambda b, s, f=f: (f(b, s), 0)) for f in rows]
    tok2 = [pl.BlockSpec((1, CHUNK, BW), lambda b, s, f=f, d=d: (d, f(b, s), 0)) for d, f in enumerate(rows)]
    st_spec = pl.BlockSpec((2, 1, HD, BW), lambda b, s: (0, b, 0, 0))
    in_specs, args = [], []
    for d in range(2):
        in_specs += [tok[d]] * 4 + [tok2[d]] * 2
        args += [r, k, v, kk, a2, lw2]
    in_specs += [pl.BlockSpec((1, BW), lambda b, s: (0, 0)), st_spec]
    out_specs = [st_spec]
    out_shape = [jax.ShapeDtypeStruct((2, bsz, HD, BW), F32)]
    if with_output:
        out_specs = tok + out_specs
        out_shape = [jax.ShapeDtypeStruct((n_tok, BW), F32)] * 2 + out_shape
    res = pl.pallas_call(
        functools.partial(_scan_kernel, with_output=with_output),
        grid=(bsz, nc),
        in_specs=in_specs,
        out_specs=out_specs,
        out_shape=out_shape,
        scratch_shapes=[pltpu.VMEM((2, HD, BW), F32)],
        compiler_params=_cp("arbitrary", "arbitrary"),
        name="rwkv_scan",
    )(*args, k_a, s0)
    if with_output:
        return res[2], (res[0], res[1])
    return res[0], None


def _rwkv_out_kernel(yf_ref, yb_ref, r_ref, k_ref, v_ref, a_ref, g_ref, ka_ref, rk_ref, gnw_ref, gnb_ref, e_ref,
                     o_ref):
    y = yf_ref[...] + yb_ref[...]
    mean = _seg_sum(y, e_ref) * (1.0 / HD)
    yc = y - mean
    var = _seg_sum(yc * yc, e_ref) * (1.0 / HD)
    yn = yc * lax.rsqrt(var + GN_EPS) * gnw_ref[...] + gnb_ref[...]
    r = r_ref[...].astype(F32)
    k = k_ref[...].astype(F32)
    asum = a_ref[0].astype(F32) + a_ref[1].astype(F32)
    kd_sum = k * (2.0 + (asum - 2.0) * ka_ref[...])
    bonus = _seg_sum(r * kd_sum * rk_ref[...], e_ref) * v_ref[...].astype(F32)
    o_ref[...] = ((yn + bonus) * g_ref[...].astype(F32)).astype(BF16)


def _rwkv_out_call(y2, streams, lp):
    r, k, v, _, g, a2, _ = streams
    n_tok = r.shape[0]
    tt = 256
    tok = pl.BlockSpec((tt, BW), lambda i: (i, 0))
    tok2 = pl.BlockSpec((2, tt, BW), lambda i: (0, i, 0))
    vec = pl.BlockSpec((1, BW), lambda i: (0, 0))
    return pl.pallas_call(
        _rwkv_out_kernel,
        grid=(n_tok // tt,),
        in_specs=[tok, tok, tok, tok, tok, tok2, tok, vec, vec, vec, vec, pl.BlockSpec((BW, 128), lambda i: (0, 0))],
        out_specs=tok,
        out_shape=jax.ShapeDtypeStruct((n_tok, BW), BF16),
        compiler_params=_cp("arbitrary"),
        name="rwkv_out",
    )(y2[0], y2[1], r, k, v, a2, g, lp["k_a"], lp["r_k"], lp["gn_w"], lp["gn_b"], lp["eblk"])


def _conv_kernel(b_ref, c_ref, cp_ref, cn_ref, u_ref, up_ref, un_ref, w_ref, o_ref, *, tiles_per_seq):
    j = pl.program_id(0) % tiles_per_seq
    cu = c_ref[...].astype(F32) * u_ref[...].astype(F32)
    cu_p = cp_ref[...].astype(F32) * up_ref[...].astype(F32)
    cu_n = cn_ref[...].astype(F32) * un_ref[...].astype(F32)
    prev, nxt = _shifted(cu, cu_p, cu_n, j == 0, j == tiles_per_seq - 1)
    w = w_ref[...]
    conv = w[0:1] * prev + w[1:2] * cu + w[2:3] * nxt
    o_ref[...] = (b_ref[...].astype(F32) * conv).astype(BF16)


def _conv_call(p, seq_len, conv_w):
    n_tok = p.shape[0]
    tt = 256
    cb = C_CONV // BW
    return pl.pallas_call(
        functools.partial(_conv_kernel, tiles_per_seq=seq_len // tt),
        grid=(n_tok // tt,),
        in_specs=[pl.BlockSpec((tt, BW), lambda i: (i, cb))] + _halo_specs(tt, BW, cb + 1, n_tok)
        + _halo_specs(tt, BW, cb + 2, n_tok) + [pl.BlockSpec((3, BW), lambda i: (0, 0))],
        out_specs=pl.BlockSpec((tt, BW), lambda i: (i, 0)),
        out_shape=jax.ShapeDtypeStruct((n_tok, BW), BF16),
        compiler_params=_cp("arbitrary"),
        name="short_conv",
    )(p, p, p, p, p, p, p, conv_w)


def _rope(x, cos, sin):
    w = x.shape[1]
    lane = lax.broadcasted_iota(jnp.int32, x.shape, 1)
    partner = jnp.where((lane % 32) < 16, pltpu.roll(x, w - 16, axis=1), pltpu.roll(x, 16, axis=1))
    return x * cos + partner * sin


LOG2E = 1.4426950408889634
Q_SCALE = HD ** -0.5 * LOG2E


def _softmax_pv(s2, sink2_col, v):
    m = jnp.maximum(jnp.max(s2, axis=-1, keepdims=True), sink2_col)
    p = jnp.exp2(s2 - m)
    den = jnp.sum(p, axis=-1, keepdims=True) + jnp.exp2(sink2_col - m)
    return _dot(p.astype(BF16), v) / den


def _attn_lat_kernel(q_ref, kp_ref, kc_ref, kn_ref, vp_ref, vc_ref, vn_ref, kx_ref, vx_ref,
                     cos_ref, cosp_ref, cosn_ref, sin_ref, sinp_ref, sinn_ref, sink_ref, o_ref, *, nblk):
    n = pl.program_id(1)
    kvw = KVH * HD
    q = (_rope(q_ref[...].astype(F32), cos_ref[...], sin_ref[...]) * Q_SCALE).astype(BF16)
    kb = jnp.concatenate([
        _rope(kp_ref[...].astype(F32), cosp_ref[:, :kvw], sinp_ref[:, :kvw]),
        _rope(kc_ref[...].astype(F32), cos_ref[:, :kvw], sin_ref[:, :kvw]),
        _rope(kn_ref[...].astype(F32), cosn_ref[:, :kvw], sinn_ref[:, :kvw])], axis=0).astype(BF16)
    k_all = jnp.concatenate([kb, kx_ref[...]], axis=0)
    v_all = jnp.concatenate([vp_ref[...], vc_ref[...], vn_ref[...], vx_ref[...]], axis=0)
    rows = GQ * BLK
    qpos = lax.broadcasted_iota(jnp.int32, (rows, BLK), 0) % BLK
    kidx = lax.broadcasted_iota(jnp.int32, (rows, BLK), 1)
    bias_p = jnp.where((kidx >= qpos) & (n > 0), 0.0, NEG_INF)
    bias_n = jnp.where((kidx <= qpos) & (n < nblk - 1), 0.0, NEG_INF)
    sink = sink_ref[...] * LOG2E
    gs = range(KVH)
    qg = [jnp.concatenate([q[:, (gi * GQ + t) * HD:(gi * GQ + t + 1) * HD] for t in range(GQ)], axis=0)
          for gi in gs]
    sk = [jnp.concatenate([jnp.broadcast_to(sink[:, gi * GQ + t:gi * GQ + t + 1], (BLK, 1))
                           for t in range(GQ)], axis=0) for gi in gs]
    s = [_dot_nt(qg[gi], k_all[:, gi * HD:(gi + 1) * HD]) for gi in gs]
    s = [jnp.concatenate([z[:, :BLK] + bias_p, z[:, BLK:2 * BLK], z[:, 2 * BLK:3 * BLK] + bias_n,
                          z[:, 3 * BLK:]], axis=1) for z in s]
    m = [jnp.maximum(jnp.max(s[gi], axis=-1, keepdims=True), sk[gi]) for gi in gs]
    p = [jnp.exp2(s[gi] - m[gi]) for gi in gs]
    den = [jnp.sum(p[gi], axis=-1, keepdims=True) + jnp.exp2(sk[gi] - m[gi]) for gi in gs]
    og = [_dot(p[gi].astype(BF16), v_all[:, gi * HD:(gi + 1) * HD]) / den[gi] for gi in gs]
    o_ref[...] = jnp.concatenate([og[gi][t * BLK:(t + 1) * BLK] for gi in gs for t in range(GQ)],
                                 axis=1).astype(BF16)


def _attn_lat_call(p_l, p_c, kv_blk_c, cos, sin, sink, bsz, seq_len, ctx_len):
    nblk = seq_len // BLK
    kvw = KVH * HD
    kb, vb = C_AK // kvw, C_AV // kvw
    rowq = lambda b, n: b * nblk + n
    rowp = lambda b, n: b * nblk + jnp.maximum(n - 1, 0)
    rown = lambda b, n: b * nblk + jnp.minimum(n + 1, nblk - 1)
    tabp = lambda b, n: (jnp.maximum(n - 1, 0), 0)
    tabn = lambda b, n: (jnp.minimum(n + 1, nblk - 1), 0)
    kv = lambda rf, cb: pl.BlockSpec((BLK, kvw), lambda b, n: (rf(b, n), cb))
    tab = lambda f: pl.BlockSpec((BLK, BW), f)
    return pl.pallas_call(
        functools.partial(_attn_lat_kernel, nblk=nblk),
        grid=(bsz, nblk),
        in_specs=[pl.BlockSpec((BLK, BW), lambda b, n: (rowq(b, n), C_Q // BW)),
                  kv(rowp, kb), kv(rowq, kb), kv(rown, kb), kv(rowp, vb), kv(rowq, vb), kv(rown, vb),
                  pl.BlockSpec((ctx_len, kvw), lambda b, n: (b, kv_blk_c)),
                  pl.BlockSpec((ctx_len, kvw), lambda b, n: (b, kv_blk_c + 1)),
                  tab(lambda b, n: (n, 0)), tab(tabp), tab(tabn),
                  tab(lambda b, n: (n, 0)), tab(tabp), tab(tabn),
                  pl.BlockSpec((1, NH), lambda b, n: (0, 0))],
        out_specs=pl.BlockSpec((BLK, BW), lambda b, n: (rowq(b, n), 0)),
        out_shape=jax.ShapeDtypeStruct((bsz * seq_len, BW), BF16),
        compiler_params=_cp("arbitrary", "arbitrary"),
        name="attn_latent",
    )(p_l, p_l, p_l, p_l, p_l, p_l, p_l, p_c, p_c, cos, cos, cos, sin, sin, sin, sink)


def _attn_ctx_kernel(q_ref, kx_ref, vx_ref, sink_ref, o_ref):
    q = (q_ref[...].astype(F32) * Q_SCALE).astype(BF16)
    kx = kx_ref[...]
    vx = vx_ref[...]
    sink = sink_ref[...] * LOG2E
    tq = q.shape[0]
    outs = []
    for gi in range(KVH):
        qg = jnp.concatenate([q[:, (gi * GQ + t) * HD:(gi * GQ + t + 1) * HD] for t in range(GQ)], axis=0)
        sk = jnp.concatenate([jnp.broadcast_to(sink[:, gi * GQ + t:gi * GQ + t + 1], (tq, 1))
                              for t in range(GQ)], axis=0)
        s = _dot_nt(qg, kx[:, gi * HD:(gi + 1) * HD])
        og = _softmax_pv(s, sk, vx[:, gi * HD:(gi + 1) * HD])
        outs += [og[t * tq:(t + 1) * tq] for t in range(GQ)]
    o_ref[...] = jnp.concatenate(outs, axis=1).astype(BF16)


def _attn_ctx_call(p_c, sink, bsz, ctx_len):
    kvw = KVH * HD
    tq = 128
    nq = ctx_len // tq
    return pl.pallas_call(
        _attn_ctx_kernel,
        grid=(bsz, nq),
        in_specs=[pl.BlockSpec((tq, BW), lambda b, n: (b * nq + n, C_Q // BW)),
                  pl.BlockSpec((ctx_len, kvw), lambda b, n: (b, C_AK // kvw)),
                  pl.BlockSpec((ctx_len, kvw), lambda b, n: (b, C_AV // kvw)),
                  pl.BlockSpec((1, NH), lambda b, n: (0, 0))],
        out_specs=pl.BlockSpec((tq, BW), lambda b, n: (b * nq + n, 0)),
        out_shape=jax.ShapeDtypeStruct((bsz * ctx_len, BW), BF16),
        compiler_params=_cp("arbitrary", "arbitrary"),
        name="attn_context",
    )(p_c, p_c, p_c, sink)


def _merge_kernel(b0_ref, b1_ref, b2_ref, g0_ref, g1_ref, g2_ref, w_ref, o_ref):
    acc = _sigmoid(g0_ref[...].astype(F32)) * _dot(b0_ref[...], w_ref[0])
    acc += _sigmoid(g1_ref[...].astype(F32)) * _dot(b1_ref[...], w_ref[1])
    acc += _sigmoid(g2_ref[...].astype(F32)) * _dot(b2_ref[...], w_ref[2])
    o_ref[...] = acc.astype(BF16)


def _merge_call(br_rwkv, br_conv, br_attn, p, w_branch):
    n_tok = p.shape[0]
    tm, tn = 512, 1024
    nn = D // tn
    br = pl.BlockSpec((tm, BW), lambda j, i: (i, 0))
    gate = lambda t: pl.BlockSpec((tm, tn), lambda j, i: (i, (C_GATE + t * D) // tn + j))
    return pl.pallas_call(
        _merge_kernel,
        grid=(nn, n_tok // tm),
        in_specs=[br, br, br, gate(0), gate(1), gate(2), pl.BlockSpec((3, BW, tn), lambda j, i: (0, 0, j))],
        out_specs=pl.BlockSpec((tm, tn), lambda j, i: (i, j)),
        out_shape=jax.ShapeDtypeStruct((n_tok, D), BF16),
        compiler_params=_cp("arbitrary", "arbitrary"),
        name="merge_branches",
    )(br_rwkv, br_conv, br_attn, p, p, p, w_branch)


def _outproj_kernel(m_ref, w_ref, x_ref, gate_ref, o_ref):
    o_ref[...] = x_ref[...] + gate_ref[0] * _dot(m_ref[...], w_ref[...])


def _outproj_call(m, w_out, x2, mod3, mod_row, tm):
    n_tok = x2.shape[0]
    return pl.pallas_call(
        _outproj_kernel,
        grid=(n_tok // tm,),
        in_specs=[pl.BlockSpec((tm, D), lambda i: (i, 0)),
                  pl.BlockSpec((D, D), lambda i: (0, 0)),
                  pl.BlockSpec((tm, D), lambda i: (i, 0)),
                  pl.BlockSpec((1, 1, D), lambda i: (mod_row(i) * 6 + 2, 0, 0))],
        out_specs=pl.BlockSpec((tm, D), lambda i: (i, 0)),
        out_shape=jax.ShapeDtypeStruct((n_tok, D), F32),
        compiler_params=_cp("arbitrary"),
        name="out_proj",
    )(m, w_out, x2, mod3)


def _ffn_prep_kernel(x_ref, gain_ref, shift_ref, scale_ref, wr_ref, h_ref, aff_ref):
    h = _norm_mod(x_ref[...], gain_ref[...], shift_ref[0], scale_ref[0])
    h_ref[...] = h.astype(BF16)
    logits = jnp.dot(h, wr_ref[...], precision=HIGHEST, preferred_element_type=F32)
    lane = lax.broadcasted_iota(jnp.int32, logits.shape, 1)
    logits = jnp.where(lane < N_EXPERTS, logits, NEG_INF)
    m = jnp.max(logits, axis=-1, keepdims=True)
    e = jnp.exp(logits - m)
    aff_ref[...] = e / jnp.sum(e, axis=-1, keepdims=True)


def _ffn_prep_call(x2, gain, mod3, mod_row, wr_pad, tm):
    n_tok = x2.shape[0]
    return pl.pallas_call(
        _ffn_prep_kernel,
        grid=(n_tok // tm,),
        in_specs=[pl.BlockSpec((tm, D), lambda i: (i, 0)),
                  pl.BlockSpec((1, D), lambda i: (0, 0)),
                  pl.BlockSpec((1, 1, D), lambda i: (mod_row(i) * 6 + 3, 0, 0)),
                  pl.BlockSpec((1, 1, D), lambda i: (mod_row(i) * 6 + 4, 0, 0)),
                  pl.BlockSpec((D, 128), lambda i: (0, 0))],
        out_specs=[pl.BlockSpec((tm, D), lambda i: (i, 0)), pl.BlockSpec((tm, 128), lambda i: (i, 0))],
        out_shape=[jax.ShapeDtypeStruct((n_tok, D), BF16), jax.ShapeDtypeStruct((n_tok, 128), F32)],
        compiler_params=_cp("arbitrary"),
        name="ffn_prep",
    )(x2, gain.reshape(1, D), mod3, mod3, wr_pad)


def _select_kernel(aff_ref, slot_ref, *, cap):
    a = aff_ref[0]
    n = a.shape[1]
    bits = lax.bitcast_convert_type(a, jnp.int32)

    def body(i, t):
        cand = t | jnp.left_shift(jnp.int32(1), 30 - i)
        cnt = jnp.sum((bits >= cand).astype(jnp.int32), axis=-1, keepdims=True)
        return jnp.where(cnt >= cap, cand, t)

    thr = lax.fori_loop(0, 31, body, jnp.zeros((a.shape[0], 1), jnp.int32))
    gt = bits > thr
    eq = bits == thr
    n_gt = jnp.sum(gt.astype(jnp.int32), axis=-1, keepdims=True)
    tc = min(n, 512)

    def prefix(mask_bf16):
        cols = []
        for j0 in range(0, n, tc):
            ri = lax.broadcasted_iota(jnp.int32, (n, tc), 0)
            ci = lax.broadcasted_iota(jnp.int32, (n, tc), 1) + j0
            cols.append(_dot(mask_bf16, (ri <= ci).astype(BF16)))
        return jnp.concatenate(cols, axis=1)

    eq_f = eq.astype(BF16)
    excl_eq = prefix(eq_f) - eq_f.astype(F32)
    sel = gt | (eq & (excl_eq < (cap - n_gt).astype(F32)))
    pos = prefix(sel.astype(BF16))
    slot_ref[0] = jnp.where(sel, pos.astype(jnp.int32) - 1, -1)


def _select_call(aff_t, cap):
    bsz, ne, n = aff_t.shape
    return pl.pallas_call(
        functools.partial(_select_kernel, cap=cap),
        grid=(bsz,),
        in_specs=[pl.BlockSpec((1, ne, n), lambda b: (b, 0, 0))],
        out_specs=pl.BlockSpec((1, ne, n), lambda b: (b, 0, 0)),
        out_shape=jax.ShapeDtypeStruct((bsz, ne, n), jnp.int32),
        compiler_params=_cp("arbitrary"),
        name="expert_select",
    )(aff_t)


def _gather_kernel(slot_ref, aff_ref, h_ref, xs_ref, g_ref, *, cap):
    slot = slot_ref[0]
    n = slot.shape[1]
    onehot = lax.broadcasted_iota(jnp.int32, (cap, n), 0) == slot
    xs_ref[0] = _dot(onehot.astype(BF16), h_ref[...]).astype(BF16)
    g_ref[0] = jnp.sum(jnp.where(onehot, aff_ref[0], 0.0), axis=-1, keepdims=True)


def _gather_call(slot, aff_t, h2, cap):
    bsz, ne, n = slot.shape
    return pl.pallas_call(
        functools.partial(_gather_kernel, cap=cap),
        grid=(bsz, ne),
        in_specs=[pl.BlockSpec((1, 1, n), lambda b, e: (b * ne + e, 0, 0)),
                  pl.BlockSpec((1, 1, n), lambda b, e: (b * ne + e, 0, 0)),
                  pl.BlockSpec((n, D), lambda b, e: (b, 0))],
        out_specs=[pl.BlockSpec((1, cap, D), lambda b, e: (e, b, 0)),
                   pl.BlockSpec((1, cap, 1), lambda b, e: (e, b, 0))],
        out_shape=[jax.ShapeDtypeStruct((ne, bsz * cap, D), BF16),
                   jax.ShapeDtypeStruct((ne, bsz * cap, 1), F32)],
        compiler_params=_cp("arbitrary", "arbitrary"),
        name="expert_gather",
    )(slot.reshape(bsz * ne, 1, n), aff_t.reshape(bsz * ne, 1, n), h2)


def _expert_kernel(x_ref, wg_ref, wu_ref, wd_ref, g_ref, o_ref, acc_ref):
    f = pl.program_id(2)

    @pl.when(f == 0)
    def _():
        acc_ref[...] = jnp.zeros_like(acc_ref)

    x = x_ref[0]
    gate = _dot(x, wg_ref[0].astype(BF16))
    hid = gate * _sigmoid(gate) * _dot(x, wu_ref[0].astype(BF16))
    acc_ref[...] += _dot(hid.astype(BF16), wd_ref[0].astype(BF16))

    @pl.when(f == pl.num_programs(2) - 1)
    def _():
        o_ref[0] = (acc_ref[...] * g_ref[0]).astype(BF16)


def _expert_call(xs, g, wg, wu, wd):
    ne, rows, _ = xs.shape
    tm = min(rows, 1024)
    tf = 256
    return pl.pallas_call(
        _expert_kernel,
        grid=(ne, rows // tm, EXPERT_FF // tf),
        in_specs=[pl.BlockSpec((1, tm, D), lambda e, i, f: (e, i, 0)),
                  pl.BlockSpec((1, D, tf), lambda e, i, f: (e, 0, f)),
                  pl.BlockSpec((1, D, tf), lambda e, i, f: (e, 0, f)),
                  pl.BlockSpec((1, tf, D), lambda e, i, f: (e, f, 0)),
                  pl.BlockSpec((1, tm, 1), lambda e, i, f: (e, i, 0))],
        out_specs=pl.BlockSpec((1, tm, D), lambda e, i, f: (e, i, 0)),
        out_shape=jax.ShapeDtypeStruct((ne, rows, D), BF16),
        scratch_shapes=[pltpu.VMEM((tm, D), F32)],
        compiler_params=_cp("arbitrary", "arbitrary", "arbitrary"),
        name="expert_mlp",
    )(xs, wg, wu, wd, g)


def _scatter_kernel(slot_ref, ys_ref, x_ref, gate_ref, o_ref, *, cap):
    slot_t = slot_ref[0]
    n = slot_t.shape[0]
    lane = lax.broadcasted_iota(jnp.int32, (n, cap), 1)
    acc = jnp.zeros(x_ref.shape, F32)
    for e in range(N_EXPERTS):
        onehot = (slot_t[:, e:e + 1] == lane).astype(BF16)
        acc += _dot(onehot, ys_ref[e])
    o_ref[...] = x_ref[...] + gate_ref[0] * acc


def _scatter_call(slot_t, ys, x2, mod3, mod_row_b, cap):
    bsz, n, ne = slot_t.shape
    tn = 512
    return pl.pallas_call(
        functools.partial(_scatter_kernel, cap=cap),
        grid=(bsz, D // tn),
        in_specs=[pl.BlockSpec((1, n, ne), lambda b, j: (b, 0, 0)),
                  pl.BlockSpec((ne, cap, tn), lambda b, j: (0, b, j)),
                  pl.BlockSpec((n, tn), lambda b, j: (b, j)),
                  pl.BlockSpec((1, 1, tn), lambda b, j: (mod_row_b(b) * 6 + 5, 0, j))],
        out_specs=pl.BlockSpec((n, tn), lambda b, j: (b, j)),
        out_shape=jax.ShapeDtypeStruct(x2.shape, F32),
        compiler_params=_cp("arbitrary", "arbitrary"),
        name="expert_scatter",
    )(slot_t, ys, x2, mod3)


def _ffn(x2, bsz, n, lp, mod3, mod_row, mod_row_b, tm):
    cap = CAPACITY_FACTOR * n // N_EXPERTS
    h2, aff = _ffn_prep_call(x2, lp["norm_ffn"], mod3, mod_row, lp["w_router"], tm)
    aff_t = jnp.swapaxes(aff[:, :N_EXPERTS].reshape(bsz, n, N_EXPERTS), 1, 2)
    slot = _select_call(aff_t, cap)
    xs, g = _gather_call(slot, aff_t, h2, cap)
    ys = _expert_call(xs, g, lp["w_exp_gate"], lp["w_exp_up"], lp["w_exp_down"])
    return _scatter_call(jnp.swapaxes(slot, 1, 2), ys, x2, mod3, mod_row_b, cap)


def _final_norm_kernel(x_ref, gain_ref, o_ref):
    x = x_ref[...]
    ms = jnp.mean(x * x, axis=-1, keepdims=True)
    o_ref[...] = (x * lax.rsqrt(ms + NORM_EPS)) * gain_ref[...]


def _final_norm_call(x2, gain):
    n_tok = x2.shape[0]
    tm = 512
    return pl.pallas_call(
        _final_norm_kernel,
        grid=(n_tok // tm,),
        in_specs=[pl.BlockSpec((tm, D), lambda i: (i, 0)), pl.BlockSpec((1, D), lambda i: (0, 0))],
        out_specs=pl.BlockSpec((tm, D), lambda i: (i, 0)),
        out_shape=jax.ShapeDtypeStruct(x2.shape, F32),
        compiler_params=_cp("arbitrary"),
        name="final_norm",
    )(x2, gain.reshape(1, D))


def _pad_cols(parts):
    out = []
    for a, width in parts:
        out.append(a)
        if a.shape[-1] < width:
            out.append(jnp.zeros(a.shape[:-1] + (width - a.shape[-1],), a.dtype))
    return jnp.concatenate(out, axis=-1)


def _rwkv_col_parts(w):
    o = 3 * BW
    parts = [(w[..., :o], o)]
    for _ in range(2):
        parts.append((w[..., o:o + DECAY_RANK], 128))
        o += DECAY_RANK
    for _ in range(2):
        parts.append((w[..., o:o + ICLR_RANK], 128))
        o += ICLR_RANK
    parts.append((w[..., o:o + GATE_RANK], RW_PAD - C_GD))
    return parts, o + GATE_RANK


def _prep_layer(l, depth, w_in, shift_mu, decay_up, decay_bias, iclr_up, iclr_bias, gate_up, vres_down, vres_up,
                vres_bias, k_k, k_a, r_k, gn_w, gn_b, conv_w, attn_sink, w_branch, w_out, w_router,
                w_exp_gate, w_exp_up, w_exp_down, norm_mix, norm_ffn, eblk):
    w = w_in[l].astype(BF16)
    parts, rw_end = _rwkv_col_parts(w)
    kvw = KVH * HD
    ak = w[:, rw_end:rw_end + kvw]
    av = w[:, rw_end + kvw:rw_end + 2 * kvw]
    o = rw_end + 2 * kvw
    q = w[:, o:o + BW]
    conv = w[:, o + BW:o + 4 * BW]
    gates = w[:, o + 4 * BW:]
    w_p = jnp.concatenate([_pad_cols(parts), q, conv, gates, ak, av], axis=1)
    mu_parts, _ = _rwkv_col_parts(shift_mu[l][None, :])
    pad_rank = lambda u: jnp.pad(u, ((0, 0), (0, 128 - u.shape[1]), (0, 0))).astype(BF16)
    lp = dict(
        w_in=w_p, mu=_pad_cols(mu_parts),
        decay_up=pad_rank(decay_up[l]), decay_bias=decay_bias[l].reshape(2, 1, BW),
        iclr_up=pad_rank(iclr_up[l]), iclr_bias=iclr_bias[l].reshape(2, 1, BW),
        gate_up=gate_up[l].astype(BF16), k_k=k_k[l].reshape(1, BW), k_a=k_a[l].reshape(1, BW),
        r_k=r_k[l].reshape(1, BW), gn_w=gn_w[l].reshape(1, BW), gn_b=gn_b[l].reshape(1, BW),
        conv_w=conv_w[l], sink=attn_sink[l].reshape(1, NH),
        w_branch=w_branch[l].astype(BF16), w_out=w_out[l].astype(BF16),
        w_router=jnp.pad(w_router[l], ((0, 0), (0, 128 - N_EXPERTS))),
        w_exp_gate=w_exp_gate[l], w_exp_up=w_exp_up[l], w_exp_down=w_exp_down[l],
        norm_mix=norm_mix[l], norm_ffn=norm_ffn[l], eblk=eblk)
    if l > 0:
        lp["vres_down"] = jnp.pad(vres_down[l - 1], ((0, 0), (0, 128 - VRES_RANK))).astype(BF16)
        lp["vres_up"] = jnp.pad(vres_up[l - 1], ((0, 128 - VRES_RANK), (0, 0))).astype(BF16)
        lp["vres_bias"] = vres_bias[l - 1].reshape(1, BW)
    if l == depth - 1:
        lp["w_in_ctx"] = jnp.concatenate([w_p[:, :RW_PAD], w_p[:, C_AK:]], axis=1)
    return lp


def _rope_tables(seq_len):
    quarter = HD // 4
    inv = ROPE_BASE ** (-jnp.arange(quarter, dtype=F32) / quarter)
    pos = jnp.arange(seq_len)
    ang_r = (pos // GRID_W).astype(F32)[:, None] * inv[None, :]
    ang_c = (pos % GRID_W).astype(F32)[:, None] * inv[None, :]
    cos = jnp.concatenate([jnp.cos(ang_r)] * 2 + [jnp.cos(ang_c)] * 2, axis=1)
    sin = jnp.concatenate([-jnp.sin(ang_r), jnp.sin(ang_r), -jnp.sin(ang_c), jnp.sin(ang_c)], axis=1)
    return jnp.tile(cos, (1, NH)), jnp.tile(sin, (1, NH))


def kernel(x, c, ctx, c_ctx, w_mod, b_mod, norm_mix, norm_ffn, w_in, shift_mu, decay_up, decay_bias, iclr_up,
           iclr_bias, gate_up, vres_down, vres_up, vres_bias, k_k, k_a, r_k, gn_w, gn_b, conv_w, attn_sink,
           w_branch, w_out, w_router, w_exp_gate, w_exp_up, w_exp_down, norm_final):
    bsz, seq_len, _ = x.shape
    ctx_len = ctx.shape[1]
    depth = w_in.shape[0]
    mod_rows = -(-(bsz + 1) // 8) * 8
    cc = jnp.concatenate([c, c_ctx[None, :], jnp.zeros((mod_rows - bsz - 1, D), F32)], axis=0)
    mod_all = _mod_call(cc, w_mod, b_mod)

    hd_i = jnp.arange(BW) // HD
    eblk = (hd_i[:, None] == jnp.arange(128)[None, :]).astype(BF16)
    cos, sin = _rope_tables(seq_len)

    tm_l = min(1024, seq_len)
    tm_c = min(1024, bsz * ctx_len)
    tiles_per_sample = seq_len // tm_l
    row_l = lambda i: i // tiles_per_sample
    row_c = lambda i: bsz
    tm2 = 512
    row_l2 = lambda i: i // (seq_len // tm2)

    x_l = x.reshape(bsz * seq_len, D)
    x_c = ctx.reshape(bsz * ctx_len, D)
    vf_l = vf_c = None
    zero_state = jnp.zeros((2, bsz, HD, BW), F32)

    for l in range(depth):
        last = l == depth - 1
        lp = _prep_layer(l, depth, w_in, shift_mu, decay_up, decay_bias, iclr_up, iclr_bias, gate_up, vres_down,
                         vres_up, vres_bias, k_k, k_a, r_k, gn_w, gn_b, conv_w, attn_sink, w_branch, w_out,
                         w_router, w_exp_gate, w_exp_up, w_exp_down, norm_mix, norm_ffn, eblk)
        mod3 = mod_all[l].reshape(mod_rows * 6, 1, D)

        p_l = _inproj_call(x_l, lp["norm_mix"], mod3, row_l, lp["w_in"], tm_l)
        p_c = _inproj_call(x_c, lp["norm_mix"], mod3, row_c, lp["w_in_ctx"] if last else lp["w_in"], tm_c)
        kv_blk_c = (RW_PAD if last else C_AK) // (KVH * HD)

        st_c = _streams_call(p_c, ctx_len, lp, vf_c)
        st_l = _streams_call(p_l, seq_len, lp, vf_l)
        if l == 0:
            vf_c, vf_l = st_c[2], st_l[2]
        state_c, y_c = _scan_call(st_c, lp["k_a"], zero_state, bsz, ctx_len, not last)
        _, y_l = _scan_call(st_l, lp["k_a"], state_c, bsz, seq_len, True)
        br_rwkv_l = _rwkv_out_call(y_l, st_l, lp)

        br_attn_l = _attn_lat_call(p_l, p_c, kv_blk_c, cos, sin, lp["sink"], bsz, seq_len, ctx_len)
        br_conv_l = _conv_call(p_l, seq_len, lp["conv_w"])

        m_l = _merge_call(br_rwkv_l, br_conv_l, br_attn_l, p_l, lp["w_branch"])
        x_l = _outproj_call(m_l, lp["w_out"], x_l, mod3, row_l2, tm2)
        x_l = _ffn(x_l, bsz, seq_len, lp, mod3, row_l2, lambda b: b, tm2)

        if not last:
            br_rwkv_c = _rwkv_out_call(y_c, st_c, lp)
            br_attn_c = _attn_ctx_call(p_c, lp["sink"], bsz, ctx_len)
            br_conv_c = _conv_call(p_c, ctx_len, lp["conv_w"])
            m_c = _merge_call(br_rwkv_c, br_conv_c, br_attn_c, p_c, lp["w_branch"])
            x_c = _outproj_call(m_c, lp["w_out"], x_c, mod3, row_c, tm2)
            x_c = _ffn(x_c, bsz, ctx_len, lp, mod3, row_c, lambda b: bsz, tm2)

    return _final_norm_call(x_l, norm_final).reshape(bsz, seq_len, D)
```

```python
import functools

import jax
import jax.numpy as jnp
from jax import lax
from jax.experimental import pallas as pl
from jax.experimental.pallas import tpu as pltpu

F32 = jnp.float32
BF16 = jnp.bfloat16
HIGHEST = lax.Precision.HIGHEST

D = 2048
HD = 64
BW = 1024
NH = BW // HD
KVH = 4
GQ = NH // KVH
DECAY_RANK = 96
ICLR_RANK = 96
GATE_RANK = 256
VRES_RANK = 64
GN_EPS = 64e-5
NORM_EPS = 1e-6
WINDOW = 128
BLK = 128
assert WINDOW == BLK
GRID_W = 64
ROPE_BASE = 10000.0
NEG_INF = -1e30
N_EXPERTS = 16
EXPERT_FF = 2048
CAPACITY_FACTOR = 2
CHUNK = 64

C_R, C_K, C_V = 0, 1024, 2048
C_WD = (3072, 3200)
C_AD = (3328, 3456)
C_GD = 3584
RW_PAD = 4096
C_Q = 4096
C_CONV = 5120
C_GATE = 8192
C_AK = 14336
C_AV = 14592
NP = 14848
TN_IN = 512

VMEM_LIMIT = 56 * 1024 * 1024


def _cp(*sem, vmem=VMEM_LIMIT):
    return pltpu.CompilerParams(dimension_semantics=tuple(sem), vmem_limit_bytes=vmem)


def _dot(a, b):
    return jnp.dot(a, b, preferred_element_type=F32)


def _dot_nt(a, b):
    return lax.dot_general(a, b, (((1,), (1,)), ((), ())), preferred_element_type=F32)


def _dot_tn(a, b):
    return lax.dot_general(a, b, (((0,), (0,)), ((), ())), preferred_element_type=F32)


def _sigmoid(x):
    return 1.0 / (1.0 + jnp.exp(-x))


def _seg_sum(x, e_ref):
    e = e_ref[...]

    def split(z):
        hi = z.astype(BF16)
        return hi, (z - hi.astype(F32)).astype(BF16)

    hi, lo = split(x)
    hi2, lo2 = split(_dot(hi, e) + _dot(lo, e))
    return _dot_nt(hi2, e) + _dot_nt(lo2, e)


def _mod_kernel(c_ref, w_ref, b_ref, o_ref):
    c = c_ref[...]
    sc = c * _sigmoid(c)
    o_ref[0] = _dot(sc.astype(BF16), w_ref[0].astype(BF16)) + b_ref[0]


def _mod_call(cc, w_mod, b_mod):
    depth, _, n6 = w_mod.shape
    rows = cc.shape[0]
    tn = 1024
    return pl.pallas_call(
        _mod_kernel,
        grid=(depth, n6 // tn),
        in_specs=[pl.BlockSpec((rows, D), lambda l, j: (0, 0)),
                  pl.BlockSpec((1, D, tn), lambda l, j: (l, 0, j)),
                  pl.BlockSpec((1, 1, tn), lambda l, j: (l, 0, j))],
        out_specs=pl.BlockSpec((1, rows, tn), lambda l, j: (l, 0, j)),
        out_shape=jax.ShapeDtypeStruct((depth, rows, n6), F32),
        compiler_params=_cp("arbitrary", "arbitrary"),
        name="mod_proj",
    )(cc, w_mod, b_mod.reshape(depth, 1, n6))


def _norm_mod(x, gain, shift, scale):
    ms = jnp.mean(x * x, axis=-1, keepdims=True)
    y = x * lax.rsqrt(ms + NORM_EPS)
    return (y * gain) * (1.0 + scale) + shift


def _inproj_kernel(x_ref, gain_ref, shift_ref, scale_ref, w_ref, o_ref, h_ref):
    @pl.when(pl.program_id(1) == 0)
    def _():
        h_ref[...] = _norm_mod(x_ref[...], gain_ref[...], shift_ref[0], scale_ref[0]).astype(BF16)

    o_ref[...] = _dot(h_ref[...], w_ref[0]).astype(BF16)


def _inproj_call(x2, gain, mod3, mod_row, w_p, layer, tm, rwkv_kv_only=False):
    n_tok = x2.shape[0]
    n_rw = RW_PAD // TN_IN
    if rwkv_kv_only:
        ncols = RW_PAD + NP - C_AK
        col = lambda j: jnp.where(j < n_rw, j, j - n_rw + C_AK // TN_IN)
    else:
        ncols = NP
        col = lambda j: j
    return pl.pallas_call(
        _inproj_kernel,
        grid=(n_tok // tm, ncols // TN_IN),
        in_specs=[pl.BlockSpec((tm, D), lambda i, j: (i, 0)),
                  pl.BlockSpec((1, D), lambda i, j: (0, 0)),
                  pl.BlockSpec((1, 1, D), lambda i, j: (mod_row(i) * 6 + 0, 0, 0)),
                  pl.BlockSpec((1, 1, D), lambda i, j: (mod_row(i) * 6 + 1, 0, 0)),
                  pl.BlockSpec((1, D, TN_IN), lambda i, j: (layer, 0, col(j)))],
        out_specs=pl.BlockSpec((tm, TN_IN), lambda i, j: (i, j)),
        out_shape=jax.ShapeDtypeStruct((n_tok, ncols), BF16),
        scratch_shapes=[pltpu.VMEM((tm, D), BF16)],
        compiler_params=_cp("arbitrary", "arbitrary"),
        name="in_proj",
    )(x2, gain.reshape(1, D), mod3, mod3, w_p)


HALO = 16


def _shifted(cur, prev_blk, next_blk, first, last):
    tt = cur.shape[0]
    row = lax.broadcasted_iota(jnp.int32, cur.shape, 0)
    p_row = jnp.where(first, 0.0, prev_blk[HALO - 1:HALO, :].astype(F32))
    n_row = jnp.where(last, 0.0, next_blk[0:1, :].astype(F32))
    prev = jnp.where(row == 0, p_row, pltpu.roll(cur, 1, axis=0))
    nxt = jnp.where(row == tt - 1, n_row, pltpu.roll(cur, tt - 1, axis=0))
    return prev, nxt


def _halo_specs(tt, width, col_blk, n_tok):
    nb = n_tok // HALO
    r = tt // HALO
    return [pl.BlockSpec((tt, width), lambda i: (i, col_blk)),
            pl.BlockSpec((HALO, width), lambda i: (jnp.maximum(i * r - 1, 0), col_blk)),
            pl.BlockSpec((HALO, width), lambda i: (jnp.minimum((i + 1) * r, nb - 1), col_blk))]


def _streams_kernel(*refs, tiles_per_seq, has_vres):
    if has_vres:
        (p_ref, pp_ref, pn_ref, mu_ref, dup_ref, dbias_ref, iup_ref, ibias_ref, gup_ref, kk_ref_, e_ref,
         vd_ref, vu_ref, vb_ref, vf_ref,
         r_o, k_o, v_o, kk_o, g_o, a_o, lw_o) = refs
    else:
        (p_ref, pp_ref, pn_ref, mu_ref, dup_ref, dbias_ref, iup_ref, ibias_ref, gup_ref, kk_ref_, e_ref,
         r_o, k_o, v_o, kk_o, g_o, a_o, lw_o) = refs
    j = pl.program_id(0) % tiles_per_seq
    cur_b = p_ref[...]
    tt = cur_b.shape[0]
    ext = jnp.concatenate([pp_ref[...], cur_b, pn_ref[...]], axis=0)
    t_i = lax.broadcasted_iota(jnp.int32, (tt, tt + 2 * HALO), 0)
    e_i = lax.broadcasted_iota(jnp.int32, (tt, tt + 2 * HALO), 1) - HALO
    tap = ((e_i == t_i - 1) & ((e_i >= 0) | (j > 0))) | ((e_i == t_i + 1) & ((e_i < tt) | (j < tiles_per_seq - 1)))
    avg = _dot(jnp.where(tap, 0.5, 0.0).astype(BF16), ext)
    cur = cur_b.astype(F32)
    ps = cur + mu_ref[...] * (avg - cur)
    r = ps[:, C_R:C_R + BW]
    k = ps[:, C_K:C_K + BW]
    v = ps[:, C_V:C_V + BW]
    gd = ps[:, C_GD:C_GD + GATE_RANK]
    if has_vres:
        low = _dot(v.astype(BF16), vd_ref[...])
        mix = _sigmoid(vb_ref[...] + _dot(low.astype(BF16), vu_ref[...]))
        v = v + (vf_ref[...].astype(F32) - v) * mix
    for d in range(2):
        wd = ps[:, C_WD[d]:C_WD[d] + 128]
        ad = ps[:, C_AD[d]:C_AD[d] + 128]
        w_logit = dbias_ref[d] + _dot(jnp.tanh(wd).astype(BF16), dup_ref[d])
        lw_o[d] = -jnp.exp(-0.5) * _sigmoid(w_logit)
        a_o[d] = _sigmoid(ibias_ref[d] + _dot(ad.astype(BF16), iup_ref[d])).astype(BF16)
    kh = k * kk_ref_[...]
    ss = _seg_sum(kh * kh, e_ref)
    kk = kh * lax.rsqrt(jnp.maximum(ss, 1e-24))
    r_o[...] = r.astype(BF16)
    k_o[...] = k.astype(BF16)
    v_o[...] = v.astype(BF16)
    kk_o[...] = kk.astype(BF16)
    g_o[...] = _dot(_sigmoid(gd).astype(BF16), gup_ref[...]).astype(BF16)


def _streams_call(p, seq_len, lp, v_first):
    n_tok = p.shape[0]
    tt = 256
    has_vres = v_first is not None
    full = lambda *s: pl.BlockSpec(s, lambda i: (0,) * len(s))
    tok = pl.BlockSpec((tt, BW), lambda i: (i, 0))
    tok2 = pl.BlockSpec((2, tt, BW), lambda i: (0, i, 0))
    in_specs = _halo_specs(tt, RW_PAD, 0, n_tok) + [
        full(1, RW_PAD), full(2, 128, BW), full(2, 1, BW), full(2, 128, BW), full(2, 1, BW),
        full(GATE_RANK, BW), full(1, BW), full(BW, 128)]
    args = [p, p, p, lp["mu"], lp["decay_up"], lp["decay_bias"], lp["iclr_up"], lp["iclr_bias"],
            lp["gate_up"], lp["k_k"], lp["eblk"]]
    if has_vres:
        in_specs += [full(BW, 128), full(128, BW), full(1, BW), tok]
        args += [lp["vres_down"], lp["vres_up"], lp["vres_bias"], v_first]
    sd = lambda dt: jax.ShapeDtypeStruct((n_tok, BW), dt)
    sd2 = lambda dt: jax.ShapeDtypeStruct((2, n_tok, BW), dt)
    return pl.pallas_call(
        functools.partial(_streams_kernel, tiles_per_seq=seq_len // tt, has_vres=has_vres),
        grid=(n_tok // tt,),
        in_specs=in_specs,
        out_specs=[tok, tok, tok, tok, tok, tok2, tok2],
        out_shape=[sd(BF16), sd(BF16), sd(BF16), sd(BF16), sd(BF16), sd2(BF16), sd2(F32)],
        compiler_params=_cp("arbitrary"),
        name="rwkv_streams",
    )(*args)


def _scan_prologue(d, r_ref, k_ref, v_ref, kk_ref, a_ref, lw_ref, ka_ref):
    c = CHUNK
    ri = lax.broadcasted_iota(jnp.int32, (c, 2 * c), 0)
    ci = lax.broadcasted_iota(jnp.int32, (c, 2 * c), 1) % c
    diff = (ci - ri) if d else (ri - ci)
    incl = diff >= 0
    strict = diff > 0
    lw = lw_ref[0]
    tri = incl[:, :c].astype(BF16)
    lw_hi = lw.astype(BF16)
    lw_lo = (lw - lw_hi.astype(F32)).astype(BF16)
    b = _dot(tri, lw_hi) + _dot(tri, lw_lo)
    b_tot = b[0:1, :] if d else b[c - 1:c, :]
    a = a_ref[0].astype(F32)
    kk = kk_ref[...].astype(F32)
    kd = k_ref[...].astype(F32) * (1.0 + (a - 1.0) * ka_ref[...])
    kka = kk * a
    enb = jnp.exp(-b)
    etail = jnp.exp(b_tot - b)
    return dict(
        incl=incl, strict=strict,
        rt=(r_ref[...].astype(F32) * jnp.exp(b)).astype(BF16),
        bt=(-kk * jnp.exp(b - lw)).astype(BF16),
        kt=(kd * enb).astype(BF16), at=(kka * enb).astype(BF16),
        kh=(kd * etail).astype(BF16), ah=(kka * etail).astype(BF16),
        vv=v_ref[...], e_tot=jnp.exp(b_tot))


def _scan_kernel(*refs, with_output):
    ins = refs[:14]
    if with_output:
        yf_ref, yb_ref, sout_ref, st_ref = refs[14:]
    else:
        sout_ref, st_ref = refs[14:]
    ka_ref, s0_ref = ins[12], ins[13]
    s = pl.program_id(1)
    c = CHUNK

    @pl.when(s == 0)
    def _():
        st_ref[...] = s0_ref[:, 0]

    pro = [_scan_prologue(d, *ins[6 * d:6 * d + 6], ka_ref) for d in range(2)]
    st_all = [st_ref[d] for d in range(2)]

    npair = NH // 2
    units = [(d, slice(p * 2 * HD, (p + 1) * 2 * HD)) for d in range(2) for p in range(npair)]
    us = range(len(units))
    col = lambda name: [pro[d][name][:, sl] for d, sl in units]
    even1 = lax.broadcasted_iota(jnp.int32, (c, 2 * HD), 1) < HD

    def bd(z):
        zero = jnp.zeros_like(z)
        return jnp.concatenate([jnp.where(even1, z, zero), jnp.where(even1, zero, z)], axis=0)

    strict = [pro[d]["strict"] for d, _ in units]
    incl = [pro[d]["incl"] for d, _ in units]
    st = [st_all[d][:, sl] for d, sl in units]
    vbd = [bd(z) for z in col("vv")]
    rb = [jnp.concatenate(p, axis=0) for p in zip(col("rt"), col("bt"))]
    gk = [_dot_nt(rb[u], bd(z)) for u, z in zip(us, col("kt"))]
    ga = [_dot_nt(rb[u], bd(z)) for u, z in zip(us, col("at"))]
    rbs = [_dot_nt(rb[u], bd(st[u].astype(BF16))) for u in us]
    x = [jnp.where(strict[u], ga[u][c:], 0.0).astype(BF16) for u in us]
    a_bk = [jnp.where(strict[u], gk[u][c:], 0.0) for u in us]
    if with_output:
        a_rk = [jnp.where(incl[u], gk[u][:c], 0.0) for u in us]
        a_ra = [jnp.where(incl[u], ga[u][:c], 0.0).astype(BF16) for u in us]
        av = [_dot(jnp.concatenate([a_rk[u], a_bk[u]], axis=0).astype(BF16), vbd[u]) for u in us]
        w = [rbs[u][c:] + av[u][c:] for u in us]
    else:
        w = [rbs[u][c:] + _dot(a_bk[u].astype(BF16), vbd[u]) for u in us]
    for rnd in range(6):
        if rnd < 5:
            m = [_dot(x[u], jnp.concatenate([bd(x[u]), bd(w[u].astype(BF16))], axis=1)) for u in us]
            x = [m[u][:, :2 * c].astype(BF16) for u in us]
            w = [w[u] + m[u][:, 2 * c:] for u in us]
        else:
            w = [w[u] + _dot(x[u], bd(w[u].astype(BF16))) for u in us]
    ub = [w[u].astype(BF16) for u in us]
    if with_output:
        ys = [rbs[u][:c] + av[u][:c] + _dot(a_ra[u], bd(ub[u])) for u in us]
        yf_ref[...] = jnp.concatenate(ys[:npair], axis=1)
        yb_ref[...] = jnp.concatenate(ys[npair:], axis=1)
    full = [_dot_tn(jnp.concatenate([z, ub[u]], axis=0), jnp.concatenate(p, axis=0))
            for u, z, p in zip(us, col("vv"), zip(col("kh"), col("ah")))]
    upd = [jnp.where(even1, f[:HD], f[HD:]) for f in full]
    for d in range(2):
        st_ref[d] = st_all[d] * pro[d]["e_tot"] + jnp.concatenate(upd[d * npair:(d + 1) * npair], axis=1)

    @pl.when(s == pl.num_programs(1) - 1)
    def _():
        sout_ref[:, 0] = st_ref[...]


def _scan_call(streams, k_a, s0, bsz, seq_len, with_output):
    r, k, v, kk, _, a2, lw2 = streams
    n_tok = r.shape[0]
    nc = seq_len // CHUNK

    rows = (lambda b, s: b * nc + s, lambda b, s: b * nc + nc - 1 - s)
    tok = [pl.BlockSpec((CHUNK, BW), lambda b, s, f=f: (f(b, s), 0)) for f in rows]
    tok2 = [pl.BlockSpec((1, CHUNK, BW), lambda b, s, f=f, d=d: (d, f(b, s), 0)) for d, f in enumerate(rows)]
    st_spec = pl.BlockSpec((2, 1, HD, BW), lambda b, s: (0, b, 0, 0))
    in_specs, args = [], []
    for d in range(2):
        in_specs += [tok[d]] * 4 + [tok2[d]] * 2
        args += [r, k, v, kk, a2, lw2]
    in_specs += [pl.BlockSpec((1, BW), lambda b, s: (0, 0)), st_spec]
    out_specs = [st_spec]
    out_shape = [jax.ShapeDtypeStruct((2, bsz, HD, BW), F32)]
    if with_output:
        out_specs = tok + out_specs
        out_shape = [jax.ShapeDtypeStruct((n_tok, BW), F32)] * 2 + out_shape
    res = pl.pallas_call(
        functools.partial(_scan_kernel, with_output=with_output),
        grid=(bsz, nc),
        in_specs=in_specs,
        out_specs=out_specs,
        out_shape=out_shape,
        scratch_shapes=[pltpu.VMEM((2, HD, BW), F32)],
        compiler_params=_cp("arbitrary", "arbitrary"),
        name="rwkv_scan",
    )(*args, k_a, s0)
    if with_output:
        return res[2], (res[0], res[1])
    return res[0], None


def _rwkv_out_kernel(yf_ref, yb_ref, r_ref, k_ref, v_ref, a_ref, g_ref, ka_ref, rk_ref, gnw_ref, gnb_ref, e_ref,
                     o_ref):
    y = yf_ref[...] + yb_ref[...]
    mean = _seg_sum(y, e_ref) * (1.0 / HD)
    yc = y - mean
    var = _seg_sum(yc * yc, e_ref) * (1.0 / HD)
    yn = yc * lax.rsqrt(var + GN_EPS) * gnw_ref[...] + gnb_ref[...]
    r = r_ref[...].astype(F32)
    k = k_ref[...].astype(F32)
    asum = a_ref[0].astype(F32) + a_ref[1].astype(F32)
    kd_sum = k * (2.0 + (asum - 2.0) * ka_ref[...])
    bonus = _seg_sum(r * kd_sum * rk_ref[...], e_ref) * v_ref[...].astype(F32)
    o_ref[...] = ((yn + bonus) * g_ref[...].astype(F32)).astype(BF16)


def _rwkv_out_call(y2, streams, lp):
    r, k, v, _, g, a2, _ = streams
    n_tok = r.shape[0]
    tt = 256
    tok = pl.BlockSpec((tt, BW), lambda i: (i, 0))
    tok2 = pl.BlockSpec((2, tt, BW), lambda i: (0, i, 0))
    vec = pl.BlockSpec((1, BW), lambda i: (0, 0))
    return pl.pallas_call(
        _rwkv_out_kernel,
        grid=(n_tok // tt,),
        in_specs=[tok, tok, tok, tok, tok, tok2, tok, vec, vec, vec, vec, pl.BlockSpec((BW, 128), lambda i: (0, 0))],
        out_specs=tok,
        out_shape=jax.ShapeDtypeStruct((n_tok, BW), BF16),
        compiler_params=_cp("arbitrary"),
        name="rwkv_out",
    )(y2[0], y2[1], r, k, v, a2, g, lp["k_a"], lp["r_k"], lp["gn_w"], lp["gn_b"], lp["eblk"])


def _conv_kernel(b_ref, c_ref, cp_ref, cn_ref, u_ref, up_ref, un_ref, w_ref, o_ref, *, tiles_per_seq):
    j = pl.program_id(0) % tiles_per_seq
    cu = c_ref[...].astype(F32) * u_ref[...].astype(F32)
    cu_p = cp_ref[...].astype(F32) * up_ref[...].astype(F32)
    cu_n = cn_ref[...].astype(F32) * un_ref[...].astype(F32)
    prev, nxt = _shifted(cu, cu_p, cu_n, j == 0, j == tiles_per_seq - 1)
    w = w_ref[...]
    conv = w[0:1] * prev + w[1:2] * cu + w[2:3] * nxt
    o_ref[...] = (b_ref[...].astype(F32) * conv).astype(BF16)


def _conv_call(p, seq_len, conv_w):
    n_tok = p.shape[0]
    tt = 256
    cb = C_CONV // BW
    return pl.pallas_call(
        functools.partial(_conv_kernel, tiles_per_seq=seq_len // tt),
        grid=(n_tok // tt,),
        in_specs=[pl.BlockSpec((tt, BW), lambda i: (i, cb))] + _halo_specs(tt, BW, cb + 1, n_tok)
        + _halo_specs(tt, BW, cb + 2, n_tok) + [pl.BlockSpec((3, BW), lambda i: (0, 0))],
        out_specs=pl.BlockSpec((tt, BW), lambda i: (i, 0)),
        out_shape=jax.ShapeDtypeStruct((n_tok, BW), BF16),
        compiler_params=_cp("arbitrary"),
        name="short_conv",
    )(p, p, p, p, p, p, p, conv_w)


def _rope(x, cos, sin):
    w = x.shape[1]
    lane = lax.broadcasted_iota(jnp.int32, x.shape, 1)
    partner = jnp.where((lane % 32) < 16, pltpu.roll(x, w - 16, axis=1), pltpu.roll(x, 16, axis=1))
    return x * cos + partner * sin


LOG2E = 1.4426950408889634
Q_SCALE = HD ** -0.5 * LOG2E


def _softmax_pv(s2, sink2_col, v):
    m = jnp.maximum(jnp.max(s2, axis=-1, keepdims=True), sink2_col)
    p = jnp.exp2(s2 - m)
    den = jnp.sum(p, axis=-1, keepdims=True) + jnp.exp2(sink2_col - m)
    return _dot(p.astype(BF16), v) / den


def _attn_lat_kernel(q_ref, kp_ref, kc_ref, kn_ref, vp_ref, vc_ref, vn_ref, kx_ref, vx_ref,
                     cos_ref, cosp_ref, cosn_ref, sin_ref, sinp_ref, sinn_ref, sink_ref, o_ref, *, nblk):
    n = pl.program_id(1)
    kvw = KVH * HD
    q = (_rope(q_ref[...].astype(F32), cos_ref[...], sin_ref[...]) * Q_SCALE).astype(BF16)
    kb = jnp.concatenate([
        _rope(kp_ref[...].astype(F32), cosp_ref[:, :kvw], sinp_ref[:, :kvw]),
        _rope(kc_ref[...].astype(F32), cos_ref[:, :kvw], sin_ref[:, :kvw]),
        _rope(kn_ref[...].astype(F32), cosn_ref[:, :kvw], sinn_ref[:, :kvw])], axis=0).astype(BF16)
    k_all = jnp.concatenate([kb, kx_ref[...]], axis=0)
    v_all = jnp.concatenate([vp_ref[...], vc_ref[...], vn_ref[...], vx_ref[...]], axis=0)
    rows = GQ * BLK
    qpos = lax.broadcasted_iota(jnp.int32, (rows, BLK), 0) % BLK
    kidx = lax.broadcasted_iota(jnp.int32, (rows, BLK), 1)
    bias_p = jnp.where((kidx >= qpos) & (n > 0), 0.0, NEG_INF)
    bias_n = jnp.where((kidx <= qpos) & (n < nblk - 1), 0.0, NEG_INF)
    sink = sink_ref[...] * LOG2E
    gs = range(KVH)
    qg = [jnp.concatenate([q[:, (gi * GQ + t) * HD:(gi * GQ + t + 1) * HD] for t in range(GQ)], axis=0)
          for gi in gs]
    sk = [jnp.concatenate([jnp.broadcast_to(sink[:, gi * GQ + t:gi * GQ + t + 1], (BLK, 1))
                           for t in range(GQ)], axis=0) for gi in gs]
    s = [_dot_nt(qg[gi], k_all[:, gi * HD:(gi + 1) * HD]) for gi in gs]
    s = [jnp.concatenate([z[:, :BLK] + bias_p, z[:, BLK:2 * BLK], z[:, 2 * BLK:3 * BLK] + bias_n,
                          z[:, 3 * BLK:]], axis=1) for z in s]
    m = [jnp.maximum(jnp.max(s[gi], axis=-1, keepdims=True), sk[gi]) for gi in gs]
    p = [jnp.exp2(s[gi] - m[gi]) for gi in gs]
    den = [jnp.sum(p[gi], axis=-1, keepdims=True) + jnp.exp2(sk[gi] - m[gi]) for gi in gs]
    og = [_dot(p[gi].astype(BF16), v_all[:, gi * HD:(gi + 1) * HD]) / den[gi] for gi in gs]
    o_ref[...] = jnp.concatenate([og[gi][t * BLK:(t + 1) * BLK] for gi in gs for t in range(GQ)],
                                 axis=1).astype(BF16)


def _attn_lat_call(p_l, p_c, kv_blk_c, cos, sin, sink, bsz, seq_len, ctx_len):
    nblk = seq_len // BLK
    kvw = KVH * HD
    kb, vb = C_AK // kvw, C_AV // kvw
    rowq = lambda b, n: b * nblk + n
    rowp = lambda b, n: b * nblk + jnp.maximum(n - 1, 0)
    rown = lambda b, n: b * nblk + jnp.minimum(n + 1, nblk - 1)
    tabp = lambda b, n: (jnp.maximum(n - 1, 0), 0)
    tabn = lambda b, n: (jnp.minimum(n + 1, nblk - 1), 0)
    kv = lambda rf, cb: pl.BlockSpec((BLK, kvw), lambda b, n: (rf(b, n), cb))
    tab = lambda f: pl.BlockSpec((BLK, BW), f)
    return pl.pallas_call(
        functools.partial(_attn_lat_kernel, nblk=nblk),
        grid=(bsz, nblk),
        in_specs=[pl.BlockSpec((BLK, BW), lambda b, n: (rowq(b, n), C_Q // BW)),
                  kv(rowp, kb), kv(rowq, kb), kv(rown, kb), kv(rowp, vb), kv(rowq, vb), kv(rown, vb),
                  pl.BlockSpec((ctx_len, kvw), lambda b, n: (b, kv_blk_c)),
                  pl.BlockSpec((ctx_len, kvw), lambda b, n: (b, kv_blk_c + 1)),
                  tab(lambda b, n: (n, 0)), tab(tabp), tab(tabn),
                  tab(lambda b, n: (n, 0)), tab(tabp), tab(tabn),
                  pl.BlockSpec((1, NH), lambda b, n: (0, 0))],
        out_specs=pl.BlockSpec((BLK, BW), lambda b, n: (rowq(b, n), 0)),
        out_shape=jax.ShapeDtypeStruct((bsz * seq_len, BW), BF16),
        compiler_params=_cp("arbitrary", "arbitrary"),
        name="attn_latent",
    )(p_l, p_l, p_l, p_l, p_l, p_l, p_l, p_c, p_c, cos, cos, cos, sin, sin, sin, sink)


def _attn_ctx_kernel(q_ref, kx_ref, vx_ref, sink_ref, o_ref):
    q = (q_ref[...].astype(F32) * Q_SCALE).astype(BF16)
    kx = kx_ref[...]
    vx = vx_ref[...]
    sink = sink_ref[...] * LOG2E
    tq = q.shape[0]
    outs = []
    for gi in range(KVH):
        qg = jnp.concatenate([q[:, (gi * GQ + t) * HD:(gi * GQ + t + 1) * HD] for t in range(GQ)], axis=0)
        sk = jnp.concatenate([jnp.broadcast_to(sink[:, gi * GQ + t:gi * GQ + t + 1], (tq, 1))
                              for t in range(GQ)], axis=0)
        s = _dot_nt(qg, kx[:, gi * HD:(gi + 1) * HD])
        og = _softmax_pv(s, sk, vx[:, gi * HD:(gi + 1) * HD])
        outs += [og[t * tq:(t + 1) * tq] for t in range(GQ)]
    o_ref[...] = jnp.concatenate(outs, axis=1).astype(BF16)


def _attn_ctx_call(p_c, sink, bsz, ctx_len):
    kvw = KVH * HD
    tq = 128
    nq = ctx_len // tq
    return pl.pallas_call(
        _attn_ctx_kernel,
        grid=(bsz, nq),
        in_specs=[pl.BlockSpec((tq, BW), lambda b, n: (b * nq + n, C_Q // BW)),
                  pl.BlockSpec((ctx_len, kvw), lambda b, n: (b, C_AK // kvw)),
                  pl.BlockSpec((ctx_len, kvw), lambda b, n: (b, C_AV // kvw)),
                  pl.BlockSpec((1, NH), lambda b, n: (0, 0))],
        out_specs=pl.BlockSpec((tq, BW), lambda b, n: (b * nq + n, 0)),
        out_shape=jax.ShapeDtypeStruct((bsz * ctx_len, BW), BF16),
        compiler_params=_cp("arbitrary", "arbitrary"),
        name="attn_context",
    )(p_c, p_c, p_c, sink)


def _merge_kernel(b0_ref, b1_ref, b2_ref, g0_ref, g1_ref, g2_ref, w_ref, o_ref):
    acc = _sigmoid(g0_ref[...].astype(F32)) * _dot(b0_ref[...], w_ref[0])
    acc += _sigmoid(g1_ref[...].astype(F32)) * _dot(b1_ref[...], w_ref[1])
    acc += _sigmoid(g2_ref[...].astype(F32)) * _dot(b2_ref[...], w_ref[2])
    o_ref[...] = acc.astype(BF16)


def _merge_call(br_rwkv, br_conv, br_attn, p, w_branch):
    n_tok = p.shape[0]
    tm, tn = 512, 1024
    nn = D // tn
    br = pl.BlockSpec((tm, BW), lambda j, i: (i, 0))
    gate = lambda t: pl.BlockSpec((tm, tn), lambda j, i: (i, (C_GATE + t * D) // tn + j))
    return pl.pallas_call(
        _merge_kernel,
        grid=(nn, n_tok // tm),
        in_specs=[br, br, br, gate(0), gate(1), gate(2), pl.BlockSpec((3, BW, tn), lambda j, i: (0, 0, j))],
        out_specs=pl.BlockSpec((tm, tn), lambda j, i: (i, j)),
        out_shape=jax.ShapeDtypeStruct((n_tok, D), BF16),
        compiler_params=_cp("arbitrary", "arbitrary"),
        name="merge_branches",
    )(br_rwkv, br_conv, br_attn, p, p, p, w_branch)


def _outproj_kernel(m_ref, w_ref, x_ref, gate_ref, o_ref):
    o_ref[...] = x_ref[...] + gate_ref[0] * _dot(m_ref[...], w_ref[...])


def _outproj_call(m, w_out, x2, mod3, mod_row, tm):
    n_tok = x2.shape[0]
    return pl.pallas_call(
        _outproj_kernel,
        grid=(n_tok // tm,),
        in_specs=[pl.BlockSpec((tm, D), lambda i: (i, 0)),
                  pl.BlockSpec((D, D), lambda i: (0, 0)),
                  pl.BlockSpec((tm, D), lambda i: (i, 0)),
                  pl.BlockSpec((1, 1, D), lambda i: (mod_row(i) * 6 + 2, 0, 0))],
        out_specs=pl.BlockSpec((tm, D), lambda i: (i, 0)),
        out_shape=jax.ShapeDtypeStruct((n_tok, D), F32),
        compiler_params=_cp("arbitrary"),
        name="out_proj",
    )(m, w_out, x2, mod3)


def _ffn_prep_kernel(x_ref, gain_ref, shift_ref, scale_ref, wr_ref, h_ref, aff_ref):
    h = _norm_mod(x_ref[...], gain_ref[...], shift_ref[0], scale_ref[0])
    h_hi = h.astype(BF16)
    h_ref[...] = h_hi
    h_lo = (h - h_hi.astype(F32)).astype(BF16)
    wr = wr_ref[...]
    w_hi = wr.astype(BF16)
    w_lo = (wr - w_hi.astype(F32)).astype(BF16)
    logits = _dot(h_hi, w_hi) + (_dot(h_hi, w_lo) + _dot(h_lo, w_hi))
    lane = lax.broadcasted_iota(jnp.int32, logits.shape, 1)
    logits = jnp.where(lane < N_EXPERTS, logits, NEG_INF)
    m = jnp.max(logits, axis=-1, keepdims=True)
    e = jnp.exp(logits - m)
    aff_ref[...] = e / jnp.sum(e, axis=-1, keepdims=True)


def _ffn_prep_call(x2, gain, mod3, mod_row, wr_pad, tm):
    n_tok = x2.shape[0]
    return pl.pallas_call(
        _ffn_prep_kernel,
        grid=(n_tok // tm,),
        in_specs=[pl.BlockSpec((tm, D), lambda i: (i, 0)),
                  pl.BlockSpec((1, D), lambda i: (0, 0)),
                  pl.BlockSpec((1, 1, D), lambda i: (mod_row(i) * 6 + 3, 0, 0)),
                  pl.BlockSpec((1, 1, D), lambda i: (mod_row(i) * 6 + 4, 0, 0)),
                  pl.BlockSpec((D, 128), lambda i: (0, 0))],
        out_specs=[pl.BlockSpec((tm, D), lambda i: (i, 0)), pl.BlockSpec((tm, 128), lambda i: (i, 0))],
        out_shape=[jax.ShapeDtypeStruct((n_tok, D), BF16), jax.ShapeDtypeStruct((n_tok, 128), F32)],
        compiler_params=_cp("arbitrary"),
        name="ffn_prep",
    )(x2, gain.reshape(1, D), mod3, mod3, wr_pad)


def _select_kernel(aff_ref, slot_ref, *, cap):
    a = aff_ref[0]
    n = a.shape[1]
    bits = lax.bitcast_convert_type(a, jnp.int32)

    def body(i, t):
        cand = t | jnp.left_shift(jnp.int32(1), 30 - i)
        cnt = jnp.sum((bits >= cand).astype(jnp.int32), axis=-1, keepdims=True)
        return jnp.where(cnt >= cap, cand, t)

    thr = lax.fori_loop(0, 31, body, jnp.zeros((a.shape[0], 1), jnp.int32))
    gt = bits > thr
    eq = bits == thr
    n_gt = jnp.sum(gt.astype(jnp.int32), axis=-1, keepdims=True)
    tc = min(n, 512)

    def prefix(mask_bf16):
        cols = []
        for j0 in range(0, n, tc):
            ri = lax.broadcasted_iota(jnp.int32, (n, tc), 0)
            ci = lax.broadcasted_iota(jnp.int32, (n, tc), 1) + j0
            cols.append(_dot(mask_bf16, (ri <= ci).astype(BF16)))
        return jnp.concatenate(cols, axis=1)

    eq_f = eq.astype(BF16)
    excl_eq = prefix(eq_f) - eq_f.astype(F32)
    sel = gt | (eq & (excl_eq < (cap - n_gt).astype(F32)))
    pos = prefix(sel.astype(BF16))
    slot_ref[0] = jnp.where(sel, pos.astype(jnp.int32) - 1, -1)


def _select_call(aff_t, cap):
    bsz, ne, n = aff_t.shape
    return pl.pallas_call(
        functools.partial(_select_kernel, cap=cap),
        grid=(bsz,),
        in_specs=[pl.BlockSpec((1, ne, n), lambda b: (b, 0, 0))],
        out_specs=pl.BlockSpec((1, ne, n), lambda b: (b, 0, 0)),
        out_shape=jax.ShapeDtypeStruct((bsz, ne, n), jnp.int32),
        compiler_params=_cp("arbitrary"),
        name="expert_select",
    )(aff_t)


def _gather_kernel(slot_ref, aff_ref, h_ref, xs_ref, g_ref, *, cap):
    slot = slot_ref[0]
    n = slot.shape[1]
    onehot = lax.broadcasted_iota(jnp.int32, (cap, n), 0) == slot
    xs_ref[0] = _dot(onehot.astype(BF16), h_ref[...]).astype(BF16)
    g_ref[0] = jnp.sum(jnp.where(onehot, aff_ref[0], 0.0), axis=-1, keepdims=True)


def _gather_call(slot, aff_t, h2, cap):
    bsz, ne, n = slot.shape
    return pl.pallas_call(
        functools.partial(_gather_kernel, cap=cap),
        grid=(bsz, ne),
        in_specs=[pl.BlockSpec((1, 1, n), lambda b, e: (b * ne + e, 0, 0)),
                  pl.BlockSpec((1, 1, n), lambda b, e: (b * ne + e, 0, 0)),
                  pl.BlockSpec((n, D), lambda b, e: (b, 0))],
        out_specs=[pl.BlockSpec((1, cap, D), lambda b, e: (e, b, 0)),
                   pl.BlockSpec((1, cap, 1), lambda b, e: (e, b, 0))],
        out_shape=[jax.ShapeDtypeStruct((ne, bsz * cap, D), BF16),
                   jax.ShapeDtypeStruct((ne, bsz * cap, 1), F32)],
        compiler_params=_cp("arbitrary", "arbitrary"),
        name="expert_gather",
    )(slot.reshape(bsz * ne, 1, n), aff_t.reshape(bsz * ne, 1, n), h2)


def _expert_kernel(x_ref, wg_ref, wu_ref, wd_ref, g_ref, o_ref, acc_ref):
    f = pl.program_id(2)

    @pl.when(f == 0)
    def _():
        acc_ref[...] = jnp.zeros_like(acc_ref)

    x = x_ref[0]
    gate = _dot(x, wg_ref[0].astype(BF16))
    hid = gate * _sigmoid(gate) * _dot(x, wu_ref[0].astype(BF16))
    acc_ref[...] += _dot(hid.astype(BF16), wd_ref[0].astype(BF16))

    @pl.when(f == pl.num_programs(2) - 1)
    def _():
        o_ref[0] = (acc_ref[...] * g_ref[0]).astype(BF16)


def _expert_call(xs, g, wg, wu, wd):
    ne, rows, _ = xs.shape
    tm = min(rows, 1024)
    tf = 256
    return pl.pallas_call(
        _expert_kernel,
        grid=(ne, rows // tm, EXPERT_FF // tf),
        in_specs=[pl.BlockSpec((1, tm, D), lambda e, i, f: (e, i, 0)),
                  pl.BlockSpec((1, D, tf), lambda e, i, f: (e, 0, f)),
                  pl.BlockSpec((1, D, tf), lambda e, i, f: (e, 0, f)),
                  pl.BlockSpec((1, tf, D), lambda e, i, f: (e, f, 0)),
                  pl.BlockSpec((1, tm, 1), lambda e, i, f: (e, i, 0))],
        out_specs=pl.BlockSpec((1, tm, D), lambda e, i, f: (e, i, 0)),
        out_shape=jax.ShapeDtypeStruct((ne, rows, D), BF16),
        scratch_shapes=[pltpu.VMEM((tm, D), F32)],
        compiler_params=_cp("arbitrary", "arbitrary", "arbitrary"),
        name="expert_mlp",
    )(xs, wg, wu, wd, g)


def _scatter_kernel(slot_ref, ys_ref, x_ref, gate_ref, o_ref, *, cap):
    slot_t = slot_ref[0]
    n = slot_t.shape[0]
    lane = lax.broadcasted_iota(jnp.int32, (n, cap), 1)
    acc = jnp.zeros(x_ref.shape, F32)
    for e in range(N_EXPERTS):
        onehot = (slot_t[:, e:e + 1] == lane).astype(BF16)
        acc += _dot(onehot, ys_ref[e])
    o_ref[...] = x_ref[...] + gate_ref[0] * acc


def _scatter_call(slot_t, ys, x2, mod3, mod_row_b, cap):
    bsz, n, ne = slot_t.shape
    tn = 512
    return pl.pallas_call(
        functools.partial(_scatter_kernel, cap=cap),
        grid=(bsz, D // tn),
        in_specs=[pl.BlockSpec((1, n, ne), lambda b, j: (b, 0, 0)),
                  pl.BlockSpec((ne, cap, tn), lambda b, j: (0, b, j)),
                  pl.BlockSpec((n, tn), lambda b, j: (b, j)),
                  pl.BlockSpec((1, 1, tn), lambda b, j: (mod_row_b(b) * 6 + 5, 0, j))],
        out_specs=pl.BlockSpec((n, tn), lambda b, j: (b, j)),
        out_shape=jax.ShapeDtypeStruct(x2.shape, F32),
        compiler_params=_cp("arbitrary", "arbitrary"),
        name="expert_scatter",
    )(slot_t, ys, x2, mod3)


def _ffn(x2, bsz, n, lp, mod3, mod_row, mod_row_b, tm):
    cap = CAPACITY_FACTOR * n // N_EXPERTS
    h2, aff = _ffn_prep_call(x2, lp["norm_ffn"], mod3, mod_row, lp["w_router"], tm)
    aff_t = jnp.swapaxes(aff[:, :N_EXPERTS].reshape(bsz, n, N_EXPERTS), 1, 2)
    slot = _select_call(aff_t, cap)
    xs, g = _gather_call(slot, aff_t, h2, cap)
    ys = _expert_call(xs, g, lp["w_exp_gate"], lp["w_exp_up"], lp["w_exp_down"])
    return _scatter_call(jnp.swapaxes(slot, 1, 2), ys, x2, mod3, mod_row_b, cap)


def _final_norm_kernel(x_ref, gain_ref, o_ref):
    x = x_ref[...]
    ms = jnp.mean(x * x, axis=-1, keepdims=True)
    o_ref[...] = (x * lax.rsqrt(ms + NORM_EPS)) * gain_ref[...]


def _final_norm_call(x2, gain):
    n_tok = x2.shape[0]
    tm = 512
    return pl.pallas_call(
        _final_norm_kernel,
        grid=(n_tok // tm,),
        in_specs=[pl.BlockSpec((tm, D), lambda i: (i, 0)), pl.BlockSpec((1, D), lambda i: (0, 0))],
        out_specs=pl.BlockSpec((tm, D), lambda i: (i, 0)),
        out_shape=jax.ShapeDtypeStruct(x2.shape, F32),
        compiler_params=_cp("arbitrary"),
        name="final_norm",
    )(x2, gain.reshape(1, D))


def _pad_cols(parts):
    out = []
    for a, width in parts:
        out.append(a)
        if a.shape[-1] < width:
            out.append(jnp.zeros(a.shape[:-1] + (width - a.shape[-1],), a.dtype))
    return jnp.concatenate(out, axis=-1)


def _rwkv_col_parts(w):
    o = 3 * BW
    parts = [(w[..., :o], o)]
    for _ in range(2):
        parts.append((w[..., o:o + DECAY_RANK], 128))
        o += DECAY_RANK
    for _ in range(2):
        parts.append((w[..., o:o + ICLR_RANK], 128))
        o += ICLR_RANK
    parts.append((w[..., o:o + GATE_RANK], RW_PAD - C_GD))
    return parts, o + GATE_RANK


LANE = 128
RW_END = 3 * BW + 2 * DECAY_RANK + 2 * ICLR_RANK + GATE_RANK
CODE_W = RW_PAD - 3 * BW


def _wrelayout_kernel(w_ref, misc_ref, o_ref):
    j = pl.program_id(1)
    is_code = (j >= 3 * BW // LANE) & (j < RW_PAD // LANE)
    o_ref[0] = jnp.where(is_code, misc_ref[0], w_ref[0]).astype(BF16)


def _wrelayout_call(w_in):
    depth = w_in.shape[0]
    code_parts, _ = _rwkv_col_parts(w_in[:, :, :RW_END])
    misc = _pad_cols(code_parts[1:])
    nb = NP // LANE
    kv_src = RW_END // LANE

    def src(j):
        shifted = jnp.where(j >= C_AK // LANE, j - C_AK // LANE + kv_src, j + (RW_END + 2 * KVH * HD - C_Q) // LANE)
        return jnp.where(j < RW_PAD // LANE, jnp.minimum(j, 3 * BW // LANE - 1), shifted)

    def misc_blk(j):
        return jnp.clip(j - 3 * BW // LANE, 0, CODE_W // LANE - 1)

    return pl.pallas_call(
        _wrelayout_kernel,
        grid=(depth, nb),
        in_specs=[pl.BlockSpec((1, D, LANE), lambda l, j: (l, 0, src(j))),
                  pl.BlockSpec((1, D, LANE), lambda l, j: (l, 0, misc_blk(j)))],
        out_specs=pl.BlockSpec((1, D, LANE), lambda l, j: (l, 0, j)),
        out_shape=jax.ShapeDtypeStruct((depth, D, NP), BF16),
        compiler_params=_cp("arbitrary", "arbitrary"),
        name="w_in_relayout",
    )(w_in, misc)


def _prep_layer(l, depth, shift_mu, decay_up, decay_bias, iclr_up, iclr_bias, gate_up, vres_down, vres_up,
                vres_bias, k_k, k_a, r_k, gn_w, gn_b, conv_w, attn_sink, w_branch, w_out, w_router,
                w_exp_gate, w_exp_up, w_exp_down, norm_mix, norm_ffn, eblk):
    mu_parts, _ = _rwkv_col_parts(shift_mu[l][None, :])
    pad_rank = lambda u: jnp.pad(u, ((0, 0), (0, 128 - u.shape[1]), (0, 0))).astype(BF16)
    lp = dict(
        mu=_pad_cols(mu_parts),
        decay_up=pad_rank(decay_up[l]), decay_bias=decay_bias[l].reshape(2, 1, BW),
        iclr_up=pad_rank(iclr_up[l]), iclr_bias=iclr_bias[l].reshape(2, 1, BW),
        gate_up=gate_up[l].astype(BF16), k_k=k_k[l].reshape(1, BW), k_a=k_a[l].reshape(1, BW),
        r_k=r_k[l].reshape(1, BW), gn_w=gn_w[l].reshape(1, BW), gn_b=gn_b[l].reshape(1, BW),
        conv_w=conv_w[l], sink=attn_sink[l].reshape(1, NH),
        w_branch=w_branch[l].astype(BF16), w_out=w_out[l].astype(BF16),
        w_router=jnp.pad(w_router[l], ((0, 0), (0, 128 - N_EXPERTS))),
        w_exp_gate=w_exp_gate[l], w_exp_up=w_exp_up[l], w_exp_down=w_exp_down[l],
        norm_mix=norm_mix[l], norm_ffn=norm_ffn[l], eblk=eblk)
    if l > 0:
        lp["vres_down"] = jnp.pad(vres_down[l - 1], ((0, 0), (0, 128 - VRES_RANK))).astype(BF16)
        lp["vres_up"] = jnp.pad(vres_up[l - 1], ((0, 128 - VRES_RANK), (0, 0))).astype(BF16)
        lp["vres_bias"] = vres_bias[l - 1].reshape(1, BW)
    return lp


def _rope_tables(seq_len):
    quarter = HD // 4
    inv = ROPE_BASE ** (-jnp.arange(quarter, dtype=F32) / quarter)
    pos = jnp.arange(seq_len)
    ang_r = (pos // GRID_W).astype(F32)[:, None] * inv[None, :]
    ang_c = (pos % GRID_W).astype(F32)[:, None] * inv[None, :]
    cos = jnp.concatenate([jnp.cos(ang_r)] * 2 + [jnp.cos(ang_c)] * 2, axis=1)
    sin = jnp.concatenate([-jnp.sin(ang_r), jnp.sin(ang_r), -jnp.sin(ang_c), jnp.sin(ang_c)], axis=1)
    return jnp.tile(cos, (1, NH)), jnp.tile(sin, (1, NH))


def kernel(x, c, ctx, c_ctx, w_mod, b_mod, norm_mix, norm_ffn, w_in, shift_mu, decay_up, decay_bias, iclr_up,
           iclr_bias, gate_up, vres_down, vres_up, vres_bias, k_k, k_a, r_k, gn_w, gn_b, conv_w, attn_sink,
           w_branch, w_out, w_router, w_exp_gate, w_exp_up, w_exp_down, norm_final):
    bsz, seq_len, _ = x.shape
    ctx_len = ctx.shape[1]
    depth = w_in.shape[0]
    mod_rows = -(-(bsz + 1) // 8) * 8
    cc = jnp.concatenate([c, c_ctx[None, :], jnp.zeros((mod_rows - bsz - 1, D), F32)], axis=0)
    mod_all = _mod_call(cc, w_mod, b_mod)

    hd_i = jnp.arange(BW) // HD
    eblk = (hd_i[:, None] == jnp.arange(128)[None, :]).astype(BF16)
    cos, sin = _rope_tables(seq_len)

    tm_l = min(1024, seq_len)
    tm_c = min(1024, bsz * ctx_len)
    tiles_per_sample = seq_len // tm_l
    row_l = lambda i: i // tiles_per_sample
    row_c = lambda i: bsz
    tm2 = 512
    row_l2 = lambda i: i // (seq_len // tm2)

    x_l = x.reshape(bsz * seq_len, D)
    x_c = ctx.reshape(bsz * ctx_len, D)
    vf_l = vf_c = None
    zero_state = jnp.zeros((2, bsz, HD, BW), F32)

    w_p = _wrelayout_call(w_in)

    for l in range(depth):
        last = l == depth - 1
        lp = _prep_layer(l, depth, shift_mu, decay_up, decay_bias, iclr_up, iclr_bias, gate_up, vres_down,
                         vres_up, vres_bias, k_k, k_a, r_k, gn_w, gn_b, conv_w, attn_sink, w_branch, w_out,
                         w_router, w_exp_gate, w_exp_up, w_exp_down, norm_mix, norm_ffn, eblk)
        mod3 = mod_all[l].reshape(mod_rows * 6, 1, D)

        p_l = _inproj_call(x_l, lp["norm_mix"], mod3, row_l, w_p, l, tm_l)
        p_c = _inproj_call(x_c, lp["norm_mix"], mod3, row_c, w_p, l, tm_c, rwkv_kv_only=last)
        kv_blk_c = (RW_PAD if last else C_AK) // (KVH * HD)

        st_c = _streams_call(p_c, ctx_len, lp, vf_c)
        st_l = _streams_call(p_l, seq_len, lp, vf_l)
        if l == 0:
            vf_c, vf_l = st_c[2], st_l[2]
        state_c, y_c = _scan_call(st_c, lp["k_a"], zero_state, bsz, ctx_len, not last)
        _, y_l = _scan_call(st_l, lp["k_a"], state_c, bsz, seq_len, True)
        br_rwkv_l = _rwkv_out_call(y_l, st_l, lp)

        br_attn_l = _attn_lat_call(p_l, p_c, kv_blk_c, cos, sin, lp["sink"], bsz, seq_len, ctx_len)
        br_conv_l = _conv_call(p_l, seq_len, lp["conv_w"])

        m_l = _merge_call(br_rwkv_l, br_conv_l, br_attn_l, p_l, lp["w_branch"])
        x_l = _outproj_call(m_l, lp["w_out"], x_l, mod3, row_l2, tm2)
        x_l = _ffn(x_l, bsz, seq_len, lp, mod3, row_l2, lambda b: b, tm2)

        if not last:
            br_rwkv_c = _rwkv_out_call(y_c, st_c, lp)
            br_attn_c = _attn_ctx_call(p_c, lp["sink"], bsz, ctx_len)
            br_conv_c = _conv_call(p_c, ctx_len, lp["conv_w"])
            m_c = _merge_call(br_rwkv_c, br_conv_c, br_attn_c, p_c, lp["w_branch"])
            x_c = _outproj_call(m_c, lp["w_out"], x_c, mod3, row_c, tm2)
            x_c = _ffn(x_c, bsz, ctx_len, lp, mod3, row_c, lambda b: bsz, tm2)

    return _final_norm_call(x_l, norm_final).reshape(bsz, seq_len, D)
```

```python
import functools

import jax
import jax.numpy as jnp
from jax import lax
from jax.experimental import pallas as pl
from jax.experimental.pallas import tpu as pltpu

F32 = jnp.float32
BF16 = jnp.bfloat16
HIGHEST = lax.Precision.HIGHEST

D = 2048
HD = 64
BW = 1024
NH = BW // HD
KVH = 4
GQ = NH // KVH
DECAY_RANK = 96
ICLR_RANK = 96
GATE_RANK = 256
VRES_RANK = 64
GN_EPS = 64e-5
NORM_EPS = 1e-6
WINDOW = 128
BLK = 128
assert WINDOW == BLK
GRID_W = 64
ROPE_BASE = 10000.0
NEG_INF = -1e30
N_EXPERTS = 16
EXPERT_FF = 2048
CAPACITY_FACTOR = 2
CHUNK = 64

C_R, C_K, C_V = 0, 1024, 2048
C_WD = (3072, 3200)
C_AD = (3328, 3456)
C_GD = 3584
RW_PAD = 4096
C_Q = 4096
C_CONV = 5120
C_GATE = 8192
C_AK = 14336
C_AV = 14592
NP = 14848
TN_IN = 512

VMEM_LIMIT = 56 * 1024 * 1024


def _cp(*sem, vmem=VMEM_LIMIT):
    return pltpu.CompilerParams(dimension_semantics=tuple(sem), vmem_limit_bytes=vmem)


def _dot(a, b):
    return jnp.dot(a, b, preferred_element_type=F32)


def _dot_nt(a, b):
    return lax.dot_general(a, b, (((1,), (1,)), ((), ())), preferred_element_type=F32)


def _dot_tn(a, b):
    return lax.dot_general(a, b, (((0,), (0,)), ((), ())), preferred_element_type=F32)


def _sigmoid(x):
    return 1.0 / (1.0 + jnp.exp(-x))


def _seg_sum(x, e_ref):
    e = e_ref[...]

    def split(z):
        hi = z.astype(BF16)
        return hi, (z - hi.astype(F32)).astype(BF16)

    hi, lo = split(x)
    hi2, lo2 = split(_dot(hi, e) + _dot(lo, e))
    return _dot_nt(hi2, e) + _dot_nt(lo2, e)


def _mod_kernel(c_ref, w_ref, b_ref, o_ref):
    c = c_ref[...]
    sc = c * _sigmoid(c)
    o_ref[0] = _dot(sc.astype(BF16), w_ref[0].astype(BF16)) + b_ref[0]


def _mod_call(cc, w_mod, b_mod):
    depth, _, n6 = w_mod.shape
    rows = cc.shape[0]
    tn = 1024
    return pl.pallas_call(
        _mod_kernel,
        grid=(depth, n6 // tn),
        in_specs=[pl.BlockSpec((rows, D), lambda l, j: (0, 0)),
                  pl.BlockSpec((1, D, tn), lambda l, j: (l, 0, j)),
                  pl.BlockSpec((1, 1, tn), lambda l, j: (l, 0, j))],
        out_specs=pl.BlockSpec((1, rows, tn), lambda l, j: (l, 0, j)),
        out_shape=jax.ShapeDtypeStruct((depth, rows, n6), F32),
        compiler_params=_cp("arbitrary", "arbitrary"),
        name="mod_proj",
    )(cc, w_mod, b_mod.reshape(depth, 1, n6))


def _norm_mod(x, gain, shift, scale):
    ms = jnp.mean(x * x, axis=-1, keepdims=True)
    y = x * lax.rsqrt(ms + NORM_EPS)
    return (y * gain) * (1.0 + scale) + shift


def _inproj_kernel(x_ref, gain_ref, shift_ref, scale_ref, w_ref, o_ref, h_ref):
    @pl.when(pl.program_id(1) == 0)
    def _():
        h_ref[...] = _norm_mod(x_ref[...], gain_ref[...], shift_ref[0], scale_ref[0]).astype(BF16)

    o_ref[...] = _dot(h_ref[...], w_ref[0]).astype(BF16)


def _inproj_call(x2, gain, mod3, mod_row, w_p, layer, tm, rwkv_kv_only=False):
    n_tok = x2.shape[0]
    n_rw = RW_PAD // TN_IN
    if rwkv_kv_only:
        ncols = RW_PAD + NP - C_AK
        col = lambda j: jnp.where(j < n_rw, j, j - n_rw + C_AK // TN_IN)
    else:
        ncols = NP
        col = lambda j: j
    return pl.pallas_call(
        _inproj_kernel,
        grid=(n_tok // tm, ncols // TN_IN),
        in_specs=[pl.BlockSpec((tm, D), lambda i, j: (i, 0)),
                  pl.BlockSpec((1, D), lambda i, j: (0, 0)),
                  pl.BlockSpec((1, 1, D), lambda i, j: (mod_row(i) * 6 + 0, 0, 0)),
                  pl.BlockSpec((1, 1, D), lambda i, j: (mod_row(i) * 6 + 1, 0, 0)),
                  pl.BlockSpec((1, D, TN_IN), lambda i, j: (layer, 0, col(j)))],
        out_specs=pl.BlockSpec((tm, TN_IN), lambda i, j: (i, j)),
        out_shape=jax.ShapeDtypeStruct((n_tok, ncols), BF16),
        scratch_shapes=[pltpu.VMEM((tm, D), BF16)],
        compiler_params=_cp("arbitrary", "arbitrary"),
        name="in_proj",
    )(x2, gain.reshape(1, D), mod3, mod3, w_p)


HALO = 16


def _shifted(cur, prev_blk, next_blk, first, last):
    tt = cur.shape[0]
    row = lax.broadcasted_iota(jnp.int32, cur.shape, 0)
    p_row = jnp.where(first, 0.0, prev_blk[HALO - 1:HALO, :].astype(F32))
    n_row = jnp.where(last, 0.0, next_blk[0:1, :].astype(F32))
    prev = jnp.where(row == 0, p_row, pltpu.roll(cur, 1, axis=0))
    nxt = jnp.where(row == tt - 1, n_row, pltpu.roll(cur, tt - 1, axis=0))
    return prev, nxt


def _halo_specs(tt, width, col_blk, n_tok):
    nb = n_tok // HALO
    r = tt // HALO
    return [pl.BlockSpec((tt, width), lambda i: (i, col_blk)),
            pl.BlockSpec((HALO, width), lambda i: (jnp.maximum(i * r - 1, 0), col_blk)),
            pl.BlockSpec((HALO, width), lambda i: (jnp.minimum((i + 1) * r, nb - 1), col_blk))]


def _streams_kernel(*refs, tiles_per_seq, has_vres):
    if has_vres:
        (p_ref, pp_ref, pn_ref, mu_ref, dup_ref, dbias_ref, iup_ref, ibias_ref, gup_ref, kk_ref_, e_ref,
         vd_ref, vu_ref, vb_ref, vf_ref,
         r_o, k_o, v_o, kk_o, g_o, a_o, lw_o) = refs
    else:
        (p_ref, pp_ref, pn_ref, mu_ref, dup_ref, dbias_ref, iup_ref, ibias_ref, gup_ref, kk_ref_, e_ref,
         r_o, k_o, v_o, kk_o, g_o, a_o, lw_o) = refs
    j = pl.program_id(0) % tiles_per_seq
    cur_b = p_ref[...]
    tt = cur_b.shape[0]
    ext = jnp.concatenate([pp_ref[...], cur_b, pn_ref[...]], axis=0)
    t_i = lax.broadcasted_iota(jnp.int32, (tt, tt + 2 * HALO), 0)
    e_i = lax.broadcasted_iota(jnp.int32, (tt, tt + 2 * HALO), 1) - HALO
    tap = ((e_i == t_i - 1) & ((e_i >= 0) | (j > 0))) | ((e_i == t_i + 1) & ((e_i < tt) | (j < tiles_per_seq - 1)))
    avg = _dot(jnp.where(tap, 0.5, 0.0).astype(BF16), ext)
    cur = cur_b.astype(F32)
    ps = cur + mu_ref[...] * (avg - cur)
    r = ps[:, C_R:C_R + BW]
    k = ps[:, C_K:C_K + BW]
    v = ps[:, C_V:C_V + BW]
    gd = ps[:, C_GD:C_GD + GATE_RANK]
    if has_vres:
        low = _dot(v.astype(BF16), vd_ref[...])
        mix = _sigmoid(vb_ref[...] + _dot(low.astype(BF16), vu_ref[...]))
        v = v + (vf_ref[...].astype(F32) - v) * mix
    for d in range(2):
        wd = ps[:, C_WD[d]:C_WD[d] + 128]
        ad = ps[:, C_AD[d]:C_AD[d] + 128]
        w_logit = dbias_ref[d] + _dot(jnp.tanh(wd).astype(BF16), dup_ref[d])
        lw_o[d] = -jnp.exp(-0.5) * _sigmoid(w_logit)
        a_o[d] = _sigmoid(ibias_ref[d] + _dot(ad.astype(BF16), iup_ref[d])).astype(BF16)
    kh = k * kk_ref_[...]
    ss = _seg_sum(kh * kh, e_ref)
    kk = kh * lax.rsqrt(jnp.maximum(ss, 1e-24))
    r_o[...] = r.astype(BF16)
    k_o[...] = k.astype(BF16)
    v_o[...] = v.astype(BF16)
    kk_o[...] = kk.astype(BF16)
    g_o[...] = _dot(_sigmoid(gd).astype(BF16), gup_ref[...]).astype(BF16)


def _streams_call(p, seq_len, lp, v_first):
    n_tok = p.shape[0]
    tt = 256
    has_vres = v_first is not None
    full = lambda *s: pl.BlockSpec(s, lambda i: (0,) * len(s))
    tok = pl.BlockSpec((tt, BW), lambda i: (i, 0))
    tok2 = pl.BlockSpec((2, tt, BW), lambda i: (0, i, 0))
    in_specs = _halo_specs(tt, RW_PAD, 0, n_tok) + [
        full(1, RW_PAD), full(2, 128, BW), full(2, 1, BW), full(2, 128, BW), full(2, 1, BW),
        full(GATE_RANK, BW), full(1, BW), full(BW, 128)]
    args = [p, p, p, lp["mu"], lp["decay_up"], lp["decay_bias"], lp["iclr_up"], lp["iclr_bias"],
            lp["gate_up"], lp["k_k"], lp["eblk"]]
    if has_vres:
        in_specs += [full(BW, 128), full(128, BW), full(1, BW), tok]
        args += [lp["vres_down"], lp["vres_up"], lp["vres_bias"], v_first]
    sd = lambda dt: jax.ShapeDtypeStruct((n_tok, BW), dt)
    sd2 = lambda dt: jax.ShapeDtypeStruct((2, n_tok, BW), dt)
    return pl.pallas_call(
        functools.partial(_streams_kernel, tiles_per_seq=seq_len // tt, has_vres=has_vres),
        grid=(n_tok // tt,),
        in_specs=in_specs,
        out_specs=[tok, tok, tok, tok, tok, tok2, tok2],
        out_shape=[sd(BF16), sd(BF16), sd(BF16), sd(BF16), sd(BF16), sd2(BF16), sd2(F32)],
        compiler_params=_cp("arbitrary"),
        name="rwkv_streams",
    )(*args)


def _scan_prologue(d, r_ref, k_ref, v_ref, kk_ref, a_ref, lw_ref, ka_ref):
    c = CHUNK
    ri = lax.broadcasted_iota(jnp.int32, (c, 2 * c), 0)
    ci = lax.broadcasted_iota(jnp.int32, (c, 2 * c), 1) % c
    diff = (ci - ri) if d else (ri - ci)
    incl = diff >= 0
    strict = diff > 0
    lw = lw_ref[0]
    tri = incl[:, :c].astype(BF16)
    lw_hi = lw.astype(BF16)
    lw_lo = (lw - lw_hi.astype(F32)).astype(BF16)
    b = _dot(tri, lw_hi) + _dot(tri, lw_lo)
    b_tot = b[0:1, :] if d else b[c - 1:c, :]
    a = a_ref[0].astype(F32)
    kk = kk_ref[...].astype(F32)
    kd = k_ref[...].astype(F32) * (1.0 + (a - 1.0) * ka_ref[...])
    kka = kk * a
    enb = jnp.exp(-b)
    etail = jnp.exp(b_tot - b)
    return dict(
        incl=incl, strict=strict,
        rt=(r_ref[...].astype(F32) * jnp.exp(b)).astype(BF16),
        bt=(-kk * jnp.exp(b - lw)).astype(BF16),
        kt=(kd * enb).astype(BF16), at=(kka * enb).astype(BF16),
        kh=(kd * etail).astype(BF16), ah=(kka * etail).astype(BF16),
        vv=v_ref[...], e_tot=jnp.exp(b_tot))


def _scan_kernel(*refs, with_output):
    ins = refs[:14]
    if with_output:
        yf_ref, yb_ref, sout_ref, st_ref = refs[14:]
    else:
        sout_ref, st_ref = refs[14:]
    ka_ref, s0_ref = ins[12], ins[13]
    s = pl.program_id(1)
    c = CHUNK

    @pl.when(s == 0)
    def _():
        st_ref[...] = s0_ref[:, 0]

    pro = [_scan_prologue(d, *ins[6 * d:6 * d + 6], ka_ref) for d in range(2)]
    st_all = [st_ref[d] for d in range(2)]

    npair = NH // 2
    units = [(d, slice(p * 2 * HD, (p + 1) * 2 * HD)) for d in range(2) for p in range(npair)]
    us = range(len(units))
    col = lambda name: [pro[d][name][:, sl] for d, sl in units]
    even1 = lax.broadcasted_iota(jnp.int32, (c, 2 * HD), 1) < HD

    def bd(z):
        zero = jnp.zeros_like(z)
        return jnp.concatenate([jnp.where(even1, z, zero), jnp.where(even1, zero, z)], axis=0)

    strict = [pro[d]["strict"] for d, _ in units]
    incl = [pro[d]["incl"] for d, _ in units]
    st = [st_all[d][:, sl] for d, sl in units]
    vbd = [bd(z) for z in col("vv")]
    rb = [jnp.concatenate(p, axis=0) for p in zip(col("rt"), col("bt"))]
    gk = [_dot_nt(rb[u], bd(z)) for u, z in zip(us, col("kt"))]
    ga = [_dot_nt(rb[u], bd(z)) for u, z in zip(us, col("at"))]
    rbs = [_dot_nt(rb[u], bd(st[u].astype(BF16))) for u in us]
    x = [jnp.where(strict[u], ga[u][c:], 0.0).astype(BF16) for u in us]
    a_bk = [jnp.where(strict[u], gk[u][c:], 0.0) for u in us]
    if with_output:
        a_rk = [jnp.where(incl[u], gk[u][:c], 0.0) for u in us]
        a_ra = [jnp.where(incl[u], ga[u][:c], 0.0).astype(BF16) for u in us]
        av = [_dot(jnp.concatenate([a_rk[u], a_bk[u]], axis=0).astype(BF16), vbd[u]) for u in us]
        w = [rbs[u][c:] + av[u][c:] for u in us]
    else:
        w = [rbs[u][c:] + _dot(a_bk[u].astype(BF16), vbd[u]) for u in us]
    for rnd in range(6):
        if rnd < 5:
            m = [_dot(x[u], jnp.concatenate([bd(x[u]), bd(w[u].astype(BF16))], axis=1)) for u in us]
            x = [m[u][:, :2 * c].astype(BF16) for u in us]
            w = [w[u] + m[u][:, 2 * c:] for u in us]
        else:
            w = [w[u] + _dot(x[u], bd(w[u].astype(BF16))) for u in us]
    ub = [w[u].astype(BF16) for u in us]
    if with_output:
        ys = [rbs[u][:c] + av[u][:c] + _dot(a_ra[u], bd(ub[u])) for u in us]
        yf_ref[...] = jnp.concatenate(ys[:npair], axis=1)
        yb_ref[...] = jnp.concatenate(ys[npair:], axis=1)
    full = [_dot_tn(jnp.concatenate([z, ub[u]], axis=0), jnp.concatenate(p, axis=0))
            for u, z, p in zip(us, col("vv"), zip(col("kh"), col("ah")))]
    upd = [jnp.where(even1, f[:HD], f[HD:]) for f in full]
    for d in range(2):
        st_ref[d] = st_all[d] * pro[d]["e_tot"] + jnp.concatenate(upd[d * npair:(d + 1) * npair], axis=1)

    @pl.when(s == pl.num_programs(1) - 1)
    def _():
        sout_ref[:, 0] = st_ref[...]


def _scan_call(streams, k_a, s0, bsz, seq_len, with_output):
    r, k, v, kk, _, a2, lw2 = streams
    n_tok = r.shape[0]
    nc = seq_len // CHUNK

    rows = (lambda b, s: b * nc + s, lambda b, s: b * nc + nc - 1 - s)
    tok = [pl.BlockSpec((CHUNK, BW), lambda b, s, f=f: (f(b, s), 0)) for f in rows]
    tok2 = [pl.BlockSpec((1, CHUNK, BW), lambda b, s, f=f, d=d: (d, f(b, s), 0)) for d, f in enumerate(rows)]
    st_spec = pl.BlockSpec((2, 1, HD, BW), lambda b, s: (0, b, 0, 0))
    in_specs, args = [], []
    for d in range(2):
        in_specs += [tok[d]] * 4 + [tok2[d]] * 2
        args += [r, k, v, kk, a2, lw2]
    in_specs += [pl.BlockSpec((1, BW), lambda b, s: (0, 0)), st_spec]
    out_specs = [st_spec]
    out_shape = [jax.ShapeDtypeStruct((2, bsz, HD, BW), F32)]
    if with_output:
        out_specs = tok + out_specs
        out_shape = [jax.ShapeDtypeStruct((n_tok, BW), F32)] * 2 + out_shape
    res = pl.pallas_call(
        functools.partial(_scan_kernel, with_output=with_output),
        grid=(bsz, nc),
        in_specs=in_specs,
        out_specs=out_specs,
        out_shape=out_shape,
        scratch_shapes=[pltpu.VMEM((2, HD, BW), F32)],
        compiler_params=_cp("arbitrary", "arbitrary"),
        name="rwkv_scan",
    )(*args, k_a, s0)
    if with_output:
        return res[2], (res[0], res[1])
    return res[0], None


def _rwkv_out_kernel(yf_ref, yb_ref, r_ref, k_ref, v_ref, a_ref, g_ref, ka_ref, rk_ref, gnw_ref, gnb_ref, e_ref,
                     o_ref):
    y = yf_ref[...] + yb_ref[...]
    mean = _seg_sum(y, e_ref) * (1.0 / HD)
    yc = y - mean
    var = _seg_sum(yc * yc, e_ref) * (1.0 / HD)
    yn = yc * lax.rsqrt(var + GN_EPS) * gnw_ref[...] + gnb_ref[...]
    r = r_ref[...].astype(F32)
    k = k_ref[...].astype(F32)
    asum = a_ref[0].astype(F32) + a_ref[1].astype(F32)
    kd_sum = k * (2.0 + (asum - 2.0) * ka_ref[...])
    bonus = _seg_sum(r * kd_sum * rk_ref[...], e_ref) * v_ref[...].astype(F32)
    o_ref[...] = ((yn + bonus) * g_ref[...].astype(F32)).astype(BF16)


def _rwkv_out_call(y2, streams, lp):
    r, k, v, _, g, a2, _ = streams
    n_tok = r.shape[0]
    tt = 256
    tok = pl.BlockSpec((tt, BW), lambda i: (i, 0))
    tok2 = pl.BlockSpec((2, tt, BW), lambda i: (0, i, 0))
    vec = pl.BlockSpec((1, BW), lambda i: (0, 0))
    return pl.pallas_call(
        _rwkv_out_kernel,
        grid=(n_tok // tt,),
        in_specs=[tok, tok, tok, tok, tok, tok2, tok, vec, vec, vec, vec, pl.BlockSpec((BW, 128), lambda i: (0, 0))],
        out_specs=tok,
        out_shape=jax.ShapeDtypeStruct((n_tok, BW), BF16),
        compiler_params=_cp("arbitrary"),
        name="rwkv_out",
    )(y2[0], y2[1], r, k, v, a2, g, lp["k_a"], lp["r_k"], lp["gn_w"], lp["gn_b"], lp["eblk"])


def _conv_kernel(b_ref, c_ref, cp_ref, cn_ref, u_ref, up_ref, un_ref, w_ref, o_ref, *, tiles_per_seq):
    j = pl.program_id(0) % tiles_per_seq
    cu = c_ref[...].astype(F32) * u_ref[...].astype(F32)
    cu_p = cp_ref[...].astype(F32) * up_ref[...].astype(F32)
    cu_n = cn_ref[...].astype(F32) * un_ref[...].astype(F32)
    prev, nxt = _shifted(cu, cu_p, cu_n, j == 0, j == tiles_per_seq - 1)
    w = w_ref[...]
    conv = w[0:1] * prev + w[1:2] * cu + w[2:3] * nxt
    o_ref[...] = (b_ref[...].astype(F32) * conv).astype(BF16)


def _conv_call(p, seq_len, conv_w):
    n_tok = p.shape[0]
    tt = 256
    cb = C_CONV // BW
    return pl.pallas_call(
        functools.partial(_conv_kernel, tiles_per_seq=seq_len // tt),
        grid=(n_tok // tt,),
        in_specs=[pl.BlockSpec((tt, BW), lambda i: (i, cb))] + _halo_specs(tt, BW, cb + 1, n_tok)
        + _halo_specs(tt, BW, cb + 2, n_tok) + [pl.BlockSpec((3, BW), lambda i: (0, 0))],
        out_specs=pl.BlockSpec((tt, BW), lambda i: (i, 0)),
        out_shape=jax.ShapeDtypeStruct((n_tok, BW), BF16),
        compiler_params=_cp("arbitrary"),
        name="short_conv",
    )(p, p, p, p, p, p, p, conv_w)


def _rope(x, cos, sin):
    w = x.shape[1]
    lane = lax.broadcasted_iota(jnp.int32, x.shape, 1)
    partner = jnp.where((lane % 32) < 16, pltpu.roll(x, w - 16, axis=1), pltpu.roll(x, 16, axis=1))
    return x * cos + partner * sin


LOG2E = 1.4426950408889634
Q_SCALE = HD ** -0.5 * LOG2E


def _softmax_pv(s2, sink2_col, v):
    m = jnp.maximum(jnp.max(s2, axis=-1, keepdims=True), sink2_col)
    p = jnp.exp2(s2 - m)
    den = jnp.sum(p, axis=-1, keepdims=True) + jnp.exp2(sink2_col - m)
    return _dot(p.astype(BF16), v) / den


def _attn_lat_kernel(q_ref, kp_ref, kc_ref, kn_ref, vp_ref, vc_ref, vn_ref, kx_ref, vx_ref,
                     cos_ref, cosp_ref, cosn_ref, sin_ref, sinp_ref, sinn_ref, sink_ref, o_ref, *, nblk):
    n = pl.program_id(1)
    kvw = KVH * HD
    q = (_rope(q_ref[...].astype(F32), cos_ref[...], sin_ref[...]) * Q_SCALE).astype(BF16)
    kb = jnp.concatenate([
        _rope(kp_ref[...].astype(F32), cosp_ref[:, :kvw], sinp_ref[:, :kvw]),
        _rope(kc_ref[...].astype(F32), cos_ref[:, :kvw], sin_ref[:, :kvw]),
        _rope(kn_ref[...].astype(F32), cosn_ref[:, :kvw], sinn_ref[:, :kvw])], axis=0).astype(BF16)
    k_all = jnp.concatenate([kb, kx_ref[...]], axis=0)
    v_all = jnp.concatenate([vp_ref[...], vc_ref[...], vn_ref[...], vx_ref[...]], axis=0)
    cols = GQ * BLK
    kidx = lax.broadcasted_iota(jnp.int32, (BLK, cols), 0)
    qpos = lax.broadcasted_iota(jnp.int32, (BLK, cols), 1) % BLK
    bias_p = jnp.where((kidx >= qpos) & (n > 0), 0.0, NEG_INF)
    bias_n = jnp.where((kidx <= qpos) & (n < nblk - 1), 0.0, NEG_INF)
    sink = sink_ref[...] * LOG2E
    gs = range(KVH)
    even = lax.broadcasted_iota(jnp.int32, (BLK, 2 * HD), 1) < HD
    zero = jnp.zeros((BLK, 2 * HD), BF16)
    eye = (lax.broadcasted_iota(jnp.int32, (2 * HD, 2 * HD), 0)
           == lax.broadcasted_iota(jnp.int32, (2 * HD, 2 * HD), 1)).astype(BF16)

    def head_rows(gi):
        parts = []
        for t in range(GQ // 2):
            qp = q[:, (gi * GQ + 2 * t) * HD:(gi * GQ + 2 * t + 2) * HD]
            parts += [jnp.where(even, qp, zero), jnp.where(even, zero, qp)]
        return jnp.concatenate(parts, axis=0)

    qg = [head_rows(gi) for gi in gs]
    kd = [jnp.concatenate([k_all[:, gi * HD:(gi + 1) * HD]] * 2, axis=1) for gi in gs]
    vt = [_dot_nt(eye[:HD, :HD], v_all[:, gi * HD:(gi + 1) * HD]).astype(BF16) for gi in gs]
    sk = [jnp.concatenate([jnp.broadcast_to(sink[:, gi * GQ + t:gi * GQ + t + 1], (1, BLK))
                           for t in range(GQ)], axis=1) for gi in gs]
    s = [_dot_nt(kd[gi], qg[gi]) for gi in gs]
    s = [jnp.concatenate([z[:BLK] + bias_p, z[BLK:2 * BLK], z[2 * BLK:3 * BLK] + bias_n, z[3 * BLK:]], axis=0)
         for z in s]
    m = [jnp.maximum(jnp.max(s[gi], axis=0, keepdims=True), sk[gi]) for gi in gs]
    p = [jnp.exp2(s[gi] - m[gi]) for gi in gs]
    den = [jnp.sum(p[gi], axis=0, keepdims=True) + jnp.exp2(sk[gi] - m[gi]) for gi in gs]
    og = [(_dot(vt[gi], p[gi].astype(BF16)) / den[gi]).astype(BF16) for gi in gs]
    o_ref[...] = jnp.concatenate(
        [_dot_nt(eye, jnp.concatenate([og[gi][:, 2 * t * BLK:(2 * t + 1) * BLK],
                                       og[gi][:, (2 * t + 1) * BLK:(2 * t + 2) * BLK]], axis=0))
         for gi in gs for t in range(GQ // 2)], axis=1).astype(BF16)


def _attn_lat_call(p_l, p_c, kv_blk_c, cos, sin, sink, bsz, seq_len, ctx_len):
    nblk = seq_len // BLK
    kvw = KVH * HD
    kb, vb = C_AK // kvw, C_AV // kvw
    rowq = lambda b, n: b * nblk + n
    rowp = lambda b, n: b * nblk + jnp.maximum(n - 1, 0)
    rown = lambda b, n: b * nblk + jnp.minimum(n + 1, nblk - 1)
    tabp = lambda b, n: (jnp.maximum(n - 1, 0), 0)
    tabn = lambda b, n: (jnp.minimum(n + 1, nblk - 1), 0)
    kv = lambda rf, cb: pl.BlockSpec((BLK, kvw), lambda b, n: (rf(b, n), cb))
    tab = lambda f: pl.BlockSpec((BLK, BW), f)
    return pl.pallas_call(
        functools.partial(_attn_lat_kernel, nblk=nblk),
        grid=(bsz, nblk),
        in_specs=[pl.BlockSpec((BLK, BW), lambda b, n: (rowq(b, n), C_Q // BW)),
                  kv(rowp, kb), kv(rowq, kb), kv(rown, kb), kv(rowp, vb), kv(rowq, vb), kv(rown, vb),
                  pl.BlockSpec((ctx_len, kvw), lambda b, n: (b, kv_blk_c)),
                  pl.BlockSpec((ctx_len, kvw), lambda b, n: (b, kv_blk_c + 1)),
                  tab(lambda b, n: (n, 0)), tab(tabp), tab(tabn),
                  tab(lambda b, n: (n, 0)), tab(tabp), tab(tabn),
                  pl.BlockSpec((1, NH), lambda b, n: (0, 0))],
        out_specs=pl.BlockSpec((BLK, BW), lambda b, n: (rowq(b, n), 0)),
        out_shape=jax.ShapeDtypeStruct((bsz * seq_len, BW), BF16),
        compiler_params=_cp("arbitrary", "arbitrary"),
        name="attn_latent",
    )(p_l, p_l, p_l, p_l, p_l, p_l, p_l, p_c, p_c, cos, cos, cos, sin, sin, sin, sink)


def _attn_ctx_kernel(q_ref, kx_ref, vx_ref, sink_ref, o_ref):
    q = (q_ref[...].astype(F32) * Q_SCALE).astype(BF16)
    kx = kx_ref[...]
    vx = vx_ref[...]
    sink = sink_ref[...] * LOG2E
    tq = q.shape[0]
    outs = []
    for gi in range(KVH):
        qg = jnp.concatenate([q[:, (gi * GQ + t) * HD:(gi * GQ + t + 1) * HD] for t in range(GQ)], axis=0)
        sk = jnp.concatenate([jnp.broadcast_to(sink[:, gi * GQ + t:gi * GQ + t + 1], (tq, 1))
                              for t in range(GQ)], axis=0)
        s = _dot_nt(qg, kx[:, gi * HD:(gi + 1) * HD])
        og = _softmax_pv(s, sk, vx[:, gi * HD:(gi + 1) * HD])
        outs += [og[t * tq:(t + 1) * tq] for t in range(GQ)]
    o_ref[...] = jnp.concatenate(outs, axis=1).astype(BF16)


def _attn_ctx_call(p_c, sink, bsz, ctx_len):
    kvw = KVH * HD
    tq = 128
    nq = ctx_len // tq
    return pl.pallas_call(
        _attn_ctx_kernel,
        grid=(bsz, nq),
        in_specs=[pl.BlockSpec((tq, BW), lambda b, n: (b * nq + n, C_Q // BW)),
                  pl.BlockSpec((ctx_len, kvw), lambda b, n: (b, C_AK // kvw)),
                  pl.BlockSpec((ctx_len, kvw), lambda b, n: (b, C_AV // kvw)),
                  pl.BlockSpec((1, NH), lambda b, n: (0, 0))],
        out_specs=pl.BlockSpec((tq, BW), lambda b, n: (b * nq + n, 0)),
        out_shape=jax.ShapeDtypeStruct((bsz * ctx_len, BW), BF16),
        compiler_params=_cp("arbitrary", "arbitrary"),
        name="attn_context",
    )(p_c, p_c, p_c, sink)


def _merge_kernel(b0_ref, b1_ref, b2_ref, g0_ref, g1_ref, g2_ref, w_ref, o_ref):
    acc = _sigmoid(g0_ref[...].astype(F32)) * _dot(b0_ref[...], w_ref[0])
    acc += _sigmoid(g1_ref[...].astype(F32)) * _dot(b1_ref[...], w_ref[1])
    acc += _sigmoid(g2_ref[...].astype(F32)) * _dot(b2_ref[...], w_ref[2])
    o_ref[...] = acc.astype(BF16)


def _merge_call(br_rwkv, br_conv, br_attn, p, w_branch):
    n_tok = p.shape[0]
    tm, tn = 512, 1024
    nn = D // tn
    br = pl.BlockSpec((tm, BW), lambda j, i: (i, 0))
    gate = lambda t: pl.BlockSpec((tm, tn), lambda j, i: (i, (C_GATE + t * D) // tn + j))
    return pl.pallas_call(
        _merge_kernel,
        grid=(nn, n_tok // tm),
        in_specs=[br, br, br, gate(0), gate(1), gate(2), pl.BlockSpec((3, BW, tn), lambda j, i: (0, 0, j))],
        out_specs=pl.BlockSpec((tm, tn), lambda j, i: (i, j)),
        out_shape=jax.ShapeDtypeStruct((n_tok, D), BF16),
        compiler_params=_cp("arbitrary", "arbitrary"),
        name="merge_branches",
    )(br_rwkv, br_conv, br_attn, p, p, p, w_branch)


def _outproj_kernel(m_ref, w_ref, x_ref, gate_ref, o_ref):
    o_ref[...] = x_ref[...] + gate_ref[0] * _dot(m_ref[...], w_ref[...])


def _outproj_call(m, w_out, x2, mod3, mod_row, tm):
    n_tok = x2.shape[0]
    return pl.pallas_call(
        _outproj_kernel,
        grid=(n_tok // tm,),
        in_specs=[pl.BlockSpec((tm, D), lambda i: (i, 0)),
                  pl.BlockSpec((D, D), lambda i: (0, 0)),
                  pl.BlockSpec((tm, D), lambda i: (i, 0)),
                  pl.BlockSpec((1, 1, D), lambda i: (mod_row(i) * 6 + 2, 0, 0))],
        out_specs=pl.BlockSpec((tm, D), lambda i: (i, 0)),
        out_shape=jax.ShapeDtypeStruct((n_tok, D), F32),
        compiler_params=_cp("arbitrary"),
        name="out_proj",
    )(m, w_out, x2, mod3)


def _ffn_prep_kernel(x_ref, gain_ref, shift_ref, scale_ref, wr_ref, h_ref, aff_ref):
    h = _norm_mod(x_ref[...], gain_ref[...], shift_ref[0], scale_ref[0])
    h_hi = h.astype(BF16)
    h_ref[...] = h_hi
    h_lo = (h - h_hi.astype(F32)).astype(BF16)
    wr = wr_ref[...]
    w_hi = wr.astype(BF16)
    w_lo = (wr - w_hi.astype(F32)).astype(BF16)
    logits = _dot(h_hi, w_hi) + (_dot(h_hi, w_lo) + _dot(h_lo, w_hi))
    lane = lax.broadcasted_iota(jnp.int32, logits.shape, 1)
    logits = jnp.where(lane < N_EXPERTS, logits, NEG_INF)
    m = jnp.max(logits, axis=-1, keepdims=True)
    e = jnp.exp(logits - m)
    aff_ref[...] = e / jnp.sum(e, axis=-1, keepdims=True)


def _ffn_prep_call(x2, gain, mod3, mod_row, wr_pad, tm):
    n_tok = x2.shape[0]
    return pl.pallas_call(
        _ffn_prep_kernel,
        grid=(n_tok // tm,),
        in_specs=[pl.BlockSpec((tm, D), lambda i: (i, 0)),
                  pl.BlockSpec((1, D), lambda i: (0, 0)),
                  pl.BlockSpec((1, 1, D), lambda i: (mod_row(i) * 6 + 3, 0, 0)),
                  pl.BlockSpec((1, 1, D), lambda i: (mod_row(i) * 6 + 4, 0, 0)),
                  pl.BlockSpec((D, 128), lambda i: (0, 0))],
        out_specs=[pl.BlockSpec((tm, D), lambda i: (i, 0)), pl.BlockSpec((tm, 128), lambda i: (i, 0))],
        out_shape=[jax.ShapeDtypeStruct((n_tok, D), BF16), jax.ShapeDtypeStruct((n_tok, 128), F32)],
        compiler_params=_cp("arbitrary"),
        name="ffn_prep",
    )(x2, gain.reshape(1, D), mod3, mod3, wr_pad)


def _select_kernel(aff_ref, slot_ref, *, cap):
    a = aff_ref[0]
    n = a.shape[1]
    bits = lax.bitcast_convert_type(a, jnp.int32)

    def body(i, t):
        cand = t | jnp.left_shift(jnp.int32(1), 30 - i)
        cnt = jnp.sum((bits >= cand).astype(jnp.int32), axis=-1, keepdims=True)
        return jnp.where(cnt >= cap, cand, t)

    thr = lax.fori_loop(0, 31, body, jnp.zeros((a.shape[0], 1), jnp.int32))
    gt = bits > thr
    eq = bits == thr
    n_gt = jnp.sum(gt.astype(jnp.int32), axis=-1, keepdims=True)
    tc = min(n, 512)

    def prefix(mask_bf16):
        cols = []
        for j0 in range(0, n, tc):
            ri = lax.broadcasted_iota(jnp.int32, (n, tc), 0)
            ci = lax.broadcasted_iota(jnp.int32, (n, tc), 1) + j0
            cols.append(_dot(mask_bf16, (ri <= ci).astype(BF16)))
        return jnp.concatenate(cols, axis=1)

    eq_f = eq.astype(BF16)
    excl_eq = prefix(eq_f) - eq_f.astype(F32)
    sel = gt | (eq & (excl_eq < (cap - n_gt).astype(F32)))
    pos = prefix(sel.astype(BF16))
    slot_ref[0] = jnp.where(sel, pos.astype(jnp.int32) - 1, -1)


def _select_call(aff_t, cap):
    bsz, ne, n = aff_t.shape
    return pl.pallas_call(
        functools.partial(_select_kernel, cap=cap),
        grid=(bsz,),
        in_specs=[pl.BlockSpec((1, ne, n), lambda b: (b, 0, 0))],
        out_specs=pl.BlockSpec((1, ne, n), lambda b: (b, 0, 0)),
        out_shape=jax.ShapeDtypeStruct((bsz, ne, n), jnp.int32),
        compiler_params=_cp("arbitrary"),
        name="expert_select",
    )(aff_t)


def _gather_kernel(slot_ref, aff_ref, h_ref, xs_ref, g_ref, *, cap):
    slot = slot_ref[0]
    n = slot.shape[1]
    onehot = lax.broadcasted_iota(jnp.int32, (cap, n), 0) == slot
    xs_ref[0] = _dot(onehot.astype(BF16), h_ref[...]).astype(BF16)
    g_ref[0] = jnp.sum(jnp.where(onehot, aff_ref[0], 0.0), axis=-1, keepdims=True)


def _gather_call(slot, aff_t, h2, cap):
    bsz, ne, n = slot.shape
    return pl.pallas_call(
        functools.partial(_gather_kernel, cap=cap),
        grid=(bsz, ne),
        in_specs=[pl.BlockSpec((1, 1, n), lambda b, e: (b * ne + e, 0, 0)),
                  pl.BlockSpec((1, 1, n), lambda b, e: (b * ne + e, 0, 0)),
                  pl.BlockSpec((n, D), lambda b, e: (b, 0))],
        out_specs=[pl.BlockSpec((1, cap, D), lambda b, e: (e, b, 0)),
                   pl.BlockSpec((1, cap, 1), lambda b, e: (e, b, 0))],
        out_shape=[jax.ShapeDtypeStruct((ne, bsz * cap, D), BF16),
                   jax.ShapeDtypeStruct((ne, bsz * cap, 1), F32)],
        compiler_params=_cp("arbitrary", "arbitrary"),
        name="expert_gather",
    )(slot.reshape(bsz * ne, 1, n), aff_t.reshape(bsz * ne, 1, n), h2)


def _expert_kernel(x_ref, wg_ref, wu_ref, wd_ref, g_ref, o_ref, acc_ref):
    f = pl.program_id(2)

    @pl.when(f == 0)
    def _():
        acc_ref[...] = jnp.zeros_like(acc_ref)

    x = x_ref[0]
    gate = _dot(x, wg_ref[0, 0].astype(BF16))
    hid = gate * _sigmoid(gate) * _dot(x, wu_ref[0, 0].astype(BF16))
    acc_ref[...] += _dot(hid.astype(BF16), wd_ref[0, 0].astype(BF16))

    @pl.when(f == pl.num_programs(2) - 1)
    def _():
        o_ref[0] = (acc_ref[...] * g_ref[0]).astype(BF16)


def _expert_call(xs, g, wg, wu, wd, layer):
    ne, rows, _ = xs.shape
    tm = min(rows, 1024)
    tf = 256
    return pl.pallas_call(
        _expert_kernel,
        grid=(ne, rows // tm, EXPERT_FF // tf),
        in_specs=[pl.BlockSpec((1, tm, D), lambda e, i, f: (e, i, 0)),
                  pl.BlockSpec((1, 1, D, tf), lambda e, i, f: (layer, e, 0, f)),
                  pl.BlockSpec((1, 1, D, tf), lambda e, i, f: (layer, e, 0, f)),
                  pl.BlockSpec((1, 1, tf, D), lambda e, i, f: (layer, e, f, 0)),
                  pl.BlockSpec((1, tm, 1), lambda e, i, f: (e, i, 0))],
        out_specs=pl.BlockSpec((1, tm, D), lambda e, i, f: (e, i, 0)),
        out_shape=jax.ShapeDtypeStruct((ne, rows, D), BF16),
        scratch_shapes=[pltpu.VMEM((tm, D), F32)],
        compiler_params=_cp("arbitrary", "arbitrary", "arbitrary"),
        name="expert_mlp",
    )(xs, wg, wu, wd, g)


def _scatter_kernel(slot_ref, ys_ref, x_ref, gate_ref, o_ref, *, cap):
    slot_t = slot_ref[0]
    n = slot_t.shape[0]
    lane = lax.broadcasted_iota(jnp.int32, (n, cap), 1)
    acc = jnp.zeros(x_ref.shape, F32)
    for e in range(N_EXPERTS):
        onehot = (slot_t[:, e:e + 1] == lane).astype(BF16)
        acc += _dot(onehot, ys_ref[e])
    o_ref[...] = x_ref[...] + gate_ref[0] * acc


def _scatter_call(slot_t, ys, x2, mod3, mod_row_b, cap):
    bsz, n, ne = slot_t.shape
    tn = 512
    return pl.pallas_call(
        functools.partial(_scatter_kernel, cap=cap),
        grid=(bsz, D // tn),
        in_specs=[pl.BlockSpec((1, n, ne), lambda b, j: (b, 0, 0)),
                  pl.BlockSpec((ne, cap, tn), lambda b, j: (0, b, j)),
                  pl.BlockSpec((n, tn), lambda b, j: (b, j)),
                  pl.BlockSpec((1, 1, tn), lambda b, j: (mod_row_b(b) * 6 + 5, 0, j))],
        out_specs=pl.BlockSpec((n, tn), lambda b, j: (b, j)),
        out_shape=jax.ShapeDtypeStruct(x2.shape, F32),
        compiler_params=_cp("arbitrary", "arbitrary"),
        name="expert_scatter",
    )(slot_t, ys, x2, mod3)


def _ffn(x2, bsz, n, lp, mod3, mod_row, mod_row_b, tm):
    cap = CAPACITY_FACTOR * n // N_EXPERTS
    h2, aff = _ffn_prep_call(x2, lp["norm_ffn"], mod3, mod_row, lp["w_router"], tm)
    aff_t = jnp.swapaxes(aff[:, :N_EXPERTS].reshape(bsz, n, N_EXPERTS), 1, 2)
    slot = _select_call(aff_t, cap)
    xs, g = _gather_call(slot, aff_t, h2, cap)
    ys = _expert_call(xs, g, lp["w_exp_gate"], lp["w_exp_up"], lp["w_exp_down"], lp["layer"])
    return _scatter_call(jnp.swapaxes(slot, 1, 2), ys, x2, mod3, mod_row_b, cap)


def _final_norm_kernel(x_ref, gain_ref, o_ref):
    x = x_ref[...]
    ms = jnp.mean(x * x, axis=-1, keepdims=True)
    o_ref[...] = (x * lax.rsqrt(ms + NORM_EPS)) * gain_ref[...]


def _final_norm_call(x2, gain):
    n_tok = x2.shape[0]
    tm = 512
    return pl.pallas_call(
        _final_norm_kernel,
        grid=(n_tok // tm,),
        in_specs=[pl.BlockSpec((tm, D), lambda i: (i, 0)), pl.BlockSpec((1, D), lambda i: (0, 0))],
        out_specs=pl.BlockSpec((tm, D), lambda i: (i, 0)),
        out_shape=jax.ShapeDtypeStruct(x2.shape, F32),
        compiler_params=_cp("arbitrary"),
        name="final_norm",
    )(x2, gain.reshape(1, D))


def _pad_cols(parts):
    out = []
    for a, width in parts:
        out.append(a)
        if a.shape[-1] < width:
            out.append(jnp.zeros(a.shape[:-1] + (width - a.shape[-1],), a.dtype))
    return jnp.concatenate(out, axis=-1)


def _rwkv_col_parts(w):
    o = 3 * BW
    parts = [(w[..., :o], o)]
    for _ in range(2):
        parts.append((w[..., o:o + DECAY_RANK], 128))
        o += DECAY_RANK
    for _ in range(2):
        parts.append((w[..., o:o + ICLR_RANK], 128))
        o += ICLR_RANK
    parts.append((w[..., o:o + GATE_RANK], RW_PAD - C_GD))
    return parts, o + GATE_RANK


LANE = 128
RW_END = 3 * BW + 2 * DECAY_RANK + 2 * ICLR_RANK + GATE_RANK
CODE_W = RW_PAD - 3 * BW


def _wrelayout_kernel(w_ref, misc_ref, o_ref):
    j = pl.program_id(1)
    is_code = (j >= 3 * BW // LANE) & (j < RW_PAD // LANE)
    o_ref[0] = jnp.where(is_code, misc_ref[0], w_ref[0]).astype(BF16)


def _wrelayout_call(w_in):
    depth = w_in.shape[0]
    code_parts, _ = _rwkv_col_parts(w_in[:, :, :RW_END])
    misc = _pad_cols(code_parts[1:])
    nb = NP // LANE
    kv_src = RW_END // LANE

    def src(j):
        shifted = jnp.where(j >= C_AK // LANE, j - C_AK // LANE + kv_src, j + (RW_END + 2 * KVH * HD - C_Q) // LANE)
        return jnp.where(j < RW_PAD // LANE, jnp.minimum(j, 3 * BW // LANE - 1), shifted)

    def misc_blk(j):
        return jnp.clip(j - 3 * BW // LANE, 0, CODE_W // LANE - 1)

    return pl.pallas_call(
        _wrelayout_kernel,
        grid=(depth, nb),
        in_specs=[pl.BlockSpec((1, D, LANE), lambda l, j: (l, 0, src(j))),
                  pl.BlockSpec((1, D, LANE), lambda l, j: (l, 0, misc_blk(j)))],
        out_specs=pl.BlockSpec((1, D, LANE), lambda l, j: (l, 0, j)),
        out_shape=jax.ShapeDtypeStruct((depth, D, NP), BF16),
        compiler_params=_cp("arbitrary", "arbitrary"),
        name="w_in_relayout",
    )(w_in, misc)


def _prep_layer(l, depth, shift_mu, decay_up, decay_bias, iclr_up, iclr_bias, gate_up, vres_down, vres_up,
                vres_bias, k_k, k_a, r_k, gn_w, gn_b, conv_w, attn_sink, w_branch, w_out, w_router,
                w_exp_gate, w_exp_up, w_exp_down, norm_mix, norm_ffn, eblk):
    mu_parts, _ = _rwkv_col_parts(shift_mu[l][None, :])
    pad_rank = lambda u: jnp.pad(u, ((0, 0), (0, 128 - u.shape[1]), (0, 0))).astype(BF16)
    lp = dict(
        mu=_pad_cols(mu_parts),
        decay_up=pad_rank(decay_up[l]), decay_bias=decay_bias[l].reshape(2, 1, BW),
        iclr_up=pad_rank(iclr_up[l]), iclr_bias=iclr_bias[l].reshape(2, 1, BW),
        gate_up=gate_up[l].astype(BF16), k_k=k_k[l].reshape(1, BW), k_a=k_a[l].reshape(1, BW),
        r_k=r_k[l].reshape(1, BW), gn_w=gn_w[l].reshape(1, BW), gn_b=gn_b[l].reshape(1, BW),
        conv_w=conv_w[l], sink=attn_sink[l].reshape(1, NH),
        w_branch=w_branch[l].astype(BF16), w_out=w_out[l].astype(BF16),
        w_router=jnp.pad(w_router[l], ((0, 0), (0, 128 - N_EXPERTS))),
        layer=l, w_exp_gate=w_exp_gate, w_exp_up=w_exp_up, w_exp_down=w_exp_down,
        norm_mix=norm_mix[l], norm_ffn=norm_ffn[l], eblk=eblk)
    if l > 0:
        lp["vres_down"] = jnp.pad(vres_down[l - 1], ((0, 0), (0, 128 - VRES_RANK))).astype(BF16)
        lp["vres_up"] = jnp.pad(vres_up[l - 1], ((0, 128 - VRES_RANK), (0, 0))).astype(BF16)
        lp["vres_bias"] = vres_bias[l - 1].reshape(1, BW)
    return lp


def _rope_tables(seq_len):
    quarter = HD // 4
    inv = ROPE_BASE ** (-jnp.arange(quarter, dtype=F32) / quarter)
    pos = jnp.arange(seq_len)
    ang_r = (pos // GRID_W).astype(F32)[:, None] * inv[None, :]
    ang_c = (pos % GRID_W).astype(F32)[:, None] * inv[None, :]
    cos = jnp.concatenate([jnp.cos(ang_r)] * 2 + [jnp.cos(ang_c)] * 2, axis=1)
    sin = jnp.concatenate([-jnp.sin(ang_r), jnp.sin(ang_r), -jnp.sin(ang_c), jnp.sin(ang_c)], axis=1)
    return jnp.tile(cos, (1, NH)), jnp.tile(sin, (1, NH))


def kernel(x, c, ctx, c_ctx, w_mod, b_mod, norm_mix, norm_ffn, w_in, shift_mu, decay_up, decay_bias, iclr_up,
           iclr_bias, gate_up, vres_down, vres_up, vres_bias, k_k, k_a, r_k, gn_w, gn_b, conv_w, attn_sink,
           w_branch, w_out, w_router, w_exp_gate, w_exp_up, w_exp_down, norm_final):
    bsz, seq_len, _ = x.shape
    ctx_len = ctx.shape[1]
    depth = w_in.shape[0]
    mod_rows = -(-(bsz + 1) // 8) * 8
    cc = jnp.concatenate([c, c_ctx[None, :], jnp.zeros((mod_rows - bsz - 1, D), F32)], axis=0)
    mod_all = _mod_call(cc, w_mod, b_mod)

    hd_i = jnp.arange(BW) // HD
    eblk = (hd_i[:, None] == jnp.arange(128)[None, :]).astype(BF16)
    cos, sin = _rope_tables(seq_len)

    tm_l = min(1024, seq_len)
    tm_c = min(1024, bsz * ctx_len)
    tiles_per_sample = seq_len // tm_l
    row_l = lambda i: i // tiles_per_sample
    row_c = lambda i: bsz
    tm2 = 512
    row_l2 = lambda i: i // (seq_len // tm2)

    x_l = x.reshape(bsz * seq_len, D)
    x_c = ctx.reshape(bsz * ctx_len, D)
    vf_l = vf_c = None
    zero_state = jnp.zeros((2, bsz, HD, BW), F32)

    w_p = _wrelayout_call(w_in)

    for l in range(depth):
        last = l == depth - 1
        lp = _prep_layer(l, depth, shift_mu, decay_up, decay_bias, iclr_up, iclr_bias, gate_up, vres_down,
                         vres_up, vres_bias, k_k, k_a, r_k, gn_w, gn_b, conv_w, attn_sink, w_branch, w_out,
                         w_router, w_exp_gate, w_exp_up, w_exp_down, norm_mix, norm_ffn, eblk)
        mod3 = mod_all[l].reshape(mod_rows * 6, 1, D)

        p_l = _inproj_call(x_l, lp["norm_mix"], mod3, row_l, w_p, l, tm_l)
        p_c = _inproj_call(x_c, lp["norm_mix"], mod3, row_c, w_p, l, tm_c, rwkv_kv_only=last)
        kv_blk_c = (RW_PAD if last else C_AK) // (KVH * HD)

        st_c = _streams_call(p_c, ctx_len, lp, vf_c)
        st_l = _streams_call(p_l, seq_len, lp, vf_l)
        if l == 0:
            vf_c, vf_l = st_c[2], st_l[2]
        state_c, y_c = _scan_call(st_c, lp["k_a"], zero_state, bsz, ctx_len, not last)
        _, y_l = _scan_call(st_l, lp["k_a"], state_c, bsz, seq_len, True)
        br_rwkv_l = _rwkv_out_call(y_l, st_l, lp)

        br_attn_l = _attn_lat_call(p_l, p_c, kv_blk_c, cos, sin, lp["sink"], bsz, seq_len, ctx_len)
        br_conv_l = _conv_call(p_l, seq_len, lp["conv_w"])

        m_l = _merge_call(br_rwkv_l, br_conv_l, br_attn_l, p_l, lp["w_branch"])
        x_l = _outproj_call(m_l, lp["w_out"], x_l, mod3, row_l2, tm2)
        x_l = _ffn(x_l, bsz, seq_len, lp, mod3, row_l2, lambda b: b, tm2)

        if not last:
            br_rwkv_c = _rwkv_out_call(y_c, st_c, lp)
            br_attn_c = _attn_ctx_call(p_c, lp["sink"], bsz, ctx_len)
            br_conv_c = _conv_call(p_c, ctx_len, lp["conv_w"])
            m_c = _merge_call(br_rwkv_c, br_conv_c, br_attn_c, p_c, lp["w_branch"])
            x_c = _outproj_call(m_c, lp["w_out"], x_c, mod3, row_c, tm2)
            x_c = _ffn(x_c, bsz, ctx_len, lp, mod3, row_c, lambda b: bsz, tm2)

    return _final_norm_call(x_l, norm_final).reshape(bsz, seq_len, D)
```

```python
import functools

import jax
import jax.numpy as jnp
from jax import lax
from jax.experimental import pallas as pl
from jax.experimental.pallas import tpu as pltpu

F32 = jnp.float32
BF16 = jnp.bfloat16
HIGHEST = lax.Precision.HIGHEST

D = 2048
HD = 64
BW = 1024
NH = BW // HD
KVH = 4
GQ = NH // KVH
DECAY_RANK = 96
ICLR_RANK = 96
GATE_RANK = 256
VRES_RANK = 64
GN_EPS = 64e-5
NORM_EPS = 1e-6
WINDOW = 128
BLK = 128
assert WINDOW == BLK
GRID_W = 64
ROPE_BASE = 10000.0
NEG_INF = -1e30
N_EXPERTS = 16
EXPERT_FF = 2048
CAPACITY_FACTOR = 2
CHUNK = 64

C_R, C_K, C_V = 0, 1024, 2048
C_WD = (3072, 3200)
C_AD = (3328, 3456)
C_GD = 3584
RW_PAD = 4096
C_Q = 4096
C_CONV = 5120
C_GATE = 8192
C_AK = 14336
C_AV = 14592
NP = 14848
TN_IN = 512

VMEM_LIMIT = 56 * 1024 * 1024


def _cp(*sem, vmem=VMEM_LIMIT):
    return pltpu.CompilerParams(dimension_semantics=tuple(sem), vmem_limit_bytes=vmem)


def _dot(a, b):
    return jnp.dot(a, b, preferred_element_type=F32)


def _dot_nt(a, b):
    return lax.dot_general(a, b, (((1,), (1,)), ((), ())), preferred_element_type=F32)


def _dot_tn(a, b):
    return lax.dot_general(a, b, (((0,), (0,)), ((), ())), preferred_element_type=F32)


def _sigmoid(x):
    return 1.0 / (1.0 + jnp.exp(-x))


def _seg_sum(x, e_ref):
    e = e_ref[...]

    def split(z):
        hi = z.astype(BF16)
        return hi, (z - hi.astype(F32)).astype(BF16)

    hi, lo = split(x)
    hi2, lo2 = split(_dot(hi, e) + _dot(lo, e))
    return _dot_nt(hi2, e) + _dot_nt(lo2, e)


def _mod_kernel(c_ref, w_ref, b_ref, o_ref):
    c = c_ref[...]
    sc = c * _sigmoid(c)
    o_ref[0] = _dot(sc.astype(BF16), w_ref[0].astype(BF16)) + b_ref[0]


def _mod_call(cc, w_mod, b_mod):
    depth, _, n6 = w_mod.shape
    rows = cc.shape[0]
    tn = 1024
    return pl.pallas_call(
        _mod_kernel,
        grid=(depth, n6 // tn),
        in_specs=[pl.BlockSpec((rows, D), lambda l, j: (0, 0)),
                  pl.BlockSpec((1, D, tn), lambda l, j: (l, 0, j)),
                  pl.BlockSpec((1, 1, tn), lambda l, j: (l, 0, j))],
        out_specs=pl.BlockSpec((1, rows, tn), lambda l, j: (l, 0, j)),
        out_shape=jax.ShapeDtypeStruct((depth, rows, n6), F32),
        compiler_params=_cp("arbitrary", "arbitrary"),
        name="mod_proj",
    )(cc, w_mod, b_mod.reshape(depth, 1, n6))


def _norm_mod(x, gain, shift, scale):
    ms = jnp.mean(x * x, axis=-1, keepdims=True)
    y = x * lax.rsqrt(ms + NORM_EPS)
    return (y * gain) * (1.0 + scale) + shift


def _inproj_kernel(x_ref, gain_ref, shift_ref, scale_ref, w_ref, o_ref, h_ref):
    @pl.when(pl.program_id(1) == 0)
    def _():
        h_ref[...] = _norm_mod(x_ref[...], gain_ref[...], shift_ref[0], scale_ref[0]).astype(BF16)

    o_ref[...] = _dot(h_ref[...], w_ref[0]).astype(BF16)


def _inproj_call(x2, gain, mod3, mod_row, w_p, layer, tm, rwkv_kv_only=False):
    n_tok = x2.shape[0]
    n_rw = RW_PAD // TN_IN
    if rwkv_kv_only:
        ncols = RW_PAD + NP - C_AK
        col = lambda j: jnp.where(j < n_rw, j, j - n_rw + C_AK // TN_IN)
    else:
        ncols = NP
        col = lambda j: j
    return pl.pallas_call(
        _inproj_kernel,
        grid=(n_tok // tm, ncols // TN_IN),
        in_specs=[pl.BlockSpec((tm, D), lambda i, j: (i, 0)),
                  pl.BlockSpec((1, D), lambda i, j: (0, 0)),
                  pl.BlockSpec((1, 1, D), lambda i, j: (mod_row(i) * 6 + 0, 0, 0)),
                  pl.BlockSpec((1, 1, D), lambda i, j: (mod_row(i) * 6 + 1, 0, 0)),
                  pl.BlockSpec((1, D, TN_IN), lambda i, j: (layer, 0, col(j)))],
        out_specs=pl.BlockSpec((tm, TN_IN), lambda i, j: (i, j)),
        out_shape=jax.ShapeDtypeStruct((n_tok, ncols), BF16),
        scratch_shapes=[pltpu.VMEM((tm, D), BF16)],
        compiler_params=_cp("arbitrary", "arbitrary"),
        name="in_proj",
    )(x2, gain.reshape(1, D), mod3, mod3, w_p)


HALO = 16


def _shifted(cur, prev_blk, next_blk, first, last):
    tt = cur.shape[0]
    row = lax.broadcasted_iota(jnp.int32, cur.shape, 0)
    p_row = jnp.where(first, 0.0, prev_blk[HALO - 1:HALO, :].astype(F32))
    n_row = jnp.where(last, 0.0, next_blk[0:1, :].astype(F32))
    prev = jnp.where(row == 0, p_row, pltpu.roll(cur, 1, axis=0))
    nxt = jnp.where(row == tt - 1, n_row, pltpu.roll(cur, tt - 1, axis=0))
    return prev, nxt


def _halo_specs(tt, width, col_blk, n_tok):
    nb = n_tok // HALO
    r = tt // HALO
    return [pl.BlockSpec((tt, width), lambda i: (i, col_blk)),
            pl.BlockSpec((HALO, width), lambda i: (jnp.maximum(i * r - 1, 0), col_blk)),
            pl.BlockSpec((HALO, width), lambda i: (jnp.minimum((i + 1) * r, nb - 1), col_blk))]


def _streams_kernel(*refs, tiles_per_seq, has_vres):
    if has_vres:
        (p_ref, pp_ref, pn_ref, mu_ref, dup_ref, dbias_ref, iup_ref, ibias_ref, gup_ref, kk_ref_, e_ref,
         vd_ref, vu_ref, vb_ref, vf_ref,
         r_o, k_o, v_o, kk_o, g_o, a_o, lw_o) = refs
    else:
        (p_ref, pp_ref, pn_ref, mu_ref, dup_ref, dbias_ref, iup_ref, ibias_ref, gup_ref, kk_ref_, e_ref,
         r_o, k_o, v_o, kk_o, g_o, a_o, lw_o) = refs
    j = pl.program_id(0) % tiles_per_seq
    cur_b = p_ref[...]
    tt = cur_b.shape[0]
    ext = jnp.concatenate([pp_ref[...], cur_b, pn_ref[...]], axis=0)
    t_i = lax.broadcasted_iota(jnp.int32, (tt, tt + 2 * HALO), 0)
    e_i = lax.broadcasted_iota(jnp.int32, (tt, tt + 2 * HALO), 1) - HALO
    tap = ((e_i == t_i - 1) & ((e_i >= 0) | (j > 0))) | ((e_i == t_i + 1) & ((e_i < tt) | (j < tiles_per_seq - 1)))
    avg = _dot(jnp.where(tap, 0.5, 0.0).astype(BF16), ext)
    cur = cur_b.astype(F32)
    ps = cur + mu_ref[...] * (avg - cur)
    r = ps[:, C_R:C_R + BW]
    k = ps[:, C_K:C_K + BW]
    v = ps[:, C_V:C_V + BW]
    gd = ps[:, C_GD:C_GD + GATE_RANK]
    if has_vres:
        low = _dot(v.astype(BF16), vd_ref[...])
        mix = _sigmoid(vb_ref[...] + _dot(low.astype(BF16), vu_ref[...]))
        v = v + (vf_ref[...].astype(F32) - v) * mix
    for d in range(2):
        wd = ps[:, C_WD[d]:C_WD[d] + 128]
        ad = ps[:, C_AD[d]:C_AD[d] + 128]
        w_logit = dbias_ref[d] + _dot(jnp.tanh(wd).astype(BF16), dup_ref[d])
        lw_o[d] = -jnp.exp(-0.5) * _sigmoid(w_logit)
        a_o[d] = _sigmoid(ibias_ref[d] + _dot(ad.astype(BF16), iup_ref[d])).astype(BF16)
    kh = k * kk_ref_[...]
    ss = _seg_sum(kh * kh, e_ref)
    kk = kh * lax.rsqrt(jnp.maximum(ss, 1e-24))
    r_o[...] = r.astype(BF16)
    k_o[...] = k.astype(BF16)
    v_o[...] = v.astype(BF16)
    kk_o[...] = kk.astype(BF16)
    g_o[...] = _dot(_sigmoid(gd).astype(BF16), gup_ref[...]).astype(BF16)


def _streams_call(p, seq_len, lp, v_first):
    n_tok = p.shape[0]
    tt = 256
    has_vres = v_first is not None
    full = lambda *s: pl.BlockSpec(s, lambda i: (0,) * len(s))
    tok = pl.BlockSpec((tt, BW), lambda i: (i, 0))
    tok2 = pl.BlockSpec((2, tt, BW), lambda i: (0, i, 0))
    in_specs = _halo_specs(tt, RW_PAD, 0, n_tok) + [
        full(1, RW_PAD), full(2, 128, BW), full(2, 1, BW), full(2, 128, BW), full(2, 1, BW),
        full(GATE_RANK, BW), full(1, BW), full(BW, 128)]
    args = [p, p, p, lp["mu"], lp["decay_up"], lp["decay_bias"], lp["iclr_up"], lp["iclr_bias"],
            lp["gate_up"], lp["k_k"], lp["eblk"]]
    if has_vres:
        in_specs += [full(BW, 128), full(128, BW), full(1, BW), tok]
        args += [lp["vres_down"], lp["vres_up"], lp["vres_bias"], v_first]
    sd = lambda dt: jax.ShapeDtypeStruct((n_tok, BW), dt)
    sd2 = lambda dt: jax.ShapeDtypeStruct((2, n_tok, BW), dt)
    return pl.pallas_call(
        functools.partial(_streams_kernel, tiles_per_seq=seq_len // tt, has_vres=has_vres),
        grid=(n_tok // tt,),
        in_specs=in_specs,
        out_specs=[tok, tok, tok, tok, tok, tok2, tok2],
        out_shape=[sd(BF16), sd(BF16), sd(BF16), sd(BF16), sd(BF16), sd2(BF16), sd2(F32)],
        compiler_params=_cp("arbitrary"),
        name="rwkv_streams",
    )(*args)


def _scan_prologue(d, r, k, v, kk, a, lw, ka):
    c = CHUNK
    ri = lax.broadcasted_iota(jnp.int32, (c, 2 * c), 0)
    ci = lax.broadcasted_iota(jnp.int32, (c, 2 * c), 1) % c
    diff = (ci - ri) if d else (ri - ci)
    incl = diff >= 0
    strict = diff > 0
    tri = incl[:, :c].astype(BF16)
    lw_hi = lw.astype(BF16)
    lw_lo = (lw - lw_hi.astype(F32)).astype(BF16)
    b = _dot(tri, lw_hi) + _dot(tri, lw_lo)
    b_tot = b[0:1, :] if d else b[c - 1:c, :]
    a = a.astype(F32)
    kk = kk.astype(F32)
    kd = k.astype(F32) * (1.0 + (a - 1.0) * ka)
    kka = kk * a
    enb = jnp.exp(-b)
    etail = jnp.exp(b_tot - b)
    return dict(
        incl=incl, strict=strict,
        rt=(r.astype(F32) * jnp.exp(b)).astype(BF16),
        bt=(-kk * jnp.exp(b - lw)).astype(BF16),
        kt=(kd * enb).astype(BF16), at=(kka * enb).astype(BF16),
        kh=(kd * etail).astype(BF16), ah=(kka * etail).astype(BF16),
        vv=v, e_tot=jnp.exp(b_tot))


def _scan_kernel(*refs, with_output, sb):
    ins = refs[:14]
    if with_output:
        yf_ref, yb_ref, sout_ref, st_ref = refs[14:]
    else:
        sout_ref, st_ref = refs[14:]
    ka_ref, s0_ref = ins[12], ins[13]
    s = pl.program_id(1)
    c = CHUNK

    @pl.when(s == 0)
    def _():
        st_ref[...] = s0_ref[...]

    streams = [(d, j) for d in range(2) for j in range(sb)]
    ka = ka_ref[...]
    pro = [_scan_prologue(d, *[ref[j] for ref in ins[6 * d:6 * d + 4]], ins[6 * d + 4][0, j], ins[6 * d + 5][0, j], ka)
           for d, j in streams]
    st_all = [st_ref[d, j] for d, j in streams]

    npair = NH // 2
    units = [(si, slice(p * 2 * HD, (p + 1) * 2 * HD)) for si in range(len(streams)) for p in range(npair)]
    us = range(len(units))
    col = lambda name: [pro[d][name][:, sl] for d, sl in units]
    even1 = lax.broadcasted_iota(jnp.int32, (c, 2 * HD), 1) < HD

    def bd(z):
        zero = jnp.zeros_like(z)
        return jnp.concatenate([jnp.where(even1, z, zero), jnp.where(even1, zero, z)], axis=0)

    strict = [pro[d]["strict"] for d, _ in units]
    incl = [pro[d]["incl"] for d, _ in units]
    st = [st_all[d][:, sl] for d, sl in units]
    vbd = [bd(z) for z in col("vv")]
    rb = [jnp.concatenate(p, axis=0) for p in zip(col("rt"), col("bt"))]
    gk = [_dot_nt(rb[u], bd(z)) for u, z in zip(us, col("kt"))]
    ga = [_dot_nt(rb[u], bd(z)) for u, z in zip(us, col("at"))]
    rbs = [_dot_nt(rb[u], bd(st[u].astype(BF16))) for u in us]
    x = [jnp.where(strict[u], ga[u][c:], 0.0).astype(BF16) for u in us]
    a_bk = [jnp.where(strict[u], gk[u][c:], 0.0) for u in us]
    if with_output:
        a_rk = [jnp.where(incl[u], gk[u][:c], 0.0) for u in us]
        a_ra = [jnp.where(incl[u], ga[u][:c], 0.0).astype(BF16) for u in us]
        av = [_dot(jnp.concatenate([a_rk[u], a_bk[u]], axis=0).astype(BF16), vbd[u]) for u in us]
        w = [rbs[u][c:] + av[u][c:] for u in us]
    else:
        w = [rbs[u][c:] + _dot(a_bk[u].astype(BF16), vbd[u]) for u in us]
    for rnd in range(6):
        if rnd < 5:
            m = [_dot(x[u], jnp.concatenate([bd(x[u]), bd(w[u].astype(BF16))], axis=1)) for u in us]
            x = [m[u][:, :2 * c].astype(BF16) for u in us]
            w = [w[u] + m[u][:, 2 * c:] for u in us]
        else:
            w = [w[u] + _dot(x[u], bd(w[u].astype(BF16))) for u in us]
    ub = [w[u].astype(BF16) for u in us]
    if with_output:
        ys = [rbs[u][:c] + av[u][:c] + _dot(a_ra[u], bd(ub[u])) for u in us]
        for si, (d, j) in enumerate(streams):
            (yb_ref if d else yf_ref)[j] = jnp.concatenate(ys[si * npair:(si + 1) * npair], axis=1)
    full = [_dot_tn(jnp.concatenate([z, ub[u]], axis=0), jnp.concatenate(p, axis=0))
            for u, z, p in zip(us, col("vv"), zip(col("kh"), col("ah")))]
    upd = [jnp.where(even1, f[:HD], f[HD:]) for f in full]
    for si, (d, j) in enumerate(streams):
        st_ref[d, j] = st_all[si] * pro[si]["e_tot"] + jnp.concatenate(upd[si * npair:(si + 1) * npair], axis=1)

    @pl.when(s == pl.num_programs(1) - 1)
    def _():
        sout_ref[...] = st_ref[...]


def _scan_call(streams, k_a, s0, bsz, seq_len, with_output):
    n_tok = streams[0].shape[0]
    nc = seq_len // CHUNK
    sb = 2 if bsz % 2 == 0 else 1
    r, k, v, kk = (z.reshape(bsz, seq_len, BW) for z in streams[:4])
    a2, lw2 = (z.reshape(2, bsz, seq_len, BW) for z in streams[5:])

    chunk = (lambda s: s, lambda s: nc - 1 - s)
    tok = [pl.BlockSpec((sb, CHUNK, BW), lambda b, s, f=f: (b, f(s), 0)) for f in chunk]
    tok2 = [pl.BlockSpec((1, sb, CHUNK, BW), lambda b, s, f=f, d=d: (d, b, f(s), 0)) for d, f in enumerate(chunk)]
    st_spec = pl.BlockSpec((2, sb, HD, BW), lambda b, s: (0, b, 0, 0))
    in_specs, args = [], []
    for d in range(2):
        in_specs += [tok[d]] * 4 + [tok2[d]] * 2
        args += [r, k, v, kk, a2, lw2]
    in_specs += [pl.BlockSpec((1, BW), lambda b, s: (0, 0)), st_spec]
    out_specs = [st_spec]
    out_shape = [jax.ShapeDtypeStruct((2, bsz, HD, BW), F32)]
    if with_output:
        out_specs = tok + out_specs
        out_shape = [jax.ShapeDtypeStruct((bsz, seq_len, BW), F32)] * 2 + out_shape
    res = pl.pallas_call(
        functools.partial(_scan_kernel, with_output=with_output, sb=sb),
        grid=(bsz // sb, nc),
        in_specs=in_specs,
        out_specs=out_specs,
        out_shape=out_shape,
        scratch_shapes=[pltpu.VMEM((2, sb, HD, BW), F32)],
        compiler_params=_cp("arbitrary", "arbitrary"),
        name="rwkv_scan",
    )(*args, k_a, s0)
    if with_output:
        return res[2], (res[0].reshape(n_tok, BW), res[1].reshape(n_tok, BW))
    return res[0], None


def _rwkv_out_kernel(yf_ref, yb_ref, r_ref, k_ref, v_ref, a_ref, g_ref, ka_ref, rk_ref, gnw_ref, gnb_ref, e_ref,
                     o_ref):
    y = yf_ref[...] + yb_ref[...]
    mean = _seg_sum(y, e_ref) * (1.0 / HD)
    yc = y - mean
    var = _seg_sum(yc * yc, e_ref) * (1.0 / HD)
    yn = yc * lax.rsqrt(var + GN_EPS) * gnw_ref[...] + gnb_ref[...]
    r = r_ref[...].astype(F32)
    k = k_ref[...].astype(F32)
    asum = a_ref[0].astype(F32) + a_ref[1].astype(F32)
    kd_sum = k * (2.0 + (asum - 2.0) * ka_ref[...])
    bonus = _seg_sum(r * kd_sum * rk_ref[...], e_ref) * v_ref[...].astype(F32)
    o_ref[...] = ((yn + bonus) * g_ref[...].astype(F32)).astype(BF16)


def _rwkv_out_call(y2, streams, lp):
    r, k, v, _, g, a2, _ = streams
    n_tok = r.shape[0]
    tt = 256
    tok = pl.BlockSpec((tt, BW), lambda i: (i, 0))
    tok2 = pl.BlockSpec((2, tt, BW), lambda i: (0, i, 0))
    vec = pl.BlockSpec((1, BW), lambda i: (0, 0))
    return pl.pallas_call(
        _rwkv_out_kernel,
        grid=(n_tok // tt,),
        in_specs=[tok, tok, tok, tok, tok, tok2, tok, vec, vec, vec, vec, pl.BlockSpec((BW, 128), lambda i: (0, 0))],
        out_specs=tok,
        out_shape=jax.ShapeDtypeStruct((n_tok, BW), BF16),
        compiler_params=_cp("arbitrary"),
        name="rwkv_out",
    )(y2[0], y2[1], r, k, v, a2, g, lp["k_a"], lp["r_k"], lp["gn_w"], lp["gn_b"], lp["eblk"])


def _conv_kernel(b_ref, c_ref, cp_ref, cn_ref, u_ref, up_ref, un_ref, w_ref, o_ref, *, tiles_per_seq):
    j = pl.program_id(0) % tiles_per_seq
    cu = c_ref[...].astype(F32) * u_ref[...].astype(F32)
    cu_p = cp_ref[...].astype(F32) * up_ref[...].astype(F32)
    cu_n = cn_ref[...].astype(F32) * un_ref[...].astype(F32)
    prev, nxt = _shifted(cu, cu_p, cu_n, j == 0, j == tiles_per_seq - 1)
    w = w_ref[...]
    conv = w[0:1] * prev + w[1:2] * cu + w[2:3] * nxt
    o_ref[...] = (b_ref[...].astype(F32) * conv).astype(BF16)


def _conv_call(p, seq_len, conv_w):
    n_tok = p.shape[0]
    tt = 256
    cb = C_CONV // BW
    return pl.pallas_call(
        functools.partial(_conv_kernel, tiles_per_seq=seq_len // tt),
        grid=(n_tok // tt,),
        in_specs=[pl.BlockSpec((tt, BW), lambda i: (i, cb))] + _halo_specs(tt, BW, cb + 1, n_tok)
        + _halo_specs(tt, BW, cb + 2, n_tok) + [pl.BlockSpec((3, BW), lambda i: (0, 0))],
        out_specs=pl.BlockSpec((tt, BW), lambda i: (i, 0)),
        out_shape=jax.ShapeDtypeStruct((n_tok, BW), BF16),
        compiler_params=_cp("arbitrary"),
        name="short_conv",
    )(p, p, p, p, p, p, p, conv_w)


def _rope(x, cos, sin):
    w = x.shape[1]
    lane = lax.broadcasted_iota(jnp.int32, x.shape, 1)
    partner = jnp.where((lane % 32) < 16, pltpu.roll(x, w - 16, axis=1), pltpu.roll(x, 16, axis=1))
    return x * cos + partner * sin


LOG2E = 1.4426950408889634
Q_SCALE = HD ** -0.5 * LOG2E


def _softmax_pv(s2, sink2_col, v):
    m = jnp.maximum(jnp.max(s2, axis=-1, keepdims=True), sink2_col)
    p = jnp.exp2(s2 - m)
    den = jnp.sum(p, axis=-1, keepdims=True) + jnp.exp2(sink2_col - m)
    return _dot(p.astype(BF16), v) / den


def _attn_lat_kernel(q_ref, kp_ref, kc_ref, kn_ref, vp_ref, vc_ref, vn_ref, kx_ref, vx_ref,
                     cos_ref, cosp_ref, cosn_ref, sin_ref, sinp_ref, sinn_ref, sink_ref, o_ref, *, nblk):
    n = pl.program_id(1)
    kvw = KVH * HD
    q = (_rope(q_ref[...].astype(F32), cos_ref[...], sin_ref[...]) * Q_SCALE).astype(BF16)
    kb = jnp.concatenate([
        _rope(kp_ref[...].astype(F32), cosp_ref[:, :kvw], sinp_ref[:, :kvw]),
        _rope(kc_ref[...].astype(F32), cos_ref[:, :kvw], sin_ref[:, :kvw]),
        _rope(kn_ref[...].astype(F32), cosn_ref[:, :kvw], sinn_ref[:, :kvw])], axis=0).astype(BF16)
    k_all = jnp.concatenate([kb, kx_ref[...]], axis=0)
    v_all = jnp.concatenate([vp_ref[...], vc_ref[...], vn_ref[...], vx_ref[...]], axis=0)
    cols = GQ * BLK
    kidx = lax.broadcasted_iota(jnp.int32, (BLK, cols), 0)
    qpos = lax.broadcasted_iota(jnp.int32, (BLK, cols), 1) % BLK
    bias_p = jnp.where((kidx >= qpos) & (n > 0), 0.0, NEG_INF)
    bias_n = jnp.where((kidx <= qpos) & (n < nblk - 1), 0.0, NEG_INF)
    sink = sink_ref[...] * LOG2E
    gs = range(KVH)
    even = lax.broadcasted_iota(jnp.int32, (BLK, 2 * HD), 1) < HD
    zero = jnp.zeros((BLK, 2 * HD), BF16)
    eye = (lax.broadcasted_iota(jnp.int32, (2 * HD, 2 * HD), 0)
           == lax.broadcasted_iota(jnp.int32, (2 * HD, 2 * HD), 1)).astype(BF16)

    def head_rows(gi):
        parts = []
        for t in range(GQ // 2):
            qp = q[:, (gi * GQ + 2 * t) * HD:(gi * GQ + 2 * t + 2) * HD]
            parts += [jnp.where(even, qp, zero), jnp.where(even, zero, qp)]
        return jnp.concatenate(parts, axis=0)

    qg = [head_rows(gi) for gi in gs]
    kd = [jnp.concatenate([k_all[:, gi * HD:(gi + 1) * HD]] * 2, axis=1) for gi in gs]
    vt = [_dot_nt(eye[:HD, :HD], v_all[:, gi * HD:(gi + 1) * HD]).astype(BF16) for gi in gs]
    sk = [jnp.concatenate([jnp.broadcast_to(sink[:, gi * GQ + t:gi * GQ + t + 1], (1, BLK))
                           for t in range(GQ)], axis=1) for gi in gs]
    s = [_dot_nt(kd[gi], qg[gi]) for gi in gs]
    s = [jnp.concatenate([z[:BLK] + bias_p, z[BLK:2 * BLK], z[2 * BLK:3 * BLK] + bias_n, z[3 * BLK:]], axis=0)
         for z in s]
    m = [jnp.maximum(jnp.max(s[gi], axis=0, keepdims=True), sk[gi]) for gi in gs]
    p = [jnp.exp2(s[gi] - m[gi]) for gi in gs]
    den = [jnp.sum(p[gi], axis=0, keepdims=True) + jnp.exp2(sk[gi] - m[gi]) for gi in gs]
    og = [(_dot(vt[gi], p[gi].astype(BF16)) / den[gi]).astype(BF16) for gi in gs]
    o_ref[...] = jnp.concatenate(
        [_dot_nt(eye, jnp.concatenate([og[gi][:, 2 * t * BLK:(2 * t + 1) * BLK],
                                       og[gi][:, (2 * t + 1) * BLK:(2 * t + 2) * BLK]], axis=0))
         for gi in gs for t in range(GQ // 2)], axis=1).astype(BF16)


def _attn_lat_call(p_l, p_c, kv_blk_c, cos, sin, sink, bsz, seq_len, ctx_len):
    nblk = seq_len // BLK
    kvw = KVH * HD
    kb, vb = C_AK // kvw, C_AV // kvw
    rowq = lambda b, n: b * nblk + n
    rowp = lambda b, n: b * nblk + jnp.maximum(n - 1, 0)
    rown = lambda b, n: b * nblk + jnp.minimum(n + 1, nblk - 1)
    tabp = lambda b, n: (jnp.maximum(n - 1, 0), 0)
    tabn = lambda b, n: (jnp.minimum(n + 1, nblk - 1), 0)
    kv = lambda rf, cb: pl.BlockSpec((BLK, kvw), lambda b, n: (rf(b, n), cb))
    tab = lambda f: pl.BlockSpec((BLK, BW), f)
    return pl.pallas_call(
        functools.partial(_attn_lat_kernel, nblk=nblk),
        grid=(bsz, nblk),
        in_specs=[pl.BlockSpec((BLK, BW), lambda b, n: (rowq(b, n), C_Q // BW)),
                  kv(rowp, kb), kv(rowq, kb), kv(rown, kb), kv(rowp, vb), kv(rowq, vb), kv(rown, vb),
                  pl.BlockSpec((ctx_len, kvw), lambda b, n: (b, kv_blk_c)),
                  pl.BlockSpec((ctx_len, kvw), lambda b, n: (b, kv_blk_c + 1)),
                  tab(lambda b, n: (n, 0)), tab(tabp), tab(tabn),
                  tab(lambda b, n: (n, 0)), tab(tabp), tab(tabn),
                  pl.BlockSpec((1, NH), lambda b, n: (0, 0))],
        out_specs=pl.BlockSpec((BLK, BW), lambda b, n: (rowq(b, n), 0)),
        out_shape=jax.ShapeDtypeStruct((bsz * seq_len, BW), BF16),
        compiler_params=_cp("arbitrary", "arbitrary"),
        name="attn_latent",
    )(p_l, p_l, p_l, p_l, p_l, p_l, p_l, p_c, p_c, cos, cos, cos, sin, sin, sin, sink)


def _attn_ctx_kernel(q_ref, kx_ref, vx_ref, sink_ref, o_ref):
    q = (q_ref[...].astype(F32) * Q_SCALE).astype(BF16)
    kx = kx_ref[...]
    vx = vx_ref[...]
    sink = sink_ref[...] * LOG2E
    tq = q.shape[0]
    outs = []
    for gi in range(KVH):
        qg = jnp.concatenate([q[:, (gi * GQ + t) * HD:(gi * GQ + t + 1) * HD] for t in range(GQ)], axis=0)
        sk = jnp.concatenate([jnp.broadcast_to(sink[:, gi * GQ + t:gi * GQ + t + 1], (tq, 1))
                              for t in range(GQ)], axis=0)
        s = _dot_nt(qg, kx[:, gi * HD:(gi + 1) * HD])
        og = _softmax_pv(s, sk, vx[:, gi * HD:(gi + 1) * HD])
        outs += [og[t * tq:(t + 1) * tq] for t in range(GQ)]
    o_ref[...] = jnp.concatenate(outs, axis=1).astype(BF16)


def _attn_ctx_call(p_c, sink, bsz, ctx_len):
    kvw = KVH * HD
    tq = 128
    nq = ctx_len // tq
    return pl.pallas_call(
        _attn_ctx_kernel,
        grid=(bsz, nq),
        in_specs=[pl.BlockSpec((tq, BW), lambda b, n: (b * nq + n, C_Q // BW)),
                  pl.BlockSpec((ctx_len, kvw), lambda b, n: (b, C_AK // kvw)),
                  pl.BlockSpec((ctx_len, kvw), lambda b, n: (b, C_AV // kvw)),
                  pl.BlockSpec((1, NH), lambda b, n: (0, 0))],
        out_specs=pl.BlockSpec((tq, BW), lambda b, n: (b * nq + n, 0)),
        out_shape=jax.ShapeDtypeStruct((bsz * ctx_len, BW), BF16),
        compiler_params=_cp("arbitrary", "arbitrary"),
        name="attn_context",
    )(p_c, p_c, p_c, sink)


def _merge_kernel(b0_ref, b1_ref, b2_ref, g0_ref, g1_ref, g2_ref, w_ref, o_ref):
    acc = _sigmoid(g0_ref[...].astype(F32)) * _dot(b0_ref[...], w_ref[0])
    acc += _sigmoid(g1_ref[...].astype(F32)) * _dot(b1_ref[...], w_ref[1])
    acc += _sigmoid(g2_ref[...].astype(F32)) * _dot(b2_ref[...], w_ref[2])
    o_ref[...] = acc.astype(BF16)


def _merge_call(br_rwkv, br_conv, br_attn, p, w_branch):
    n_tok = p.shape[0]
    tm, tn = 512, 1024
    nn = D // tn
    br = pl.BlockSpec((tm, BW), lambda j, i: (i, 0))
    gate = lambda t: pl.BlockSpec((tm, tn), lambda j, i: (i, (C_GATE + t * D) // tn + j))
    return pl.pallas_call(
        _merge_kernel,
        grid=(nn, n_tok // tm),
        in_specs=[br, br, br, gate(0), gate(1), gate(2), pl.BlockSpec((3, BW, tn), lambda j, i: (0, 0, j))],
        out_specs=pl.BlockSpec((tm, tn), lambda j, i: (i, j)),
        out_shape=jax.ShapeDtypeStruct((n_tok, D), BF16),
        compiler_params=_cp("arbitrary", "arbitrary"),
        name="merge_branches",
    )(br_rwkv, br_conv, br_attn, p, p, p, w_branch)


def _outproj_kernel(m_ref, w_ref, x_ref, gate_ref, o_ref):
    o_ref[...] = x_ref[...] + gate_ref[0] * _dot(m_ref[...], w_ref[...])


def _outproj_call(m, w_out, x2, mod3, mod_row, tm):
    n_tok = x2.shape[0]
    return pl.pallas_call(
        _outproj_kernel,
        grid=(n_tok // tm,),
        in_specs=[pl.BlockSpec((tm, D), lambda i: (i, 0)),
                  pl.BlockSpec((D, D), lambda i: (0, 0)),
                  pl.BlockSpec((tm, D), lambda i: (i, 0)),
                  pl.BlockSpec((1, 1, D), lambda i: (mod_row(i) * 6 + 2, 0, 0))],
        out_specs=pl.BlockSpec((tm, D), lambda i: (i, 0)),
        out_shape=jax.ShapeDtypeStruct((n_tok, D), F32),
        compiler_params=_cp("arbitrary"),
        name="out_proj",
    )(m, w_out, x2, mod3)


def _ffn_prep_kernel(x_ref, gain_ref, shift_ref, scale_ref, wr_ref, h_ref, aff_ref):
    h = _norm_mod(x_ref[...], gain_ref[...], shift_ref[0], scale_ref[0])
    h_hi = h.astype(BF16)
    h_ref[...] = h_hi
    h_lo = (h - h_hi.astype(F32)).astype(BF16)
    wr = wr_ref[...]
    w_hi = wr.astype(BF16)
    w_lo = (wr - w_hi.astype(F32)).astype(BF16)
    logits = _dot(h_hi, w_hi) + (_dot(h_hi, w_lo) + _dot(h_lo, w_hi))
    lane = lax.broadcasted_iota(jnp.int32, logits.shape, 1)
    logits = jnp.where(lane < N_EXPERTS, logits, NEG_INF)
    m = jnp.max(logits, axis=-1, keepdims=True)
    e = jnp.exp(logits - m)
    aff_ref[...] = e / jnp.sum(e, axis=-1, keepdims=True)


def _ffn_prep_call(x2, gain, mod3, mod_row, wr_pad, tm):
    n_tok = x2.shape[0]
    return pl.pallas_call(
        _ffn_prep_kernel,
        grid=(n_tok // tm,),
        in_specs=[pl.BlockSpec((tm, D), lambda i: (i, 0)),
                  pl.BlockSpec((1, D), lambda i: (0, 0)),
                  pl.BlockSpec((1, 1, D), lambda i: (mod_row(i) * 6 + 3, 0, 0)),
                  pl.BlockSpec((1, 1, D), lambda i: (mod_row(i) * 6 + 4, 0, 0)),
                  pl.BlockSpec((D, 128), lambda i: (0, 0))],
        out_specs=[pl.BlockSpec((tm, D), lambda i: (i, 0)), pl.BlockSpec((tm, 128), lambda i: (i, 0))],
        out_shape=[jax.ShapeDtypeStruct((n_tok, D), BF16), jax.ShapeDtypeStruct((n_tok, 128), F32)],
        compiler_params=_cp("arbitrary"),
        name="ffn_prep",
    )(x2, gain.reshape(1, D), mod3, mod3, wr_pad)


def _select_kernel(aff_ref, slot_ref, *, cap):
    a = aff_ref[0]
    n = a.shape[1]
    bits = lax.bitcast_convert_type(a, jnp.int32)

    def body(i, t):
        cand = t | jnp.left_shift(jnp.int32(1), 30 - i)
        cnt = jnp.sum((bits >= cand).astype(jnp.int32), axis=-1, keepdims=True)
        return jnp.where(cnt >= cap, cand, t)

    thr = lax.fori_loop(0, 31, body, jnp.zeros((a.shape[0], 1), jnp.int32))
    gt = bits > thr
    eq = bits == thr
    n_gt = jnp.sum(gt.astype(jnp.int32), axis=-1, keepdims=True)
    tc = min(n, 512)

    def prefix(mask_bf16):
        cols = []
        for j0 in range(0, n, tc):
            ri = lax.broadcasted_iota(jnp.int32, (n, tc), 0)
            ci = lax.broadcasted_iota(jnp.int32, (n, tc), 1) + j0
            cols.append(_dot(mask_bf16, (ri <= ci).astype(BF16)))
        return jnp.concatenate(cols, axis=1)

    eq_f = eq.astype(BF16)
    excl_eq = prefix(eq_f) - eq_f.astype(F32)
    sel = gt | (eq & (excl_eq < (cap - n_gt).astype(F32)))
    pos = prefix(sel.astype(BF16))
    slot_ref[0] = jnp.where(sel, pos.astype(jnp.int32) - 1, -1)


def _select_call(aff_t, cap):
    bsz, ne, n = aff_t.shape
    return pl.pallas_call(
        functools.partial(_select_kernel, cap=cap),
        grid=(bsz,),
        in_specs=[pl.BlockSpec((1, ne, n), lambda b: (b, 0, 0))],
        out_specs=pl.BlockSpec((1, ne, n), lambda b: (b, 0, 0)),
        out_shape=jax.ShapeDtypeStruct((bsz, ne, n), jnp.int32),
        compiler_params=_cp("arbitrary"),
        name="expert_select",
    )(aff_t)


def _gather_kernel(slot_ref, aff_ref, h_ref, xs_ref, g_ref, *, cap):
    slot = slot_ref[0]
    n = slot.shape[1]
    onehot = lax.broadcasted_iota(jnp.int32, (cap, n), 0) == slot
    xs_ref[0] = _dot(onehot.astype(BF16), h_ref[...]).astype(BF16)
    g_ref[0] = jnp.sum(jnp.where(onehot, aff_ref[0], 0.0), axis=-1, keepdims=True)


def _gather_call(slot, aff_t, h2, cap):
    bsz, ne, n = slot.shape
    return pl.pallas_call(
        functools.partial(_gather_kernel, cap=cap),
        grid=(bsz, ne),
        in_specs=[pl.BlockSpec((1, 1, n), lambda b, e: (b * ne + e, 0, 0)),
                  pl.BlockSpec((1, 1, n), lambda b, e: (b * ne + e, 0, 0)),
                  pl.BlockSpec((n, D), lambda b, e: (b, 0))],
        out_specs=[pl.BlockSpec((1, cap, D), lambda b, e: (e, b, 0)),
                   pl.BlockSpec((1, cap, 1), lambda b, e: (e, b, 0))],
        out_shape=[jax.ShapeDtypeStruct((ne, bsz * cap, D), BF16),
                   jax.ShapeDtypeStruct((ne, bsz * cap, 1), F32)],
        compiler_params=_cp("arbitrary", "arbitrary"),
        name="expert_gather",
    )(slot.reshape(bsz * ne, 1, n), aff_t.reshape(bsz * ne, 1, n), h2)


def _expert_kernel(x_ref, wg_ref, wu_ref, wd_ref, g_ref, o_ref, acc_ref):
    f = pl.program_id(2)

    @pl.when(f == 0)
    def _():
        acc_ref[...] = jnp.zeros_like(acc_ref)

    x = x_ref[0]
    gate = _dot(x, wg_ref[0, 0].astype(BF16))
    hid = gate * _sigmoid(gate) * _dot(x, wu_ref[0, 0].astype(BF16))
    acc_ref[...] += _dot(hid.astype(BF16), wd_ref[0, 0].astype(BF16))

    @pl.when(f == pl.num_programs(2) - 1)
    def _():
        o_ref[0] = (acc_ref[...] * g_ref[0]).astype(BF16)


def _expert_call(xs, g, wg, wu, wd, layer):
    ne, rows, _ = xs.shape
    tm = min(rows, 1024)
    tf = 256
    return pl.pallas_call(
        _expert_kernel,
        grid=(ne, rows // tm, EXPERT_FF // tf),
        in_specs=[pl.BlockSpec((1, tm, D), lambda e, i, f: (e, i, 0)),
                  pl.BlockSpec((1, 1, D, tf), lambda e, i, f: (layer, e, 0, f)),
                  pl.BlockSpec((1, 1, D, tf), lambda e, i, f: (layer, e, 0, f)),
                  pl.BlockSpec((1, 1, tf, D), lambda e, i, f: (layer, e, f, 0)),
                  pl.BlockSpec((1, tm, 1), lambda e, i, f: (e, i, 0))],
        out_specs=pl.BlockSpec((1, tm, D), lambda e, i, f: (e, i, 0)),
        out_shape=jax.ShapeDtypeStruct((ne, rows, D), BF16),
        scratch_shapes=[pltpu.VMEM((tm, D), F32)],
        compiler_params=_cp("arbitrary", "arbitrary", "arbitrary"),
        name="expert_mlp",
    )(xs, wg, wu, wd, g)


def _scatter_kernel(*refs, cap, final_norm):
    if final_norm:
        slot_ref, ys_ref, x_ref, gate_ref, gain_ref, o_ref = refs
    else:
        slot_ref, ys_ref, x_ref, gate_ref, o_ref = refs
    slot_t = slot_ref[0]
    tq = slot_t.shape[0]
    lane = lax.broadcasted_iota(jnp.int32, (tq, cap), 1)
    acc = jnp.zeros(x_ref.shape, F32)
    for e in range(N_EXPERTS):
        onehot = (slot_t[:, e:e + 1] == lane).astype(BF16)
        acc += _dot(onehot, ys_ref[e])
    x = x_ref[...] + gate_ref[0] * acc
    if final_norm:
        ms = jnp.mean(x * x, axis=-1, keepdims=True)
        x = (x * lax.rsqrt(ms + NORM_EPS)) * gain_ref[...]
    o_ref[...] = x


def _scatter_call(slot_t, ys, x2, mod3, mod_row_b, cap, final_gain=None):
    bsz, n, ne = slot_t.shape
    tq = min(n, 256)
    nt = n // tq
    in_specs = [pl.BlockSpec((1, tq, ne), lambda b, t: (b, t, 0)),
                pl.BlockSpec((ne, cap, D), lambda b, t: (0, b, 0)),
                pl.BlockSpec((tq, D), lambda b, t: (b * nt + t, 0)),
                pl.BlockSpec((1, 1, D), lambda b, t: (mod_row_b(b) * 6 + 5, 0, 0))]
    args = [slot_t, ys, x2, mod3]
    if final_gain is not None:
        in_specs.append(pl.BlockSpec((1, D), lambda b, t: (0, 0)))
        args.append(final_gain.reshape(1, D))
    return pl.pallas_call(
        functools.partial(_scatter_kernel, cap=cap, final_norm=final_gain is not None),
        grid=(bsz, nt),
        in_specs=in_specs,
        out_specs=pl.BlockSpec((tq, D), lambda b, t: (b * nt + t, 0)),
        out_shape=jax.ShapeDtypeStruct(x2.shape, F32),
        compiler_params=_cp("arbitrary", "arbitrary"),
        name="expert_scatter",
    )(*args)


def _ffn(x2, bsz, n, lp, mod3, mod_row, mod_row_b, tm, final_gain=None):
    cap = CAPACITY_FACTOR * n // N_EXPERTS
    h2, aff = _ffn_prep_call(x2, lp["norm_ffn"], mod3, mod_row, lp["w_router"], tm)
    aff_t = jnp.swapaxes(aff[:, :N_EXPERTS].reshape(bsz, n, N_EXPERTS), 1, 2)
    slot = _select_call(aff_t, cap)
    xs, g = _gather_call(slot, aff_t, h2, cap)
    ys = _expert_call(xs, g, lp["w_exp_gate"], lp["w_exp_up"], lp["w_exp_down"], lp["layer"])
    return _scatter_call(jnp.swapaxes(slot, 1, 2), ys, x2, mod3, mod_row_b, cap, final_gain)


def _pad_cols(parts):
    out = []
    for a, width in parts:
        out.append(a)
        if a.shape[-1] < width:
            out.append(jnp.zeros(a.shape[:-1] + (width - a.shape[-1],), a.dtype))
    return jnp.concatenate(out, axis=-1)


def _rwkv_col_parts(w):
    o = 3 * BW
    parts = [(w[..., :o], o)]
    for _ in range(2):
        parts.append((w[..., o:o + DECAY_RANK], 128))
        o += DECAY_RANK
    for _ in range(2):
        parts.append((w[..., o:o + ICLR_RANK], 128))
        o += ICLR_RANK
    parts.append((w[..., o:o + GATE_RANK], RW_PAD - C_GD))
    return parts, o + GATE_RANK


LANE = 128
RW_END = 3 * BW + 2 * DECAY_RANK + 2 * ICLR_RANK + GATE_RANK
CODE_W = RW_PAD - 3 * BW


def _wrelayout_kernel(w_ref, misc_ref, o_ref):
    j = pl.program_id(1)
    is_code = (j >= 3 * BW // LANE) & (j < RW_PAD // LANE)
    o_ref[0] = jnp.where(is_code, misc_ref[0], w_ref[0]).astype(BF16)


def _wrelayout_call(w_in):
    depth = w_in.shape[0]
    code_parts, _ = _rwkv_col_parts(w_in[:, :, :RW_END])
    misc = _pad_cols(code_parts[1:])
    nb = NP // LANE
    kv_src = RW_END // LANE

    def src(j):
        shifted = jnp.where(j >= C_AK // LANE, j - C_AK // LANE + kv_src, j + (RW_END + 2 * KVH * HD - C_Q) // LANE)
        return jnp.where(j < RW_PAD // LANE, jnp.minimum(j, 3 * BW // LANE - 1), shifted)

    def misc_blk(j):
        return jnp.clip(j - 3 * BW // LANE, 0, CODE_W // LANE - 1)

    return pl.pallas_call(
        _wrelayout_kernel,
        grid=(depth, nb),
        in_specs=[pl.BlockSpec((1, D, LANE), lambda l, j: (l, 0, src(j))),
                  pl.BlockSpec((1, D, LANE), lambda l, j: (l, 0, misc_blk(j)))],
        out_specs=pl.BlockSpec((1, D, LANE), lambda l, j: (l, 0, j)),
        out_shape=jax.ShapeDtypeStruct((depth, D, NP), BF16),
        compiler_params=_cp("arbitrary", "arbitrary"),
        name="w_in_relayout",
    )(w_in, misc)


def _prep_layer(l, depth, shift_mu, decay_up, decay_bias, iclr_up, iclr_bias, gate_up, vres_down, vres_up,
                vres_bias, k_k, k_a, r_k, gn_w, gn_b, conv_w, attn_sink, w_branch, w_out, w_router,
                w_exp_gate, w_exp_up, w_exp_down, norm_mix, norm_ffn, eblk):
    mu_parts, _ = _rwkv_col_parts(shift_mu[l][None, :])
    pad_rank = lambda u: jnp.pad(u, ((0, 0), (0, 128 - u.shape[1]), (0, 0))).astype(BF16)
    lp = dict(
        mu=_pad_cols(mu_parts),
        decay_up=pad_rank(decay_up[l]), decay_bias=decay_bias[l].reshape(2, 1, BW),
        iclr_up=pad_rank(iclr_up[l]), iclr_bias=iclr_bias[l].reshape(2, 1, BW),
        gate_up=gate_up[l].astype(BF16), k_k=k_k[l].reshape(1, BW), k_a=k_a[l].reshape(1, BW),
        r_k=r_k[l].reshape(1, BW), gn_w=gn_w[l].reshape(1, BW), gn_b=gn_b[l].reshape(1, BW),
        conv_w=conv_w[l], sink=attn_sink[l].reshape(1, NH),
        w_branch=w_branch[l].astype(BF16), w_out=w_out[l].astype(BF16),
        w_router=jnp.pad(w_router[l], ((0, 0), (0, 128 - N_EXPERTS))),
        layer=l, w_exp_gate=w_exp_gate, w_exp_up=w_exp_up, w_exp_down=w_exp_down,
        norm_mix=norm_mix[l], norm_ffn=norm_ffn[l], eblk=eblk)
    if l > 0:
        lp["vres_down"] = jnp.pad(vres_down[l - 1], ((0, 0), (0, 128 - VRES_RANK))).astype(BF16)
        lp["vres_up"] = jnp.pad(vres_up[l - 1], ((0, 128 - VRES_RANK), (0, 0))).astype(BF16)
        lp["vres_bias"] = vres_bias[l - 1].reshape(1, BW)
    return lp


def _rope_tables(seq_len):
    quarter = HD // 4
    inv = ROPE_BASE ** (-jnp.arange(quarter, dtype=F32) / quarter)
    pos = jnp.arange(seq_len)
    ang_r = (pos // GRID_W).astype(F32)[:, None] * inv[None, :]
    ang_c = (pos % GRID_W).astype(F32)[:, None] * inv[None, :]
    cos = jnp.concatenate([jnp.cos(ang_r)] * 2 + [jnp.cos(ang_c)] * 2, axis=1)
    sin = jnp.concatenate([-jnp.sin(ang_r), jnp.sin(ang_r), -jnp.sin(ang_c), jnp.sin(ang_c)], axis=1)
    return jnp.tile(cos, (1, NH)), jnp.tile(sin, (1, NH))


def kernel(x, c, ctx, c_ctx, w_mod, b_mod, norm_mix, norm_ffn, w_in, shift_mu, decay_up, decay_bias, iclr_up,
           iclr_bias, gate_up, vres_down, vres_up, vres_bias, k_k, k_a, r_k, gn_w, gn_b, conv_w, attn_sink,
           w_branch, w_out, w_router, w_exp_gate, w_exp_up, w_exp_down, norm_final):
    bsz, seq_len, _ = x.shape
    ctx_len = ctx.shape[1]
    depth = w_in.shape[0]
    mod_rows = -(-(bsz + 1) // 8) * 8
    cc = jnp.concatenate([c, c_ctx[None, :], jnp.zeros((mod_rows - bsz - 1, D), F32)], axis=0)
    mod_all = _mod_call(cc, w_mod, b_mod)

    hd_i = jnp.arange(BW) // HD
    eblk = (hd_i[:, None] == jnp.arange(128)[None, :]).astype(BF16)
    cos, sin = _rope_tables(seq_len)

    tm_l = min(1024, seq_len)
    tm_c = min(1024, bsz * ctx_len)
    tiles_per_sample = seq_len // tm_l
    row_l = lambda i: i // tiles_per_sample
    row_c = lambda i: bsz
    tm2 = 512
    row_l2 = lambda i: i // (seq_len // tm2)

    x_l = x.reshape(bsz * seq_len, D)
    x_c = ctx.reshape(bsz * ctx_len, D)
    vf_l = vf_c = None
    zero_state = jnp.zeros((2, bsz, HD, BW), F32)

    w_p = _wrelayout_call(w_in)

    for l in range(depth):
        last = l == depth - 1
        lp = _prep_layer(l, depth, shift_mu, decay_up, decay_bias, iclr_up, iclr_bias, gate_up, vres_down,
                         vres_up, vres_bias, k_k, k_a, r_k, gn_w, gn_b, conv_w, attn_sink, w_branch, w_out,
                         w_router, w_exp_gate, w_exp_up, w_exp_down, norm_mix, norm_ffn, eblk)
        mod3 = mod_all[l].reshape(mod_rows * 6, 1, D)

        p_l = _inproj_call(x_l, lp["norm_mix"], mod3, row_l, w_p, l, tm_l)
        p_c = _inproj_call(x_c, lp["norm_mix"], mod3, row_c, w_p, l, tm_c, rwkv_kv_only=last)
        kv_blk_c = (RW_PAD if last else C_AK) // (KVH * HD)

        st_c = _streams_call(p_c, ctx_len, lp, vf_c)
        st_l = _streams_call(p_l, seq_len, lp, vf_l)
        if l == 0:
            vf_c, vf_l = st_c[2], st_l[2]
        state_c, y_c = _scan_call(st_c, lp["k_a"], zero_state, bsz, ctx_len, not last)
        _, y_l = _scan_call(st_l, lp["k_a"], state_c, bsz, seq_len, True)
        br_rwkv_l = _rwkv_out_call(y_l, st_l, lp)

        br_attn_l = _attn_lat_call(p_l, p_c, kv_blk_c, cos, sin, lp["sink"], bsz, seq_len, ctx_len)
        br_conv_l = _conv_call(p_l, seq_len, lp["conv_w"])

        m_l = _merge_call(br_rwkv_l, br_conv_l, br_attn_l, p_l, lp["w_branch"])
        x_l = _outproj_call(m_l, lp["w_out"], x_l, mod3, row_l2, tm2)
        x_l = _ffn(x_l, bsz, seq_len, lp, mod3, row_l2, lambda b: b, tm2, norm_final if last else None)

        if not last:
            br_rwkv_c = _rwkv_out_call(y_c, st_c, lp)
            br_attn_c = _attn_ctx_call(p_c, lp["sink"], bsz, ctx_len)
            br_conv_c = _conv_call(p_c, ctx_len, lp["conv_w"])
            m_c = _merge_call(br_rwkv_c, br_conv_c, br_attn_c, p_c, lp["w_branch"])
            x_c = _outproj_call(m_c, lp["w_out"], x_c, mod3, row_c, tm2)
            x_c = _ffn(x_c, bsz, ctx_len, lp, mod3, row_c, lambda b: bsz, tm2)

    return x_l.reshape(bsz, seq_len, D)
```

```python
import functools

import jax
import jax.numpy as jnp
from jax import lax
from jax.experimental import pallas as pl
from jax.experimental.pallas import tpu as pltpu

F32 = jnp.float32
BF16 = jnp.bfloat16
HIGHEST = lax.Precision.HIGHEST

D = 2048
HD = 64
BW = 1024
NH = BW // HD
KVH = 4
GQ = NH // KVH
DECAY_RANK = 96
ICLR_RANK = 96
GATE_RANK = 256
VRES_RANK = 64
GN_EPS = 64e-5
NORM_EPS = 1e-6
WINDOW = 128
BLK = 128
assert WINDOW == BLK
GRID_W = 64
ROPE_BASE = 10000.0
NEG_INF = -1e30
N_EXPERTS = 16
EXPERT_FF = 2048
CAPACITY_FACTOR = 2
CHUNK = 64

C_R, C_K, C_V = 0, 1024, 2048
C_WD = (3072, 3200)
C_AD = (3328, 3456)
C_GD = 3584
RW_PAD = 4096
C_Q = 4096
C_CONV = 5120
C_GATE = 8192
C_AK = 14336
C_AV = 14592
NP = 14848
TN_IN = 512

VMEM_LIMIT = 56 * 1024 * 1024


def _cp(*sem, vmem=VMEM_LIMIT):
    return pltpu.CompilerParams(dimension_semantics=tuple(sem), vmem_limit_bytes=vmem)


def _dot(a, b):
    return jnp.dot(a, b, preferred_element_type=F32)


def _dot_nt(a, b):
    return lax.dot_general(a, b, (((1,), (1,)), ((), ())), preferred_element_type=F32)


def _dot_tn(a, b):
    return lax.dot_general(a, b, (((0,), (0,)), ((), ())), preferred_element_type=F32)


def _sigmoid(x):
    return 1.0 / (1.0 + jnp.exp(-x))


def _seg_sum(x, e_ref):
    e = e_ref[...]

    def split(z):
        hi = z.astype(BF16)
        return hi, (z - hi.astype(F32)).astype(BF16)

    hi, lo = split(x)
    hi2, lo2 = split(_dot(hi, e) + _dot(lo, e))
    return _dot_nt(hi2, e) + _dot_nt(lo2, e)


def _mod_kernel(c_ref, w_ref, b_ref, o_ref):
    c = c_ref[...]
    sc = c * _sigmoid(c)
    o_ref[0] = _dot(sc.astype(BF16), w_ref[0].astype(BF16)) + b_ref[0]


def _mod_call(cc, w_mod, b_mod):
    depth, _, n6 = w_mod.shape
    rows = cc.shape[0]
    tn = 1024
    return pl.pallas_call(
        _mod_kernel,
        grid=(depth, n6 // tn),
        in_specs=[pl.BlockSpec((rows, D), lambda l, j: (0, 0)),
                  pl.BlockSpec((1, D, tn), lambda l, j: (l, 0, j)),
                  pl.BlockSpec((1, 1, tn), lambda l, j: (l, 0, j))],
        out_specs=pl.BlockSpec((1, rows, tn), lambda l, j: (l, 0, j)),
        out_shape=jax.ShapeDtypeStruct((depth, rows, n6), F32),
        compiler_params=_cp("arbitrary", "arbitrary"),
        name="mod_proj",
    )(cc, w_mod, b_mod.reshape(depth, 1, n6))


def _norm_mod(x, gain, shift, scale):
    ms = jnp.mean(x * x, axis=-1, keepdims=True)
    y = x * lax.rsqrt(ms + NORM_EPS)
    return (y * gain) * (1.0 + scale) + shift


def _inproj_kernel(x_ref, gain_ref, shift_ref, scale_ref, w_ref, o_ref, h_ref):
    @pl.when(pl.program_id(1) == 0)
    def _():
        h_ref[...] = _norm_mod(x_ref[...], gain_ref[...], shift_ref[0], scale_ref[0]).astype(BF16)

    o_ref[...] = _dot(h_ref[...], w_ref[0]).astype(BF16)


def _inproj_call(x2, gain, mod3, mod_row, w_p, layer, tm, rwkv_kv_only=False):
    n_tok = x2.shape[0]
    n_rw = RW_PAD // TN_IN
    if rwkv_kv_only:
        ncols = RW_PAD + NP - C_AK
        col = lambda j: jnp.where(j < n_rw, j, j - n_rw + C_AK // TN_IN)
    else:
        ncols = NP
        col = lambda j: j
    return pl.pallas_call(
        _inproj_kernel,
        grid=(n_tok // tm, ncols // TN_IN),
        in_specs=[pl.BlockSpec((tm, D), lambda i, j: (i, 0)),
                  pl.BlockSpec((1, D), lambda i, j: (0, 0)),
                  pl.BlockSpec((1, 1, D), lambda i, j: (mod_row(i) * 6 + 0, 0, 0)),
                  pl.BlockSpec((1, 1, D), lambda i, j: (mod_row(i) * 6 + 1, 0, 0)),
                  pl.BlockSpec((1, D, TN_IN), lambda i, j: (layer, 0, col(j)))],
        out_specs=pl.BlockSpec((tm, TN_IN), lambda i, j: (i, j)),
        out_shape=jax.ShapeDtypeStruct((n_tok, ncols), BF16),
        scratch_shapes=[pltpu.VMEM((tm, D), BF16)],
        compiler_params=_cp("arbitrary", "arbitrary"),
        name="in_proj",
    )(x2, gain.reshape(1, D), mod3, mod3, w_p)


HALO = 16


def _shifted(cur, prev_blk, next_blk, first, last):
    tt = cur.shape[0]
    row = lax.broadcasted_iota(jnp.int32, cur.shape, 0)
    p_row = jnp.where(first, 0.0, prev_blk[HALO - 1:HALO, :].astype(F32))
    n_row = jnp.where(last, 0.0, next_blk[0:1, :].astype(F32))
    prev = jnp.where(row == 0, p_row, pltpu.roll(cur, 1, axis=0))
    nxt = jnp.where(row == tt - 1, n_row, pltpu.roll(cur, tt - 1, axis=0))
    return prev, nxt


def _halo_specs(tt, width, col_blk, n_tok):
    nb = n_tok // HALO
    r = tt // HALO
    return [pl.BlockSpec((tt, width), lambda i: (i, col_blk)),
            pl.BlockSpec((HALO, width), lambda i: (jnp.maximum(i * r - 1, 0), col_blk)),
            pl.BlockSpec((HALO, width), lambda i: (jnp.minimum((i + 1) * r, nb - 1), col_blk))]


def _streams_kernel(*refs, tiles_per_seq, has_vres):
    if has_vres:
        (p_ref, pp_ref, pn_ref, mu_ref, dup_ref, dbias_ref, iup_ref, ibias_ref, gup_ref, kk_ref_, e_ref,
         vd_ref, vu_ref, vb_ref, vf_ref,
         r_o, k_o, v_o, kk_o, g_o, a_o, lw_o) = refs
    else:
        (p_ref, pp_ref, pn_ref, mu_ref, dup_ref, dbias_ref, iup_ref, ibias_ref, gup_ref, kk_ref_, e_ref,
         r_o, k_o, v_o, kk_o, g_o, a_o, lw_o) = refs
    j = pl.program_id(0) % tiles_per_seq
    used = C_GD + GATE_RANK
    cur_b = p_ref[:, :used]
    tt = cur_b.shape[0]
    ext = jnp.concatenate([pp_ref[:, :used], cur_b, pn_ref[:, :used]], axis=0)
    t_i = lax.broadcasted_iota(jnp.int32, (tt, tt + 2 * HALO), 0)
    e_i = lax.broadcasted_iota(jnp.int32, (tt, tt + 2 * HALO), 1) - HALO
    tap = ((e_i == t_i - 1) & ((e_i >= 0) | (j > 0))) | ((e_i == t_i + 1) & ((e_i < tt) | (j < tiles_per_seq - 1)))
    avg = _dot(jnp.where(tap, 0.5, 0.0).astype(BF16), ext)
    cur = cur_b.astype(F32)
    ps = cur + mu_ref[:, :used] * (avg - cur)
    r = ps[:, C_R:C_R + BW]
    k = ps[:, C_K:C_K + BW]
    v = ps[:, C_V:C_V + BW]
    gd = ps[:, C_GD:C_GD + GATE_RANK]
    if has_vres:
        low = _dot(v.astype(BF16), vd_ref[...])
        mix = _sigmoid(vb_ref[...] + _dot(low.astype(BF16), vu_ref[...]))
        v = v + (vf_ref[...].astype(F32) - v) * mix
    for d in range(2):
        wd = ps[:, C_WD[d]:C_WD[d] + 128]
        ad = ps[:, C_AD[d]:C_AD[d] + 128]
        w_logit = dbias_ref[d] + _dot(jnp.tanh(wd).astype(BF16), dup_ref[d])
        lw_o[d] = -jnp.exp(-0.5) * _sigmoid(w_logit)
        a_o[d] = _sigmoid(ibias_ref[d] + _dot(ad.astype(BF16), iup_ref[d])).astype(BF16)
    kh = k * kk_ref_[...]
    ss = _seg_sum(kh * kh, e_ref)
    kk = kh * lax.rsqrt(jnp.maximum(ss, 1e-24))
    r_o[...] = r.astype(BF16)
    k_o[...] = k.astype(BF16)
    v_o[...] = v.astype(BF16)
    kk_o[...] = kk.astype(BF16)
    g_o[...] = _dot(_sigmoid(gd).astype(BF16), gup_ref[...]).astype(BF16)


def _streams_call(p, seq_len, lp, v_first):
    n_tok = p.shape[0]
    tt = 256
    has_vres = v_first is not None
    full = lambda *s: pl.BlockSpec(s, lambda i: (0,) * len(s))
    tok = pl.BlockSpec((tt, BW), lambda i: (i, 0))
    tok2 = pl.BlockSpec((2, tt, BW), lambda i: (0, i, 0))
    in_specs = _halo_specs(tt, C_GD + GATE_RANK, 0, n_tok) + [
        full(1, RW_PAD), full(2, 128, BW), full(2, 1, BW), full(2, 128, BW), full(2, 1, BW),
        full(GATE_RANK, BW), full(1, BW), full(BW, 128)]
    args = [p, p, p, lp["mu"], lp["decay_up"], lp["decay_bias"], lp["iclr_up"], lp["iclr_bias"],
            lp["gate_up"], lp["k_k"], lp["eblk"]]
    if has_vres:
        in_specs += [full(BW, 128), full(128, BW), full(1, BW), tok]
        args += [lp["vres_down"], lp["vres_up"], lp["vres_bias"], v_first]
    sd = lambda dt: jax.ShapeDtypeStruct((n_tok, BW), dt)
    sd2 = lambda dt: jax.ShapeDtypeStruct((2, n_tok, BW), dt)
    return pl.pallas_call(
        functools.partial(_streams_kernel, tiles_per_seq=seq_len // tt, has_vres=has_vres),
        grid=(n_tok // tt,),
        in_specs=in_specs,
        out_specs=[tok, tok, tok, tok, tok, tok2, tok2],
        out_shape=[sd(BF16), sd(BF16), sd(BF16), sd(BF16), sd(BF16), sd2(BF16), sd2(F32)],
        compiler_params=_cp("arbitrary"),
        name="rwkv_streams",
    )(*args)


def _scan_prologue(d, r, k, v, kk, a, lw, ka):
    c = CHUNK
    ri = lax.broadcasted_iota(jnp.int32, (c, 2 * c), 0)
    ci = lax.broadcasted_iota(jnp.int32, (c, 2 * c), 1) % c
    diff = (ci - ri) if d else (ri - ci)
    incl = diff >= 0
    strict = diff > 0
    tri = incl[:, :c].astype(BF16)
    lw_hi = lw.astype(BF16)
    lw_lo = (lw - lw_hi.astype(F32)).astype(BF16)
    b = _dot(tri, lw_hi) + _dot(tri, lw_lo)
    b_tot = b[0:1, :] if d else b[c - 1:c, :]
    a = a.astype(F32)
    kk = kk.astype(F32)
    kd = k.astype(F32) * (1.0 + (a - 1.0) * ka)
    kka = kk * a
    enb = jnp.exp(-b)
    etail = jnp.exp(b_tot - b)
    return dict(
        incl=incl, strict=strict,
        rt=(r.astype(F32) * jnp.exp(b)).astype(BF16),
        bt=(-kk * jnp.exp(b - lw)).astype(BF16),
        kt=(kd * enb).astype(BF16), at=(kka * enb).astype(BF16),
        kh=(kd * etail).astype(BF16), ah=(kka * etail).astype(BF16),
        vv=v, e_tot=jnp.exp(b_tot))


def _scan_kernel(*refs, with_output, sb):
    ins = refs[:14]
    if with_output:
        yf_ref, yb_ref, sout_ref, st_ref = refs[14:]
    else:
        sout_ref, st_ref = refs[14:]
    ka_ref, s0_ref = ins[12], ins[13]
    s = pl.program_id(1)
    c = CHUNK

    @pl.when(s == 0)
    def _():
        st_ref[...] = s0_ref[...]

    streams = [(d, j) for d in range(2) for j in range(sb)]
    ka = ka_ref[...]
    pro = [_scan_prologue(d, *[ref[j] for ref in ins[6 * d:6 * d + 4]], ins[6 * d + 4][0, j], ins[6 * d + 5][0, j], ka)
           for d, j in streams]
    st_all = [st_ref[d, j] for d, j in streams]

    npair = NH // 2
    units = [(si, slice(p * 2 * HD, (p + 1) * 2 * HD)) for si in range(len(streams)) for p in range(npair)]
    us = range(len(units))
    col = lambda name: [pro[d][name][:, sl] for d, sl in units]
    even1 = lax.broadcasted_iota(jnp.int32, (c, 2 * HD), 1) < HD

    def bd(z):
        zero = jnp.zeros_like(z)
        return jnp.concatenate([jnp.where(even1, z, zero), jnp.where(even1, zero, z)], axis=0)

    strict = [pro[d]["strict"] for d, _ in units]
    incl = [pro[d]["incl"] for d, _ in units]
    st = [st_all[d][:, sl] for d, sl in units]
    vbd = [bd(z) for z in col("vv")]
    rb = [jnp.concatenate(p, axis=0) for p in zip(col("rt"), col("bt"))]
    gk = [_dot_nt(rb[u], bd(z)) for u, z in zip(us, col("kt"))]
    ga = [_dot_nt(rb[u], bd(z)) for u, z in zip(us, col("at"))]
    rbs = [_dot_nt(rb[u], bd(st[u].astype(BF16))) for u in us]
    x = [jnp.where(strict[u], ga[u][c:], 0.0).astype(BF16) for u in us]
    a_bk = [jnp.where(strict[u], gk[u][c:], 0.0) for u in us]
    if with_output:
        a_rk = [jnp.where(incl[u], gk[u][:c], 0.0) for u in us]
        a_ra = [jnp.where(incl[u], ga[u][:c], 0.0).astype(BF16) for u in us]
        av = [_dot(jnp.concatenate([a_rk[u], a_bk[u]], axis=0).astype(BF16), vbd[u]) for u in us]
        w = [rbs[u][c:] + av[u][c:] for u in us]
    else:
        w = [rbs[u][c:] + _dot(a_bk[u].astype(BF16), vbd[u]) for u in us]
    for rnd in range(6):
        if rnd < 5:
            m = [_dot(x[u], jnp.concatenate([bd(x[u]), bd(w[u].astype(BF16))], axis=1)) for u in us]
            x = [m[u][:, :2 * c].astype(BF16) for u in us]
            w = [w[u] + m[u][:, 2 * c:] for u in us]
        else:
            w = [w[u] + _dot(x[u], bd(w[u].astype(BF16))) for u in us]
    ub = [w[u].astype(BF16) for u in us]
    if with_output:
        ys = [rbs[u][:c] + av[u][:c] + _dot(a_ra[u], bd(ub[u])) for u in us]
        for si, (d, j) in enumerate(streams):
            (yb_ref if d else yf_ref)[j] = jnp.concatenate(ys[si * npair:(si + 1) * npair], axis=1)
    full = [_dot_tn(jnp.concatenate([z, ub[u]], axis=0), jnp.concatenate(p, axis=0))
            for u, z, p in zip(us, col("vv"), zip(col("kh"), col("ah")))]
    upd = [jnp.where(even1, f[:HD], f[HD:]) for f in full]
    for si, (d, j) in enumerate(streams):
        st_ref[d, j] = st_all[si] * pro[si]["e_tot"] + jnp.concatenate(upd[si * npair:(si + 1) * npair], axis=1)

    @pl.when(s == pl.num_programs(1) - 1)
    def _():
        sout_ref[...] = st_ref[...]


def _scan_call(streams, k_a, s0, bsz, seq_len, with_output):
    n_tok = streams[0].shape[0]
    nc = seq_len // CHUNK
    sb = 2 if bsz % 2 == 0 else 1
    r, k, v, kk = (z.reshape(bsz, seq_len, BW) for z in streams[:4])
    a2, lw2 = (z.reshape(2, bsz, seq_len, BW) for z in streams[5:])

    chunk = (lambda s: s, lambda s: nc - 1 - s)
    tok = [pl.BlockSpec((sb, CHUNK, BW), lambda b, s, f=f: (b, f(s), 0)) for f in chunk]
    tok2 = [pl.BlockSpec((1, sb, CHUNK, BW), lambda b, s, f=f, d=d: (d, b, f(s), 0)) for d, f in enumerate(chunk)]
    st_spec = pl.BlockSpec((2, sb, HD, BW), lambda b, s: (0, b, 0, 0))
    in_specs, args = [], []
    for d in range(2):
        in_specs += [tok[d]] * 4 + [tok2[d]] * 2
        args += [r, k, v, kk, a2, lw2]
    in_specs += [pl.BlockSpec((1, BW), lambda b, s: (0, 0)), st_spec]
    out_specs = [st_spec]
    out_shape = [jax.ShapeDtypeStruct((2, bsz, HD, BW), F32)]
    if with_output:
        out_specs = tok + out_specs
        out_shape = [jax.ShapeDtypeStruct((bsz, seq_len, BW), F32)] * 2 + out_shape
    res = pl.pallas_call(
        functools.partial(_scan_kernel, with_output=with_output, sb=sb),
        grid=(bsz // sb, nc),
        in_specs=in_specs,
        out_specs=out_specs,
        out_shape=out_shape,
        scratch_shapes=[pltpu.VMEM((2, sb, HD, BW), F32)],
        compiler_params=_cp("arbitrary", "arbitrary"),
        name="rwkv_scan",
    )(*args, k_a, s0)
    if with_output:
        return res[2], (res[0].reshape(n_tok, BW), res[1].reshape(n_tok, BW))
    return res[0], None


def _rwkv_out_kernel(yf_ref, yb_ref, r_ref, k_ref, v_ref, a_ref, g_ref, ka_ref, rk_ref, gnw_ref, gnb_ref, e_ref,
                     o_ref):
    y = yf_ref[...] + yb_ref[...]
    mean = _seg_sum(y, e_ref) * (1.0 / HD)
    yc = y - mean
    var = _seg_sum(yc * yc, e_ref) * (1.0 / HD)
    yn = yc * lax.rsqrt(var + GN_EPS) * gnw_ref[...] + gnb_ref[...]
    r = r_ref[...].astype(F32)
    k = k_ref[...].astype(F32)
    asum = a_ref[0].astype(F32) + a_ref[1].astype(F32)
    kd_sum = k * (2.0 + (asum - 2.0) * ka_ref[...])
    bonus = _seg_sum(r * kd_sum * rk_ref[...], e_ref) * v_ref[...].astype(F32)
    o_ref[...] = ((yn + bonus) * g_ref[...].astype(F32)).astype(BF16)


def _rwkv_out_call(y2, streams, lp):
    r, k, v, _, g, a2, _ = streams
    n_tok = r.shape[0]
    tt = 256
    tok = pl.BlockSpec((tt, BW), lambda i: (i, 0))
    tok2 = pl.BlockSpec((2, tt, BW), lambda i: (0, i, 0))
    vec = pl.BlockSpec((1, BW), lambda i: (0, 0))
    return pl.pallas_call(
        _rwkv_out_kernel,
        grid=(n_tok // tt,),
        in_specs=[tok, tok, tok, tok, tok, tok2, tok, vec, vec, vec, vec, pl.BlockSpec((BW, 128), lambda i: (0, 0))],
        out_specs=tok,
        out_shape=jax.ShapeDtypeStruct((n_tok, BW), BF16),
        compiler_params=_cp("arbitrary"),
        name="rwkv_out",
    )(y2[0], y2[1], r, k, v, a2, g, lp["k_a"], lp["r_k"], lp["gn_w"], lp["gn_b"], lp["eblk"])


def _conv_kernel(b_ref, c_ref, cp_ref, cn_ref, u_ref, up_ref, un_ref, w_ref, o_ref, *, tiles_per_seq):
    j = pl.program_id(0) % tiles_per_seq
    cu = c_ref[...].astype(F32) * u_ref[...].astype(F32)
    cu_p = cp_ref[...].astype(F32) * up_ref[...].astype(F32)
    cu_n = cn_ref[...].astype(F32) * un_ref[...].astype(F32)
    prev, nxt = _shifted(cu, cu_p, cu_n, j == 0, j == tiles_per_seq - 1)
    w = w_ref[...]
    conv = w[0:1] * prev + w[1:2] * cu + w[2:3] * nxt
    o_ref[...] = (b_ref[...].astype(F32) * conv).astype(BF16)


def _conv_call(p, seq_len, conv_w):
    n_tok = p.shape[0]
    tt = 256
    cb = C_CONV // BW
    return pl.pallas_call(
        functools.partial(_conv_kernel, tiles_per_seq=seq_len // tt),
        grid=(n_tok // tt,),
        in_specs=[pl.BlockSpec((tt, BW), lambda i: (i, cb))] + _halo_specs(tt, BW, cb + 1, n_tok)
        + _halo_specs(tt, BW, cb + 2, n_tok) + [pl.BlockSpec((3, BW), lambda i: (0, 0))],
        out_specs=pl.BlockSpec((tt, BW), lambda i: (i, 0)),
        out_shape=jax.ShapeDtypeStruct((n_tok, BW), BF16),
        compiler_params=_cp("arbitrary"),
        name="short_conv",
    )(p, p, p, p, p, p, p, conv_w)


def _rope(x, cos, sin):
    w = x.shape[1]
    lane = lax.broadcasted_iota(jnp.int32, x.shape, 1)
    partner = jnp.where((lane % 32) < 16, pltpu.roll(x, w - 16, axis=1), pltpu.roll(x, 16, axis=1))
    return x * cos + partner * sin


LOG2E = 1.4426950408889634
Q_SCALE = HD ** -0.5 * LOG2E


def _softmax_pv(s2, sink2_col, v):
    m = jnp.maximum(jnp.max(s2, axis=-1, keepdims=True), sink2_col)
    p = jnp.exp2(s2 - m)
    den = jnp.sum(p, axis=-1, keepdims=True) + jnp.exp2(sink2_col - m)
    return _dot(p.astype(BF16), v) / den


def _attn_lat_kernel(q_ref, kp_ref, kc_ref, kn_ref, vp_ref, vc_ref, vn_ref, kx_ref, vx_ref,
                     cos_ref, cosp_ref, cosn_ref, sin_ref, sinp_ref, sinn_ref, sink_ref, o_ref, *, nblk):
    n = pl.program_id(1)
    kvw = KVH * HD
    q = (_rope(q_ref[...].astype(F32), cos_ref[...], sin_ref[...]) * Q_SCALE).astype(BF16)
    kb = jnp.concatenate([
        _rope(kp_ref[...].astype(F32), cosp_ref[:, :kvw], sinp_ref[:, :kvw]),
        _rope(kc_ref[...].astype(F32), cos_ref[:, :kvw], sin_ref[:, :kvw]),
        _rope(kn_ref[...].astype(F32), cosn_ref[:, :kvw], sinn_ref[:, :kvw])], axis=0).astype(BF16)
    k_all = jnp.concatenate([kb, kx_ref[...]], axis=0)
    v_all = jnp.concatenate([vp_ref[...], vc_ref[...], vn_ref[...], vx_ref[...]], axis=0)
    cols = GQ * BLK
    kidx = lax.broadcasted_iota(jnp.int32, (BLK, cols), 0)
    qpos = lax.broadcasted_iota(jnp.int32, (BLK, cols), 1) % BLK
    bias_p = jnp.where((kidx >= qpos) & (n > 0), 0.0, NEG_INF)
    bias_n = jnp.where((kidx <= qpos) & (n < nblk - 1), 0.0, NEG_INF)
    sink = sink_ref[...] * LOG2E
    gs = range(KVH)
    even = lax.broadcasted_iota(jnp.int32, (BLK, 2 * HD), 1) < HD
    zero = jnp.zeros((BLK, 2 * HD), BF16)
    eye = (lax.broadcasted_iota(jnp.int32, (2 * HD, 2 * HD), 0)
           == lax.broadcasted_iota(jnp.int32, (2 * HD, 2 * HD), 1)).astype(BF16)

    def head_rows(gi):
        parts = []
        for t in range(GQ // 2):
            qp = q[:, (gi * GQ + 2 * t) * HD:(gi * GQ + 2 * t + 2) * HD]
            parts += [jnp.where(even, qp, zero), jnp.where(even, zero, qp)]
        return jnp.concatenate(parts, axis=0)

    qg = [head_rows(gi) for gi in gs]
    kd = [jnp.concatenate([k_all[:, gi * HD:(gi + 1) * HD]] * 2, axis=1) for gi in gs]
    vt = [_dot_nt(eye[:HD, :HD], v_all[:, gi * HD:(gi + 1) * HD]).astype(BF16) for gi in gs]
    sk = [jnp.concatenate([jnp.broadcast_to(sink[:, gi * GQ + t:gi * GQ + t + 1], (1, BLK))
                           for t in range(GQ)], axis=1) for gi in gs]
    s = [_dot_nt(kd[gi], qg[gi]) for gi in gs]
    s = [jnp.concatenate([z[:BLK] + bias_p, z[BLK:2 * BLK], z[2 * BLK:3 * BLK] + bias_n, z[3 * BLK:]], axis=0)
         for z in s]
    m = [jnp.maximum(jnp.max(s[gi], axis=0, keepdims=True), sk[gi]) for gi in gs]
    p = [jnp.exp2(s[gi] - m[gi]).astype(BF16) for gi in gs]
    ones = jnp.ones((16, vt[0].shape[1]), BF16)
    pv = [_dot(jnp.concatenate([vt[gi], ones], axis=0), p[gi]) for gi in gs]
    og = [(pv[gi][:HD] / (pv[gi][HD:HD + 1] + jnp.exp2(sk[gi] - m[gi]))).astype(BF16) for gi in gs]
    o_ref[...] = jnp.concatenate(
        [_dot_nt(eye, jnp.concatenate([og[gi][:, 2 * t * BLK:(2 * t + 1) * BLK],
                                       og[gi][:, (2 * t + 1) * BLK:(2 * t + 2) * BLK]], axis=0))
         for gi in gs for t in range(GQ // 2)], axis=1).astype(BF16)


def _attn_lat_call(p_l, p_c, kv_blk_c, cos, sin, sink, bsz, seq_len, ctx_len):
    nblk = seq_len // BLK
    kvw = KVH * HD
    kb, vb = C_AK // kvw, C_AV // kvw
    rowq = lambda b, n: b * nblk + n
    rowp = lambda b, n: b * nblk + jnp.maximum(n - 1, 0)
    rown = lambda b, n: b * nblk + jnp.minimum(n + 1, nblk - 1)
    tabp = lambda b, n: (jnp.maximum(n - 1, 0), 0)
    tabn = lambda b, n: (jnp.minimum(n + 1, nblk - 1), 0)
    kv = lambda rf, cb: pl.BlockSpec((BLK, kvw), lambda b, n: (rf(b, n), cb))
    tab = lambda f: pl.BlockSpec((BLK, BW), f)
    return pl.pallas_call(
        functools.partial(_attn_lat_kernel, nblk=nblk),
        grid=(bsz, nblk),
        in_specs=[pl.BlockSpec((BLK, BW), lambda b, n: (rowq(b, n), C_Q // BW)),
                  kv(rowp, kb), kv(rowq, kb), kv(rown, kb), kv(rowp, vb), kv(rowq, vb), kv(rown, vb),
                  pl.BlockSpec((ctx_len, kvw), lambda b, n: (b, kv_blk_c)),
                  pl.BlockSpec((ctx_len, kvw), lambda b, n: (b, kv_blk_c + 1)),
                  tab(lambda b, n: (n, 0)), tab(tabp), tab(tabn),
                  tab(lambda b, n: (n, 0)), tab(tabp), tab(tabn),
                  pl.BlockSpec((1, NH), lambda b, n: (0, 0))],
        out_specs=pl.BlockSpec((BLK, BW), lambda b, n: (rowq(b, n), 0)),
        out_shape=jax.ShapeDtypeStruct((bsz * seq_len, BW), BF16),
        compiler_params=_cp("arbitrary", "arbitrary"),
        name="attn_latent",
    )(p_l, p_l, p_l, p_l, p_l, p_l, p_l, p_c, p_c, cos, cos, cos, sin, sin, sin, sink)


def _attn_ctx_kernel(q_ref, kx_ref, vx_ref, sink_ref, o_ref):
    q = (q_ref[...].astype(F32) * Q_SCALE).astype(BF16)
    kx = kx_ref[...]
    vx = vx_ref[...]
    sink = sink_ref[...] * LOG2E
    tq = q.shape[0]
    outs = []
    for gi in range(KVH):
        qg = jnp.concatenate([q[:, (gi * GQ + t) * HD:(gi * GQ + t + 1) * HD] for t in range(GQ)], axis=0)
        sk = jnp.concatenate([jnp.broadcast_to(sink[:, gi * GQ + t:gi * GQ + t + 1], (tq, 1))
                              for t in range(GQ)], axis=0)
        s = _dot_nt(qg, kx[:, gi * HD:(gi + 1) * HD])
        og = _softmax_pv(s, sk, vx[:, gi * HD:(gi + 1) * HD])
        outs += [og[t * tq:(t + 1) * tq] for t in range(GQ)]
    o_ref[...] = jnp.concatenate(outs, axis=1).astype(BF16)


def _attn_ctx_call(p_c, sink, bsz, ctx_len):
    kvw = KVH * HD
    tq = 128
    nq = ctx_len // tq
    return pl.pallas_call(
        _attn_ctx_kernel,
        grid=(bsz, nq),
        in_specs=[pl.BlockSpec((tq, BW), lambda b, n: (b * nq + n, C_Q // BW)),
                  pl.BlockSpec((ctx_len, kvw), lambda b, n: (b, C_AK // kvw)),
                  pl.BlockSpec((ctx_len, kvw), lambda b, n: (b, C_AV // kvw)),
                  pl.BlockSpec((1, NH), lambda b, n: (0, 0))],
        out_specs=pl.BlockSpec((tq, BW), lambda b, n: (b * nq + n, 0)),
        out_shape=jax.ShapeDtypeStruct((bsz * ctx_len, BW), BF16),
        compiler_params=_cp("arbitrary", "arbitrary"),
        name="attn_context",
    )(p_c, p_c, p_c, sink)


def _merge_kernel(b0_ref, b1_ref, b2_ref, g0_ref, g1_ref, g2_ref, w_ref, o_ref):
    acc = _sigmoid(g0_ref[...].astype(F32)) * _dot(b0_ref[...], w_ref[0])
    acc += _sigmoid(g1_ref[...].astype(F32)) * _dot(b1_ref[...], w_ref[1])
    acc += _sigmoid(g2_ref[...].astype(F32)) * _dot(b2_ref[...], w_ref[2])
    o_ref[...] = acc.astype(BF16)


def _merge_call(br_rwkv, br_conv, br_attn, p, w_branch):
    n_tok = p.shape[0]
    tm, tn = 512, 1024
    nn = D // tn
    br = pl.BlockSpec((tm, BW), lambda j, i: (i, 0))
    gate = lambda t: pl.BlockSpec((tm, tn), lambda j, i: (i, (C_GATE + t * D) // tn + j))
    return pl.pallas_call(
        _merge_kernel,
        grid=(nn, n_tok // tm),
        in_specs=[br, br, br, gate(0), gate(1), gate(2), pl.BlockSpec((3, BW, tn), lambda j, i: (0, 0, j))],
        out_specs=pl.BlockSpec((tm, tn), lambda j, i: (i, j)),
        out_shape=jax.ShapeDtypeStruct((n_tok, D), BF16),
        compiler_params=_cp("arbitrary", "arbitrary"),
        name="merge_branches",
    )(br_rwkv, br_conv, br_attn, p, p, p, w_branch)


def _outproj_kernel(m_ref, w_ref, x_ref, gate_ref, o_ref):
    o_ref[...] = x_ref[...] + gate_ref[0] * _dot(m_ref[...], w_ref[...])


def _outproj_call(m, w_out, x2, mod3, mod_row, tm):
    n_tok = x2.shape[0]
    return pl.pallas_call(
        _outproj_kernel,
        grid=(n_tok // tm,),
        in_specs=[pl.BlockSpec((tm, D), lambda i: (i, 0)),
                  pl.BlockSpec((D, D), lambda i: (0, 0)),
                  pl.BlockSpec((tm, D), lambda i: (i, 0)),
                  pl.BlockSpec((1, 1, D), lambda i: (mod_row(i) * 6 + 2, 0, 0))],
        out_specs=pl.BlockSpec((tm, D), lambda i: (i, 0)),
        out_shape=jax.ShapeDtypeStruct((n_tok, D), F32),
        compiler_params=_cp("arbitrary"),
        name="out_proj",
    )(m, w_out, x2, mod3)


def _ffn_prep_kernel(x_ref, gain_ref, shift_ref, scale_ref, wr_ref, h_ref, aff_ref):
    h = _norm_mod(x_ref[...], gain_ref[...], shift_ref[0], scale_ref[0])
    h_hi = h.astype(BF16)
    h_ref[...] = h_hi
    h_lo = (h - h_hi.astype(F32)).astype(BF16)
    wr = wr_ref[...]
    w_hi = wr.astype(BF16)
    w_lo = (wr - w_hi.astype(F32)).astype(BF16)
    logits = _dot(h_hi, w_hi) + (_dot(h_hi, w_lo) + _dot(h_lo, w_hi))
    lane = lax.broadcasted_iota(jnp.int32, logits.shape, 1)
    logits = jnp.where(lane < N_EXPERTS, logits, NEG_INF)
    m = jnp.max(logits, axis=-1, keepdims=True)
    e = jnp.exp(logits - m)
    aff_ref[...] = e / jnp.sum(e, axis=-1, keepdims=True)


def _ffn_prep_call(x2, gain, mod3, mod_row, wr_pad, tm):
    n_tok = x2.shape[0]
    return pl.pallas_call(
        _ffn_prep_kernel,
        grid=(n_tok // tm,),
        in_specs=[pl.BlockSpec((tm, D), lambda i: (i, 0)),
                  pl.BlockSpec((1, D), lambda i: (0, 0)),
                  pl.BlockSpec((1, 1, D), lambda i: (mod_row(i) * 6 + 3, 0, 0)),
                  pl.BlockSpec((1, 1, D), lambda i: (mod_row(i) * 6 + 4, 0, 0)),
                  pl.BlockSpec((D, 128), lambda i: (0, 0))],
        out_specs=[pl.BlockSpec((tm, D), lambda i: (i, 0)), pl.BlockSpec((tm, 128), lambda i: (i, 0))],
        out_shape=[jax.ShapeDtypeStruct((n_tok, D), BF16), jax.ShapeDtypeStruct((n_tok, 128), F32)],
        compiler_params=_cp("arbitrary"),
        name="ffn_prep",
    )(x2, gain.reshape(1, D), mod3, mod3, wr_pad)


def _select_kernel(aff_ref, slot_ref, *, cap):
    a = aff_ref[0]
    n = a.shape[1]
    bits = lax.bitcast_convert_type(a, jnp.int32)

    def body(i, t):
        cand = t | jnp.left_shift(jnp.int32(1), 30 - i)
        cnt = jnp.sum((bits >= cand).astype(jnp.int32), axis=-1, keepdims=True)
        return jnp.where(cnt >= cap, cand, t)

    thr = lax.fori_loop(0, 31, body, jnp.zeros((a.shape[0], 1), jnp.int32))
    gt = bits > thr
    eq = bits == thr
    n_gt = jnp.sum(gt.astype(jnp.int32), axis=-1, keepdims=True)
    tc = min(n, 512)

    def prefix(mask_bf16):
        cols = []
        for j0 in range(0, n, tc):
            ri = lax.broadcasted_iota(jnp.int32, (n, tc), 0)
            ci = lax.broadcasted_iota(jnp.int32, (n, tc), 1) + j0
            cols.append(_dot(mask_bf16, (ri <= ci).astype(BF16)))
        return jnp.concatenate(cols, axis=1)

    eq_f = eq.astype(BF16)
    excl_eq = prefix(eq_f) - eq_f.astype(F32)
    sel = gt | (eq & (excl_eq < (cap - n_gt).astype(F32)))
    pos = prefix(sel.astype(BF16))
    slot_ref[0] = jnp.where(sel, pos.astype(jnp.int32) - 1, -1)


def _select_call(aff_t, cap):
    bsz, ne, n = aff_t.shape
    return pl.pallas_call(
        functools.partial(_select_kernel, cap=cap),
        grid=(bsz,),
        in_specs=[pl.BlockSpec((1, ne, n), lambda b: (b, 0, 0))],
        out_specs=pl.BlockSpec((1, ne, n), lambda b: (b, 0, 0)),
        out_shape=jax.ShapeDtypeStruct((bsz, ne, n), jnp.int32),
        compiler_params=_cp("arbitrary"),
        name="expert_select",
    )(aff_t)


def _gather_kernel(slot_ref, aff_ref, h_ref, xs_ref, g_ref, *, cap):
    slot = slot_ref[0]
    n = slot.shape[1]
    onehot = lax.broadcasted_iota(jnp.int32, (cap, n), 0) == slot
    xs_ref[0] = _dot(onehot.astype(BF16), h_ref[...]).astype(BF16)
    g_ref[0] = jnp.sum(jnp.where(onehot, aff_ref[0], 0.0), axis=-1, keepdims=True)


def _gather_call(slot, aff_t, h2, cap):
    bsz, ne, n = slot.shape
    return pl.pallas_call(
        functools.partial(_gather_kernel, cap=cap),
        grid=(bsz, ne),
        in_specs=[pl.BlockSpec((1, 1, n), lambda b, e: (b * ne + e, 0, 0)),
                  pl.BlockSpec((1, 1, n), lambda b, e: (b * ne + e, 0, 0)),
                  pl.BlockSpec((n, D), lambda b, e: (b, 0))],
        out_specs=[pl.BlockSpec((1, cap, D), lambda b, e: (e, b, 0)),
                   pl.BlockSpec((1, cap, 1), lambda b, e: (e, b, 0))],
        out_shape=[jax.ShapeDtypeStruct((ne, bsz * cap, D), BF16),
                   jax.ShapeDtypeStruct((ne, bsz * cap, 1), F32)],
        compiler_params=_cp("arbitrary", "arbitrary"),
        name="expert_gather",
    )(slot.reshape(bsz * ne, 1, n), aff_t.reshape(bsz * ne, 1, n), h2)


def _expert_kernel(x_ref, wg_ref, wu_ref, wd_ref, g_ref, o_ref, acc_ref):
    f = pl.program_id(2)

    @pl.when(f == 0)
    def _():
        acc_ref[...] = jnp.zeros_like(acc_ref)

    x = x_ref[0]
    gate = _dot(x, wg_ref[0, 0].astype(BF16))
    hid = gate * _sigmoid(gate) * _dot(x, wu_ref[0, 0].astype(BF16))
    acc_ref[...] += _dot(hid.astype(BF16), wd_ref[0, 0].astype(BF16))

    @pl.when(f == pl.num_programs(2) - 1)
    def _():
        o_ref[0] = (acc_ref[...] * g_ref[0]).astype(BF16)


def _expert_call(xs, g, wg, wu, wd, layer):
    ne, rows, _ = xs.shape
    tm = min(rows, 1024)
    tf = 256
    return pl.pallas_call(
        _expert_kernel,
        grid=(ne, rows // tm, EXPERT_FF // tf),
        in_specs=[pl.BlockSpec((1, tm, D), lambda e, i, f: (e, i, 0)),
                  pl.BlockSpec((1, 1, D, tf), lambda e, i, f: (layer, e, 0, f)),
                  pl.BlockSpec((1, 1, D, tf), lambda e, i, f: (layer, e, 0, f)),
                  pl.BlockSpec((1, 1, tf, D), lambda e, i, f: (layer, e, f, 0)),
                  pl.BlockSpec((1, tm, 1), lambda e, i, f: (e, i, 0))],
        out_specs=pl.BlockSpec((1, tm, D), lambda e, i, f: (e, i, 0)),
        out_shape=jax.ShapeDtypeStruct((ne, rows, D), BF16),
        scratch_shapes=[pltpu.VMEM((tm, D), F32)],
        compiler_params=_cp("arbitrary", "arbitrary", "arbitrary"),
        name="expert_mlp",
    )(xs, wg, wu, wd, g)


def _scatter_kernel(*refs, cap, final_norm):
    if final_norm:
        slot_ref, ys_ref, x_ref, gate_ref, gain_ref, o_ref = refs
    else:
        slot_ref, ys_ref, x_ref, gate_ref, o_ref = refs
    slot_t = slot_ref[0]
    tq = slot_t.shape[0]
    lane = lax.broadcasted_iota(jnp.int32, (tq, cap), 1)
    acc = jnp.zeros(x_ref.shape, F32)
    for e in range(N_EXPERTS):
        onehot = (slot_t[:, e:e + 1] == lane).astype(BF16)
        acc += _dot(onehot, ys_ref[e])
    x = x_ref[...] + gate_ref[0] * acc
    if final_norm:
        ms = jnp.mean(x * x, axis=-1, keepdims=True)
        x = (x * lax.rsqrt(ms + NORM_EPS)) * gain_ref[...]
    o_ref[...] = x


def _scatter_call(slot_t, ys, x2, mod3, mod_row_b, cap, final_gain=None):
    bsz, n, ne = slot_t.shape
    tq = min(n, 256)
    nt = n // tq
    in_specs = [pl.BlockSpec((1, tq, ne), lambda b, t: (b, t, 0)),
                pl.BlockSpec((ne, cap, D), lambda b, t: (0, b, 0)),
                pl.BlockSpec((tq, D), lambda b, t: (b * nt + t, 0)),
                pl.BlockSpec((1, 1, D), lambda b, t: (mod_row_b(b) * 6 + 5, 0, 0))]
    args = [slot_t, ys, x2, mod3]
    if final_gain is not None:
        in_specs.append(pl.BlockSpec((1, D), lambda b, t: (0, 0)))
        args.append(final_gain.reshape(1, D))
    return pl.pallas_call(
        functools.partial(_scatter_kernel, cap=cap, final_norm=final_gain is not None),
        grid=(bsz, nt),
        in_specs=in_specs,
        out_specs=pl.BlockSpec((tq, D), lambda b, t: (b * nt + t, 0)),
        out_shape=jax.ShapeDtypeStruct(x2.shape, F32),
        compiler_params=_cp("arbitrary", "arbitrary"),
        name="expert_scatter",
    )(*args)


def _ffn(x2, bsz, n, lp, mod3, mod_row, mod_row_b, tm, final_gain=None):
    cap = CAPACITY_FACTOR * n // N_EXPERTS
    h2, aff = _ffn_prep_call(x2, lp["norm_ffn"], mod3, mod_row, lp["w_router"], tm)
    aff_t = jnp.swapaxes(aff[:, :N_EXPERTS].reshape(bsz, n, N_EXPERTS), 1, 2)
    slot = _select_call(aff_t, cap)
    xs, g = _gather_call(slot, aff_t, h2, cap)
    ys = _expert_call(xs, g, lp["w_exp_gate"], lp["w_exp_up"], lp["w_exp_down"], lp["layer"])
    return _scatter_call(jnp.swapaxes(slot, 1, 2), ys, x2, mod3, mod_row_b, cap, final_gain)


def _pad_cols(parts):
    out = []
    for a, width in parts:
        out.append(a)
        if a.shape[-1] < width:
            out.append(jnp.zeros(a.shape[:-1] + (width - a.shape[-1],), a.dtype))
    return jnp.concatenate(out, axis=-1)


def _rwkv_col_parts(w):
    o = 3 * BW
    parts = [(w[..., :o], o)]
    for _ in range(2):
        parts.append((w[..., o:o + DECAY_RANK], 128))
        o += DECAY_RANK
    for _ in range(2):
        parts.append((w[..., o:o + ICLR_RANK], 128))
        o += ICLR_RANK
    parts.append((w[..., o:o + GATE_RANK], RW_PAD - C_GD))
    return parts, o + GATE_RANK


LANE = 128
RW_END = 3 * BW + 2 * DECAY_RANK + 2 * ICLR_RANK + GATE_RANK
CODE_W = RW_PAD - 3 * BW


def _wrelayout_kernel(w_ref, misc_ref, o_ref):
    j = pl.program_id(1)
    is_code = (j >= 3 * BW // LANE) & (j < RW_PAD // LANE)
    o_ref[0] = jnp.where(is_code, misc_ref[0], w_ref[0]).astype(BF16)


def _wrelayout_call(w_in):
    depth = w_in.shape[0]
    code_parts, _ = _rwkv_col_parts(w_in[:, :, :RW_END])
    misc = _pad_cols(code_parts[1:])
    nb = NP // LANE
    kv_src = RW_END // LANE

    def src(j):
        shifted = jnp.where(j >= C_AK // LANE, j - C_AK // LANE + kv_src, j + (RW_END + 2 * KVH * HD - C_Q) // LANE)
        return jnp.where(j < RW_PAD // LANE, jnp.minimum(j, 3 * BW // LANE - 1), shifted)

    def misc_blk(j):
        return jnp.clip(j - 3 * BW // LANE, 0, CODE_W // LANE - 1)

    return pl.pallas_call(
        _wrelayout_kernel,
        grid=(depth, nb),
        in_specs=[pl.BlockSpec((1, D, LANE), lambda l, j: (l, 0, src(j))),
                  pl.BlockSpec((1, D, LANE), lambda l, j: (l, 0, misc_blk(j)))],
        out_specs=pl.BlockSpec((1, D, LANE), lambda l, j: (l, 0, j)),
        out_shape=jax.ShapeDtypeStruct((depth, D, NP), BF16),
        compiler_params=_cp("arbitrary", "arbitrary"),
        name="w_in_relayout",
    )(w_in, misc)


def _prep_layer(l, depth, shift_mu, decay_up, decay_bias, iclr_up, iclr_bias, gate_up, vres_down, vres_up,
                vres_bias, k_k, k_a, r_k, gn_w, gn_b, conv_w, attn_sink, w_branch, w_out, w_router,
                w_exp_gate, w_exp_up, w_exp_down, norm_mix, norm_ffn, eblk):
    mu_parts, _ = _rwkv_col_parts(shift_mu[l][None, :])
    pad_rank = lambda u: jnp.pad(u, ((0, 0), (0, 128 - u.shape[1]), (0, 0))).astype(BF16)
    lp = dict(
        mu=_pad_cols(mu_parts),
        decay_up=pad_rank(decay_up[l]), decay_bias=decay_bias[l].reshape(2, 1, BW),
        iclr_up=pad_rank(iclr_up[l]), iclr_bias=iclr_bias[l].reshape(2, 1, BW),
        gate_up=gate_up[l].astype(BF16), k_k=k_k[l].reshape(1, BW), k_a=k_a[l].reshape(1, BW),
        r_k=r_k[l].reshape(1, BW), gn_w=gn_w[l].reshape(1, BW), gn_b=gn_b[l].reshape(1, BW),
        conv_w=conv_w[l], sink=attn_sink[l].reshape(1, NH),
        w_branch=w_branch[l].astype(BF16), w_out=w_out[l].astype(BF16),
        w_router=jnp.pad(w_router[l], ((0, 0), (0, 128 - N_EXPERTS))),
        layer=l, w_exp_gate=w_exp_gate, w_exp_up=w_exp_up, w_exp_down=w_exp_down,
        norm_mix=norm_mix[l], norm_ffn=norm_ffn[l], eblk=eblk)
    if l > 0:
        lp["vres_down"] = jnp.pad(vres_down[l - 1], ((0, 0), (0, 128 - VRES_RANK))).astype(BF16)
        lp["vres_up"] = jnp.pad(vres_up[l - 1], ((0, 128 - VRES_RANK), (0, 0))).astype(BF16)
        lp["vres_bias"] = vres_bias[l - 1].reshape(1, BW)
    return lp


def _rope_tables(seq_len):
    quarter = HD // 4
    inv = ROPE_BASE ** (-jnp.arange(quarter, dtype=F32) / quarter)
    pos = jnp.arange(seq_len)
    ang_r = (pos // GRID_W).astype(F32)[:, None] * inv[None, :]
    ang_c = (pos % GRID_W).astype(F32)[:, None] * inv[None, :]
    cos = jnp.concatenate([jnp.cos(ang_r)] * 2 + [jnp.cos(ang_c)] * 2, axis=1)
    sin = jnp.concatenate([-jnp.sin(ang_r), jnp.sin(ang_r), -jnp.sin(ang_c), jnp.sin(ang_c)], axis=1)
    return jnp.tile(cos, (1, NH)), jnp.tile(sin, (1, NH))


def kernel(x, c, ctx, c_ctx, w_mod, b_mod, norm_mix, norm_ffn, w_in, shift_mu, decay_up, decay_bias, iclr_up,
           iclr_bias, gate_up, vres_down, vres_up, vres_bias, k_k, k_a, r_k, gn_w, gn_b, conv_w, attn_sink,
           w_branch, w_out, w_router, w_exp_gate, w_exp_up, w_exp_down, norm_final):
    bsz, seq_len, _ = x.shape
    ctx_len = ctx.shape[1]
    depth = w_in.shape[0]
    mod_rows = -(-(bsz + 1) // 8) * 8
    cc = jnp.concatenate([c, c_ctx[None, :], jnp.zeros((mod_rows - bsz - 1, D), F32)], axis=0)
    mod_all = _mod_call(cc, w_mod, b_mod)

    hd_i = jnp.arange(BW) // HD
    eblk = (hd_i[:, None] == jnp.arange(128)[None, :]).astype(BF16)
    cos, sin = _rope_tables(seq_len)

    tm_l = min(1024, seq_len)
    tm_c = min(1024, bsz * ctx_len)
    tiles_per_sample = seq_len // tm_l
    row_l = lambda i: i // tiles_per_sample
    row_c = lambda i: bsz
    tm2 = 512
    row_l2 = lambda i: i // (seq_len // tm2)

    x_l = x.reshape(bsz * seq_len, D)
    x_c = ctx.reshape(bsz * ctx_len, D)
    vf_l = vf_c = None
    zero_state = jnp.zeros((2, bsz, HD, BW), F32)

    w_p = _wrelayout_call(w_in)

    for l in range(depth):
        last = l == depth - 1
        lp = _prep_layer(l, depth, shift_mu, decay_up, decay_bias, iclr_up, iclr_bias, gate_up, vres_down,
                         vres_up, vres_bias, k_k, k_a, r_k, gn_w, gn_b, conv_w, attn_sink, w_branch, w_out,
                         w_router, w_exp_gate, w_exp_up, w_exp_down, norm_mix, norm_ffn, eblk)
        mod3 = mod_all[l].reshape(mod_rows * 6, 1, D)

        p_l = _inproj_call(x_l, lp["norm_mix"], mod3, row_l, w_p, l, tm_l)
        p_c = _inproj_call(x_c, lp["norm_mix"], mod3, row_c, w_p, l, tm_c, rwkv_kv_only=last)
        kv_blk_c = (RW_PAD if last else C_AK) // (KVH * HD)

        st_c = _streams_call(p_c, ctx_len, lp, vf_c)
        st_l = _streams_call(p_l, seq_len, lp, vf_l)
        if l == 0:
            vf_c, vf_l = st_c[2], st_l[2]
        state_c, y_c = _scan_call(st_c, lp["k_a"], zero_state, bsz, ctx_len, not last)
        _, y_l = _scan_call(st_l, lp["k_a"], state_c, bsz, seq_len, True)
        br_rwkv_l = _rwkv_out_call(y_l, st_l, lp)

        br_attn_l = _attn_lat_call(p_l, p_c, kv_blk_c, cos, sin, lp["sink"], bsz, seq_len, ctx_len)
        br_conv_l = _conv_call(p_l, seq_len, lp["conv_w"])

        m_l = _merge_call(br_rwkv_l, br_conv_l, br_attn_l, p_l, lp["w_branch"])
        x_l = _outproj_call(m_l, lp["w_out"], x_l, mod3, row_l2, tm2)
        x_l = _ffn(x_l, bsz, seq_len, lp, mod3, row_l2, lambda b: b, tm2, norm_final if last else None)

        if not last:
            br_rwkv_c = _rwkv_out_call(y_c, st_c, lp)
            br_attn_c = _attn_ctx_call(p_c, lp["sink"], bsz, ctx_len)
            br_conv_c = _conv_call(p_c, ctx_len, lp["conv_w"])
            m_c = _merge_call(br_rwkv_c, br_conv_c, br_attn_c, p_c, lp["w_branch"])
            x_c = _outproj_call(m_c, lp["w_out"], x_c, mod3, row_c, tm2)
            x_c = _ffn(x_c, bsz, ctx_len, lp, mod3, row_c, lambda b: bsz, tm2)

    return x_l.reshape(bsz, seq_len, D)
```

```python
import functools

import jax
import jax.numpy as jnp
from jax import lax
from jax.experimental import pallas as pl
from jax.experimental.pallas import tpu as pltpu

F32 = jnp.float32
BF16 = jnp.bfloat16
HIGHEST = lax.Precision.HIGHEST

D = 2048
HD = 64
BW = 1024
NH = BW // HD
KVH = 4
GQ = NH // KVH
DECAY_RANK = 96
ICLR_RANK = 96
GATE_RANK = 256
VRES_RANK = 64
GN_EPS = 64e-5
NORM_EPS = 1e-6
WINDOW = 128
BLK = 128
assert WINDOW == BLK
GRID_W = 64
ROPE_BASE = 10000.0
NEG_INF = -1e30
N_EXPERTS = 16
EXPERT_FF = 2048
CAPACITY_FACTOR = 2
CHUNK = 64

C_R, C_K, C_V = 0, 1024, 2048
C_WD = (3072, 3200)
C_AD = (3328, 3456)
C_GD = 3584
RW_PAD = 4096
C_Q = 4096
C_CONV = 5120
C_GATE = 8192
C_AK = 14336
C_AV = 14592
NP = 14848
TN_IN = 512

VMEM_LIMIT = 56 * 1024 * 1024


def _cp(*sem, vmem=VMEM_LIMIT):
    return pltpu.CompilerParams(dimension_semantics=tuple(sem), vmem_limit_bytes=vmem)


def _dot(a, b):
    return jnp.dot(a, b, preferred_element_type=F32)


def _dot_nt(a, b):
    return lax.dot_general(a, b, (((1,), (1,)), ((), ())), preferred_element_type=F32)


def _dot_tn(a, b):
    return lax.dot_general(a, b, (((0,), (0,)), ((), ())), preferred_element_type=F32)


def _sigmoid(x):
    return 1.0 / (1.0 + jnp.exp(-x))


def _seg_sum(x, e_ref):
    e = e_ref[...]

    def split(z):
        hi = z.astype(BF16)
        return hi, (z - hi.astype(F32)).astype(BF16)

    hi, lo = split(x)
    hi2, lo2 = split(_dot(hi, e) + _dot(lo, e))
    return _dot_nt(hi2, e) + _dot_nt(lo2, e)


def _mod_kernel(c_ref, w_ref, b_ref, o_ref):
    c = c_ref[...]
    sc = c * _sigmoid(c)
    o_ref[0] = _dot(sc.astype(BF16), w_ref[0].astype(BF16)) + b_ref[0]


def _mod_call(cc, w_mod, b_mod):
    depth, _, n6 = w_mod.shape
    rows = cc.shape[0]
    tn = 1024
    return pl.pallas_call(
        _mod_kernel,
        grid=(depth, n6 // tn),
        in_specs=[pl.BlockSpec((rows, D), lambda l, j: (0, 0)),
                  pl.BlockSpec((1, D, tn), lambda l, j: (l, 0, j)),
                  pl.BlockSpec((1, 1, tn), lambda l, j: (l, 0, j))],
        out_specs=pl.BlockSpec((1, rows, tn), lambda l, j: (l, 0, j)),
        out_shape=jax.ShapeDtypeStruct((depth, rows, n6), F32),
        compiler_params=_cp("arbitrary", "arbitrary"),
        name="mod_proj",
    )(cc, w_mod, b_mod.reshape(depth, 1, n6))


def _norm_mod(x, gain, shift, scale):
    ms = jnp.mean(x * x, axis=-1, keepdims=True)
    y = x * lax.rsqrt(ms + NORM_EPS)
    return (y * gain) * (1.0 + scale) + shift


def _normmod_kernel(x_ref, gain_ref, shift_ref, scale_ref, h_ref):
    h_ref[...] = _norm_mod(x_ref[...], gain_ref[...], shift_ref[0], scale_ref[0]).astype(BF16)


def _normmod_call(x2, gain, mod3, mod_row, tm):
    n_tok = x2.shape[0]
    return pl.pallas_call(
        _normmod_kernel,
        grid=(n_tok // tm,),
        in_specs=[pl.BlockSpec((tm, D), lambda i: (i, 0)),
                  pl.BlockSpec((1, D), lambda i: (0, 0)),
                  pl.BlockSpec((1, 1, D), lambda i: (mod_row(i) * 6 + 0, 0, 0)),
                  pl.BlockSpec((1, 1, D), lambda i: (mod_row(i) * 6 + 1, 0, 0))],
        out_specs=pl.BlockSpec((tm, D), lambda i: (i, 0)),
        out_shape=jax.ShapeDtypeStruct((n_tok, D), BF16),
        compiler_params=_cp("arbitrary"),
        name="norm_mod",
    )(x2, gain.reshape(1, D), mod3, mod3)


def _inproj_kernel(h_ref, w_ref, o_ref):
    o_ref[...] = _dot(h_ref[...], w_ref[0]).astype(BF16)


def _inproj_call(h, w_p, layer, rwkv_kv_only=False):
    n_tok = h.shape[0]
    tm = min(2048, n_tok)
    n_rw = RW_PAD // TN_IN
    if rwkv_kv_only:
        ncols = RW_PAD + NP - C_AK
        col = lambda j: jnp.where(j < n_rw, j, j - n_rw + C_AK // TN_IN)
    else:
        ncols = NP
        col = lambda j: j
    return pl.pallas_call(
        _inproj_kernel,
        grid=(n_tok // tm, ncols // TN_IN),
        in_specs=[pl.BlockSpec((tm, D), lambda i, j: (i, 0)),
                  pl.BlockSpec((1, D, TN_IN), lambda i, j: (layer, 0, col(j)))],
        out_specs=pl.BlockSpec((tm, TN_IN), lambda i, j: (i, j)),
        out_shape=jax.ShapeDtypeStruct((n_tok, ncols), BF16),
        compiler_params=_cp("arbitrary", "arbitrary"),
        name="in_proj",
    )(h, w_p)


HALO = 16


def _shifted(cur, prev_blk, next_blk, first, last):
    tt = cur.shape[0]
    row = lax.broadcasted_iota(jnp.int32, cur.shape, 0)
    p_row = jnp.where(first, 0.0, prev_blk[HALO - 1:HALO, :].astype(F32))
    n_row = jnp.where(last, 0.0, next_blk[0:1, :].astype(F32))
    prev = jnp.where(row == 0, p_row, pltpu.roll(cur, 1, axis=0))
    nxt = jnp.where(row == tt - 1, n_row, pltpu.roll(cur, tt - 1, axis=0))
    return prev, nxt


def _halo_specs(tt, width, col_blk, n_tok):
    nb = n_tok // HALO
    r = tt // HALO
    return [pl.BlockSpec((tt, width), lambda i: (i, col_blk)),
            pl.BlockSpec((HALO, width), lambda i: (jnp.maximum(i * r - 1, 0), col_blk)),
            pl.BlockSpec((HALO, width), lambda i: (jnp.minimum((i + 1) * r, nb - 1), col_blk))]


def _streams_kernel(*refs, tiles_per_seq, has_vres):
    if has_vres:
        (p_ref, pp_ref, pn_ref, mu_ref, dup_ref, dbias_ref, iup_ref, ibias_ref, gup_ref, kk_ref_, e_ref,
         vd_ref, vu_ref, vb_ref, vf_ref,
         r_o, k_o, v_o, kk_o, g_o, a_o, lw_o) = refs
    else:
        (p_ref, pp_ref, pn_ref, mu_ref, dup_ref, dbias_ref, iup_ref, ibias_ref, gup_ref, kk_ref_, e_ref,
         r_o, k_o, v_o, kk_o, g_o, a_o, lw_o) = refs
    j = pl.program_id(0) % tiles_per_seq
    used = C_GD + GATE_RANK
    cur_b = p_ref[:, :used]
    tt = cur_b.shape[0]
    ext = jnp.concatenate([pp_ref[:, :used], cur_b, pn_ref[:, :used]], axis=0)
    t_i = lax.broadcasted_iota(jnp.int32, (tt, tt + 2 * HALO), 0)
    e_i = lax.broadcasted_iota(jnp.int32, (tt, tt + 2 * HALO), 1) - HALO
    tap = ((e_i == t_i - 1) & ((e_i >= 0) | (j > 0))) | ((e_i == t_i + 1) & ((e_i < tt) | (j < tiles_per_seq - 1)))
    avg = _dot(jnp.where(tap, 0.5, 0.0).astype(BF16), ext)
    cur = cur_b.astype(F32)
    ps = cur + mu_ref[:, :used] * (avg - cur)
    r = ps[:, C_R:C_R + BW]
    k = ps[:, C_K:C_K + BW]
    v = ps[:, C_V:C_V + BW]
    gd = ps[:, C_GD:C_GD + GATE_RANK]
    if has_vres:
        low = _dot(v.astype(BF16), vd_ref[...])
        mix = _sigmoid(vb_ref[...] + _dot(low.astype(BF16), vu_ref[...]))
        v = v + (vf_ref[...].astype(F32) - v) * mix
    for d in range(2):
        wd = ps[:, C_WD[d]:C_WD[d] + 128]
        ad = ps[:, C_AD[d]:C_AD[d] + 128]
        w_logit = dbias_ref[d] + _dot(jnp.tanh(wd).astype(BF16), dup_ref[d])
        lw_o[d] = -jnp.exp(-0.5) * _sigmoid(w_logit)
        a_o[d] = _sigmoid(ibias_ref[d] + _dot(ad.astype(BF16), iup_ref[d])).astype(BF16)
    kh = k * kk_ref_[...]
    ss = _seg_sum(kh * kh, e_ref)
    kk = kh * lax.rsqrt(jnp.maximum(ss, 1e-24))
    r_o[...] = r.astype(BF16)
    k_o[...] = k.astype(BF16)
    v_o[...] = v.astype(BF16)
    kk_o[...] = kk.astype(BF16)
    g_o[...] = _dot(_sigmoid(gd).astype(BF16), gup_ref[...]).astype(BF16)


def _streams_call(p, seq_len, lp, v_first):
    n_tok = p.shape[0]
    tt = 256
    has_vres = v_first is not None
    full = lambda *s: pl.BlockSpec(s, lambda i: (0,) * len(s))
    tok = pl.BlockSpec((tt, BW), lambda i: (i, 0))
    tok2 = pl.BlockSpec((2, tt, BW), lambda i: (0, i, 0))
    in_specs = _halo_specs(tt, C_GD + GATE_RANK, 0, n_tok) + [
        full(1, RW_PAD), full(2, 128, BW), full(2, 1, BW), full(2, 128, BW), full(2, 1, BW),
        full(GATE_RANK, BW), full(1, BW), full(BW, 128)]
    args = [p, p, p, lp["mu"], lp["decay_up"], lp["decay_bias"], lp["iclr_up"], lp["iclr_bias"],
            lp["gate_up"], lp["k_k"], lp["eblk"]]
    if has_vres:
        in_specs += [full(BW, 128), full(128, BW), full(1, BW), tok]
        args += [lp["vres_down"], lp["vres_up"], lp["vres_bias"], v_first]
    sd = lambda dt: jax.ShapeDtypeStruct((n_tok, BW), dt)
    sd2 = lambda dt: jax.ShapeDtypeStruct((2, n_tok, BW), dt)
    return pl.pallas_call(
        functools.partial(_streams_kernel, tiles_per_seq=seq_len // tt, has_vres=has_vres),
        grid=(n_tok // tt,),
        in_specs=in_specs,
        out_specs=[tok, tok, tok, tok, tok, tok2, tok2],
        out_shape=[sd(BF16), sd(BF16), sd(BF16), sd(BF16), sd(BF16), sd2(BF16), sd2(F32)],
        compiler_params=_cp("arbitrary"),
        name="rwkv_streams",
    )(*args)


def _scan_prologue(d, r, k, v, kk, a, lw, ka):
    c = CHUNK
    ri = lax.broadcasted_iota(jnp.int32, (c, 2 * c), 0)
    ci = lax.broadcasted_iota(jnp.int32, (c, 2 * c), 1) % c
    diff = (ci - ri) if d else (ri - ci)
    incl = diff >= 0
    strict = diff > 0
    tri = incl[:, :c].astype(BF16)
    lw_hi = lw.astype(BF16)
    lw_lo = (lw - lw_hi.astype(F32)).astype(BF16)
    b = _dot(tri, lw_hi) + _dot(tri, lw_lo)
    b_tot = b[0:1, :] if d else b[c - 1:c, :]
    a = a.astype(F32)
    kk = kk.astype(F32)
    kd = k.astype(F32) * (1.0 + (a - 1.0) * ka)
    kka = kk * a
    enb = jnp.exp(-b)
    etail = jnp.exp(b_tot - b)
    return dict(
        incl=incl, strict=strict,
        rt=(r.astype(F32) * jnp.exp(b)).astype(BF16),
        bt=(-kk * jnp.exp(b - lw)).astype(BF16),
        kt=(kd * enb).astype(BF16), at=(kka * enb).astype(BF16),
        kh=(kd * etail).astype(BF16), ah=(kka * etail).astype(BF16),
        vv=v, e_tot=jnp.exp(b_tot))


def _scan_kernel(*refs, with_output, sb):
    ins = refs[:14]
    if with_output:
        yf_ref, yb_ref, sout_ref, st_ref = refs[14:]
    else:
        sout_ref, st_ref = refs[14:]
    ka_ref, s0_ref = ins[12], ins[13]
    s = pl.program_id(1)
    c = CHUNK

    @pl.when(s == 0)
    def _():
        st_ref[...] = s0_ref[...]

    streams = [(d, j) for d in range(2) for j in range(sb)]
    ka = ka_ref[...]
    pro = [_scan_prologue(d, *[ref[j] for ref in ins[6 * d:6 * d + 4]], ins[6 * d + 4][0, j], ins[6 * d + 5][0, j], ka)
           for d, j in streams]
    st_all = [st_ref[d, j] for d, j in streams]

    npair = NH // 2
    units = [(si, slice(p * 2 * HD, (p + 1) * 2 * HD)) for si in range(len(streams)) for p in range(npair)]
    us = range(len(units))
    col = lambda name: [pro[d][name][:, sl] for d, sl in units]
    even1 = lax.broadcasted_iota(jnp.int32, (c, 2 * HD), 1) < HD

    def bd(z):
        zero = jnp.zeros_like(z)
        return jnp.concatenate([jnp.where(even1, z, zero), jnp.where(even1, zero, z)], axis=0)

    strict = [pro[d]["strict"] for d, _ in units]
    incl = [pro[d]["incl"] for d, _ in units]
    st = [st_all[d][:, sl] for d, sl in units]
    vbd = [bd(z) for z in col("vv")]
    rb = [jnp.concatenate(p, axis=0) for p in zip(col("rt"), col("bt"))]
    gk = [_dot_nt(rb[u], bd(z)) for u, z in zip(us, col("kt"))]
    ga = [_dot_nt(rb[u], bd(z)) for u, z in zip(us, col("at"))]
    rbs = [_dot_nt(rb[u], bd(st[u].astype(BF16))) for u in us]
    x = [jnp.where(strict[u], ga[u][c:], 0.0).astype(BF16) for u in us]
    a_bk = [jnp.where(strict[u], gk[u][c:], 0.0) for u in us]
    if with_output:
        a_rk = [jnp.where(incl[u], gk[u][:c], 0.0) for u in us]
        a_ra = [jnp.where(incl[u], ga[u][:c], 0.0).astype(BF16) for u in us]
        av = [_dot(jnp.concatenate([a_rk[u], a_bk[u]], axis=0).astype(BF16), vbd[u]) for u in us]
        w = [rbs[u][c:] + av[u][c:] for u in us]
    else:
        w = [rbs[u][c:] + _dot(a_bk[u].astype(BF16), vbd[u]) for u in us]
    for rnd in range(6):
        if rnd < 5:
            m = [_dot(x[u], jnp.concatenate([bd(x[u]), bd(w[u].astype(BF16))], axis=1)) for u in us]
            x = [m[u][:, :2 * c].astype(BF16) for u in us]
            w = [w[u] + m[u][:, 2 * c:] for u in us]
        else:
            w = [w[u] + _dot(x[u], bd(w[u].astype(BF16))) for u in us]
    ub = [w[u].astype(BF16) for u in us]
    if with_output:
        ys = [rbs[u][:c] + av[u][:c] + _dot(a_ra[u], bd(ub[u])) for u in us]
        for si, (d, j) in enumerate(streams):
            (yb_ref if d else yf_ref)[j] = jnp.concatenate(ys[si * npair:(si + 1) * npair], axis=1)
    full = [_dot_tn(jnp.concatenate([z, ub[u]], axis=0), jnp.concatenate(p, axis=0))
            for u, z, p in zip(us, col("vv"), zip(col("kh"), col("ah")))]
    upd = [jnp.where(even1, f[:HD], f[HD:]) for f in full]
    for si, (d, j) in enumerate(streams):
        st_ref[d, j] = st_all[si] * pro[si]["e_tot"] + jnp.concatenate(upd[si * npair:(si + 1) * npair], axis=1)

    @pl.when(s == pl.num_programs(1) - 1)
    def _():
        sout_ref[...] = st_ref[...]


def _scan_call(streams, k_a, s0, bsz, seq_len, with_output):
    n_tok = streams[0].shape[0]
    nc = seq_len // CHUNK
    sb = 2 if bsz % 2 == 0 else 1
    r, k, v, kk = (z.reshape(bsz, seq_len, BW) for z in streams[:4])
    a2, lw2 = (z.reshape(2, bsz, seq_len, BW) for z in streams[5:])

    chunk = (lambda s: s, lambda s: nc - 1 - s)
    tok = [pl.BlockSpec((sb, CHUNK, BW), lambda b, s, f=f: (b, f(s), 0)) for f in chunk]
    tok2 = [pl.BlockSpec((1, sb, CHUNK, BW), lambda b, s, f=f, d=d: (d, b, f(s), 0)) for d, f in enumerate(chunk)]
    st_spec = pl.BlockSpec((2, sb, HD, BW), lambda b, s: (0, b, 0, 0))
    in_specs, args = [], []
    for d in range(2):
        in_specs += [tok[d]] * 4 + [tok2[d]] * 2
        args += [r, k, v, kk, a2, lw2]
    in_specs += [pl.BlockSpec((1, BW), lambda b, s: (0, 0)), st_spec]
    out_specs = [st_spec]
    out_shape = [jax.ShapeDtypeStruct((2, bsz, HD, BW), F32)]
    if with_output:
        out_specs = tok + out_specs
        out_shape = [jax.ShapeDtypeStruct((bsz, seq_len, BW), F32)] * 2 + out_shape
    res = pl.pallas_call(
        functools.partial(_scan_kernel, with_output=with_output, sb=sb),
        grid=(bsz // sb, nc),
        in_specs=in_specs,
        out_specs=out_specs,
        out_shape=out_shape,
        scratch_shapes=[pltpu.VMEM((2, sb, HD, BW), F32)],
        compiler_params=_cp("arbitrary", "arbitrary"),
        name="rwkv_scan",
    )(*args, k_a, s0)
    if with_output:
        return res[2], (res[0].reshape(n_tok, BW), res[1].reshape(n_tok, BW))
    return res[0], None


def _rwkv_out_kernel(yf_ref, yb_ref, r_ref, k_ref, v_ref, a_ref, g_ref, ka_ref, rk_ref, gnw_ref, gnb_ref, e_ref,
                     o_ref):
    y = yf_ref[...] + yb_ref[...]
    mean = _seg_sum(y, e_ref) * (1.0 / HD)
    yc = y - mean
    var = _seg_sum(yc * yc, e_ref) * (1.0 / HD)
    yn = yc * lax.rsqrt(var + GN_EPS) * gnw_ref[...] + gnb_ref[...]
    r = r_ref[...].astype(F32)
    k = k_ref[...].astype(F32)
    asum = a_ref[0].astype(F32) + a_ref[1].astype(F32)
    kd_sum = k * (2.0 + (asum - 2.0) * ka_ref[...])
    bonus = _seg_sum(r * kd_sum * rk_ref[...], e_ref) * v_ref[...].astype(F32)
    o_ref[...] = ((yn + bonus) * g_ref[...].astype(F32)).astype(BF16)


def _rwkv_out_call(y2, streams, lp):
    r, k, v, _, g, a2, _ = streams
    n_tok = r.shape[0]
    tt = 256
    tok = pl.BlockSpec((tt, BW), lambda i: (i, 0))
    tok2 = pl.BlockSpec((2, tt, BW), lambda i: (0, i, 0))
    vec = pl.BlockSpec((1, BW), lambda i: (0, 0))
    return pl.pallas_call(
        _rwkv_out_kernel,
        grid=(n_tok // tt,),
        in_specs=[tok, tok, tok, tok, tok, tok2, tok, vec, vec, vec, vec, pl.BlockSpec((BW, 128), lambda i: (0, 0))],
        out_specs=tok,
        out_shape=jax.ShapeDtypeStruct((n_tok, BW), BF16),
        compiler_params=_cp("arbitrary"),
        name="rwkv_out",
    )(y2[0], y2[1], r, k, v, a2, g, lp["k_a"], lp["r_k"], lp["gn_w"], lp["gn_b"], lp["eblk"])


def _conv_kernel(b_ref, c_ref, cp_ref, cn_ref, u_ref, up_ref, un_ref, w_ref, o_ref, *, tiles_per_seq):
    j = pl.program_id(0) % tiles_per_seq
    cu = c_ref[...].astype(F32) * u_ref[...].astype(F32)
    cu_p = cp_ref[...].astype(F32) * up_ref[...].astype(F32)
    cu_n = cn_ref[...].astype(F32) * un_ref[...].astype(F32)
    prev, nxt = _shifted(cu, cu_p, cu_n, j == 0, j == tiles_per_seq - 1)
    w = w_ref[...]
    conv = w[0:1] * prev + w[1:2] * cu + w[2:3] * nxt
    o_ref[...] = (b_ref[...].astype(F32) * conv).astype(BF16)


def _conv_call(p, seq_len, conv_w):
    n_tok = p.shape[0]
    tt = 256
    cb = C_CONV // BW
    return pl.pallas_call(
        functools.partial(_conv_kernel, tiles_per_seq=seq_len // tt),
        grid=(n_tok // tt,),
        in_specs=[pl.BlockSpec((tt, BW), lambda i: (i, cb))] + _halo_specs(tt, BW, cb + 1, n_tok)
        + _halo_specs(tt, BW, cb + 2, n_tok) + [pl.BlockSpec((3, BW), lambda i: (0, 0))],
        out_specs=pl.BlockSpec((tt, BW), lambda i: (i, 0)),
        out_shape=jax.ShapeDtypeStruct((n_tok, BW), BF16),
        compiler_params=_cp("arbitrary"),
        name="short_conv",
    )(p, p, p, p, p, p, p, conv_w)


def _rope(x, cos, sin):
    w = x.shape[1]
    lane = lax.broadcasted_iota(jnp.int32, x.shape, 1)
    partner = jnp.where((lane % 32) < 16, pltpu.roll(x, w - 16, axis=1), pltpu.roll(x, 16, axis=1))
    return x * cos + partner * sin


LOG2E = 1.4426950408889634
Q_SCALE = HD ** -0.5 * LOG2E


def _softmax_pv(s2, sink2_col, v):
    m = jnp.maximum(jnp.max(s2, axis=-1, keepdims=True), sink2_col)
    p = jnp.exp2(s2 - m)
    den = jnp.sum(p, axis=-1, keepdims=True) + jnp.exp2(sink2_col - m)
    return _dot(p.astype(BF16), v) / den


def _attn_lat_kernel(q_ref, kp_ref, kc_ref, kn_ref, vp_ref, vc_ref, vn_ref, kx_ref, vx_ref,
                     cos_ref, cosp_ref, cosn_ref, sin_ref, sinp_ref, sinn_ref, sink_ref, o_ref, *, nblk):
    n = pl.program_id(1)
    kvw = KVH * HD
    q = (_rope(q_ref[...].astype(F32), cos_ref[...], sin_ref[...]) * Q_SCALE).astype(BF16)
    kb = jnp.concatenate([
        _rope(kp_ref[...].astype(F32), cosp_ref[:, :kvw], sinp_ref[:, :kvw]),
        _rope(kc_ref[...].astype(F32), cos_ref[:, :kvw], sin_ref[:, :kvw]),
        _rope(kn_ref[...].astype(F32), cosn_ref[:, :kvw], sinn_ref[:, :kvw])], axis=0).astype(BF16)
    k_all = jnp.concatenate([kb, kx_ref[...]], axis=0)
    v_all = jnp.concatenate([vp_ref[...], vc_ref[...], vn_ref[...], vx_ref[...]], axis=0)
    cols = GQ * BLK
    kidx = lax.broadcasted_iota(jnp.int32, (BLK, cols), 0)
    qpos = lax.broadcasted_iota(jnp.int32, (BLK, cols), 1) % BLK
    bias_p = jnp.where((kidx >= qpos) & (n > 0), 0.0, NEG_INF)
    bias_n = jnp.where((kidx <= qpos) & (n < nblk - 1), 0.0, NEG_INF)
    sink = sink_ref[...] * LOG2E
    gs = range(KVH)
    even = lax.broadcasted_iota(jnp.int32, (BLK, 2 * HD), 1) < HD
    zero = jnp.zeros((BLK, 2 * HD), BF16)
    eye = (lax.broadcasted_iota(jnp.int32, (2 * HD, 2 * HD), 0)
           == lax.broadcasted_iota(jnp.int32, (2 * HD, 2 * HD), 1)).astype(BF16)

    def head_rows(gi):
        parts = []
        for t in range(GQ // 2):
            qp = q[:, (gi * GQ + 2 * t) * HD:(gi * GQ + 2 * t + 2) * HD]
            parts += [jnp.where(even, qp, zero), jnp.where(even, zero, qp)]
        return jnp.concatenate(parts, axis=0)

    qg = [head_rows(gi) for gi in gs]
    kd = [jnp.concatenate([k_all[:, gi * HD:(gi + 1) * HD]] * 2, axis=1) for gi in gs]
    vt = [_dot_nt(eye[:HD, :HD], v_all[:, gi * HD:(gi + 1) * HD]).astype(BF16) for gi in gs]
    sk = [jnp.concatenate([jnp.broadcast_to(sink[:, gi * GQ + t:gi * GQ + t + 1], (1, BLK))
                           for t in range(GQ)], axis=1) for gi in gs]
    s = [_dot_nt(kd[gi], qg[gi]) for gi in gs]
    s = [jnp.concatenate([z[:BLK] + bias_p, z[BLK:2 * BLK], z[2 * BLK:3 * BLK] + bias_n, z[3 * BLK:]], axis=0)
         for z in s]
    m = [jnp.maximum(jnp.max(s[gi], axis=0, keepdims=True), sk[gi]) for gi in gs]
    p = [jnp.exp2(s[gi] - m[gi]).astype(BF16) for gi in gs]
    ones = jnp.ones((16, vt[0].shape[1]), BF16)
    pv = [_dot(jnp.concatenate([vt[gi], ones], axis=0), p[gi]) for gi in gs]
    og = [(pv[gi][:HD] / (pv[gi][HD:HD + 1] + jnp.exp2(sk[gi] - m[gi]))).astype(BF16) for gi in gs]
    o_ref[...] = jnp.concatenate(
        [_dot_nt(eye, jnp.concatenate([og[gi][:, 2 * t * BLK:(2 * t + 1) * BLK],
                                       og[gi][:, (2 * t + 1) * BLK:(2 * t + 2) * BLK]], axis=0))
         for gi in gs for t in range(GQ // 2)], axis=1).astype(BF16)


def _attn_lat_call(p_l, p_c, kv_blk_c, cos, sin, sink, bsz, seq_len, ctx_len):
    nblk = seq_len // BLK
    kvw = KVH * HD
    kb, vb = C_AK // kvw, C_AV // kvw
    rowq = lambda b, n: b * nblk + n
    rowp = lambda b, n: b * nblk + jnp.maximum(n - 1, 0)
    rown = lambda b, n: b * nblk + jnp.minimum(n + 1, nblk - 1)
    tabp = lambda b, n: (jnp.maximum(n - 1, 0), 0)
    tabn = lambda b, n: (jnp.minimum(n + 1, nblk - 1), 0)
    kv = lambda rf, cb: pl.BlockSpec((BLK, kvw), lambda b, n: (rf(b, n), cb))
    tab = lambda f: pl.BlockSpec((BLK, BW), f)
    return pl.pallas_call(
        functools.partial(_attn_lat_kernel, nblk=nblk),
        grid=(bsz, nblk),
        in_specs=[pl.BlockSpec((BLK, BW), lambda b, n: (rowq(b, n), C_Q // BW)),
                  kv(rowp, kb), kv(rowq, kb), kv(rown, kb), kv(rowp, vb), kv(rowq, vb), kv(rown, vb),
                  pl.BlockSpec((ctx_len, kvw), lambda b, n: (b, kv_blk_c)),
                  pl.BlockSpec((ctx_len, kvw), lambda b, n: (b, kv_blk_c + 1)),
                  tab(lambda b, n: (n, 0)), tab(tabp), tab(tabn),
                  tab(lambda b, n: (n, 0)), tab(tabp), tab(tabn),
                  pl.BlockSpec((1, NH), lambda b, n: (0, 0))],
        out_specs=pl.BlockSpec((BLK, BW), lambda b, n: (rowq(b, n), 0)),
        out_shape=jax.ShapeDtypeStruct((bsz * seq_len, BW), BF16),
        compiler_params=_cp("arbitrary", "arbitrary"),
        name="attn_latent",
    )(p_l, p_l, p_l, p_l, p_l, p_l, p_l, p_c, p_c, cos, cos, cos, sin, sin, sin, sink)


def _attn_ctx_kernel(q_ref, kx_ref, vx_ref, sink_ref, o_ref):
    q = (q_ref[...].astype(F32) * Q_SCALE).astype(BF16)
    kx = kx_ref[...]
    vx = vx_ref[...]
    sink = sink_ref[...] * LOG2E
    tq = q.shape[0]
    outs = []
    for gi in range(KVH):
        qg = jnp.concatenate([q[:, (gi * GQ + t) * HD:(gi * GQ + t + 1) * HD] for t in range(GQ)], axis=0)
        sk = jnp.concatenate([jnp.broadcast_to(sink[:, gi * GQ + t:gi * GQ + t + 1], (tq, 1))
                              for t in range(GQ)], axis=0)
        s = _dot_nt(qg, kx[:, gi * HD:(gi + 1) * HD])
        og = _softmax_pv(s, sk, vx[:, gi * HD:(gi + 1) * HD])
        outs += [og[t * tq:(t + 1) * tq] for t in range(GQ)]
    o_ref[...] = jnp.concatenate(outs, axis=1).astype(BF16)


def _attn_ctx_call(p_c, sink, bsz, ctx_len):
    kvw = KVH * HD
    tq = 128
    nq = ctx_len // tq
    return pl.pallas_call(
        _attn_ctx_kernel,
        grid=(bsz, nq),
        in_specs=[pl.BlockSpec((tq, BW), lambda b, n: (b * nq + n, C_Q // BW)),
                  pl.BlockSpec((ctx_len, kvw), lambda b, n: (b, C_AK // kvw)),
                  pl.BlockSpec((ctx_len, kvw), lambda b, n: (b, C_AV // kvw)),
                  pl.BlockSpec((1, NH), lambda b, n: (0, 0))],
        out_specs=pl.BlockSpec((tq, BW), lambda b, n: (b * nq + n, 0)),
        out_shape=jax.ShapeDtypeStruct((bsz * ctx_len, BW), BF16),
        compiler_params=_cp("arbitrary", "arbitrary"),
        name="attn_context",
    )(p_c, p_c, p_c, sink)


def _merge_kernel(b0_ref, b1_ref, b2_ref, g0_ref, g1_ref, g2_ref, w_ref, o_ref):
    acc = _sigmoid(g0_ref[...].astype(F32)) * _dot(b0_ref[...], w_ref[0])
    acc += _sigmoid(g1_ref[...].astype(F32)) * _dot(b1_ref[...], w_ref[1])
    acc += _sigmoid(g2_ref[...].astype(F32)) * _dot(b2_ref[...], w_ref[2])
    o_ref[...] = acc.astype(BF16)


def _merge_call(br_rwkv, br_conv, br_attn, p, w_branch):
    n_tok = p.shape[0]
    tm, tn = 512, 1024
    nn = D // tn
    br = pl.BlockSpec((tm, BW), lambda j, i: (i, 0))
    gate = lambda t: pl.BlockSpec((tm, tn), lambda j, i: (i, (C_GATE + t * D) // tn + j))
    return pl.pallas_call(
        _merge_kernel,
        grid=(nn, n_tok // tm),
        in_specs=[br, br, br, gate(0), gate(1), gate(2), pl.BlockSpec((3, BW, tn), lambda j, i: (0, 0, j))],
        out_specs=pl.BlockSpec((tm, tn), lambda j, i: (i, j)),
        out_shape=jax.ShapeDtypeStruct((n_tok, D), BF16),
        compiler_params=_cp("arbitrary", "arbitrary"),
        name="merge_branches",
    )(br_rwkv, br_conv, br_attn, p, p, p, w_branch)


def _outproj_kernel(m_ref, w_ref, x_ref, gate_ref, o_ref):
    o_ref[...] = x_ref[...] + gate_ref[0] * _dot(m_ref[...], w_ref[...])


def _outproj_call(m, w_out, x2, mod3, mod_row, tm):
    n_tok = x2.shape[0]
    return pl.pallas_call(
        _outproj_kernel,
        grid=(n_tok // tm,),
        in_specs=[pl.BlockSpec((tm, D), lambda i: (i, 0)),
                  pl.BlockSpec((D, D), lambda i: (0, 0)),
                  pl.BlockSpec((tm, D), lambda i: (i, 0)),
                  pl.BlockSpec((1, 1, D), lambda i: (mod_row(i) * 6 + 2, 0, 0))],
        out_specs=pl.BlockSpec((tm, D), lambda i: (i, 0)),
        out_shape=jax.ShapeDtypeStruct((n_tok, D), F32),
        compiler_params=_cp("arbitrary"),
        name="out_proj",
    )(m, w_out, x2, mod3)


def _ffn_prep_kernel(x_ref, gain_ref, shift_ref, scale_ref, wr_ref, h_ref, aff_ref):
    h = _norm_mod(x_ref[...], gain_ref[...], shift_ref[0], scale_ref[0])
    h_hi = h.astype(BF16)
    h_ref[...] = h_hi
    h_lo = (h - h_hi.astype(F32)).astype(BF16)
    wr = wr_ref[...]
    w_hi = wr.astype(BF16)
    w_lo = (wr - w_hi.astype(F32)).astype(BF16)
    logits = _dot(h_hi, w_hi) + (_dot(h_hi, w_lo) + _dot(h_lo, w_hi))
    lane = lax.broadcasted_iota(jnp.int32, logits.shape, 1)
    logits = jnp.where(lane < N_EXPERTS, logits, NEG_INF)
    m = jnp.max(logits, axis=-1, keepdims=True)
    e = jnp.exp(logits - m)
    aff_ref[...] = e / jnp.sum(e, axis=-1, keepdims=True)


def _ffn_prep_call(x2, gain, mod3, mod_row, wr_pad, tm):
    n_tok = x2.shape[0]
    return pl.pallas_call(
        _ffn_prep_kernel,
        grid=(n_tok // tm,),
        in_specs=[pl.BlockSpec((tm, D), lambda i: (i, 0)),
                  pl.BlockSpec((1, D), lambda i: (0, 0)),
                  pl.BlockSpec((1, 1, D), lambda i: (mod_row(i) * 6 + 3, 0, 0)),
                  pl.BlockSpec((1, 1, D), lambda i: (mod_row(i) * 6 + 4, 0, 0)),
                  pl.BlockSpec((D, 128), lambda i: (0, 0))],
        out_specs=[pl.BlockSpec((tm, D), lambda i: (i, 0)), pl.BlockSpec((tm, 128), lambda i: (i, 0))],
        out_shape=[jax.ShapeDtypeStruct((n_tok, D), BF16), jax.ShapeDtypeStruct((n_tok, 128), F32)],
        compiler_params=_cp("arbitrary"),
        name="ffn_prep",
    )(x2, gain.reshape(1, D), mod3, mod3, wr_pad)


def _select_kernel(aff_ref, slot_ref, *, cap):
    a = aff_ref[0]
    n = a.shape[1]
    bits = lax.bitcast_convert_type(a, jnp.int32)

    def body(i, t):
        cand = t | jnp.left_shift(jnp.int32(1), 30 - i)
        cnt = jnp.sum((bits >= cand).astype(jnp.int32), axis=-1, keepdims=True)
        return jnp.where(cnt >= cap, cand, t)

    thr = lax.fori_loop(0, 31, body, jnp.zeros((a.shape[0], 1), jnp.int32))
    gt = bits > thr
    eq = bits == thr
    n_gt = jnp.sum(gt.astype(jnp.int32), axis=-1, keepdims=True)
    tc = min(n, 512)

    def prefix(mask_bf16):
        cols = []
        for j0 in range(0, n, tc):
            ri = lax.broadcasted_iota(jnp.int32, (n, tc), 0)
            ci = lax.broadcasted_iota(jnp.int32, (n, tc), 1) + j0
            cols.append(_dot(mask_bf16, (ri <= ci).astype(BF16)))
        return jnp.concatenate(cols, axis=1)

    eq_f = eq.astype(BF16)
    excl_eq = prefix(eq_f) - eq_f.astype(F32)
    sel = gt | (eq & (excl_eq < (cap - n_gt).astype(F32)))
    pos = prefix(sel.astype(BF16))
    slot_ref[0] = jnp.where(sel, pos.astype(jnp.int32) - 1, -1)


def _select_call(aff_t, cap):
    bsz, ne, n = aff_t.shape
    return pl.pallas_call(
        functools.partial(_select_kernel, cap=cap),
        grid=(bsz,),
        in_specs=[pl.BlockSpec((1, ne, n), lambda b: (b, 0, 0))],
        out_specs=pl.BlockSpec((1, ne, n), lambda b: (b, 0, 0)),
        out_shape=jax.ShapeDtypeStruct((bsz, ne, n), jnp.int32),
        compiler_params=_cp("arbitrary"),
        name="expert_select",
    )(aff_t)


def _gather_kernel(slot_ref, aff_ref, h_ref, xs_ref, g_ref, *, cap):
    slot = slot_ref[0]
    n = slot.shape[1]
    onehot = lax.broadcasted_iota(jnp.int32, (cap, n), 0) == slot
    xs_ref[0] = _dot(onehot.astype(BF16), h_ref[...]).astype(BF16)
    g_ref[0] = jnp.sum(jnp.where(onehot, aff_ref[0], 0.0), axis=-1, keepdims=True)


def _gather_call(slot, aff_t, h2, cap):
    bsz, ne, n = slot.shape
    return pl.pallas_call(
        functools.partial(_gather_kernel, cap=cap),
        grid=(bsz, ne),
        in_specs=[pl.BlockSpec((1, 1, n), lambda b, e: (b * ne + e, 0, 0)),
                  pl.BlockSpec((1, 1, n), lambda b, e: (b * ne + e, 0, 0)),
                  pl.BlockSpec((n, D), lambda b, e: (b, 0))],
        out_specs=[pl.BlockSpec((1, cap, D), lambda b, e: (e, b, 0)),
                   pl.BlockSpec((1, cap, 1), lambda b, e: (e, b, 0))],
        out_shape=[jax.ShapeDtypeStruct((ne, bsz * cap, D), BF16),
                   jax.ShapeDtypeStruct((ne, bsz * cap, 1), F32)],
        compiler_params=_cp("arbitrary", "arbitrary"),
        name="expert_gather",
    )(slot.reshape(bsz * ne, 1, n), aff_t.reshape(bsz * ne, 1, n), h2)


def _expert_kernel(x_ref, wg_ref, wu_ref, wd_ref, g_ref, o_ref, acc_ref):
    f = pl.program_id(2)

    @pl.when(f == 0)
    def _():
        acc_ref[...] = jnp.zeros_like(acc_ref)

    x = x_ref[0]
    gate = _dot(x, wg_ref[0, 0].astype(BF16))
    hid = gate * _sigmoid(gate) * _dot(x, wu_ref[0, 0].astype(BF16))
    acc_ref[...] += _dot(hid.astype(BF16), wd_ref[0, 0].astype(BF16))

    @pl.when(f == pl.num_programs(2) - 1)
    def _():
        o_ref[0] = (acc_ref[...] * g_ref[0]).astype(BF16)


def _expert_call(xs, g, wg, wu, wd, layer):
    ne, rows, _ = xs.shape
    tm = min(rows, 1024)
    tf = 256
    return pl.pallas_call(
        _expert_kernel,
        grid=(ne, rows // tm, EXPERT_FF // tf),
        in_specs=[pl.BlockSpec((1, tm, D), lambda e, i, f: (e, i, 0)),
                  pl.BlockSpec((1, 1, D, tf), lambda e, i, f: (layer, e, 0, f)),
                  pl.BlockSpec((1, 1, D, tf), lambda e, i, f: (layer, e, 0, f)),
                  pl.BlockSpec((1, 1, tf, D), lambda e, i, f: (layer, e, f, 0)),
                  pl.BlockSpec((1, tm, 1), lambda e, i, f: (e, i, 0))],
        out_specs=pl.BlockSpec((1, tm, D), lambda e, i, f: (e, i, 0)),
        out_shape=jax.ShapeDtypeStruct((ne, rows, D), BF16),
        scratch_shapes=[pltpu.VMEM((tm, D), F32)],
        compiler_params=_cp("arbitrary", "arbitrary", "arbitrary"),
        name="expert_mlp",
    )(xs, wg, wu, wd, g)


def _scatter_kernel(*refs, cap, last):
    if last:
        slot_ref, ys_ref, x_ref, gate_ref, gain_ref, o_ref = refs
    else:
        slot_ref, ys_ref, x_ref, gate_ref, gain_ref, shift_ref, scale_ref, o_ref, h_ref = refs
    slot_t = slot_ref[0]
    tq = slot_t.shape[0]
    lane = lax.broadcasted_iota(jnp.int32, (tq, cap), 1)
    acc = jnp.zeros(x_ref.shape, F32)
    for e in range(N_EXPERTS):
        onehot = (slot_t[:, e:e + 1] == lane).astype(BF16)
        acc += _dot(onehot, ys_ref[e])
    x = x_ref[...] + gate_ref[0] * acc
    if last:
        ms = jnp.mean(x * x, axis=-1, keepdims=True)
        o_ref[...] = (x * lax.rsqrt(ms + NORM_EPS)) * gain_ref[...]
    else:
        o_ref[...] = x
        h_ref[...] = _norm_mod(x, gain_ref[...], shift_ref[0], scale_ref[0]).astype(BF16)


def _scatter_call(slot_t, ys, x2, mod3, mod_row_b, cap, gain, next_mod3=None):
    bsz, n, ne = slot_t.shape
    tq = min(n, 256)
    nt = n // tq
    last = next_mod3 is None
    row = pl.BlockSpec((tq, D), lambda b, t: (b * nt + t, 0))
    mod = lambda c: pl.BlockSpec((1, 1, D), lambda b, t: (mod_row_b(b) * 6 + c, 0, 0))
    in_specs = [pl.BlockSpec((1, tq, ne), lambda b, t: (b, t, 0)),
                pl.BlockSpec((ne, cap, D), lambda b, t: (0, b, 0)),
                row, mod(5), pl.BlockSpec((1, D), lambda b, t: (0, 0))]
    args = [slot_t, ys, x2, mod3, gain.reshape(1, D)]
    out_specs, out_shape = row, jax.ShapeDtypeStruct(x2.shape, F32)
    if not last:
        in_specs += [mod(0), mod(1)]
        args += [next_mod3, next_mod3]
        out_specs, out_shape = [row, row], [out_shape, jax.ShapeDtypeStruct(x2.shape, BF16)]
    return pl.pallas_call(
        functools.partial(_scatter_kernel, cap=cap, last=last),
        grid=(bsz, nt),
        in_specs=in_specs,
        out_specs=out_specs,
        out_shape=out_shape,
        compiler_params=_cp("arbitrary", "arbitrary"),
        name="expert_scatter",
    )(*args)


def _ffn(x2, bsz, n, lp, mod3, mod_row, mod_row_b, tm, gain, next_mod3=None):
    cap = CAPACITY_FACTOR * n // N_EXPERTS
    h2, aff = _ffn_prep_call(x2, lp["norm_ffn"], mod3, mod_row, lp["w_router"], tm)
    aff_t = jnp.swapaxes(aff[:, :N_EXPERTS].reshape(bsz, n, N_EXPERTS), 1, 2)
    slot = _select_call(aff_t, cap)
    xs, g = _gather_call(slot, aff_t, h2, cap)
    ys = _expert_call(xs, g, lp["w_exp_gate"], lp["w_exp_up"], lp["w_exp_down"], lp["layer"])
    return _scatter_call(jnp.swapaxes(slot, 1, 2), ys, x2, mod3, mod_row_b, cap, gain, next_mod3)


def _pad_cols(parts):
    out = []
    for a, width in parts:
        out.append(a)
        if a.shape[-1] < width:
            out.append(jnp.zeros(a.shape[:-1] + (width - a.shape[-1],), a.dtype))
    return jnp.concatenate(out, axis=-1)


def _rwkv_col_parts(w):
    o = 3 * BW
    parts = [(w[..., :o], o)]
    for _ in range(2):
        parts.append((w[..., o:o + DECAY_RANK], 128))
        o += DECAY_RANK
    for _ in range(2):
        parts.append((w[..., o:o + ICLR_RANK], 128))
        o += ICLR_RANK
    parts.append((w[..., o:o + GATE_RANK], RW_PAD - C_GD))
    return parts, o + GATE_RANK


LANE = 128
RW_END = 3 * BW + 2 * DECAY_RANK + 2 * ICLR_RANK + GATE_RANK
CODE_W = RW_PAD - 3 * BW


def _wrelayout_kernel(w_ref, misc_ref, o_ref):
    j = pl.program_id(1)
    is_code = (j >= 3 * BW // LANE) & (j < RW_PAD // LANE)
    o_ref[0] = jnp.where(is_code, misc_ref[0], w_ref[0]).astype(BF16)


def _wrelayout_call(w_in):
    depth = w_in.shape[0]
    code_parts, _ = _rwkv_col_parts(w_in[:, :, :RW_END])
    misc = _pad_cols(code_parts[1:])
    nb = NP // LANE
    kv_src = RW_END // LANE

    def src(j):
        shifted = jnp.where(j >= C_AK // LANE, j - C_AK // LANE + kv_src, j + (RW_END + 2 * KVH * HD - C_Q) // LANE)
        return jnp.where(j < RW_PAD // LANE, jnp.minimum(j, 3 * BW // LANE - 1), shifted)

    def misc_blk(j):
        return jnp.clip(j - 3 * BW // LANE, 0, CODE_W // LANE - 1)

    return pl.pallas_call(
        _wrelayout_kernel,
        grid=(depth, nb),
        in_specs=[pl.BlockSpec((1, D, LANE), lambda l, j: (l, 0, src(j))),
                  pl.BlockSpec((1, D, LANE), lambda l, j: (l, 0, misc_blk(j)))],
        out_specs=pl.BlockSpec((1, D, LANE), lambda l, j: (l, 0, j)),
        out_shape=jax.ShapeDtypeStruct((depth, D, NP), BF16),
        compiler_params=_cp("arbitrary", "arbitrary"),
        name="w_in_relayout",
    )(w_in, misc)


def _prep_layer(l, depth, shift_mu, decay_up, decay_bias, iclr_up, iclr_bias, gate_up, vres_down, vres_up,
                vres_bias, k_k, k_a, r_k, gn_w, gn_b, conv_w, attn_sink, w_branch, w_out, w_router,
                w_exp_gate, w_exp_up, w_exp_down, norm_mix, norm_ffn, eblk):
    mu_parts, _ = _rwkv_col_parts(shift_mu[l][None, :])
    pad_rank = lambda u: jnp.pad(u, ((0, 0), (0, 128 - u.shape[1]), (0, 0))).astype(BF16)
    lp = dict(
        mu=_pad_cols(mu_parts),
        decay_up=pad_rank(decay_up[l]), decay_bias=decay_bias[l].reshape(2, 1, BW),
        iclr_up=pad_rank(iclr_up[l]), iclr_bias=iclr_bias[l].reshape(2, 1, BW),
        gate_up=gate_up[l].astype(BF16), k_k=k_k[l].reshape(1, BW), k_a=k_a[l].reshape(1, BW),
        r_k=r_k[l].reshape(1, BW), gn_w=gn_w[l].reshape(1, BW), gn_b=gn_b[l].reshape(1, BW),
        conv_w=conv_w[l], sink=attn_sink[l].reshape(1, NH),
        w_branch=w_branch[l].astype(BF16), w_out=w_out[l].astype(BF16),
        w_router=jnp.pad(w_router[l], ((0, 0), (0, 128 - N_EXPERTS))),
        layer=l, w_exp_gate=w_exp_gate, w_exp_up=w_exp_up, w_exp_down=w_exp_down,
        norm_mix=norm_mix[l], norm_ffn=norm_ffn[l], eblk=eblk)
    if l > 0:
        lp["vres_down"] = jnp.pad(vres_down[l - 1], ((0, 0), (0, 128 - VRES_RANK))).astype(BF16)
        lp["vres_up"] = jnp.pad(vres_up[l - 1], ((0, 128 - VRES_RANK), (0, 0))).astype(BF16)
        lp["vres_bias"] = vres_bias[l - 1].reshape(1, BW)
    return lp


def _rope_tables(seq_len):
    quarter = HD // 4
    inv = ROPE_BASE ** (-jnp.arange(quarter, dtype=F32) / quarter)
    pos = jnp.arange(seq_len)
    ang_r = (pos // GRID_W).astype(F32)[:, None] * inv[None, :]
    ang_c = (pos % GRID_W).astype(F32)[:, None] * inv[None, :]
    cos = jnp.concatenate([jnp.cos(ang_r)] * 2 + [jnp.cos(ang_c)] * 2, axis=1)
    sin = jnp.concatenate([-jnp.sin(ang_r), jnp.sin(ang_r), -jnp.sin(ang_c), jnp.sin(ang_c)], axis=1)
    return jnp.tile(cos, (1, NH)), jnp.tile(sin, (1, NH))


def kernel(x, c, ctx, c_ctx, w_mod, b_mod, norm_mix, norm_ffn, w_in, shift_mu, decay_up, decay_bias, iclr_up,
           iclr_bias, gate_up, vres_down, vres_up, vres_bias, k_k, k_a, r_k, gn_w, gn_b, conv_w, attn_sink,
           w_branch, w_out, w_router, w_exp_gate, w_exp_up, w_exp_down, norm_final):
    bsz, seq_len, _ = x.shape
    ctx_len = ctx.shape[1]
    depth = w_in.shape[0]
    mod_rows = -(-(bsz + 1) // 8) * 8
    cc = jnp.concatenate([c, c_ctx[None, :], jnp.zeros((mod_rows - bsz - 1, D), F32)], axis=0)
    mod_all = _mod_call(cc, w_mod, b_mod)

    hd_i = jnp.arange(BW) // HD
    eblk = (hd_i[:, None] == jnp.arange(128)[None, :]).astype(BF16)
    cos, sin = _rope_tables(seq_len)

    row_c = lambda i: bsz
    tm2 = 512
    row_l2 = lambda i: i // (seq_len // tm2)
    mod3_of = lambda l: mod_all[l].reshape(mod_rows * 6, 1, D)

    x_l = x.reshape(bsz * seq_len, D)
    x_c = ctx.reshape(bsz * ctx_len, D)
    vf_l = vf_c = None
    zero_state = jnp.zeros((2, bsz, HD, BW), F32)

    w_p = _wrelayout_call(w_in)
    h_l = _normmod_call(x_l, norm_mix[0], mod3_of(0), row_l2, tm2)
    h_c = _normmod_call(x_c, norm_mix[0], mod3_of(0), row_c, tm2)

    for l in range(depth):
        last = l == depth - 1
        lp = _prep_layer(l, depth, shift_mu, decay_up, decay_bias, iclr_up, iclr_bias, gate_up, vres_down,
                         vres_up, vres_bias, k_k, k_a, r_k, gn_w, gn_b, conv_w, attn_sink, w_branch, w_out,
                         w_router, w_exp_gate, w_exp_up, w_exp_down, norm_mix, norm_ffn, eblk)
        mod3 = mod_all[l].reshape(mod_rows * 6, 1, D)

        p_l = _inproj_call(h_l, w_p, l)
        p_c = _inproj_call(h_c, w_p, l, rwkv_kv_only=last)
        nxt = (norm_final, None) if last else (norm_mix[l + 1], mod3_of(l + 1))
        kv_blk_c = (RW_PAD if last else C_AK) // (KVH * HD)

        st_c = _streams_call(p_c, ctx_len, lp, vf_c)
        st_l = _streams_call(p_l, seq_len, lp, vf_l)
        if l == 0:
            vf_c, vf_l = st_c[2], st_l[2]
        state_c, y_c = _scan_call(st_c, lp["k_a"], zero_state, bsz, ctx_len, not last)
        _, y_l = _scan_call(st_l, lp["k_a"], state_c, bsz, seq_len, True)
        br_rwkv_l = _rwkv_out_call(y_l, st_l, lp)

        br_attn_l = _attn_lat_call(p_l, p_c, kv_blk_c, cos, sin, lp["sink"], bsz, seq_len, ctx_len)
        br_conv_l = _conv_call(p_l, seq_len, lp["conv_w"])

        m_l = _merge_call(br_rwkv_l, br_conv_l, br_attn_l, p_l, lp["w_branch"])
        x_l = _outproj_call(m_l, lp["w_out"], x_l, mod3, row_l2, tm2)
        res = _ffn(x_l, bsz, seq_len, lp, mod3, row_l2, lambda b: b, tm2, *nxt)
        x_l, h_l = (res, None) if last else res

        if not last:
            br_rwkv_c = _rwkv_out_call(y_c, st_c, lp)
            br_attn_c = _attn_ctx_call(p_c, lp["sink"], bsz, ctx_len)
            br_conv_c = _conv_call(p_c, ctx_len, lp["conv_w"])
            m_c = _merge_call(br_rwkv_c, br_conv_c, br_attn_c, p_c, lp["w_branch"])
            x_c = _outproj_call(m_c, lp["w_out"], x_c, mod3, row_c, tm2)
            x_c, h_c = _ffn(x_c, bsz, ctx_len, lp, mod3, row_c, lambda b: bsz, tm2, *nxt)

    return x_l.reshape(bsz, seq_len, D)
```

```python
import functools

import jax
import jax.numpy as jnp
from jax import lax
from jax.experimental import pallas as pl
from jax.experimental.pallas import tpu as pltpu

F32 = jnp.float32
BF16 = jnp.bfloat16
HIGHEST = lax.Precision.HIGHEST

D = 2048
HD = 64
BW = 1024
NH = BW // HD
KVH = 4
GQ = NH // KVH
DECAY_RANK = 96
ICLR_RANK = 96
GATE_RANK = 256
VRES_RANK = 64
GN_EPS = 64e-5
NORM_EPS = 1e-6
WINDOW = 128
BLK = 128
assert WINDOW == BLK
GRID_W = 64
ROPE_BASE = 10000.0
NEG_INF = -1e30
N_EXPERTS = 16
EXPERT_FF = 2048
CAPACITY_FACTOR = 2
CHUNK = 64

C_R, C_K, C_V = 0, 1024, 2048
C_WD = (3072, 3200)
C_AD = (3328, 3456)
C_GD = 3584
RW_PAD = 4096
C_Q = 4096
C_CONV = 5120
C_GATE = 8192
C_AK = 14336
C_AV = 14592
NP = 14848
TN_IN = 512

VMEM_LIMIT = 56 * 1024 * 1024


def _cp(*sem, vmem=VMEM_LIMIT):
    return pltpu.CompilerParams(dimension_semantics=tuple(sem), vmem_limit_bytes=vmem)


def _dot(a, b):
    return jnp.dot(a, b, preferred_element_type=F32)


def _dot_nt(a, b):
    return lax.dot_general(a, b, (((1,), (1,)), ((), ())), preferred_element_type=F32)


def _dot_tn(a, b):
    return lax.dot_general(a, b, (((0,), (0,)), ((), ())), preferred_element_type=F32)


def _sigmoid(x):
    return 1.0 / (1.0 + jnp.exp(-x))


def _seg_sum(x, e_ref):
    e = e_ref[...]

    def split(z):
        hi = z.astype(BF16)
        return hi, (z - hi.astype(F32)).astype(BF16)

    hi, lo = split(x)
    hi2, lo2 = split(_dot(hi, e) + _dot(lo, e))
    return _dot_nt(hi2, e) + _dot_nt(lo2, e)


def _mod_kernel(c_ref, w_ref, b_ref, o_ref):
    c = c_ref[...]
    sc = c * _sigmoid(c)
    o_ref[0] = _dot(sc.astype(BF16), w_ref[0].astype(BF16)) + b_ref[0]


def _mod_call(cc, w_mod, b_mod):
    depth, _, n6 = w_mod.shape
    rows = cc.shape[0]
    tn = 1024
    return pl.pallas_call(
        _mod_kernel,
        grid=(depth, n6 // tn),
        in_specs=[pl.BlockSpec((rows, D), lambda l, j: (0, 0)),
                  pl.BlockSpec((1, D, tn), lambda l, j: (l, 0, j)),
                  pl.BlockSpec((1, 1, tn), lambda l, j: (l, 0, j))],
        out_specs=pl.BlockSpec((1, rows, tn), lambda l, j: (l, 0, j)),
        out_shape=jax.ShapeDtypeStruct((depth, rows, n6), F32),
        compiler_params=_cp("arbitrary", "arbitrary"),
        name="mod_proj",
    )(cc, w_mod, b_mod.reshape(depth, 1, n6))


def _norm_mod(x, gain, shift, scale):
    ms = jnp.mean(x * x, axis=-1, keepdims=True)
    y = x * lax.rsqrt(ms + NORM_EPS)
    return (y * gain) * (1.0 + scale) + shift


def _normmod_kernel(x_ref, gain_ref, shift_ref, scale_ref, h_ref):
    h_ref[...] = _norm_mod(x_ref[...], gain_ref[...], shift_ref[0], scale_ref[0]).astype(BF16)


def _normmod_call(x2, gain, mod3, mod_row, tm):
    n_tok = x2.shape[0]
    return pl.pallas_call(
        _normmod_kernel,
        grid=(n_tok // tm,),
        in_specs=[pl.BlockSpec((tm, D), lambda i: (i, 0)),
                  pl.BlockSpec((1, D), lambda i: (0, 0)),
                  pl.BlockSpec((1, 1, D), lambda i: (mod_row(i) * 6 + 0, 0, 0)),
                  pl.BlockSpec((1, 1, D), lambda i: (mod_row(i) * 6 + 1, 0, 0))],
        out_specs=pl.BlockSpec((tm, D), lambda i: (i, 0)),
        out_shape=jax.ShapeDtypeStruct((n_tok, D), BF16),
        compiler_params=_cp("arbitrary"),
        name="norm_mod",
    )(x2, gain.reshape(1, D), mod3, mod3)


def _inproj_kernel(h_ref, w_ref, o_ref):
    o_ref[...] = _dot(h_ref[...], w_ref[0]).astype(BF16)


def _inproj_call(h, w_p, layer, rwkv_kv_only=False):
    n_tok = h.shape[0]
    tm = min(2048, n_tok)
    n_rw = RW_PAD // TN_IN
    if rwkv_kv_only:
        ncols = RW_PAD + NP - C_AK
        col = lambda j: jnp.where(j < n_rw, j, j - n_rw + C_AK // TN_IN)
    else:
        ncols = NP
        col = lambda j: j
    return pl.pallas_call(
        _inproj_kernel,
        grid=(n_tok // tm, ncols // TN_IN),
        in_specs=[pl.BlockSpec((tm, D), lambda i, j: (i, 0)),
                  pl.BlockSpec((1, D, TN_IN), lambda i, j: (layer, 0, col(j)))],
        out_specs=pl.BlockSpec((tm, TN_IN), lambda i, j: (i, j)),
        out_shape=jax.ShapeDtypeStruct((n_tok, ncols), BF16),
        compiler_params=_cp("arbitrary", "arbitrary"),
        name="in_proj",
    )(h, w_p)


HALO = 16


def _shifted(cur, prev_blk, next_blk, first, last):
    tt = cur.shape[0]
    row = lax.broadcasted_iota(jnp.int32, cur.shape, 0)
    p_row = jnp.where(first, 0.0, prev_blk[HALO - 1:HALO, :].astype(F32))
    n_row = jnp.where(last, 0.0, next_blk[0:1, :].astype(F32))
    prev = jnp.where(row == 0, p_row, pltpu.roll(cur, 1, axis=0))
    nxt = jnp.where(row == tt - 1, n_row, pltpu.roll(cur, tt - 1, axis=0))
    return prev, nxt


def _halo_specs(tt, width, col_blk, n_tok):
    nb = n_tok // HALO
    r = tt // HALO
    return [pl.BlockSpec((tt, width), lambda i: (i, col_blk)),
            pl.BlockSpec((HALO, width), lambda i: (jnp.maximum(i * r - 1, 0), col_blk)),
            pl.BlockSpec((HALO, width), lambda i: (jnp.minimum((i + 1) * r, nb - 1), col_blk))]


def _streams_kernel(*refs, tiles_per_seq, has_vres):
    if has_vres:
        (p_ref, pp_ref, pn_ref, mu_ref, dup_ref, dbias_ref, iup_ref, ibias_ref, gup_ref, kk_ref_, e_ref,
         vd_ref, vu_ref, vb_ref, vf_ref,
         r_o, k_o, v_o, kk_o, g_o, a_o, lw_o) = refs
    else:
        (p_ref, pp_ref, pn_ref, mu_ref, dup_ref, dbias_ref, iup_ref, ibias_ref, gup_ref, kk_ref_, e_ref,
         r_o, k_o, v_o, kk_o, g_o, a_o, lw_o) = refs
    j = pl.program_id(0) % tiles_per_seq
    used = C_GD + GATE_RANK
    cur_b = p_ref[:, :used]
    tt = cur_b.shape[0]
    ext = jnp.concatenate([pp_ref[:, :used], cur_b, pn_ref[:, :used]], axis=0)
    t_i = lax.broadcasted_iota(jnp.int32, (tt, tt + 2 * HALO), 0)
    e_i = lax.broadcasted_iota(jnp.int32, (tt, tt + 2 * HALO), 1) - HALO
    tap = ((e_i == t_i - 1) & ((e_i >= 0) | (j > 0))) | ((e_i == t_i + 1) & ((e_i < tt) | (j < tiles_per_seq - 1)))
    avg = _dot(jnp.where(tap, 0.5, 0.0).astype(BF16), ext)
    cur = cur_b.astype(F32)
    ps = cur + mu_ref[:, :used] * (avg - cur)
    r = ps[:, C_R:C_R + BW]
    k = ps[:, C_K:C_K + BW]
    v = ps[:, C_V:C_V + BW]
    gd = ps[:, C_GD:C_GD + GATE_RANK]
    if has_vres:
        low = _dot(v.astype(BF16), vd_ref[...])
        mix = _sigmoid(vb_ref[...] + _dot(low.astype(BF16), vu_ref[...]))
        v = v + (vf_ref[...].astype(F32) - v) * mix
    for d in range(2):
        wd = ps[:, C_WD[d]:C_WD[d] + 128]
        ad = ps[:, C_AD[d]:C_AD[d] + 128]
        w_logit = dbias_ref[d] + _dot(jnp.tanh(wd).astype(BF16), dup_ref[d])
        lw_o[d] = -jnp.exp(-0.5) * _sigmoid(w_logit)
        a_o[d] = _sigmoid(ibias_ref[d] + _dot(ad.astype(BF16), iup_ref[d])).astype(BF16)
    kh = k * kk_ref_[...]
    ss = _seg_sum(kh * kh, e_ref)
    kk = kh * lax.rsqrt(jnp.maximum(ss, 1e-24))
    r_o[...] = r.astype(BF16)
    k_o[...] = k.astype(BF16)
    v_o[...] = v.astype(BF16)
    kk_o[...] = kk.astype(BF16)
    g_o[...] = _dot(_sigmoid(gd).astype(BF16), gup_ref[...]).astype(BF16)


def _streams_call(p, seq_len, lp, v_first):
    n_tok = p.shape[0]
    tt = 256
    has_vres = v_first is not None
    full = lambda *s: pl.BlockSpec(s, lambda i: (0,) * len(s))
    tok = pl.BlockSpec((tt, BW), lambda i: (i, 0))
    tok2 = pl.BlockSpec((2, tt, BW), lambda i: (0, i, 0))
    in_specs = _halo_specs(tt, C_GD + GATE_RANK, 0, n_tok) + [
        full(1, RW_PAD), full(2, 128, BW), full(2, 1, BW), full(2, 128, BW), full(2, 1, BW),
        full(GATE_RANK, BW), full(1, BW), full(BW, 128)]
    args = [p, p, p, lp["mu"], lp["decay_up"], lp["decay_bias"], lp["iclr_up"], lp["iclr_bias"],
            lp["gate_up"], lp["k_k"], lp["eblk"]]
    if has_vres:
        in_specs += [full(BW, 128), full(128, BW), full(1, BW), tok]
        args += [lp["vres_down"], lp["vres_up"], lp["vres_bias"], v_first]
    sd = lambda dt: jax.ShapeDtypeStruct((n_tok, BW), dt)
    sd2 = lambda dt: jax.ShapeDtypeStruct((2, n_tok, BW), dt)
    return pl.pallas_call(
        functools.partial(_streams_kernel, tiles_per_seq=seq_len // tt, has_vres=has_vres),
        grid=(n_tok // tt,),
        in_specs=in_specs,
        out_specs=[tok, tok, tok, tok, tok, tok2, tok2],
        out_shape=[sd(BF16), sd(BF16), sd(BF16), sd(BF16), sd(BF16), sd2(BF16), sd2(F32)],
        compiler_params=_cp("arbitrary"),
        name="rwkv_streams",
    )(*args)


def _scan_prologue(d, r, k, v, kk, a, lw, ka):
    c = CHUNK
    ri = lax.broadcasted_iota(jnp.int32, (c, 2 * c), 0)
    ci = lax.broadcasted_iota(jnp.int32, (c, 2 * c), 1) % c
    diff = (ci - ri) if d else (ri - ci)
    incl = diff >= 0
    strict = diff > 0
    tri = incl[:, :c].astype(BF16)
    lw_hi = lw.astype(BF16)
    lw_lo = (lw - lw_hi.astype(F32)).astype(BF16)
    b = _dot(tri, lw_hi) + _dot(tri, lw_lo)
    b_tot = b[0:1, :] if d else b[c - 1:c, :]
    a = a.astype(F32)
    kk = kk.astype(F32)
    kd = k.astype(F32) * (1.0 + (a - 1.0) * ka)
    kka = kk * a
    enb = jnp.exp(-b)
    etail = jnp.exp(b_tot - b)
    return dict(
        incl=incl, strict=strict,
        rt=(r.astype(F32) * jnp.exp(b)).astype(BF16),
        bt=(-kk * jnp.exp(b - lw)).astype(BF16),
        kt=(kd * enb).astype(BF16), at=(kka * enb).astype(BF16),
        kh=(kd * etail).astype(BF16), ah=(kka * etail).astype(BF16),
        vv=v, e_tot=jnp.exp(b_tot))


def _scan_kernel(*refs, with_output, sb):
    ins = refs[:14]
    if with_output:
        yf_ref, yb_ref, sout_ref, st_ref = refs[14:]
    else:
        sout_ref, st_ref = refs[14:]
    ka_ref, s0_ref = ins[12], ins[13]
    s = pl.program_id(1)
    c = CHUNK

    @pl.when(s == 0)
    def _():
        st_ref[...] = s0_ref[...]

    streams = [(d, j) for d in range(2) for j in range(sb)]
    ka = ka_ref[...]
    pro = [_scan_prologue(d, *[ref[j] for ref in ins[6 * d:6 * d + 4]], ins[6 * d + 4][0, j], ins[6 * d + 5][0, j], ka)
           for d, j in streams]
    st_all = [st_ref[d, j] for d, j in streams]

    npair = NH // 2
    units = [(si, slice(p * 2 * HD, (p + 1) * 2 * HD)) for si in range(len(streams)) for p in range(npair)]
    us = range(len(units))
    col = lambda name: [pro[d][name][:, sl] for d, sl in units]
    even1 = lax.broadcasted_iota(jnp.int32, (c, 2 * HD), 1) < HD

    def bd(z):
        zero = jnp.zeros_like(z)
        return jnp.concatenate([jnp.where(even1, z, zero), jnp.where(even1, zero, z)], axis=0)

    strict = [pro[d]["strict"] for d, _ in units]
    incl = [pro[d]["incl"] for d, _ in units]
    st = [st_all[d][:, sl] for d, sl in units]
    vbd = [bd(z) for z in col("vv")]
    rb = [jnp.concatenate(p, axis=0) for p in zip(col("rt"), col("bt"))]
    gk = [_dot_nt(rb[u], bd(z)) for u, z in zip(us, col("kt"))]
    ga = [_dot_nt(rb[u], bd(z)) for u, z in zip(us, col("at"))]
    rbs = [_dot_nt(rb[u], bd(st[u].astype(BF16))) for u in us]
    x = [jnp.where(strict[u], ga[u][c:], 0.0).astype(BF16) for u in us]
    a_bk = [jnp.where(strict[u], gk[u][c:], 0.0) for u in us]
    if with_output:
        a_rk = [jnp.where(incl[u], gk[u][:c], 0.0) for u in us]
        a_ra = [jnp.where(incl[u], ga[u][:c], 0.0).astype(BF16) for u in us]
        av = [_dot(jnp.concatenate([a_rk[u], a_bk[u]], axis=0).astype(BF16), vbd[u]) for u in us]
        w = [rbs[u][c:] + av[u][c:] for u in us]
    else:
        w = [rbs[u][c:] + _dot(a_bk[u].astype(BF16), vbd[u]) for u in us]
    for rnd in range(6):
        if rnd < 5:
            m = [_dot(x[u], jnp.concatenate([bd(x[u]), bd(w[u].astype(BF16))], axis=1)) for u in us]
            x = [m[u][:, :2 * c].astype(BF16) for u in us]
            w = [w[u] + m[u][:, 2 * c:] for u in us]
        else:
            w = [w[u] + _dot(x[u], bd(w[u].astype(BF16))) for u in us]
    ub = [w[u].astype(BF16) for u in us]
    if with_output:
        ys = [rbs[u][:c] + av[u][:c] + _dot(a_ra[u], bd(ub[u])) for u in us]
        for si, (d, j) in enumerate(streams):
            (yb_ref if d else yf_ref)[j] = jnp.concatenate(ys[si * npair:(si + 1) * npair], axis=1)
    full = [_dot_tn(jnp.concatenate([z, ub[u]], axis=0), jnp.concatenate(p, axis=0))
            for u, z, p in zip(us, col("vv"), zip(col("kh"), col("ah")))]
    upd = [jnp.where(even1, f[:HD], f[HD:]) for f in full]
    for si, (d, j) in enumerate(streams):
        st_ref[d, j] = st_all[si] * pro[si]["e_tot"] + jnp.concatenate(upd[si * npair:(si + 1) * npair], axis=1)

    @pl.when(s == pl.num_programs(1) - 1)
    def _():
        sout_ref[...] = st_ref[...]


def _scan_call(streams, k_a, s0, bsz, seq_len, with_output):
    n_tok = streams[0].shape[0]
    nc = seq_len // CHUNK
    sb = 2 if bsz % 2 == 0 else 1
    r, k, v, kk = (z.reshape(bsz, seq_len, BW) for z in streams[:4])
    a2, lw2 = (z.reshape(2, bsz, seq_len, BW) for z in streams[5:])

    chunk = (lambda s: s, lambda s: nc - 1 - s)
    tok = [pl.BlockSpec((sb, CHUNK, BW), lambda b, s, f=f: (b, f(s), 0)) for f in chunk]
    tok2 = [pl.BlockSpec((1, sb, CHUNK, BW), lambda b, s, f=f, d=d: (d, b, f(s), 0)) for d, f in enumerate(chunk)]
    st_spec = pl.BlockSpec((2, sb, HD, BW), lambda b, s: (0, b, 0, 0))
    in_specs, args = [], []
    for d in range(2):
        in_specs += [tok[d]] * 4 + [tok2[d]] * 2
        args += [r, k, v, kk, a2, lw2]
    in_specs += [pl.BlockSpec((1, BW), lambda b, s: (0, 0)), st_spec]
    out_specs = [st_spec]
    out_shape = [jax.ShapeDtypeStruct((2, bsz, HD, BW), F32)]
    if with_output:
        out_specs = tok + out_specs
        out_shape = [jax.ShapeDtypeStruct((bsz, seq_len, BW), F32)] * 2 + out_shape
    res = pl.pallas_call(
        functools.partial(_scan_kernel, with_output=with_output, sb=sb),
        grid=(bsz // sb, nc),
        in_specs=in_specs,
        out_specs=out_specs,
        out_shape=out_shape,
        scratch_shapes=[pltpu.VMEM((2, sb, HD, BW), F32)],
        compiler_params=_cp("arbitrary", "arbitrary"),
        name="rwkv_scan",
    )(*args, k_a, s0)
    if with_output:
        return res[2], (res[0].reshape(n_tok, BW), res[1].reshape(n_tok, BW))
    return res[0], None


def _rwkv_out_kernel(yf_ref, yb_ref, r_ref, k_ref, v_ref, a_ref, g_ref, ka_ref, rk_ref, gnw_ref, gnb_ref, e_ref,
                     o_ref):
    y = yf_ref[...] + yb_ref[...]
    mean = _seg_sum(y, e_ref) * (1.0 / HD)
    yc = y - mean
    var = _seg_sum(yc * yc, e_ref) * (1.0 / HD)
    yn = yc * lax.rsqrt(var + GN_EPS) * gnw_ref[...] + gnb_ref[...]
    r = r_ref[...].astype(F32)
    k = k_ref[...].astype(F32)
    asum = a_ref[0].astype(F32) + a_ref[1].astype(F32)
    kd_sum = k * (2.0 + (asum - 2.0) * ka_ref[...])
    bonus = _seg_sum(r * kd_sum * rk_ref[...], e_ref) * v_ref[...].astype(F32)
    o_ref[...] = ((yn + bonus) * g_ref[...].astype(F32)).astype(BF16)


def _rwkv_out_call(y2, streams, lp):
    r, k, v, _, g, a2, _ = streams
    n_tok = r.shape[0]
    tt = 256
    tok = pl.BlockSpec((tt, BW), lambda i: (i, 0))
    tok2 = pl.BlockSpec((2, tt, BW), lambda i: (0, i, 0))
    vec = pl.BlockSpec((1, BW), lambda i: (0, 0))
    return pl.pallas_call(
        _rwkv_out_kernel,
        grid=(n_tok // tt,),
        in_specs=[tok, tok, tok, tok, tok, tok2, tok, vec, vec, vec, vec, pl.BlockSpec((BW, 128), lambda i: (0, 0))],
        out_specs=tok,
        out_shape=jax.ShapeDtypeStruct((n_tok, BW), BF16),
        compiler_params=_cp("arbitrary"),
        name="rwkv_out",
    )(y2[0], y2[1], r, k, v, a2, g, lp["k_a"], lp["r_k"], lp["gn_w"], lp["gn_b"], lp["eblk"])


def _conv_kernel(b_ref, c_ref, cp_ref, cn_ref, u_ref, up_ref, un_ref, w_ref, o_ref, *, tiles_per_seq):
    j = pl.program_id(0) % tiles_per_seq
    cu = c_ref[...].astype(F32) * u_ref[...].astype(F32)
    cu_p = cp_ref[...].astype(F32) * up_ref[...].astype(F32)
    cu_n = cn_ref[...].astype(F32) * un_ref[...].astype(F32)
    prev, nxt = _shifted(cu, cu_p, cu_n, j == 0, j == tiles_per_seq - 1)
    w = w_ref[...]
    conv = w[0:1] * prev + w[1:2] * cu + w[2:3] * nxt
    o_ref[...] = (b_ref[...].astype(F32) * conv).astype(BF16)


def _conv_call(p, seq_len, conv_w):
    n_tok = p.shape[0]
    tt = 256
    cb = C_CONV // BW
    return pl.pallas_call(
        functools.partial(_conv_kernel, tiles_per_seq=seq_len // tt),
        grid=(n_tok // tt,),
        in_specs=[pl.BlockSpec((tt, BW), lambda i: (i, cb))] + _halo_specs(tt, BW, cb + 1, n_tok)
        + _halo_specs(tt, BW, cb + 2, n_tok) + [pl.BlockSpec((3, BW), lambda i: (0, 0))],
        out_specs=pl.BlockSpec((tt, BW), lambda i: (i, 0)),
        out_shape=jax.ShapeDtypeStruct((n_tok, BW), BF16),
        compiler_params=_cp("arbitrary"),
        name="short_conv",
    )(p, p, p, p, p, p, p, conv_w)


def _rope_all(blocks, cos, sin):
    ri = lax.broadcasted_iota(jnp.int32, (LANE, LANE), 0)
    ci = lax.broadcasted_iota(jnp.int32, (LANE, LANE), 1)
    perm = (ri == jnp.where((ci % 32) < 16, ci + 16, ci - 16)).astype(BF16)
    slices = [b[:, j:j + LANE] for b in blocks for j in range(0, b.shape[1], LANE)]
    moved = _dot(jnp.concatenate(slices, axis=0), perm)
    outs, row = [], 0
    for b, c, s in zip(blocks, cos, sin):
        cols = []
        for j in range(0, b.shape[1], LANE):
            cols.append(moved[row:row + b.shape[0]])
            row += b.shape[0]
        outs.append(b.astype(F32) * c + jnp.concatenate(cols, axis=1) * s)
    return outs


LOG2E = 1.4426950408889634
Q_SCALE = HD ** -0.5 * LOG2E


def _softmax_pv(s2, sink2_col, v):
    m = jnp.maximum(jnp.max(s2, axis=-1, keepdims=True), sink2_col)
    p = jnp.exp2(s2 - m)
    den = jnp.sum(p, axis=-1, keepdims=True) + jnp.exp2(sink2_col - m)
    return _dot(p.astype(BF16), v) / den


def _attn_lat_kernel(q_ref, kp_ref, kc_ref, kn_ref, vp_ref, vc_ref, vn_ref, kx_ref, vx_ref,
                     cos_ref, cosp_ref, cosn_ref, sin_ref, sinp_ref, sinn_ref, sink_ref, band_ref, o_ref, *, nblk):
    n = pl.program_id(1)
    kvw = KVH * HD
    q, kp, kc, kn = _rope_all(
        [q_ref[...], kp_ref[...], kc_ref[...], kn_ref[...]],
        [cos_ref[...], cosp_ref[:, :kvw], cos_ref[:, :kvw], cosn_ref[:, :kvw]],
        [sin_ref[...], sinp_ref[:, :kvw], sin_ref[:, :kvw], sinn_ref[:, :kvw]])
    q = (q * Q_SCALE).astype(BF16)
    k_all = jnp.concatenate([kp.astype(BF16), kc.astype(BF16), kn.astype(BF16), kx_ref[...]], axis=0)
    v_all = jnp.concatenate([vp_ref[...], vc_ref[...], vn_ref[...], vx_ref[...]], axis=0)
    bias_p = jnp.where(n > 0, band_ref[0], NEG_INF)
    bias_n = jnp.where(n < nblk - 1, band_ref[1], NEG_INF)
    sink = sink_ref[...] * LOG2E
    gs = range(KVH)
    even = lax.broadcasted_iota(jnp.int32, (BLK, 2 * HD), 1) < HD
    zero = jnp.zeros((BLK, 2 * HD), BF16)
    eye = (lax.broadcasted_iota(jnp.int32, (2 * HD, 2 * HD), 0)
           == lax.broadcasted_iota(jnp.int32, (2 * HD, 2 * HD), 1)).astype(BF16)

    def head_rows(gi):
        parts = []
        for t in range(GQ // 2):
            qp = q[:, (gi * GQ + 2 * t) * HD:(gi * GQ + 2 * t + 2) * HD]
            parts += [jnp.where(even, qp, zero), jnp.where(even, zero, qp)]
        return jnp.concatenate(parts, axis=0)

    qg = [head_rows(gi) for gi in gs]
    kd = [jnp.concatenate([k_all[:, gi * HD:(gi + 1) * HD]] * 2, axis=1) for gi in gs]
    vt = [_dot_nt(eye[:HD, :HD], v_all[:, gi * HD:(gi + 1) * HD]).astype(BF16) for gi in gs]
    sk = [jnp.concatenate([jnp.broadcast_to(sink[:, gi * GQ + t:gi * GQ + t + 1], (1, BLK))
                           for t in range(GQ)], axis=1) for gi in gs]
    s = [_dot_nt(kd[gi], qg[gi]) for gi in gs]
    s = [jnp.concatenate([z[:BLK] + bias_p, z[BLK:2 * BLK], z[2 * BLK:3 * BLK] + bias_n, z[3 * BLK:]], axis=0)
         for z in s]
    m = [jnp.maximum(jnp.max(s[gi], axis=0, keepdims=True), sk[gi]) for gi in gs]
    p = [jnp.exp2(s[gi] - m[gi]).astype(BF16) for gi in gs]
    ones = jnp.ones((16, vt[0].shape[1]), BF16)
    pv = [_dot(jnp.concatenate([vt[gi], ones], axis=0), p[gi]) for gi in gs]
    og = [(pv[gi][:HD] / (pv[gi][HD:HD + 1] + jnp.exp2(sk[gi] - m[gi]))).astype(BF16) for gi in gs]
    o_ref[...] = jnp.concatenate(
        [_dot_nt(eye, jnp.concatenate([og[gi][:, 2 * t * BLK:(2 * t + 1) * BLK],
                                       og[gi][:, (2 * t + 1) * BLK:(2 * t + 2) * BLK]], axis=0))
         for gi in gs for t in range(GQ // 2)], axis=1).astype(BF16)


def _attn_lat_call(p_l, p_c, kv_blk_c, cos, sin, sink, bsz, seq_len, ctx_len):
    nblk = seq_len // BLK
    kvw = KVH * HD
    kb, vb = C_AK // kvw, C_AV // kvw
    rowq = lambda b, n: b * nblk + n
    rowp = lambda b, n: b * nblk + jnp.maximum(n - 1, 0)
    rown = lambda b, n: b * nblk + jnp.minimum(n + 1, nblk - 1)
    tabp = lambda b, n: (jnp.maximum(n - 1, 0), 0)
    tabn = lambda b, n: (jnp.minimum(n + 1, nblk - 1), 0)
    kv = lambda rf, cb: pl.BlockSpec((BLK, kvw), lambda b, n: (rf(b, n), cb))
    tab = lambda f: pl.BlockSpec((BLK, BW), f)
    kidx = jnp.arange(BLK)[:, None]
    qpos = jnp.arange(GQ * BLK)[None, :] % BLK
    band = jnp.where(jnp.stack([kidx >= qpos, kidx <= qpos]), 0.0, NEG_INF).astype(F32)
    return pl.pallas_call(
        functools.partial(_attn_lat_kernel, nblk=nblk),
        grid=(bsz, nblk),
        in_specs=[pl.BlockSpec((BLK, BW), lambda b, n: (rowq(b, n), C_Q // BW)),
                  kv(rowp, kb), kv(rowq, kb), kv(rown, kb), kv(rowp, vb), kv(rowq, vb), kv(rown, vb),
                  pl.BlockSpec((ctx_len, kvw), lambda b, n: (b, kv_blk_c)),
                  pl.BlockSpec((ctx_len, kvw), lambda b, n: (b, kv_blk_c + 1)),
                  tab(lambda b, n: (n, 0)), tab(tabp), tab(tabn),
                  tab(lambda b, n: (n, 0)), tab(tabp), tab(tabn),
                  pl.BlockSpec((1, NH), lambda b, n: (0, 0)),
                  pl.BlockSpec((2, BLK, GQ * BLK), lambda b, n: (0, 0, 0))],
        out_specs=pl.BlockSpec((BLK, BW), lambda b, n: (rowq(b, n), 0)),
        out_shape=jax.ShapeDtypeStruct((bsz * seq_len, BW), BF16),
        compiler_params=_cp("arbitrary", "arbitrary"),
        name="attn_latent",
    )(p_l, p_l, p_l, p_l, p_l, p_l, p_l, p_c, p_c, cos, cos, cos, sin, sin, sin, sink, band)


def _attn_ctx_kernel(q_ref, kx_ref, vx_ref, sink_ref, o_ref):
    q = (q_ref[...].astype(F32) * Q_SCALE).astype(BF16)
    kx = kx_ref[...]
    vx = vx_ref[...]
    sink = sink_ref[...] * LOG2E
    tq = q.shape[0]
    outs = []
    for gi in range(KVH):
        qg = jnp.concatenate([q[:, (gi * GQ + t) * HD:(gi * GQ + t + 1) * HD] for t in range(GQ)], axis=0)
        sk = jnp.concatenate([jnp.broadcast_to(sink[:, gi * GQ + t:gi * GQ + t + 1], (tq, 1))
                              for t in range(GQ)], axis=0)
        s = _dot_nt(qg, kx[:, gi * HD:(gi + 1) * HD])
        og = _softmax_pv(s, sk, vx[:, gi * HD:(gi + 1) * HD])
        outs += [og[t * tq:(t + 1) * tq] for t in range(GQ)]
    o_ref[...] = jnp.concatenate(outs, axis=1).astype(BF16)


def _attn_ctx_call(p_c, sink, bsz, ctx_len):
    kvw = KVH * HD
    tq = 128
    nq = ctx_len // tq
    return pl.pallas_call(
        _attn_ctx_kernel,
        grid=(bsz, nq),
        in_specs=[pl.BlockSpec((tq, BW), lambda b, n: (b * nq + n, C_Q // BW)),
                  pl.BlockSpec((ctx_len, kvw), lambda b, n: (b, C_AK // kvw)),
                  pl.BlockSpec((ctx_len, kvw), lambda b, n: (b, C_AV // kvw)),
                  pl.BlockSpec((1, NH), lambda b, n: (0, 0))],
        out_specs=pl.BlockSpec((tq, BW), lambda b, n: (b * nq + n, 0)),
        out_shape=jax.ShapeDtypeStruct((bsz * ctx_len, BW), BF16),
        compiler_params=_cp("arbitrary", "arbitrary"),
        name="attn_context",
    )(p_c, p_c, p_c, sink)


def _merge_kernel(b0_ref, b1_ref, b2_ref, g0_ref, g1_ref, g2_ref, w_ref, o_ref):
    acc = _sigmoid(g0_ref[...].astype(F32)) * _dot(b0_ref[...], w_ref[0])
    acc += _sigmoid(g1_ref[...].astype(F32)) * _dot(b1_ref[...], w_ref[1])
    acc += _sigmoid(g2_ref[...].astype(F32)) * _dot(b2_ref[...], w_ref[2])
    o_ref[...] = acc.astype(BF16)


def _merge_call(br_rwkv, br_conv, br_attn, p, w_branch):
    n_tok = p.shape[0]
    tm, tn = 512, 1024
    nn = D // tn
    br = pl.BlockSpec((tm, BW), lambda j, i: (i, 0))
    gate = lambda t: pl.BlockSpec((tm, tn), lambda j, i: (i, (C_GATE + t * D) // tn + j))
    return pl.pallas_call(
        _merge_kernel,
        grid=(nn, n_tok // tm),
        in_specs=[br, br, br, gate(0), gate(1), gate(2), pl.BlockSpec((3, BW, tn), lambda j, i: (0, 0, j))],
        out_specs=pl.BlockSpec((tm, tn), lambda j, i: (i, j)),
        out_shape=jax.ShapeDtypeStruct((n_tok, D), BF16),
        compiler_params=_cp("arbitrary", "arbitrary"),
        name="merge_branches",
    )(br_rwkv, br_conv, br_attn, p, p, p, w_branch)


def _outproj_kernel(m_ref, w_ref, x_ref, gate_ref, o_ref):
    o_ref[...] = x_ref[...] + gate_ref[0] * _dot(m_ref[...], w_ref[...])


def _outproj_call(m, w_out, x2, mod3, mod_row, tm):
    n_tok = x2.shape[0]
    return pl.pallas_call(
        _outproj_kernel,
        grid=(n_tok // tm,),
        in_specs=[pl.BlockSpec((tm, D), lambda i: (i, 0)),
                  pl.BlockSpec((D, D), lambda i: (0, 0)),
                  pl.BlockSpec((tm, D), lambda i: (i, 0)),
                  pl.BlockSpec((1, 1, D), lambda i: (mod_row(i) * 6 + 2, 0, 0))],
        out_specs=pl.BlockSpec((tm, D), lambda i: (i, 0)),
        out_shape=jax.ShapeDtypeStruct((n_tok, D), F32),
        compiler_params=_cp("arbitrary"),
        name="out_proj",
    )(m, w_out, x2, mod3)


def _ffn_prep_kernel(x_ref, gain_ref, shift_ref, scale_ref, wr_ref, h_ref, aff_ref):
    h = _norm_mod(x_ref[...], gain_ref[...], shift_ref[0], scale_ref[0])
    h_hi = h.astype(BF16)
    h_ref[...] = h_hi
    h_lo = (h - h_hi.astype(F32)).astype(BF16)
    wr = wr_ref[...]
    w_hi = wr.astype(BF16)
    w_lo = (wr - w_hi.astype(F32)).astype(BF16)
    logits = _dot(h_hi, w_hi) + (_dot(h_hi, w_lo) + _dot(h_lo, w_hi))
    lane = lax.broadcasted_iota(jnp.int32, logits.shape, 1)
    logits = jnp.where(lane < N_EXPERTS, logits, NEG_INF)
    m = jnp.max(logits, axis=-1, keepdims=True)
    e = jnp.exp(logits - m)
    aff_ref[...] = e / jnp.sum(e, axis=-1, keepdims=True)


def _ffn_prep_call(x2, gain, mod3, mod_row, wr_pad, tm):
    n_tok = x2.shape[0]
    return pl.pallas_call(
        _ffn_prep_kernel,
        grid=(n_tok // tm,),
        in_specs=[pl.BlockSpec((tm, D), lambda i: (i, 0)),
                  pl.BlockSpec((1, D), lambda i: (0, 0)),
                  pl.BlockSpec((1, 1, D), lambda i: (mod_row(i) * 6 + 3, 0, 0)),
                  pl.BlockSpec((1, 1, D), lambda i: (mod_row(i) * 6 + 4, 0, 0)),
                  pl.BlockSpec((D, 128), lambda i: (0, 0))],
        out_specs=[pl.BlockSpec((tm, D), lambda i: (i, 0)), pl.BlockSpec((tm, 128), lambda i: (i, 0))],
        out_shape=[jax.ShapeDtypeStruct((n_tok, D), BF16), jax.ShapeDtypeStruct((n_tok, 128), F32)],
        compiler_params=_cp("arbitrary"),
        name="ffn_prep",
    )(x2, gain.reshape(1, D), mod3, mod3, wr_pad)


def _select_kernel(aff_ref, slot_ref, *, cap):
    a = aff_ref[0]
    n = a.shape[1]
    bits = lax.bitcast_convert_type(a, jnp.int32)

    def body(i, t):
        cand = t | jnp.left_shift(jnp.int32(1), 30 - i)
        cnt = jnp.sum((bits >= cand).astype(jnp.int32), axis=-1, keepdims=True)
        return jnp.where(cnt >= cap, cand, t)

    thr = lax.fori_loop(0, 31, body, jnp.zeros((a.shape[0], 1), jnp.int32))
    gt = bits > thr
    eq = bits == thr
    n_gt = jnp.sum(gt.astype(jnp.int32), axis=-1, keepdims=True)
    tc = min(n, 512)

    def prefix(mask_bf16):
        cols = []
        for j0 in range(0, n, tc):
            ri = lax.broadcasted_iota(jnp.int32, (n, tc), 0)
            ci = lax.broadcasted_iota(jnp.int32, (n, tc), 1) + j0
            cols.append(_dot(mask_bf16, (ri <= ci).astype(BF16)))
        return jnp.concatenate(cols, axis=1)

    eq_f = eq.astype(BF16)
    excl_eq = prefix(eq_f) - eq_f.astype(F32)
    sel = gt | (eq & (excl_eq < (cap - n_gt).astype(F32)))
    pos = prefix(sel.astype(BF16))
    slot_ref[0] = jnp.where(sel, pos.astype(jnp.int32) - 1, -1)


def _select_call(aff_t, cap):
    bsz, ne, n = aff_t.shape
    return pl.pallas_call(
        functools.partial(_select_kernel, cap=cap),
        grid=(bsz,),
        in_specs=[pl.BlockSpec((1, ne, n), lambda b: (b, 0, 0))],
        out_specs=pl.BlockSpec((1, ne, n), lambda b: (b, 0, 0)),
        out_shape=jax.ShapeDtypeStruct((bsz, ne, n), jnp.int32),
        compiler_params=_cp("arbitrary"),
        name="expert_select",
    )(aff_t)


def _gather_kernel(slot_ref, aff_ref, h_ref, xs_ref, g_ref, *, cap):
    slot = slot_ref[0]
    n = slot.shape[1]
    onehot = lax.broadcasted_iota(jnp.int32, (cap, n), 0) == slot
    xs_ref[0] = _dot(onehot.astype(BF16), h_ref[...]).astype(BF16)
    g_ref[0] = jnp.sum(jnp.where(onehot, aff_ref[0], 0.0), axis=-1, keepdims=True)


def _gather_call(slot, aff_t, h2, cap):
    bsz, ne, n = slot.shape
    return pl.pallas_call(
        functools.partial(_gather_kernel, cap=cap),
        grid=(bsz, ne),
        in_specs=[pl.BlockSpec((1, 1, n), lambda b, e: (b * ne + e, 0, 0)),
                  pl.BlockSpec((1, 1, n), lambda b, e: (b * ne + e, 0, 0)),
                  pl.BlockSpec((n, D), lambda b, e: (b, 0))],
        out_specs=[pl.BlockSpec((1, cap, D), lambda b, e: (e, b, 0)),
                   pl.BlockSpec((1, cap, 1), lambda b, e: (e, b, 0))],
        out_shape=[jax.ShapeDtypeStruct((ne, bsz * cap, D), BF16),
                   jax.ShapeDtypeStruct((ne, bsz * cap, 1), F32)],
        compiler_params=_cp("arbitrary", "arbitrary"),
        name="expert_gather",
    )(slot.reshape(bsz * ne, 1, n), aff_t.reshape(bsz * ne, 1, n), h2)


def _expert_kernel(x_ref, wg_ref, wu_ref, wd_ref, g_ref, o_ref, acc_ref):
    f = pl.program_id(2)

    @pl.when(f == 0)
    def _():
        acc_ref[...] = jnp.zeros_like(acc_ref)

    x = x_ref[0]
    gate = _dot(x, wg_ref[0, 0].astype(BF16))
    hid = gate * _sigmoid(gate) * _dot(x, wu_ref[0, 0].astype(BF16))
    acc_ref[...] += _dot(hid.astype(BF16), wd_ref[0, 0].astype(BF16))

    @pl.when(f == pl.num_programs(2) - 1)
    def _():
        o_ref[0] = (acc_ref[...] * g_ref[0]).astype(BF16)


def _expert_call(xs, g, wg, wu, wd, layer):
    ne, rows, _ = xs.shape
    tm = min(rows, 1024)
    tf = 256
    return pl.pallas_call(
        _expert_kernel,
        grid=(ne, rows // tm, EXPERT_FF // tf),
        in_specs=[pl.BlockSpec((1, tm, D), lambda e, i, f: (e, i, 0)),
                  pl.BlockSpec((1, 1, D, tf), lambda e, i, f: (layer, e, 0, f)),
                  pl.BlockSpec((1, 1, D, tf), lambda e, i, f: (layer, e, 0, f)),
                  pl.BlockSpec((1, 1, tf, D), lambda e, i, f: (layer, e, f, 0)),
                  pl.BlockSpec((1, tm, 1), lambda e, i, f: (e, i, 0))],
        out_specs=pl.BlockSpec((1, tm, D), lambda e, i, f: (e, i, 0)),
        out_shape=jax.ShapeDtypeStruct((ne, rows, D), BF16),
        scratch_shapes=[pltpu.VMEM((tm, D), F32)],
        compiler_params=_cp("arbitrary", "arbitrary", "arbitrary"),
        name="expert_mlp",
    )(xs, wg, wu, wd, g)


def _scatter_kernel(*refs, cap, last):
    if last:
        slot_ref, ys_ref, x_ref, gate_ref, gain_ref, o_ref = refs
    else:
        slot_ref, ys_ref, x_ref, gate_ref, gain_ref, shift_ref, scale_ref, o_ref, h_ref = refs
    slot_t = slot_ref[0]
    tq = slot_t.shape[0]
    lane = lax.broadcasted_iota(jnp.int32, (tq, cap), 1)
    acc = jnp.zeros(x_ref.shape, F32)
    for e in range(N_EXPERTS):
        onehot = (slot_t[:, e:e + 1] == lane).astype(BF16)
        acc += _dot(onehot, ys_ref[e])
    x = x_ref[...] + gate_ref[0] * acc
    if last:
        ms = jnp.mean(x * x, axis=-1, keepdims=True)
        o_ref[...] = (x * lax.rsqrt(ms + NORM_EPS)) * gain_ref[...]
    else:
        o_ref[...] = x
        h_ref[...] = _norm_mod(x, gain_ref[...], shift_ref[0], scale_ref[0]).astype(BF16)


def _scatter_call(slot_t, ys, x2, mod3, mod_row_b, cap, gain, next_mod3=None):
    bsz, n, ne = slot_t.shape
    tq = min(n, 256)
    nt = n // tq
    last = next_mod3 is None
    row = pl.BlockSpec((tq, D), lambda b, t: (b * nt + t, 0))
    mod = lambda c: pl.BlockSpec((1, 1, D), lambda b, t: (mod_row_b(b) * 6 + c, 0, 0))
    in_specs = [pl.BlockSpec((1, tq, ne), lambda b, t: (b, t, 0)),
                pl.BlockSpec((ne, cap, D), lambda b, t: (0, b, 0)),
                row, mod(5), pl.BlockSpec((1, D), lambda b, t: (0, 0))]
    args = [slot_t, ys, x2, mod3, gain.reshape(1, D)]
    out_specs, out_shape = row, jax.ShapeDtypeStruct(x2.shape, F32)
    if not last:
        in_specs += [mod(0), mod(1)]
        args += [next_mod3, next_mod3]
        out_specs, out_shape = [row, row], [out_shape, jax.ShapeDtypeStruct(x2.shape, BF16)]
    return pl.pallas_call(
        functools.partial(_scatter_kernel, cap=cap, last=last),
        grid=(bsz, nt),
        in_specs=in_specs,
        out_specs=out_specs,
        out_shape=out_shape,
        compiler_params=_cp("arbitrary", "arbitrary"),
        name="expert_scatter",
    )(*args)


def _ffn(x2, bsz, n, lp, mod3, mod_row, mod_row_b, tm, gain, next_mod3=None):
    cap = CAPACITY_FACTOR * n // N_EXPERTS
    h2, aff = _ffn_prep_call(x2, lp["norm_ffn"], mod3, mod_row, lp["w_router"], tm)
    aff_t = jnp.swapaxes(aff[:, :N_EXPERTS].reshape(bsz, n, N_EXPERTS), 1, 2)
    slot = _select_call(aff_t, cap)
    xs, g = _gather_call(slot, aff_t, h2, cap)
    ys = _expert_call(xs, g, lp["w_exp_gate"], lp["w_exp_up"], lp["w_exp_down"], lp["layer"])
    return _scatter_call(jnp.swapaxes(slot, 1, 2), ys, x2, mod3, mod_row_b, cap, gain, next_mod3)


def _pad_cols(parts):
    out = []
    for a, width in parts:
        out.append(a)
        if a.shape[-1] < width:
            out.append(jnp.zeros(a.shape[:-1] + (width - a.shape[-1],), a.dtype))
    return jnp.concatenate(out, axis=-1)


def _rwkv_col_parts(w):
    o = 3 * BW
    parts = [(w[..., :o], o)]
    for _ in range(2):
        parts.append((w[..., o:o + DECAY_RANK], 128))
        o += DECAY_RANK
    for _ in range(2):
        parts.append((w[..., o:o + ICLR_RANK], 128))
        o += ICLR_RANK
    parts.append((w[..., o:o + GATE_RANK], RW_PAD - C_GD))
    return parts, o + GATE_RANK


LANE = 128
RW_END = 3 * BW + 2 * DECAY_RANK + 2 * ICLR_RANK + GATE_RANK
CODE_W = RW_PAD - 3 * BW


def _wrelayout_kernel(w_ref, misc_ref, o_ref):
    j = pl.program_id(1)
    is_code = (j >= 3 * BW // LANE) & (j < RW_PAD // LANE)
    o_ref[0] = jnp.where(is_code, misc_ref[0], w_ref[0]).astype(BF16)


def _wrelayout_call(w_in):
    depth = w_in.shape[0]
    code_parts, _ = _rwkv_col_parts(w_in[:, :, :RW_END])
    misc = _pad_cols(code_parts[1:])
    nb = NP // LANE
    kv_src = RW_END // LANE

    def src(j):
        shifted = jnp.where(j >= C_AK // LANE, j - C_AK // LANE + kv_src, j + (RW_END + 2 * KVH * HD - C_Q) // LANE)
        return jnp.where(j < RW_PAD // LANE, jnp.minimum(j, 3 * BW // LANE - 1), shifted)

    def misc_blk(j):
        return jnp.clip(j - 3 * BW // LANE, 0, CODE_W // LANE - 1)

    return pl.pallas_call(
        _wrelayout_kernel,
        grid=(depth, nb),
        in_specs=[pl.BlockSpec((1, D, LANE), lambda l, j: (l, 0, src(j))),
                  pl.BlockSpec((1, D, LANE), lambda l, j: (l, 0, misc_blk(j)))],
        out_specs=pl.BlockSpec((1, D, LANE), lambda l, j: (l, 0, j)),
        out_shape=jax.ShapeDtypeStruct((depth, D, NP), BF16),
        compiler_params=_cp("arbitrary", "arbitrary"),
        name="w_in_relayout",
    )(w_in, misc)


def _prep_layer(l, depth, shift_mu, decay_up, decay_bias, iclr_up, iclr_bias, gate_up, vres_down, vres_up,
                vres_bias, k_k, k_a, r_k, gn_w, gn_b, conv_w, attn_sink, w_branch, w_out, w_router,
                w_exp_gate, w_exp_up, w_exp_down, norm_mix, norm_ffn, eblk):
    mu_parts, _ = _rwkv_col_parts(shift_mu[l][None, :])
    pad_rank = lambda u: jnp.pad(u, ((0, 0), (0, 128 - u.shape[1]), (0, 0))).astype(BF16)
    lp = dict(
        mu=_pad_cols(mu_parts),
        decay_up=pad_rank(decay_up[l]), decay_bias=decay_bias[l].reshape(2, 1, BW),
        iclr_up=pad_rank(iclr_up[l]), iclr_bias=iclr_bias[l].reshape(2, 1, BW),
        gate_up=gate_up[l].astype(BF16), k_k=k_k[l].reshape(1, BW), k_a=k_a[l].reshape(1, BW),
        r_k=r_k[l].reshape(1, BW), gn_w=gn_w[l].reshape(1, BW), gn_b=gn_b[l].reshape(1, BW),
        conv_w=conv_w[l], sink=attn_sink[l].reshape(1, NH),
        w_branch=w_branch[l].astype(BF16), w_out=w_out[l].astype(BF16),
        w_router=jnp.pad(w_router[l], ((0, 0), (0, 128 - N_EXPERTS))),
        layer=l, w_exp_gate=w_exp_gate, w_exp_up=w_exp_up, w_exp_down=w_exp_down,
        norm_mix=norm_mix[l], norm_ffn=norm_ffn[l], eblk=eblk)
    if l > 0:
        lp["vres_down"] = jnp.pad(vres_down[l - 1], ((0, 0), (0, 128 - VRES_RANK))).astype(BF16)
        lp["vres_up"] = jnp.pad(vres_up[l - 1], ((0, 128 - VRES_RANK), (0, 0))).astype(BF16)
        lp["vres_bias"] = vres_bias[l - 1].reshape(1, BW)
    return lp


def _rope_tables(seq_len):
    quarter = HD // 4
    inv = ROPE_BASE ** (-jnp.arange(quarter, dtype=F32) / quarter)
    pos = jnp.arange(seq_len)
    ang_r = (pos // GRID_W).astype(F32)[:, None] * inv[None, :]
    ang_c = (pos % GRID_W).astype(F32)[:, None] * inv[None, :]
    cos = jnp.concatenate([jnp.cos(ang_r)] * 2 + [jnp.cos(ang_c)] * 2, axis=1)
    sin = jnp.concatenate([-jnp.sin(ang_r), jnp.sin(ang_r), -jnp.sin(ang_c), jnp.sin(ang_c)], axis=1)
    return jnp.tile(cos, (1, NH)), jnp.tile(sin, (1, NH))


def kernel(x, c, ctx, c_ctx, w_mod, b_mod, norm_mix, norm_ffn, w_in, shift_mu, decay_up, decay_bias, iclr_up,
           iclr_bias, gate_up, vres_down, vres_up, vres_bias, k_k, k_a, r_k, gn_w, gn_b, conv_w, attn_sink,
           w_branch, w_out, w_router, w_exp_gate, w_exp_up, w_exp_down, norm_final):
    bsz, seq_len, _ = x.shape
    ctx_len = ctx.shape[1]
    depth = w_in.shape[0]
    mod_rows = -(-(bsz + 1) // 8) * 8
    cc = jnp.concatenate([c, c_ctx[None, :], jnp.zeros((mod_rows - bsz - 1, D), F32)], axis=0)
    mod_all = _mod_call(cc, w_mod, b_mod)

    hd_i = jnp.arange(BW) // HD
    eblk = (hd_i[:, None] == jnp.arange(128)[None, :]).astype(BF16)
    cos, sin = _rope_tables(seq_len)

    row_c = lambda i: bsz
    tm2 = 512
    row_l2 = lambda i: i // (seq_len // tm2)
    mod3_of = lambda l: mod_all[l].reshape(mod_rows * 6, 1, D)

    x_l = x.reshape(bsz * seq_len, D)
    x_c = ctx.reshape(bsz * ctx_len, D)
    vf_l = vf_c = None
    zero_state = jnp.zeros((2, bsz, HD, BW), F32)

    w_p = _wrelayout_call(w_in)
    h_l = _normmod_call(x_l, norm_mix[0], mod3_of(0), row_l2, tm2)
    h_c = _normmod_call(x_c, norm_mix[0], mod3_of(0), row_c, tm2)

    for l in range(depth):
        last = l == depth - 1
        lp = _prep_layer(l, depth, shift_mu, decay_up, decay_bias, iclr_up, iclr_bias, gate_up, vres_down,
                         vres_up, vres_bias, k_k, k_a, r_k, gn_w, gn_b, conv_w, attn_sink, w_branch, w_out,
                         w_router, w_exp_gate, w_exp_up, w_exp_down, norm_mix, norm_ffn, eblk)
        mod3 = mod_all[l].reshape(mod_rows * 6, 1, D)

        p_l = _inproj_call(h_l, w_p, l)
        p_c = _inproj_call(h_c, w_p, l, rwkv_kv_only=last)
        nxt = (norm_final, None) if last else (norm_mix[l + 1], mod3_of(l + 1))
        kv_blk_c = (RW_PAD if last else C_AK) // (KVH * HD)

        st_c = _streams_call(p_c, ctx_len, lp, vf_c)
        st_l = _streams_call(p_l, seq_len, lp, vf_l)
        if l == 0:
            vf_c, vf_l = st_c[2], st_l[2]
        state_c, y_c = _scan_call(st_c, lp["k_a"], zero_state, bsz, ctx_len, not last)
        _, y_l = _scan_call(st_l, lp["k_a"], state_c, bsz, seq_len, True)
        br_rwkv_l = _rwkv_out_call(y_l, st_l, lp)

        br_attn_l = _attn_lat_call(p_l, p_c, kv_blk_c, cos, sin, lp["sink"], bsz, seq_len, ctx_len)
        br_conv_l = _conv_call(p_l, seq_len, lp["conv_w"])

        m_l = _merge_call(br_rwkv_l, br_conv_l, br_attn_l, p_l, lp["w_branch"])
        x_l = _outproj_call(m_l, lp["w_out"], x_l, mod3, row_l2, tm2)
        res = _ffn(x_l, bsz, seq_len, lp, mod3, row_l2, lambda b: b, tm2, *nxt)
        x_l, h_l = (res, None) if last else res

        if not last:
            br_rwkv_c = _rwkv_out_call(y_c, st_c, lp)
            br_attn_c = _attn_ctx_call(p_c, lp["sink"], bsz, ctx_len)
            br_conv_c = _conv_call(p_c, ctx_len, lp["conv_w"])
            m_c = _merge_call(br_rwkv_c, br_conv_c, br_attn_c, p_c, lp["w_branch"])
            x_c = _outproj_call(m_c, lp["w_out"], x_c, mod3, row_c, tm2)
            x_c, h_c = _ffn(x_c, bsz, ctx_len, lp, mod3, row_c, lambda b: bsz, tm2, *nxt)

    return x_l.reshape(bsz, seq_len, D)
```

```python
import functools

import jax
import jax.numpy as jnp
from jax import lax
from jax.experimental import pallas as pl
from jax.experimental.pallas import tpu as pltpu

F32 = jnp.float32
BF16 = jnp.bfloat16
LANE = 128

D = 2048
HD = 64
BW = 1024
NH = BW // HD
KVH = 4
GQ = NH // KVH
DECAY_RANK = 96
ICLR_RANK = 96
GATE_RANK = 256
VRES_RANK = 64
GN_EPS = 64e-5
NORM_EPS = 1e-6
WINDOW = 128
BLK = 128
assert WINDOW == BLK
GRID_W = 64
ROPE_BASE = 10000.0
NEG_INF = -1e30
N_EXPERTS = 16
EXPERT_FF = 2048
CAPACITY_FACTOR = 2
CHUNK = 64

C_R, C_K, C_V = 0, 1024, 2048
C_WD = (3072, 3200)
C_AD = (3328, 3456)
C_GD = 3584
RW_PAD = 4096
C_Q = 4096
C_CONV = 5120
C_GATE = 8192
C_AK = 14336
C_AV = 14592
NP = 14848
TN_IN = 512

VMEM_LIMIT = 56 * 1024 * 1024


def _cp(*sem, vmem=VMEM_LIMIT):
    return pltpu.CompilerParams(dimension_semantics=tuple(sem), vmem_limit_bytes=vmem)


def _dot(a, b):
    return jnp.dot(a, b, preferred_element_type=F32)


def _dot_nt(a, b):
    return lax.dot_general(a, b, (((1,), (1,)), ((), ())), preferred_element_type=F32)


def _dot_tn(a, b):
    return lax.dot_general(a, b, (((0,), (0,)), ((), ())), preferred_element_type=F32)


def _sigmoid(x):
    return 0.5 * jnp.tanh(0.5 * x) + 0.5


def _seg_sum(x, e_ref):
    e = e_ref[...]

    def split(z):
        hi = z.astype(BF16)
        return hi, (z - hi.astype(F32)).astype(BF16)

    hi, lo = split(x)
    hi2, lo2 = split(_dot(hi, e) + _dot(lo, e))
    return _dot_nt(hi2, e) + _dot_nt(lo2, e)


def _mod_kernel(c_ref, w_ref, b_ref, o_ref):
    c = c_ref[...]
    sc = c * _sigmoid(c)
    o_ref[0] = _dot(sc.astype(BF16), w_ref[0].astype(BF16)) + b_ref[0]


def _mod_call(cc, w_mod, b_mod):
    depth, _, n6 = w_mod.shape
    rows = cc.shape[0]
    tn = 1024
    return pl.pallas_call(
        _mod_kernel,
        grid=(depth, n6 // tn),
        in_specs=[pl.BlockSpec((rows, D), lambda l, j: (0, 0)),
                  pl.BlockSpec((1, D, tn), lambda l, j: (l, 0, j)),
                  pl.BlockSpec((1, 1, tn), lambda l, j: (l, 0, j))],
        out_specs=pl.BlockSpec((1, rows, tn), lambda l, j: (l, 0, j)),
        out_shape=jax.ShapeDtypeStruct((depth, rows, n6), F32),
        compiler_params=_cp("arbitrary", "arbitrary"),
        name="mod_proj",
    )(cc, w_mod, b_mod.reshape(depth, 1, n6))


def _norm_mod(x, gain, shift, scale):
    ms = jnp.mean(x * x, axis=-1, keepdims=True)
    y = x * lax.rsqrt(ms + NORM_EPS)
    return (y * gain) * (1.0 + scale) + shift


def _normmod_kernel(x_ref, gain_ref, shift_ref, scale_ref, h_ref):
    h_ref[...] = _norm_mod(x_ref[...], gain_ref[...], shift_ref[0], scale_ref[0]).astype(BF16)


def _normmod_call(x2, gain, mod3, mod_row, tm):
    n_tok = x2.shape[0]
    return pl.pallas_call(
        _normmod_kernel,
        grid=(n_tok // tm,),
        in_specs=[pl.BlockSpec((tm, D), lambda i: (i, 0)),
                  pl.BlockSpec((1, D), lambda i: (0, 0)),
                  pl.BlockSpec((1, 1, D), lambda i: (mod_row(i) * 6 + 0, 0, 0)),
                  pl.BlockSpec((1, 1, D), lambda i: (mod_row(i) * 6 + 1, 0, 0))],
        out_specs=pl.BlockSpec((tm, D), lambda i: (i, 0)),
        out_shape=jax.ShapeDtypeStruct((n_tok, D), BF16),
        compiler_params=_cp("arbitrary"),
        name="norm_mod",
    )(x2, gain.reshape(1, D), mod3, mod3)


def _inproj_kernel(h_ref, w_ref, o_ref):
    o_ref[...] = _dot(h_ref[...], w_ref[0]).astype(BF16)


def _inproj_call(h, w_p, layer, rwkv_kv_only=False):
    n_tok = h.shape[0]
    tm = min(2048, n_tok)
    n_rw = RW_PAD // TN_IN
    if rwkv_kv_only:
        ncols = RW_PAD + NP - C_AK
        col = lambda j: jnp.where(j < n_rw, j, j - n_rw + C_AK // TN_IN)
    else:
        ncols = NP
        col = lambda j: j
    return pl.pallas_call(
        _inproj_kernel,
        grid=(n_tok // tm, ncols // TN_IN),
        in_specs=[pl.BlockSpec((tm, D), lambda i, j: (i, 0)),
                  pl.BlockSpec((1, D, TN_IN), lambda i, j: (layer, 0, col(j)))],
        out_specs=pl.BlockSpec((tm, TN_IN), lambda i, j: (i, j)),
        out_shape=jax.ShapeDtypeStruct((n_tok, ncols), BF16),
        compiler_params=_cp("arbitrary", "arbitrary"),
        name="in_proj",
    )(h, w_p)


HALO = 16


def _shifted(cur, prev_blk, next_blk, first, last):
    tt = cur.shape[0]
    row = lax.broadcasted_iota(jnp.int32, cur.shape, 0)
    p_row = jnp.where(first, 0.0, prev_blk[HALO - 1:HALO, :].astype(F32))
    n_row = jnp.where(last, 0.0, next_blk[0:1, :].astype(F32))
    prev = jnp.where(row == 0, p_row, pltpu.roll(cur, 1, axis=0))
    nxt = jnp.where(row == tt - 1, n_row, pltpu.roll(cur, tt - 1, axis=0))
    return prev, nxt


def _halo_specs(tt, width, col_blk, n_tok):
    nb = n_tok // HALO
    r = tt // HALO
    return [pl.BlockSpec((tt, width), lambda i: (i, col_blk)),
            pl.BlockSpec((HALO, width), lambda i: (jnp.maximum(i * r - 1, 0), col_blk)),
            pl.BlockSpec((HALO, width), lambda i: (jnp.minimum((i + 1) * r, nb - 1), col_blk))]


def _streams_kernel(*refs, tiles_per_seq, has_vres):
    if has_vres:
        (p_ref, pp_ref, pn_ref, mu_ref, dup_ref, dbias_ref, iup_ref, ibias_ref, gup_ref, kk_ref_, e_ref,
         vd_ref, vu_ref, vb_ref, vf_ref,
         r_o, k_o, v_o, kk_o, g_o, a_o, lw_o) = refs
    else:
        (p_ref, pp_ref, pn_ref, mu_ref, dup_ref, dbias_ref, iup_ref, ibias_ref, gup_ref, kk_ref_, e_ref,
         r_o, k_o, v_o, kk_o, g_o, a_o, lw_o) = refs
    j = pl.program_id(0) % tiles_per_seq
    used = C_GD + GATE_RANK
    cur_b = p_ref[:, :used]
    tt = cur_b.shape[0]
    ext = jnp.concatenate([pp_ref[:, :used], cur_b, pn_ref[:, :used]], axis=0)
    t_i = lax.broadcasted_iota(jnp.int32, (tt, tt + 2 * HALO), 0)
    e_i = lax.broadcasted_iota(jnp.int32, (tt, tt + 2 * HALO), 1) - HALO
    tap = ((e_i == t_i - 1) & ((e_i >= 0) | (j > 0))) | ((e_i == t_i + 1) & ((e_i < tt) | (j < tiles_per_seq - 1)))
    avg = _dot(jnp.where(tap, 0.5, 0.0).astype(BF16), ext)
    cur = cur_b.astype(F32)
    ps = cur + mu_ref[:, :used] * (avg - cur)
    r = ps[:, C_R:C_R + BW]
    k = ps[:, C_K:C_K + BW]
    v = ps[:, C_V:C_V + BW]
    gd = ps[:, C_GD:C_GD + GATE_RANK]
    if has_vres:
        low = _dot(v.astype(BF16), vd_ref[...])
        mix = _sigmoid(vb_ref[...] + _dot(low.astype(BF16), vu_ref[...]))
        v = v + (vf_ref[...].astype(F32) - v) * mix
    for d in range(2):
        wd = ps[:, C_WD[d]:C_WD[d] + 128]
        ad = ps[:, C_AD[d]:C_AD[d] + 128]
        w_logit = dbias_ref[d] + _dot(jnp.tanh(wd).astype(BF16), dup_ref[d])
        lw_o[d] = -jnp.exp(-0.5) * _sigmoid(w_logit)
        a_o[d] = _sigmoid(ibias_ref[d] + _dot(ad.astype(BF16), iup_ref[d])).astype(BF16)
    kh = k * kk_ref_[...]
    ss = _seg_sum(kh * kh, e_ref)
    kk = kh * lax.rsqrt(jnp.maximum(ss, 1e-24))
    r_o[...] = r.astype(BF16)
    k_o[...] = k.astype(BF16)
    v_o[...] = v.astype(BF16)
    kk_o[...] = kk.astype(BF16)
    g_o[...] = _dot(_sigmoid(gd).astype(BF16), gup_ref[...]).astype(BF16)


def _streams_call(p, seq_len, lp, v_first):
    n_tok = p.shape[0]
    tt = 256
    has_vres = v_first is not None
    full = lambda *s: pl.BlockSpec(s, lambda i: (0,) * len(s))
    tok = pl.BlockSpec((tt, BW), lambda i: (i, 0))
    tok2 = pl.BlockSpec((2, tt, BW), lambda i: (0, i, 0))
    in_specs = _halo_specs(tt, C_GD + GATE_RANK, 0, n_tok) + [
        full(1, RW_PAD), full(2, 128, BW), full(2, 1, BW), full(2, 128, BW), full(2, 1, BW),
        full(GATE_RANK, BW), full(1, BW), full(BW, 128)]
    args = [p, p, p, lp["mu"], lp["decay_up"], lp["decay_bias"], lp["iclr_up"], lp["iclr_bias"],
            lp["gate_up"], lp["k_k"], lp["eblk"]]
    if has_vres:
        in_specs += [full(BW, 128), full(128, BW), full(1, BW), tok]
        args += [lp["vres_down"], lp["vres_up"], lp["vres_bias"], v_first]
    sd = lambda dt: jax.ShapeDtypeStruct((n_tok, BW), dt)
    sd2 = lambda dt: jax.ShapeDtypeStruct((2, n_tok, BW), dt)
    return pl.pallas_call(
        functools.partial(_streams_kernel, tiles_per_seq=seq_len // tt, has_vres=has_vres),
        grid=(n_tok // tt,),
        in_specs=in_specs,
        out_specs=[tok, tok, tok, tok, tok, tok2, tok2],
        out_shape=[sd(BF16), sd(BF16), sd(BF16), sd(BF16), sd(BF16), sd2(BF16), sd2(F32)],
        compiler_params=_cp("arbitrary"),
        name="rwkv_streams",
    )(*args)


def _scan_prologue(d, r, k, v, kk, a, lw, ka):
    c = CHUNK
    ri = lax.broadcasted_iota(jnp.int32, (c, 2 * c), 0)
    ci = lax.broadcasted_iota(jnp.int32, (c, 2 * c), 1) % c
    diff = (ci - ri) if d else (ri - ci)
    incl = diff >= 0
    strict = diff > 0
    tri = incl[:, :c].astype(BF16)
    lw_hi = lw.astype(BF16)
    lw_lo = (lw - lw_hi.astype(F32)).astype(BF16)
    b = _dot(tri, lw_hi) + _dot(tri, lw_lo)
    b_tot = b[0:1, :] if d else b[c - 1:c, :]
    a = a.astype(F32)
    kk = kk.astype(F32)
    kd = k.astype(F32) * (1.0 + (a - 1.0) * ka)
    kka = kk * a
    enb = jnp.exp(-b)
    etail = jnp.exp(b_tot - b)
    return dict(
        incl=incl, strict=strict,
        rt=(r.astype(F32) * jnp.exp(b)).astype(BF16),
        bt=(-kk * jnp.exp(b - lw)).astype(BF16),
        kt=(kd * enb).astype(BF16), at=(kka * enb).astype(BF16),
        kh=(kd * etail).astype(BF16), ah=(kka * etail).astype(BF16),
        vv=v, e_tot=jnp.exp(b_tot))


def _scan_kernel(*refs, with_output, sb):
    ins = refs[:14]
    if with_output:
        yf_ref, yb_ref, sout_ref, st_ref = refs[14:]
    else:
        sout_ref, st_ref = refs[14:]
    ka_ref, s0_ref = ins[12], ins[13]
    s = pl.program_id(1)
    c = CHUNK

    @pl.when(s == 0)
    def _():
        st_ref[...] = s0_ref[...]

    streams = [(d, j) for d in range(2) for j in range(sb)]
    ka = ka_ref[...]
    pro = [_scan_prologue(d, *[ref[j] for ref in ins[6 * d:6 * d + 4]], ins[6 * d + 4][0, j], ins[6 * d + 5][0, j], ka)
           for d, j in streams]
    st_all = [st_ref[d, j] for d, j in streams]

    npair = NH // 2
    units = [(si, slice(p * 2 * HD, (p + 1) * 2 * HD)) for si in range(len(streams)) for p in range(npair)]
    us = range(len(units))
    col = lambda name: [pro[d][name][:, sl] for d, sl in units]
    even1 = lax.broadcasted_iota(jnp.int32, (c, 2 * HD), 1) < HD

    def bd(z):
        zero = jnp.zeros_like(z)
        return jnp.concatenate([jnp.where(even1, z, zero), jnp.where(even1, zero, z)], axis=0)

    strict = [pro[d]["strict"] for d, _ in units]
    incl = [pro[d]["incl"] for d, _ in units]
    st = [st_all[d][:, sl] for d, sl in units]
    vbd = [bd(z) for z in col("vv")]
    rb = [jnp.concatenate(p, axis=0) for p in zip(col("rt"), col("bt"))]
    gk = [_dot_nt(rb[u], bd(z)) for u, z in zip(us, col("kt"))]
    ga = [_dot_nt(rb[u], bd(z)) for u, z in zip(us, col("at"))]
    rbs = [_dot_nt(rb[u], bd(st[u].astype(BF16))) for u in us]
    x = [jnp.where(strict[u], ga[u][c:], 0.0).astype(BF16) for u in us]
    a_bk = [jnp.where(strict[u], gk[u][c:], 0.0) for u in us]
    if with_output:
        a_rk = [jnp.where(incl[u], gk[u][:c], 0.0) for u in us]
        a_ra = [jnp.where(incl[u], ga[u][:c], 0.0).astype(BF16) for u in us]
        av = [_dot(jnp.concatenate([a_rk[u], a_bk[u]], axis=0).astype(BF16), vbd[u]) for u in us]
        w = [rbs[u][c:] + av[u][c:] for u in us]
    else:
        w = [rbs[u][c:] + _dot(a_bk[u].astype(BF16), vbd[u]) for u in us]
    for rnd in range(6):
        if rnd < 5:
            m = [_dot(x[u], jnp.concatenate([bd(x[u]), bd(w[u].astype(BF16))], axis=1)) for u in us]
            x = [m[u][:, :2 * c].astype(BF16) for u in us]
            w = [w[u] + m[u][:, 2 * c:] for u in us]
        else:
            w = [w[u] + _dot(x[u], bd(w[u].astype(BF16))) for u in us]
    ub = [w[u].astype(BF16) for u in us]
    if with_output:
        ys = [rbs[u][:c] + av[u][:c] + _dot(a_ra[u], bd(ub[u])) for u in us]
        for si, (d, j) in enumerate(streams):
            (yb_ref if d else yf_ref)[j] = jnp.concatenate(ys[si * npair:(si + 1) * npair], axis=1)
    full = [_dot_tn(jnp.concatenate([z, ub[u]], axis=0), jnp.concatenate(p, axis=0))
            for u, z, p in zip(us, col("vv"), zip(col("kh"), col("ah")))]
    upd = [jnp.where(even1, f[:HD], f[HD:]) for f in full]
    for si, (d, j) in enumerate(streams):
        st_ref[d, j] = st_all[si] * pro[si]["e_tot"] + jnp.concatenate(upd[si * npair:(si + 1) * npair], axis=1)

    @pl.when(s == pl.num_programs(1) - 1)
    def _():
        sout_ref[...] = st_ref[...]


def _scan_call(streams, k_a, s0, bsz, seq_len, with_output):
    n_tok = streams[0].shape[0]
    nc = seq_len // CHUNK
    sb = 2 if bsz % 2 == 0 else 1
    r, k, v, kk = (z.reshape(bsz, seq_len, BW) for z in streams[:4])
    a2, lw2 = (z.reshape(2, bsz, seq_len, BW) for z in streams[5:])

    chunk = (lambda s: s, lambda s: nc - 1 - s)
    tok = [pl.BlockSpec((sb, CHUNK, BW), lambda b, s, f=f: (b, f(s), 0)) for f in chunk]
    tok2 = [pl.BlockSpec((1, sb, CHUNK, BW), lambda b, s, f=f, d=d: (d, b, f(s), 0)) for d, f in enumerate(chunk)]
    st_spec = pl.BlockSpec((2, sb, HD, BW), lambda b, s: (0, b, 0, 0))
    in_specs, args = [], []
    for d in range(2):
        in_specs += [tok[d]] * 4 + [tok2[d]] * 2
        args += [r, k, v, kk, a2, lw2]
    in_specs += [pl.BlockSpec((1, BW), lambda b, s: (0, 0)), st_spec]
    out_specs = [st_spec]
    out_shape = [jax.ShapeDtypeStruct((2, bsz, HD, BW), F32)]
    if with_output:
        out_specs = tok + out_specs
        out_shape = [jax.ShapeDtypeStruct((bsz, seq_len, BW), F32)] * 2 + out_shape
    res = pl.pallas_call(
        functools.partial(_scan_kernel, with_output=with_output, sb=sb),
        grid=(bsz // sb, nc),
        in_specs=in_specs,
        out_specs=out_specs,
        out_shape=out_shape,
        scratch_shapes=[pltpu.VMEM((2, sb, HD, BW), F32)],
        compiler_params=_cp("arbitrary", "arbitrary"),
        name="rwkv_scan",
    )(*args, k_a, s0)
    if with_output:
        return res[2], (res[0].reshape(n_tok, BW), res[1].reshape(n_tok, BW))
    return res[0], None


def _rwkv_out_kernel(yf_ref, yb_ref, r_ref, k_ref, v_ref, a_ref, g_ref, ka_ref, rk_ref, gnw_ref, gnb_ref, e_ref,
                     o_ref):
    y = yf_ref[...] + yb_ref[...]
    mean = _seg_sum(y, e_ref) * (1.0 / HD)
    yc = y - mean
    var = _seg_sum(yc * yc, e_ref) * (1.0 / HD)
    yn = yc * lax.rsqrt(var + GN_EPS) * gnw_ref[...] + gnb_ref[...]
    r = r_ref[...].astype(F32)
    k = k_ref[...].astype(F32)
    asum = a_ref[0].astype(F32) + a_ref[1].astype(F32)
    kd_sum = k * (2.0 + (asum - 2.0) * ka_ref[...])
    bonus = _seg_sum(r * kd_sum * rk_ref[...], e_ref) * v_ref[...].astype(F32)
    o_ref[...] = ((yn + bonus) * g_ref[...].astype(F32)).astype(BF16)


def _rwkv_out_call(y2, streams, lp):
    r, k, v, _, g, a2, _ = streams
    n_tok = r.shape[0]
    tt = 256
    tok = pl.BlockSpec((tt, BW), lambda i: (i, 0))
    tok2 = pl.BlockSpec((2, tt, BW), lambda i: (0, i, 0))
    vec = pl.BlockSpec((1, BW), lambda i: (0, 0))
    return pl.pallas_call(
        _rwkv_out_kernel,
        grid=(n_tok // tt,),
        in_specs=[tok, tok, tok, tok, tok, tok2, tok, vec, vec, vec, vec, pl.BlockSpec((BW, 128), lambda i: (0, 0))],
        out_specs=tok,
        out_shape=jax.ShapeDtypeStruct((n_tok, BW), BF16),
        compiler_params=_cp("arbitrary"),
        name="rwkv_out",
    )(y2[0], y2[1], r, k, v, a2, g, lp["k_a"], lp["r_k"], lp["gn_w"], lp["gn_b"], lp["eblk"])


def _conv_kernel(b_ref, c_ref, cp_ref, cn_ref, u_ref, up_ref, un_ref, w_ref, o_ref, *, tiles_per_seq):
    j = pl.program_id(0) % tiles_per_seq
    cu = c_ref[...].astype(F32) * u_ref[...].astype(F32)
    cu_p = cp_ref[...].astype(F32) * up_ref[...].astype(F32)
    cu_n = cn_ref[...].astype(F32) * un_ref[...].astype(F32)
    prev, nxt = _shifted(cu, cu_p, cu_n, j == 0, j == tiles_per_seq - 1)
    w = w_ref[...]
    conv = w[0:1] * prev + w[1:2] * cu + w[2:3] * nxt
    o_ref[...] = (b_ref[...].astype(F32) * conv).astype(BF16)


def _conv_call(p, seq_len, conv_w):
    n_tok = p.shape[0]
    tt = 256
    cb = C_CONV // BW
    return pl.pallas_call(
        functools.partial(_conv_kernel, tiles_per_seq=seq_len // tt),
        grid=(n_tok // tt,),
        in_specs=[pl.BlockSpec((tt, BW), lambda i: (i, cb))] + _halo_specs(tt, BW, cb + 1, n_tok)
        + _halo_specs(tt, BW, cb + 2, n_tok) + [pl.BlockSpec((3, BW), lambda i: (0, 0))],
        out_specs=pl.BlockSpec((tt, BW), lambda i: (i, 0)),
        out_shape=jax.ShapeDtypeStruct((n_tok, BW), BF16),
        compiler_params=_cp("arbitrary"),
        name="short_conv",
    )(p, p, p, p, p, p, p, conv_w)


def _rope_all(blocks, cos, sin):
    ri = lax.broadcasted_iota(jnp.int32, (LANE, LANE), 0)
    ci = lax.broadcasted_iota(jnp.int32, (LANE, LANE), 1)
    perm = (ri == jnp.where((ci % 32) < 16, ci + 16, ci - 16)).astype(BF16)
    slices = [b[:, j:j + LANE] for b in blocks for j in range(0, b.shape[1], LANE)]
    moved = _dot(jnp.concatenate(slices, axis=0), perm)
    outs, row = [], 0
    for b, c, s in zip(blocks, cos, sin):
        cols = []
        for j in range(0, b.shape[1], LANE):
            cols.append(moved[row:row + b.shape[0]])
            row += b.shape[0]
        outs.append(b.astype(F32) * c + jnp.concatenate(cols, axis=1) * s)
    return outs


LOG2E = 1.4426950408889634
Q_SCALE = HD ** -0.5 * LOG2E


def _softmax_pv(s2, sink2_col, v):
    m = jnp.maximum(jnp.max(s2, axis=-1, keepdims=True), sink2_col)
    p = jnp.exp2(s2 - m)
    den = jnp.sum(p, axis=-1, keepdims=True) + jnp.exp2(sink2_col - m)
    return _dot(p.astype(BF16), v) / den


def _attn_lat_kernel(q_ref, kp_ref, kc_ref, kn_ref, vp_ref, vc_ref, vn_ref, kx_ref, vx_ref,
                     cos_ref, cosp_ref, cosn_ref, sin_ref, sinp_ref, sinn_ref, sink_ref, band_ref, o_ref, *, nblk):
    n = pl.program_id(1)
    kvw = KVH * HD
    q, kp, kc, kn = _rope_all(
        [q_ref[...], kp_ref[...], kc_ref[...], kn_ref[...]],
        [cos_ref[...], cosp_ref[:, :kvw], cos_ref[:, :kvw], cosn_ref[:, :kvw]],
        [sin_ref[...], sinp_ref[:, :kvw], sin_ref[:, :kvw], sinn_ref[:, :kvw]])
    q = (q * Q_SCALE).astype(BF16)
    k_all = jnp.concatenate([kp.astype(BF16), kc.astype(BF16), kn.astype(BF16), kx_ref[...]], axis=0)
    v_all = jnp.concatenate([vp_ref[...], vc_ref[...], vn_ref[...], vx_ref[...]], axis=0)
    bias_p = jnp.where(n > 0, band_ref[0], NEG_INF)
    bias_n = jnp.where(n < nblk - 1, band_ref[1], NEG_INF)
    sink = sink_ref[...] * LOG2E
    gs = range(KVH)
    even = lax.broadcasted_iota(jnp.int32, (BLK, 2 * HD), 1) < HD
    zero = jnp.zeros((BLK, 2 * HD), BF16)
    eye = (lax.broadcasted_iota(jnp.int32, (2 * HD, 2 * HD), 0)
           == lax.broadcasted_iota(jnp.int32, (2 * HD, 2 * HD), 1)).astype(BF16)

    def head_rows(gi):
        parts = []
        for t in range(GQ // 2):
            qp = q[:, (gi * GQ + 2 * t) * HD:(gi * GQ + 2 * t + 2) * HD]
            parts += [jnp.where(even, qp, zero), jnp.where(even, zero, qp)]
        return jnp.concatenate(parts, axis=0)

    qg = [head_rows(gi) for gi in gs]
    kd = [jnp.concatenate([k_all[:, gi * HD:(gi + 1) * HD]] * 2, axis=1) for gi in gs]
    vt = [_dot_nt(eye[:HD, :HD], v_all[:, gi * HD:(gi + 1) * HD]).astype(BF16) for gi in gs]
    sk = [jnp.concatenate([jnp.broadcast_to(sink[:, gi * GQ + t:gi * GQ + t + 1], (1, BLK))
                           for t in range(GQ)], axis=1) for gi in gs]
    s = [_dot_nt(kd[gi], qg[gi]) for gi in gs]
    s = [jnp.concatenate([z[:BLK] + bias_p, z[BLK:2 * BLK], z[2 * BLK:3 * BLK] + bias_n, z[3 * BLK:]], axis=0)
         for z in s]
    m = [jnp.maximum(jnp.max(s[gi], axis=0, keepdims=True), sk[gi]) for gi in gs]
    p = [jnp.exp2(s[gi] - m[gi]).astype(BF16) for gi in gs]
    ones = jnp.ones((16, vt[0].shape[1]), BF16)
    pv = [_dot(jnp.concatenate([vt[gi], ones], axis=0), p[gi]) for gi in gs]
    og = [(pv[gi][:HD] / (pv[gi][HD:HD + 1] + jnp.exp2(sk[gi] - m[gi]))).astype(BF16) for gi in gs]
    o_ref[...] = jnp.concatenate(
        [_dot_nt(eye, jnp.concatenate([og[gi][:, 2 * t * BLK:(2 * t + 1) * BLK],
                                       og[gi][:, (2 * t + 1) * BLK:(2 * t + 2) * BLK]], axis=0))
         for gi in gs for t in range(GQ // 2)], axis=1).astype(BF16)


def _attn_lat_call(p_l, p_c, kv_blk_c, cos, sin, sink, bsz, seq_len, ctx_len):
    nblk = seq_len // BLK
    kvw = KVH * HD
    kb, vb = C_AK // kvw, C_AV // kvw
    rowq = lambda b, n: b * nblk + n
    rowp = lambda b, n: b * nblk + jnp.maximum(n - 1, 0)
    rown = lambda b, n: b * nblk + jnp.minimum(n + 1, nblk - 1)
    tabp = lambda b, n: (jnp.maximum(n - 1, 0), 0)
    tabn = lambda b, n: (jnp.minimum(n + 1, nblk - 1), 0)
    kv = lambda rf, cb: pl.BlockSpec((BLK, kvw), lambda b, n: (rf(b, n), cb))
    tab = lambda f: pl.BlockSpec((BLK, BW), f)
    kidx = jnp.arange(BLK)[:, None]
    qpos = jnp.arange(GQ * BLK)[None, :] % BLK
    band = jnp.where(jnp.stack([kidx >= qpos, kidx <= qpos]), 0.0, NEG_INF).astype(F32)
    return pl.pallas_call(
        functools.partial(_attn_lat_kernel, nblk=nblk),
        grid=(bsz, nblk),
        in_specs=[pl.BlockSpec((BLK, BW), lambda b, n: (rowq(b, n), C_Q // BW)),
                  kv(rowp, kb), kv(rowq, kb), kv(rown, kb), kv(rowp, vb), kv(rowq, vb), kv(rown, vb),
                  pl.BlockSpec((ctx_len, kvw), lambda b, n: (b, kv_blk_c)),
                  pl.BlockSpec((ctx_len, kvw), lambda b, n: (b, kv_blk_c + 1)),
                  tab(lambda b, n: (n, 0)), tab(tabp), tab(tabn),
                  tab(lambda b, n: (n, 0)), tab(tabp), tab(tabn),
                  pl.BlockSpec((1, NH), lambda b, n: (0, 0)),
                  pl.BlockSpec((2, BLK, GQ * BLK), lambda b, n: (0, 0, 0))],
        out_specs=pl.BlockSpec((BLK, BW), lambda b, n: (rowq(b, n), 0)),
        out_shape=jax.ShapeDtypeStruct((bsz * seq_len, BW), BF16),
        compiler_params=_cp("arbitrary", "arbitrary"),
        name="attn_latent",
    )(p_l, p_l, p_l, p_l, p_l, p_l, p_l, p_c, p_c, cos, cos, cos, sin, sin, sin, sink, band)


def _attn_ctx_kernel(q_ref, kx_ref, vx_ref, sink_ref, o_ref):
    q = (q_ref[...].astype(F32) * Q_SCALE).astype(BF16)
    kx = kx_ref[...]
    vx = vx_ref[...]
    sink = sink_ref[...] * LOG2E
    tq = q.shape[0]
    outs = []
    for gi in range(KVH):
        qg = jnp.concatenate([q[:, (gi * GQ + t) * HD:(gi * GQ + t + 1) * HD] for t in range(GQ)], axis=0)
        sk = jnp.concatenate([jnp.broadcast_to(sink[:, gi * GQ + t:gi * GQ + t + 1], (tq, 1))
                              for t in range(GQ)], axis=0)
        s = _dot_nt(qg, kx[:, gi * HD:(gi + 1) * HD])
        og = _softmax_pv(s, sk, vx[:, gi * HD:(gi + 1) * HD])
        outs += [og[t * tq:(t + 1) * tq] for t in range(GQ)]
    o_ref[...] = jnp.concatenate(outs, axis=1).astype(BF16)


def _attn_ctx_call(p_c, sink, bsz, ctx_len):
    kvw = KVH * HD
    tq = 128
    nq = ctx_len // tq
    return pl.pallas_call(
        _attn_ctx_kernel,
        grid=(bsz, nq),
        in_specs=[pl.BlockSpec((tq, BW), lambda b, n: (b * nq + n, C_Q // BW)),
                  pl.BlockSpec((ctx_len, kvw), lambda b, n: (b, C_AK // kvw)),
                  pl.BlockSpec((ctx_len, kvw), lambda b, n: (b, C_AV // kvw)),
                  pl.BlockSpec((1, NH), lambda b, n: (0, 0))],
        out_specs=pl.BlockSpec((tq, BW), lambda b, n: (b * nq + n, 0)),
        out_shape=jax.ShapeDtypeStruct((bsz * ctx_len, BW), BF16),
        compiler_params=_cp("arbitrary", "arbitrary"),
        name="attn_context",
    )(p_c, p_c, p_c, sink)


def _merge_kernel(b0_ref, b1_ref, b2_ref, g0_ref, g1_ref, g2_ref, w_ref, o_ref):
    acc = _sigmoid(g0_ref[...].astype(F32)) * _dot(b0_ref[...], w_ref[0])
    acc += _sigmoid(g1_ref[...].astype(F32)) * _dot(b1_ref[...], w_ref[1])
    acc += _sigmoid(g2_ref[...].astype(F32)) * _dot(b2_ref[...], w_ref[2])
    o_ref[...] = acc.astype(BF16)


def _merge_call(br_rwkv, br_conv, br_attn, p, w_branch):
    n_tok = p.shape[0]
    tm, tn = 512, 1024
    nn = D // tn
    br = pl.BlockSpec((tm, BW), lambda j, i: (i, 0))
    gate = lambda t: pl.BlockSpec((tm, tn), lambda j, i: (i, (C_GATE + t * D) // tn + j))
    return pl.pallas_call(
        _merge_kernel,
        grid=(nn, n_tok // tm),
        in_specs=[br, br, br, gate(0), gate(1), gate(2), pl.BlockSpec((3, BW, tn), lambda j, i: (0, 0, j))],
        out_specs=pl.BlockSpec((tm, tn), lambda j, i: (i, j)),
        out_shape=jax.ShapeDtypeStruct((n_tok, D), BF16),
        compiler_params=_cp("arbitrary", "arbitrary"),
        name="merge_branches",
    )(br_rwkv, br_conv, br_attn, p, p, p, w_branch)


def _outproj_kernel(m_ref, w_ref, x_ref, gate_ref, o_ref):
    o_ref[...] = x_ref[...] + gate_ref[0] * _dot(m_ref[...], w_ref[...])


def _outproj_call(m, w_out, x2, mod3, mod_row, tm):
    n_tok = x2.shape[0]
    return pl.pallas_call(
        _outproj_kernel,
        grid=(n_tok // tm,),
        in_specs=[pl.BlockSpec((tm, D), lambda i: (i, 0)),
                  pl.BlockSpec((D, D), lambda i: (0, 0)),
                  pl.BlockSpec((tm, D), lambda i: (i, 0)),
                  pl.BlockSpec((1, 1, D), lambda i: (mod_row(i) * 6 + 2, 0, 0))],
        out_specs=pl.BlockSpec((tm, D), lambda i: (i, 0)),
        out_shape=jax.ShapeDtypeStruct((n_tok, D), F32),
        compiler_params=_cp("arbitrary"),
        name="out_proj",
    )(m, w_out, x2, mod3)


def _ffn_prep_kernel(x_ref, gain_ref, shift_ref, scale_ref, wr_ref, h_ref, aff_ref):
    h = _norm_mod(x_ref[...], gain_ref[...], shift_ref[0], scale_ref[0])
    h_hi = h.astype(BF16)
    h_ref[...] = h_hi
    h_lo = (h - h_hi.astype(F32)).astype(BF16)
    wr = wr_ref[...]
    w_hi = wr.astype(BF16)
    w_lo = (wr - w_hi.astype(F32)).astype(BF16)
    logits = _dot(h_hi, w_hi) + (_dot(h_hi, w_lo) + _dot(h_lo, w_hi))
    lane = lax.broadcasted_iota(jnp.int32, logits.shape, 1)
    logits = jnp.where(lane < N_EXPERTS, logits, NEG_INF)
    m = jnp.max(logits, axis=-1, keepdims=True)
    e = jnp.exp(logits - m)
    aff_ref[...] = e / jnp.sum(e, axis=-1, keepdims=True)


def _ffn_prep_call(x2, gain, mod3, mod_row, wr_pad, tm):
    n_tok = x2.shape[0]
    return pl.pallas_call(
        _ffn_prep_kernel,
        grid=(n_tok // tm,),
        in_specs=[pl.BlockSpec((tm, D), lambda i: (i, 0)),
                  pl.BlockSpec((1, D), lambda i: (0, 0)),
                  pl.BlockSpec((1, 1, D), lambda i: (mod_row(i) * 6 + 3, 0, 0)),
                  pl.BlockSpec((1, 1, D), lambda i: (mod_row(i) * 6 + 4, 0, 0)),
                  pl.BlockSpec((D, 128), lambda i: (0, 0))],
        out_specs=[pl.BlockSpec((tm, D), lambda i: (i, 0)), pl.BlockSpec((tm, 128), lambda i: (i, 0))],
        out_shape=[jax.ShapeDtypeStruct((n_tok, D), BF16), jax.ShapeDtypeStruct((n_tok, 128), F32)],
        compiler_params=_cp("arbitrary"),
        name="ffn_prep",
    )(x2, gain.reshape(1, D), mod3, mod3, wr_pad)


def _select_kernel(aff_ref, slot_ref, *, cap):
    a = aff_ref[0]
    n = a.shape[1]
    bits = lax.bitcast_convert_type(a, jnp.int32)

    def body(i, t):
        cand = t | jnp.left_shift(jnp.int32(1), 30 - i)
        cnt = jnp.sum((bits >= cand).astype(jnp.int32), axis=-1, keepdims=True)
        return jnp.where(cnt >= cap, cand, t)

    thr = lax.fori_loop(0, 31, body, jnp.zeros((a.shape[0], 1), jnp.int32))
    gt = bits > thr
    eq = bits == thr
    n_gt = jnp.sum(gt.astype(jnp.int32), axis=-1, keepdims=True)
    tc = min(n, 512)

    def prefix(mask_bf16):
        cols = []
        for j0 in range(0, n, tc):
            ri = lax.broadcasted_iota(jnp.int32, (n, tc), 0)
            ci = lax.broadcasted_iota(jnp.int32, (n, tc), 1) + j0
            cols.append(_dot(mask_bf16, (ri <= ci).astype(BF16)))
        return jnp.concatenate(cols, axis=1)

    eq_f = eq.astype(BF16)
    excl_eq = prefix(eq_f) - eq_f.astype(F32)
    sel = gt | (eq & (excl_eq < (cap - n_gt).astype(F32)))
    pos = prefix(sel.astype(BF16))
    slot_ref[0] = jnp.where(sel, pos.astype(jnp.int32) - 1, -1)


def _select_call(aff_t, cap):
    bsz, ne, n = aff_t.shape
    return pl.pallas_call(
        functools.partial(_select_kernel, cap=cap),
        grid=(bsz,),
        in_specs=[pl.BlockSpec((1, ne, n), lambda b: (b, 0, 0))],
        out_specs=pl.BlockSpec((1, ne, n), lambda b: (b, 0, 0)),
        out_shape=jax.ShapeDtypeStruct((bsz, ne, n), jnp.int32),
        compiler_params=_cp("arbitrary"),
        name="expert_select",
    )(aff_t)


def _gather_kernel(slot_ref, aff_ref, h_ref, xs_ref, g_ref, *, cap):
    slot = slot_ref[0]
    n = slot.shape[1]
    onehot = lax.broadcasted_iota(jnp.int32, (cap, n), 0) == slot
    xs_ref[0] = _dot(onehot.astype(BF16), h_ref[...]).astype(BF16)
    g_ref[0] = jnp.sum(jnp.where(onehot, aff_ref[0], 0.0), axis=-1, keepdims=True)


def _gather_call(slot, aff_t, h2, cap):
    bsz, ne, n = slot.shape
    return pl.pallas_call(
        functools.partial(_gather_kernel, cap=cap),
        grid=(bsz, ne),
        in_specs=[pl.BlockSpec((1, 1, n), lambda b, e: (b * ne + e, 0, 0)),
                  pl.BlockSpec((1, 1, n), lambda b, e: (b * ne + e, 0, 0)),
                  pl.BlockSpec((n, D), lambda b, e: (b, 0))],
        out_specs=[pl.BlockSpec((1, cap, D), lambda b, e: (e, b, 0)),
                   pl.BlockSpec((1, cap, 1), lambda b, e: (e, b, 0))],
        out_shape=[jax.ShapeDtypeStruct((ne, bsz * cap, D), BF16),
                   jax.ShapeDtypeStruct((ne, bsz * cap, 1), F32)],
        compiler_params=_cp("arbitrary", "arbitrary"),
        name="expert_gather",
    )(slot.reshape(bsz * ne, 1, n), aff_t.reshape(bsz * ne, 1, n), h2)


def _expert_kernel(x_ref, wg_ref, wu_ref, wd_ref, g_ref, o_ref, acc_ref):
    f = pl.program_id(2)

    @pl.when(f == 0)
    def _():
        acc_ref[...] = jnp.zeros_like(acc_ref)

    x = x_ref[0]
    gate = _dot(x, wg_ref[0, 0].astype(BF16))
    hid = gate * _sigmoid(gate) * _dot(x, wu_ref[0, 0].astype(BF16))
    acc_ref[...] += _dot(hid.astype(BF16), wd_ref[0, 0].astype(BF16))

    @pl.when(f == pl.num_programs(2) - 1)
    def _():
        o_ref[0] = (acc_ref[...] * g_ref[0]).astype(BF16)


def _expert_call(xs, g, wg, wu, wd, layer):
    ne, rows, _ = xs.shape
    tm = min(rows, 1024)
    tf = 256
    return pl.pallas_call(
        _expert_kernel,
        grid=(ne, rows // tm, EXPERT_FF // tf),
        in_specs=[pl.BlockSpec((1, tm, D), lambda e, i, f: (e, i, 0)),
                  pl.BlockSpec((1, 1, D, tf), lambda e, i, f: (layer, e, 0, f)),
                  pl.BlockSpec((1, 1, D, tf), lambda e, i, f: (layer, e, 0, f)),
                  pl.BlockSpec((1, 1, tf, D), lambda e, i, f: (layer, e, f, 0)),
                  pl.BlockSpec((1, tm, 1), lambda e, i, f: (e, i, 0))],
        out_specs=pl.BlockSpec((1, tm, D), lambda e, i, f: (e, i, 0)),
        out_shape=jax.ShapeDtypeStruct((ne, rows, D), BF16),
        scratch_shapes=[pltpu.VMEM((tm, D), F32)],
        compiler_params=_cp("arbitrary", "arbitrary", "arbitrary"),
        name="expert_mlp",
    )(xs, wg, wu, wd, g)


def _scatter_kernel(*refs, cap, last):
    if last:
        slot_ref, ys_ref, x_ref, gate_ref, gain_ref, o_ref = refs
    else:
        slot_ref, ys_ref, x_ref, gate_ref, gain_ref, shift_ref, scale_ref, o_ref, h_ref = refs
    slot_t = slot_ref[0]
    tq = slot_t.shape[0]
    lane = lax.broadcasted_iota(jnp.int32, (tq, cap), 1)
    acc = jnp.zeros(x_ref.shape, F32)
    for e in range(N_EXPERTS):
        onehot = (slot_t[:, e:e + 1] == lane).astype(BF16)
        acc += _dot(onehot, ys_ref[e])
    x = x_ref[...] + gate_ref[0] * acc
    if last:
        ms = jnp.mean(x * x, axis=-1, keepdims=True)
        o_ref[...] = (x * lax.rsqrt(ms + NORM_EPS)) * gain_ref[...]
    else:
        o_ref[...] = x
        h_ref[...] = _norm_mod(x, gain_ref[...], shift_ref[0], scale_ref[0]).astype(BF16)


def _scatter_call(slot_t, ys, x2, mod3, mod_row_b, cap, gain, next_mod3=None):
    bsz, n, ne = slot_t.shape
    tq = min(n, 256)
    nt = n // tq
    last = next_mod3 is None
    row = pl.BlockSpec((tq, D), lambda b, t: (b * nt + t, 0))
    mod = lambda c: pl.BlockSpec((1, 1, D), lambda b, t: (mod_row_b(b) * 6 + c, 0, 0))
    in_specs = [pl.BlockSpec((1, tq, ne), lambda b, t: (b, t, 0)),
                pl.BlockSpec((ne, cap, D), lambda b, t: (0, b, 0)),
                row, mod(5), pl.BlockSpec((1, D), lambda b, t: (0, 0))]
    args = [slot_t, ys, x2, mod3, gain.reshape(1, D)]
    out_specs, out_shape = row, jax.ShapeDtypeStruct(x2.shape, F32)
    if not last:
        in_specs += [mod(0), mod(1)]
        args += [next_mod3, next_mod3]
        out_specs, out_shape = [row, row], [out_shape, jax.ShapeDtypeStruct(x2.shape, BF16)]
    return pl.pallas_call(
        functools.partial(_scatter_kernel, cap=cap, last=last),
        grid=(bsz, nt),
        in_specs=in_specs,
        out_specs=out_specs,
        out_shape=out_shape,
        compiler_params=_cp("arbitrary", "arbitrary"),
        name="expert_scatter",
    )(*args)


def _ffn(x2, bsz, n, lp, mod3, mod_row, mod_row_b, tm, gain, next_mod3=None):
    cap = CAPACITY_FACTOR * n // N_EXPERTS
    h2, aff = _ffn_prep_call(x2, lp["norm_ffn"], mod3, mod_row, lp["w_router"], tm)
    aff_t = jnp.swapaxes(aff[:, :N_EXPERTS].reshape(bsz, n, N_EXPERTS), 1, 2)
    slot = _select_call(aff_t, cap)
    xs, g = _gather_call(slot, aff_t, h2, cap)
    ys = _expert_call(xs, g, lp["w_exp_gate"], lp["w_exp_up"], lp["w_exp_down"], lp["layer"])
    return _scatter_call(jnp.swapaxes(slot, 1, 2), ys, x2, mod3, mod_row_b, cap, gain, next_mod3)


def _pad_cols(parts):
    out = []
    for a, width in parts:
        out.append(a)
        if a.shape[-1] < width:
            out.append(jnp.zeros(a.shape[:-1] + (width - a.shape[-1],), a.dtype))
    return jnp.concatenate(out, axis=-1)


def _rwkv_col_parts(w):
    o = 3 * BW
    parts = [(w[..., :o], o)]
    for _ in range(2):
        parts.append((w[..., o:o + DECAY_RANK], 128))
        o += DECAY_RANK
    for _ in range(2):
        parts.append((w[..., o:o + ICLR_RANK], 128))
        o += ICLR_RANK
    parts.append((w[..., o:o + GATE_RANK], RW_PAD - C_GD))
    return parts, o + GATE_RANK


RW_END = 3 * BW + 2 * DECAY_RANK + 2 * ICLR_RANK + GATE_RANK
CODE_W = RW_PAD - 3 * BW


def _wrelayout_kernel(w_ref, misc_ref, o_ref):
    j = pl.program_id(1)
    is_code = (j >= 3 * BW // LANE) & (j < RW_PAD // LANE)
    o_ref[0] = jnp.where(is_code, misc_ref[0], w_ref[0]).astype(BF16)


def _wrelayout_call(w_in):
    depth = w_in.shape[0]
    code_parts, _ = _rwkv_col_parts(w_in[:, :, :RW_END])
    misc = _pad_cols(code_parts[1:])
    nb = NP // LANE
    kv_src = RW_END // LANE

    def src(j):
        shifted = jnp.where(j >= C_AK // LANE, j - C_AK // LANE + kv_src, j + (RW_END + 2 * KVH * HD - C_Q) // LANE)
        return jnp.where(j < RW_PAD // LANE, jnp.minimum(j, 3 * BW // LANE - 1), shifted)

    def misc_blk(j):
        return jnp.clip(j - 3 * BW // LANE, 0, CODE_W // LANE - 1)

    return pl.pallas_call(
        _wrelayout_kernel,
        grid=(depth, nb),
        in_specs=[pl.BlockSpec((1, D, LANE), lambda l, j: (l, 0, src(j))),
                  pl.BlockSpec((1, D, LANE), lambda l, j: (l, 0, misc_blk(j)))],
        out_specs=pl.BlockSpec((1, D, LANE), lambda l, j: (l, 0, j)),
        out_shape=jax.ShapeDtypeStruct((depth, D, NP), BF16),
        compiler_params=_cp("arbitrary", "arbitrary"),
        name="w_in_relayout",
    )(w_in, misc)


def _prep_layer(l, depth, shift_mu, decay_up, decay_bias, iclr_up, iclr_bias, gate_up, vres_down, vres_up,
                vres_bias, k_k, k_a, r_k, gn_w, gn_b, conv_w, attn_sink, w_branch, w_out, w_router,
                w_exp_gate, w_exp_up, w_exp_down, norm_mix, norm_ffn, eblk):
    mu_parts, _ = _rwkv_col_parts(shift_mu[l][None, :])
    pad_rank = lambda u: jnp.pad(u, ((0, 0), (0, 128 - u.shape[1]), (0, 0))).astype(BF16)
    lp = dict(
        mu=_pad_cols(mu_parts),
        decay_up=pad_rank(decay_up[l]), decay_bias=decay_bias[l].reshape(2, 1, BW),
        iclr_up=pad_rank(iclr_up[l]), iclr_bias=iclr_bias[l].reshape(2, 1, BW),
        gate_up=gate_up[l].astype(BF16), k_k=k_k[l].reshape(1, BW), k_a=k_a[l].reshape(1, BW),
        r_k=r_k[l].reshape(1, BW), gn_w=gn_w[l].reshape(1, BW), gn_b=gn_b[l].reshape(1, BW),
        conv_w=conv_w[l], sink=attn_sink[l].reshape(1, NH),
        w_branch=w_branch[l].astype(BF16), w_out=w_out[l].astype(BF16),
        w_router=jnp.pad(w_router[l], ((0, 0), (0, 128 - N_EXPERTS))),
        layer=l, w_exp_gate=w_exp_gate, w_exp_up=w_exp_up, w_exp_down=w_exp_down,
        norm_ffn=norm_ffn[l], eblk=eblk)
    if l > 0:
        lp["vres_down"] = jnp.pad(vres_down[l - 1], ((0, 0), (0, 128 - VRES_RANK))).astype(BF16)
        lp["vres_up"] = jnp.pad(vres_up[l - 1], ((0, 128 - VRES_RANK), (0, 0))).astype(BF16)
        lp["vres_bias"] = vres_bias[l - 1].reshape(1, BW)
    return lp


def _rope_tables(seq_len):
    quarter = HD // 4
    inv = ROPE_BASE ** (-jnp.arange(quarter, dtype=F32) / quarter)
    pos = jnp.arange(seq_len)
    ang_r = (pos // GRID_W).astype(F32)[:, None] * inv[None, :]
    ang_c = (pos % GRID_W).astype(F32)[:, None] * inv[None, :]
    cos = jnp.concatenate([jnp.cos(ang_r)] * 2 + [jnp.cos(ang_c)] * 2, axis=1)
    sin = jnp.concatenate([-jnp.sin(ang_r), jnp.sin(ang_r), -jnp.sin(ang_c), jnp.sin(ang_c)], axis=1)
    return jnp.tile(cos, (1, NH)), jnp.tile(sin, (1, NH))


def kernel(x, c, ctx, c_ctx, w_mod, b_mod, norm_mix, norm_ffn, w_in, shift_mu, decay_up, decay_bias, iclr_up,
           iclr_bias, gate_up, vres_down, vres_up, vres_bias, k_k, k_a, r_k, gn_w, gn_b, conv_w, attn_sink,
           w_branch, w_out, w_router, w_exp_gate, w_exp_up, w_exp_down, norm_final):
    bsz, seq_len, _ = x.shape
    ctx_len = ctx.shape[1]
    depth = w_in.shape[0]
    mod_rows = -(-(bsz + 1) // 8) * 8
    cc = jnp.concatenate([c, c_ctx[None, :], jnp.zeros((mod_rows - bsz - 1, D), F32)], axis=0)
    mod_all = _mod_call(cc, w_mod, b_mod)

    hd_i = jnp.arange(BW) // HD
    eblk = (hd_i[:, None] == jnp.arange(128)[None, :]).astype(BF16)
    cos, sin = _rope_tables(seq_len)

    row_c = lambda i: bsz
    tm2 = 512
    row_l2 = lambda i: i // (seq_len // tm2)
    mod3_of = lambda l: mod_all[l].reshape(mod_rows * 6, 1, D)

    x_l = x.reshape(bsz * seq_len, D)
    x_c = ctx.reshape(bsz * ctx_len, D)
    vf_l = vf_c = None
    zero_state = jnp.zeros((2, bsz, HD, BW), F32)

    w_p = _wrelayout_call(w_in)
    h_l = _normmod_call(x_l, norm_mix[0], mod3_of(0), row_l2, tm2)
    h_c = _normmod_call(x_c, norm_mix[0], mod3_of(0), row_c, tm2)

    for l in range(depth):
        last = l == depth - 1
        lp = _prep_layer(l, depth, shift_mu, decay_up, decay_bias, iclr_up, iclr_bias, gate_up, vres_down,
                         vres_up, vres_bias, k_k, k_a, r_k, gn_w, gn_b, conv_w, attn_sink, w_branch, w_out,
                         w_router, w_exp_gate, w_exp_up, w_exp_down, norm_mix, norm_ffn, eblk)
        mod3 = mod_all[l].reshape(mod_rows * 6, 1, D)

        p_l = _inproj_call(h_l, w_p, l)
        p_c = _inproj_call(h_c, w_p, l, rwkv_kv_only=last)
        nxt = (norm_final, None) if last else (norm_mix[l + 1], mod3_of(l + 1))
        kv_blk_c = (RW_PAD if last else C_AK) // (KVH * HD)

        st_c = _streams_call(p_c, ctx_len, lp, vf_c)
        st_l = _streams_call(p_l, seq_len, lp, vf_l)
        if l == 0:
            vf_c, vf_l = st_c[2], st_l[2]
        state_c, y_c = _scan_call(st_c, lp["k_a"], zero_state, bsz, ctx_len, not last)
        _, y_l = _scan_call(st_l, lp["k_a"], state_c, bsz, seq_len, True)
        br_rwkv_l = _rwkv_out_call(y_l, st_l, lp)

        br_attn_l = _attn_lat_call(p_l, p_c, kv_blk_c, cos, sin, lp["sink"], bsz, seq_len, ctx_len)
        br_conv_l = _conv_call(p_l, seq_len, lp["conv_w"])

        m_l = _merge_call(br_rwkv_l, br_conv_l, br_attn_l, p_l, lp["w_branch"])
        x_l = _outproj_call(m_l, lp["w_out"], x_l, mod3, row_l2, tm2)
        res = _ffn(x_l, bsz, seq_len, lp, mod3, row_l2, lambda b: b, tm2, *nxt)
        x_l, h_l = (res, None) if last else res

        if not last:
            br_rwkv_c = _rwkv_out_call(y_c, st_c, lp)
            br_attn_c = _attn_ctx_call(p_c, lp["sink"], bsz, ctx_len)
            br_conv_c = _conv_call(p_c, ctx_len, lp["conv_w"])
            m_c = _merge_call(br_rwkv_c, br_conv_c, br_attn_c, p_c, lp["w_branch"])
            x_c = _outproj_call(m_c, lp["w_out"], x_c, mod3, row_c, tm2)
            x_c, h_c = _ffn(x_c, bsz, ctx_len, lp, mod3, row_c, lambda b: bsz, tm2, *nxt)

    return x_l.reshape(bsz, seq_len, D)
```

```python
import functools

import jax
import jax.numpy as jnp
from jax import lax
from jax.experimental import pallas as pl
from jax.experimental.pallas import tpu as pltpu

F32 = jnp.float32
BF16 = jnp.bfloat16
LANE = 128

D = 2048
HD = 64
BW = 1024
NH = BW // HD
KVH = 4
GQ = NH // KVH
DECAY_RANK = 96
ICLR_RANK = 96
GATE_RANK = 256
VRES_RANK = 64
GN_EPS = 64e-5
NORM_EPS = 1e-6
WINDOW = 128
BLK = 128
assert WINDOW == BLK
GRID_W = 64
ROPE_BASE = 10000.0
NEG_INF = -1e30
N_EXPERTS = 16
EXPERT_FF = 2048
CAPACITY_FACTOR = 2
CHUNK = 64

C_R, C_K, C_V = 0, 1024, 2048
C_WD = (3072, 3200)
C_AD = (3328, 3456)
C_GD = 3584
RW_PAD = 4096
C_Q = 4096
C_CONV = 5120
C_GATE = 8192
C_AK = 14336
C_AV = 14592
NP = 14848
TN_IN = 512

VMEM_LIMIT = 56 * 1024 * 1024


def _cp(*sem, vmem=VMEM_LIMIT):
    return pltpu.CompilerParams(dimension_semantics=tuple(sem), vmem_limit_bytes=vmem)


def _dot(a, b):
    return jnp.dot(a, b, preferred_element_type=F32)


def _dot_nt(a, b):
    return lax.dot_general(a, b, (((1,), (1,)), ((), ())), preferred_element_type=F32)


def _dot_tn(a, b):
    return lax.dot_general(a, b, (((0,), (0,)), ((), ())), preferred_element_type=F32)


def _sigmoid(x):
    return 0.5 * jnp.tanh(0.5 * x) + 0.5


def _seg_sum(x, e_ref):
    e = e_ref[...]

    def split(z):
        hi = z.astype(BF16)
        return hi, (z - hi.astype(F32)).astype(BF16)

    hi, lo = split(x)
    hi2, lo2 = split(_dot(hi, e) + _dot(lo, e))
    return _dot_nt(hi2, e) + _dot_nt(lo2, e)


def _mod_kernel(c_ref, w_ref, b_ref, o_ref):
    c = c_ref[...]
    sc = c * _sigmoid(c)
    o_ref[0] = _dot(sc.astype(BF16), w_ref[0].astype(BF16)) + b_ref[0]


def _mod_call(cc, w_mod, b_mod):
    depth, _, n6 = w_mod.shape
    rows = cc.shape[0]
    tn = 1024
    return pl.pallas_call(
        _mod_kernel,
        grid=(depth, n6 // tn),
        in_specs=[pl.BlockSpec((rows, D), lambda l, j: (0, 0)),
                  pl.BlockSpec((1, D, tn), lambda l, j: (l, 0, j)),
                  pl.BlockSpec((1, 1, tn), lambda l, j: (l, 0, j))],
        out_specs=pl.BlockSpec((1, rows, tn), lambda l, j: (l, 0, j)),
        out_shape=jax.ShapeDtypeStruct((depth, rows, n6), F32),
        compiler_params=_cp("arbitrary", "arbitrary"),
        name="mod_proj",
    )(cc, w_mod, b_mod.reshape(depth, 1, n6))


def _norm_mod(x, gain, shift, scale):
    ms = jnp.mean(x * x, axis=-1, keepdims=True)
    y = x * lax.rsqrt(ms + NORM_EPS)
    return (y * gain) * (1.0 + scale) + shift


def _normmod_kernel(x_ref, gain_ref, shift_ref, scale_ref, h_ref):
    h_ref[...] = _norm_mod(x_ref[...], gain_ref[...], shift_ref[0], scale_ref[0]).astype(BF16)


def _normmod_call(x2, gain, mod3, mod_row, tm):
    n_tok = x2.shape[0]
    return pl.pallas_call(
        _normmod_kernel,
        grid=(n_tok // tm,),
        in_specs=[pl.BlockSpec((tm, D), lambda i: (i, 0)),
                  pl.BlockSpec((1, D), lambda i: (0, 0)),
                  pl.BlockSpec((1, 1, D), lambda i: (mod_row(i) * 6 + 0, 0, 0)),
                  pl.BlockSpec((1, 1, D), lambda i: (mod_row(i) * 6 + 1, 0, 0))],
        out_specs=pl.BlockSpec((tm, D), lambda i: (i, 0)),
        out_shape=jax.ShapeDtypeStruct((n_tok, D), BF16),
        compiler_params=_cp("arbitrary"),
        name="norm_mod",
    )(x2, gain.reshape(1, D), mod3, mod3)


def _inproj_kernel(h_ref, w_ref, o_ref):
    o_ref[...] = _dot(h_ref[...], w_ref[0]).astype(BF16)


def _inproj_call(h, w_p, layer, rwkv_kv_only=False):
    n_tok = h.shape[0]
    tm = min(2048, n_tok)
    n_rw = RW_PAD // TN_IN
    if rwkv_kv_only:
        ncols = RW_PAD + NP - C_AK
        col = lambda j: jnp.where(j < n_rw, j, j - n_rw + C_AK // TN_IN)
    else:
        ncols = NP
        col = lambda j: j
    return pl.pallas_call(
        _inproj_kernel,
        grid=(n_tok // tm, ncols // TN_IN),
        in_specs=[pl.BlockSpec((tm, D), lambda i, j: (i, 0)),
                  pl.BlockSpec((1, D, TN_IN), lambda i, j: (layer, 0, col(j)))],
        out_specs=pl.BlockSpec((tm, TN_IN), lambda i, j: (i, j)),
        out_shape=jax.ShapeDtypeStruct((n_tok, ncols), BF16),
        compiler_params=_cp("arbitrary", "arbitrary"),
        name="in_proj",
    )(h, w_p)


HALO = 16


def _shifted(cur, prev_blk, next_blk, first, last):
    tt = cur.shape[0]
    row = lax.broadcasted_iota(jnp.int32, cur.shape, 0)
    p_row = jnp.where(first, 0.0, prev_blk[HALO - 1:HALO, :].astype(F32))
    n_row = jnp.where(last, 0.0, next_blk[0:1, :].astype(F32))
    prev = jnp.where(row == 0, p_row, pltpu.roll(cur, 1, axis=0))
    nxt = jnp.where(row == tt - 1, n_row, pltpu.roll(cur, tt - 1, axis=0))
    return prev, nxt


def _halo_specs(tt, width, col_blk, n_tok):
    nb = n_tok // HALO
    r = tt // HALO
    return [pl.BlockSpec((tt, width), lambda i: (i, col_blk)),
            pl.BlockSpec((HALO, width), lambda i: (jnp.maximum(i * r - 1, 0), col_blk)),
            pl.BlockSpec((HALO, width), lambda i: (jnp.minimum((i + 1) * r, nb - 1), col_blk))]


def _streams_kernel(*refs, tiles_per_seq, has_vres):
    if has_vres:
        (p_ref, pp_ref, pn_ref, mu_ref, dup_ref, dbias_ref, iup_ref, ibias_ref, gup_ref, kk_ref_, e_ref,
         vd_ref, vu_ref, vb_ref, vf_ref,
         r_o, k_o, v_o, kk_o, g_o, a_o, lw_o) = refs
    else:
        (p_ref, pp_ref, pn_ref, mu_ref, dup_ref, dbias_ref, iup_ref, ibias_ref, gup_ref, kk_ref_, e_ref,
         r_o, k_o, v_o, kk_o, g_o, a_o, lw_o) = refs
    j = pl.program_id(0) % tiles_per_seq
    used = C_GD + GATE_RANK
    cur_b = p_ref[:, :used]
    tt = cur_b.shape[0]
    ext = jnp.concatenate([pp_ref[:, :used], cur_b, pn_ref[:, :used]], axis=0)
    t_i = lax.broadcasted_iota(jnp.int32, (tt, tt + 2 * HALO), 0)
    e_i = lax.broadcasted_iota(jnp.int32, (tt, tt + 2 * HALO), 1) - HALO
    tap = ((e_i == t_i - 1) & ((e_i >= 0) | (j > 0))) | ((e_i == t_i + 1) & ((e_i < tt) | (j < tiles_per_seq - 1)))
    avg = _dot(jnp.where(tap, 0.5, 0.0).astype(BF16), ext)
    cur = cur_b.astype(F32)
    ps = cur + mu_ref[:, :used] * (avg - cur)
    r = ps[:, C_R:C_R + BW]
    k = ps[:, C_K:C_K + BW]
    v = ps[:, C_V:C_V + BW]
    gd = ps[:, C_GD:C_GD + GATE_RANK]
    if has_vres:
        low = _dot(v.astype(BF16), vd_ref[...])
        mix = _sigmoid(vb_ref[...] + _dot(low.astype(BF16), vu_ref[...]))
        v = v + (vf_ref[...].astype(F32) - v) * mix
    for d in range(2):
        wd = ps[:, C_WD[d]:C_WD[d] + LANE]
        ad = ps[:, C_AD[d]:C_AD[d] + LANE]
        w_logit = dbias_ref[d] + _dot(jnp.tanh(wd).astype(BF16), dup_ref[d])
        lw_o[d] = -jnp.exp(-0.5) * _sigmoid(w_logit)
        a_o[d] = _sigmoid(ibias_ref[d] + _dot(ad.astype(BF16), iup_ref[d])).astype(BF16)
    kh = k * kk_ref_[...]
    ss = _seg_sum(kh * kh, e_ref)
    kk = kh * lax.rsqrt(jnp.maximum(ss, 1e-24))
    r_o[...] = r.astype(BF16)
    k_o[...] = k.astype(BF16)
    v_o[...] = v.astype(BF16)
    kk_o[...] = kk.astype(BF16)
    g_o[...] = _dot(_sigmoid(gd).astype(BF16), gup_ref[...]).astype(BF16)


def _streams_call(p, seq_len, lp, v_first):
    n_tok = p.shape[0]
    tt = 256
    has_vres = v_first is not None
    full = lambda *s: pl.BlockSpec(s, lambda i: (0,) * len(s))
    tok = pl.BlockSpec((tt, BW), lambda i: (i, 0))
    tok2 = pl.BlockSpec((2, tt, BW), lambda i: (0, i, 0))
    in_specs = _halo_specs(tt, C_GD + GATE_RANK, 0, n_tok) + [
        full(1, RW_PAD), full(2, LANE, BW), full(2, 1, BW), full(2, LANE, BW), full(2, 1, BW),
        full(GATE_RANK, BW), full(1, BW), full(BW, LANE)]
    args = [p, p, p, lp["mu"], lp["decay_up"], lp["decay_bias"], lp["iclr_up"], lp["iclr_bias"],
            lp["gate_up"], lp["k_k"], lp["eblk"]]
    if has_vres:
        in_specs += [full(BW, LANE), full(LANE, BW), full(1, BW), tok]
        args += [lp["vres_down"], lp["vres_up"], lp["vres_bias"], v_first]
    sd = lambda dt: jax.ShapeDtypeStruct((n_tok, BW), dt)
    sd2 = lambda dt: jax.ShapeDtypeStruct((2, n_tok, BW), dt)
    return pl.pallas_call(
        functools.partial(_streams_kernel, tiles_per_seq=seq_len // tt, has_vres=has_vres),
        grid=(n_tok // tt,),
        in_specs=in_specs,
        out_specs=[tok, tok, tok, tok, tok, tok2, tok2],
        out_shape=[sd(BF16), sd(BF16), sd(BF16), sd(BF16), sd(BF16), sd2(BF16), sd2(F32)],
        compiler_params=_cp("arbitrary"),
        name="rwkv_streams",
    )(*args)


def _scan_prologue(d, r, k, v, kk, a, lw, ka):
    c = CHUNK
    ri = lax.broadcasted_iota(jnp.int32, (c, 2 * c), 0)
    ci = lax.broadcasted_iota(jnp.int32, (c, 2 * c), 1) % c
    diff = (ci - ri) if d else (ri - ci)
    incl = diff >= 0
    strict = diff > 0
    tri = incl[:, :c].astype(BF16)
    lw_hi = lw.astype(BF16)
    lw_lo = (lw - lw_hi.astype(F32)).astype(BF16)
    b = _dot(tri, lw_hi) + _dot(tri, lw_lo)
    b_tot = b[0:1, :] if d else b[c - 1:c, :]
    a = a.astype(F32)
    kk = kk.astype(F32)
    kd = k.astype(F32) * (1.0 + (a - 1.0) * ka)
    kka = kk * a
    enb = jnp.exp(-b)
    etail = jnp.exp(b_tot - b)
    return dict(
        incl=incl, strict=strict,
        rt=(r.astype(F32) * jnp.exp(b)).astype(BF16),
        bt=(-kk * jnp.exp(b - lw)).astype(BF16),
        kt=(kd * enb).astype(BF16), at=(kka * enb).astype(BF16),
        kh=(kd * etail).astype(BF16), ah=(kka * etail).astype(BF16),
        vv=v, e_tot=jnp.exp(b_tot))


def _scan_kernel(*refs, with_output, sb):
    ins = refs[:14]
    if with_output:
        yf_ref, yb_ref, sout_ref, st_ref = refs[14:]
    else:
        sout_ref, st_ref = refs[14:]
    ka_ref, s0_ref = ins[12], ins[13]
    s = pl.program_id(1)
    c = CHUNK

    @pl.when(s == 0)
    def _():
        st_ref[...] = s0_ref[...]

    streams = [(d, j) for d in range(2) for j in range(sb)]
    ka = ka_ref[...]
    pro = [_scan_prologue(d, *[ref[j] for ref in ins[6 * d:6 * d + 4]], ins[6 * d + 4][0, j], ins[6 * d + 5][0, j], ka)
           for d, j in streams]
    st_all = [st_ref[d, j] for d, j in streams]

    npair = NH // 2
    units = [(si, slice(p * 2 * HD, (p + 1) * 2 * HD)) for si in range(len(streams)) for p in range(npair)]
    us = range(len(units))
    col = lambda name: [pro[d][name][:, sl] for d, sl in units]
    even1 = lax.broadcasted_iota(jnp.int32, (c, 2 * HD), 1) < HD

    def bd(z):
        zero = jnp.zeros_like(z)
        return jnp.concatenate([jnp.where(even1, z, zero), jnp.where(even1, zero, z)], axis=0)

    strict = [pro[d]["strict"] for d, _ in units]
    incl = [pro[d]["incl"] for d, _ in units]
    st = [st_all[d][:, sl] for d, sl in units]
    vbd = [bd(z) for z in col("vv")]
    rb = [jnp.concatenate(p, axis=0) for p in zip(col("rt"), col("bt"))]
    gk = [_dot_nt(rb[u], bd(z)) for u, z in zip(us, col("kt"))]
    ga = [_dot_nt(rb[u], bd(z)) for u, z in zip(us, col("at"))]
    rbs = [_dot_nt(rb[u], bd(st[u].astype(BF16))) for u in us]
    x = [jnp.where(strict[u], ga[u][c:], 0.0).astype(BF16) for u in us]
    a_bk = [jnp.where(strict[u], gk[u][c:], 0.0) for u in us]
    if with_output:
        a_rk = [jnp.where(incl[u], gk[u][:c], 0.0) for u in us]
        a_ra = [jnp.where(incl[u], ga[u][:c], 0.0).astype(BF16) for u in us]
        av = [_dot(jnp.concatenate([a_rk[u], a_bk[u]], axis=0).astype(BF16), vbd[u]) for u in us]
        w = [rbs[u][c:] + av[u][c:] for u in us]
    else:
        w = [rbs[u][c:] + _dot(a_bk[u].astype(BF16), vbd[u]) for u in us]
    for rnd in range(6):
        if rnd < 5:
            m = [_dot(x[u], jnp.concatenate([bd(x[u]), bd(w[u].astype(BF16))], axis=1)) for u in us]
            x = [m[u][:, :2 * c].astype(BF16) for u in us]
            w = [w[u] + m[u][:, 2 * c:] for u in us]
        else:
            w = [w[u] + _dot(x[u], bd(w[u].astype(BF16))) for u in us]
    ub = [w[u].astype(BF16) for u in us]
    if with_output:
        ys = [rbs[u][:c] + av[u][:c] + _dot(a_ra[u], bd(ub[u])) for u in us]
        for si, (d, j) in enumerate(streams):
            (yb_ref if d else yf_ref)[j] = jnp.concatenate(ys[si * npair:(si + 1) * npair], axis=1)
    full = [_dot_tn(jnp.concatenate([z, ub[u]], axis=0), jnp.concatenate(p, axis=0))
            for u, z, p in zip(us, col("vv"), zip(col("kh"), col("ah")))]
    upd = [jnp.where(even1, f[:HD], f[HD:]) for f in full]
    for si, (d, j) in enumerate(streams):
        st_ref[d, j] = st_all[si] * pro[si]["e_tot"] + jnp.concatenate(upd[si * npair:(si + 1) * npair], axis=1)

    @pl.when(s == pl.num_programs(1) - 1)
    def _():
        sout_ref[...] = st_ref[...]


def _scan_call(streams, k_a, s0, bsz, seq_len, with_output):
    n_tok = streams[0].shape[0]
    nc = seq_len // CHUNK
    sb = 2 if bsz % 2 == 0 else 1
    r, k, v, kk = (z.reshape(bsz, seq_len, BW) for z in streams[:4])
    a2, lw2 = (z.reshape(2, bsz, seq_len, BW) for z in streams[5:])

    chunk = (lambda s: s, lambda s: nc - 1 - s)
    tok = [pl.BlockSpec((sb, CHUNK, BW), lambda b, s, f=f: (b, f(s), 0)) for f in chunk]
    tok2 = [pl.BlockSpec((1, sb, CHUNK, BW), lambda b, s, f=f, d=d: (d, b, f(s), 0)) for d, f in enumerate(chunk)]
    st_spec = pl.BlockSpec((2, sb, HD, BW), lambda b, s: (0, b, 0, 0))
    in_specs, args = [], []
    for d in range(2):
        in_specs += [tok[d]] * 4 + [tok2[d]] * 2
        args += [r, k, v, kk, a2, lw2]
    in_specs += [pl.BlockSpec((1, BW), lambda b, s: (0, 0)), st_spec]
    out_specs = [st_spec]
    out_shape = [jax.ShapeDtypeStruct((2, bsz, HD, BW), F32)]
    if with_output:
        out_specs = tok + out_specs
        out_shape = [jax.ShapeDtypeStruct((bsz, seq_len, BW), F32)] * 2 + out_shape
    res = pl.pallas_call(
        functools.partial(_scan_kernel, with_output=with_output, sb=sb),
        grid=(bsz // sb, nc),
        in_specs=in_specs,
        out_specs=out_specs,
        out_shape=out_shape,
        scratch_shapes=[pltpu.VMEM((2, sb, HD, BW), F32)],
        compiler_params=_cp("arbitrary", "arbitrary"),
        name="rwkv_scan",
    )(*args, k_a, s0)
    if with_output:
        return res[2], (res[0].reshape(n_tok, BW), res[1].reshape(n_tok, BW))
    return res[0], None


def _rwkv_out_kernel(yf_ref, yb_ref, r_ref, k_ref, v_ref, a_ref, g_ref, ka_ref, rk_ref, gnw_ref, gnb_ref, e_ref,
                     o_ref):
    y = yf_ref[...] + yb_ref[...]
    mean = _seg_sum(y, e_ref) * (1.0 / HD)
    yc = y - mean
    var = _seg_sum(yc * yc, e_ref) * (1.0 / HD)
    yn = yc * lax.rsqrt(var + GN_EPS) * gnw_ref[...] + gnb_ref[...]
    r = r_ref[...].astype(F32)
    k = k_ref[...].astype(F32)
    asum = a_ref[0].astype(F32) + a_ref[1].astype(F32)
    kd_sum = k * (2.0 + (asum - 2.0) * ka_ref[...])
    bonus = _seg_sum(r * kd_sum * rk_ref[...], e_ref) * v_ref[...].astype(F32)
    o_ref[...] = ((yn + bonus) * g_ref[...].astype(F32)).astype(BF16)


def _rwkv_out_call(y2, streams, lp):
    r, k, v, _, g, a2, _ = streams
    n_tok = r.shape[0]
    tt = 256
    tok = pl.BlockSpec((tt, BW), lambda i: (i, 0))
    tok2 = pl.BlockSpec((2, tt, BW), lambda i: (0, i, 0))
    vec = pl.BlockSpec((1, BW), lambda i: (0, 0))
    return pl.pallas_call(
        _rwkv_out_kernel,
        grid=(n_tok // tt,),
        in_specs=[tok, tok, tok, tok, tok, tok2, tok, vec, vec, vec, vec, pl.BlockSpec((BW, LANE), lambda i: (0, 0))],
        out_specs=tok,
        out_shape=jax.ShapeDtypeStruct((n_tok, BW), BF16),
        compiler_params=_cp("arbitrary"),
        name="rwkv_out",
    )(y2[0], y2[1], r, k, v, a2, g, lp["k_a"], lp["r_k"], lp["gn_w"], lp["gn_b"], lp["eblk"])


def _conv_kernel(b_ref, c_ref, cp_ref, cn_ref, u_ref, up_ref, un_ref, w_ref, o_ref, *, tiles_per_seq):
    j = pl.program_id(0) % tiles_per_seq
    cu = c_ref[...].astype(F32) * u_ref[...].astype(F32)
    cu_p = cp_ref[...].astype(F32) * up_ref[...].astype(F32)
    cu_n = cn_ref[...].astype(F32) * un_ref[...].astype(F32)
    prev, nxt = _shifted(cu, cu_p, cu_n, j == 0, j == tiles_per_seq - 1)
    w = w_ref[...]
    conv = w[0:1] * prev + w[1:2] * cu + w[2:3] * nxt
    o_ref[...] = (b_ref[...].astype(F32) * conv).astype(BF16)


def _conv_call(p, seq_len, conv_w):
    n_tok = p.shape[0]
    tt = 256
    cb = C_CONV // BW
    return pl.pallas_call(
        functools.partial(_conv_kernel, tiles_per_seq=seq_len // tt),
        grid=(n_tok // tt,),
        in_specs=[pl.BlockSpec((tt, BW), lambda i: (i, cb))] + _halo_specs(tt, BW, cb + 1, n_tok)
        + _halo_specs(tt, BW, cb + 2, n_tok) + [pl.BlockSpec((3, BW), lambda i: (0, 0))],
        out_specs=pl.BlockSpec((tt, BW), lambda i: (i, 0)),
        out_shape=jax.ShapeDtypeStruct((n_tok, BW), BF16),
        compiler_params=_cp("arbitrary"),
        name="short_conv",
    )(p, p, p, p, p, p, p, conv_w)


def _rope_all(blocks, cos, sin):
    ri = lax.broadcasted_iota(jnp.int32, (LANE, LANE), 0)
    ci = lax.broadcasted_iota(jnp.int32, (LANE, LANE), 1)
    perm = (ri == jnp.where((ci % 32) < 16, ci + 16, ci - 16)).astype(BF16)
    slices = [b[:, j:j + LANE] for b in blocks for j in range(0, b.shape[1], LANE)]
    moved = _dot(jnp.concatenate(slices, axis=0), perm)
    outs, row = [], 0
    for b, c, s in zip(blocks, cos, sin):
        cols = []
        for j in range(0, b.shape[1], LANE):
            cols.append(moved[row:row + b.shape[0]])
            row += b.shape[0]
        outs.append(b.astype(F32) * c + jnp.concatenate(cols, axis=1) * s)
    return outs


LOG2E = 1.4426950408889634
Q_SCALE = HD ** -0.5 * LOG2E


def _softmax_pv(s2, sink2_col, v):
    m = jnp.maximum(jnp.max(s2, axis=-1, keepdims=True), sink2_col)
    p = jnp.exp2(s2 - m)
    den = jnp.sum(p, axis=-1, keepdims=True) + jnp.exp2(sink2_col - m)
    return _dot(p.astype(BF16), v) / den


def _attn_lat_kernel(q_ref, kp_ref, kc_ref, kn_ref, vp_ref, vc_ref, vn_ref, kx_ref, vx_ref,
                     cos_ref, cosp_ref, cosn_ref, sin_ref, sinp_ref, sinn_ref, sink_ref, band_ref, o_ref, *, nblk):
    n = pl.program_id(1)
    kvw = KVH * HD
    q, kp, kc, kn = _rope_all(
        [q_ref[...], kp_ref[...], kc_ref[...], kn_ref[...]],
        [cos_ref[...], cosp_ref[:, :kvw], cos_ref[:, :kvw], cosn_ref[:, :kvw]],
        [sin_ref[...], sinp_ref[:, :kvw], sin_ref[:, :kvw], sinn_ref[:, :kvw]])
    q = (q * Q_SCALE).astype(BF16)
    k_all = jnp.concatenate([kp.astype(BF16), kc.astype(BF16), kn.astype(BF16), kx_ref[...]], axis=0)
    v_all = jnp.concatenate([vp_ref[...], vc_ref[...], vn_ref[...], vx_ref[...]], axis=0)
    bias_p = jnp.where(n > 0, band_ref[0], NEG_INF)
    bias_n = jnp.where(n < nblk - 1, band_ref[1], NEG_INF)
    sink = sink_ref[...] * LOG2E
    gs = range(KVH)
    even = lax.broadcasted_iota(jnp.int32, (BLK, 2 * HD), 1) < HD
    zero = jnp.zeros((BLK, 2 * HD), BF16)
    eye = (lax.broadcasted_iota(jnp.int32, (2 * HD, 2 * HD), 0)
           == lax.broadcasted_iota(jnp.int32, (2 * HD, 2 * HD), 1)).astype(BF16)

    def head_rows(gi):
        parts = []
        for t in range(GQ // 2):
            qp = q[:, (gi * GQ + 2 * t) * HD:(gi * GQ + 2 * t + 2) * HD]
            parts += [jnp.where(even, qp, zero), jnp.where(even, zero, qp)]
        return jnp.concatenate(parts, axis=0)

    qg = [head_rows(gi) for gi in gs]
    kd = [jnp.concatenate([k_all[:, gi * HD:(gi + 1) * HD]] * 2, axis=1) for gi in gs]
    vt = [_dot_nt(eye[:HD, :HD], v_all[:, gi * HD:(gi + 1) * HD]).astype(BF16) for gi in gs]
    sk = [jnp.concatenate([jnp.broadcast_to(sink[:, gi * GQ + t:gi * GQ + t + 1], (1, BLK))
                           for t in range(GQ)], axis=1) for gi in gs]
    s = [_dot_nt(kd[gi], qg[gi]) for gi in gs]
    s = [jnp.concatenate([z[:BLK] + bias_p, z[BLK:2 * BLK], z[2 * BLK:3 * BLK] + bias_n, z[3 * BLK:]], axis=0)
         for z in s]
    m = [jnp.maximum(jnp.max(s[gi], axis=0, keepdims=True), sk[gi]) for gi in gs]
    p = [jnp.exp2(s[gi] - m[gi]).astype(BF16) for gi in gs]
    ones = jnp.ones((16, vt[0].shape[1]), BF16)
    pv = [_dot(jnp.concatenate([vt[gi], ones], axis=0), p[gi]) for gi in gs]
    og = [(pv[gi][:HD] / (pv[gi][HD:HD + 1] + jnp.exp2(sk[gi] - m[gi]))).astype(BF16) for gi in gs]
    o_ref[...] = jnp.concatenate(
        [_dot_nt(eye, jnp.concatenate([og[gi][:, 2 * t * BLK:(2 * t + 1) * BLK],
                                       og[gi][:, (2 * t + 1) * BLK:(2 * t + 2) * BLK]], axis=0))
         for gi in gs for t in range(GQ // 2)], axis=1).astype(BF16)


def _attn_lat_call(p_l, p_c, kv_blk_c, cos, sin, sink, bsz, seq_len, ctx_len):
    nblk = seq_len // BLK
    kvw = KVH * HD
    kb, vb = C_AK // kvw, C_AV // kvw
    rowq = lambda b, n: b * nblk + n
    rowp = lambda b, n: b * nblk + jnp.maximum(n - 1, 0)
    rown = lambda b, n: b * nblk + jnp.minimum(n + 1, nblk - 1)
    tabp = lambda b, n: (jnp.maximum(n - 1, 0), 0)
    tabn = lambda b, n: (jnp.minimum(n + 1, nblk - 1), 0)
    kv = lambda rf, cb: pl.BlockSpec((BLK, kvw), lambda b, n: (rf(b, n), cb))
    tab = lambda f: pl.BlockSpec((BLK, BW), f)
    kidx = jnp.arange(BLK)[:, None]
    qpos = jnp.arange(GQ * BLK)[None, :] % BLK
    band = jnp.where(jnp.stack([kidx >= qpos, kidx <= qpos]), 0.0, NEG_INF).astype(F32)
    return pl.pallas_call(
        functools.partial(_attn_lat_kernel, nblk=nblk),
        grid=(bsz, nblk),
        in_specs=[pl.BlockSpec((BLK, BW), lambda b, n: (rowq(b, n), C_Q // BW)),
                  kv(rowp, kb), kv(rowq, kb), kv(rown, kb), kv(rowp, vb), kv(rowq, vb), kv(rown, vb),
                  pl.BlockSpec((ctx_len, kvw), lambda b, n: (b, kv_blk_c)),
                  pl.BlockSpec((ctx_len, kvw), lambda b, n: (b, kv_blk_c + 1)),
                  tab(lambda b, n: (n, 0)), tab(tabp), tab(tabn),
                  tab(lambda b, n: (n, 0)), tab(tabp), tab(tabn),
                  pl.BlockSpec((1, NH), lambda b, n: (0, 0)),
                  pl.BlockSpec((2, BLK, GQ * BLK), lambda b, n: (0, 0, 0))],
        out_specs=pl.BlockSpec((BLK, BW), lambda b, n: (rowq(b, n), 0)),
        out_shape=jax.ShapeDtypeStruct((bsz * seq_len, BW), BF16),
        compiler_params=_cp("arbitrary", "arbitrary"),
        name="attn_latent",
    )(p_l, p_l, p_l, p_l, p_l, p_l, p_l, p_c, p_c, cos, cos, cos, sin, sin, sin, sink, band)


def _attn_ctx_kernel(q_ref, kx_ref, vx_ref, sink_ref, o_ref):
    q = (q_ref[...].astype(F32) * Q_SCALE).astype(BF16)
    kx = kx_ref[...]
    vx = vx_ref[...]
    sink = sink_ref[...] * LOG2E
    tq = q.shape[0]
    outs = []
    for gi in range(KVH):
        qg = jnp.concatenate([q[:, (gi * GQ + t) * HD:(gi * GQ + t + 1) * HD] for t in range(GQ)], axis=0)
        sk = jnp.concatenate([jnp.broadcast_to(sink[:, gi * GQ + t:gi * GQ + t + 1], (tq, 1))
                              for t in range(GQ)], axis=0)
        s = _dot_nt(qg, kx[:, gi * HD:(gi + 1) * HD])
        og = _softmax_pv(s, sk, vx[:, gi * HD:(gi + 1) * HD])
        outs += [og[t * tq:(t + 1) * tq] for t in range(GQ)]
    o_ref[...] = jnp.concatenate(outs, axis=1).astype(BF16)


def _attn_ctx_call(p_c, sink, bsz, ctx_len):
    kvw = KVH * HD
    tq = 128
    nq = ctx_len // tq
    return pl.pallas_call(
        _attn_ctx_kernel,
        grid=(bsz, nq),
        in_specs=[pl.BlockSpec((tq, BW), lambda b, n: (b * nq + n, C_Q // BW)),
                  pl.BlockSpec((ctx_len, kvw), lambda b, n: (b, C_AK // kvw)),
                  pl.BlockSpec((ctx_len, kvw), lambda b, n: (b, C_AV // kvw)),
                  pl.BlockSpec((1, NH), lambda b, n: (0, 0))],
        out_specs=pl.BlockSpec((tq, BW), lambda b, n: (b * nq + n, 0)),
        out_shape=jax.ShapeDtypeStruct((bsz * ctx_len, BW), BF16),
        compiler_params=_cp("arbitrary", "arbitrary"),
        name="attn_context",
    )(p_c, p_c, p_c, sink)


def _merge_kernel(b0_ref, b1_ref, b2_ref, g0_ref, g1_ref, g2_ref, w_ref, o_ref):
    acc = _sigmoid(g0_ref[...].astype(F32)) * _dot(b0_ref[...], w_ref[0])
    acc += _sigmoid(g1_ref[...].astype(F32)) * _dot(b1_ref[...], w_ref[1])
    acc += _sigmoid(g2_ref[...].astype(F32)) * _dot(b2_ref[...], w_ref[2])
    o_ref[...] = acc.astype(BF16)


def _merge_call(br_rwkv, br_conv, br_attn, p, w_branch):
    n_tok = p.shape[0]
    tm, tn = 512, 1024
    nn = D // tn
    br = pl.BlockSpec((tm, BW), lambda j, i: (i, 0))
    gate = lambda t: pl.BlockSpec((tm, tn), lambda j, i: (i, (C_GATE + t * D) // tn + j))
    return pl.pallas_call(
        _merge_kernel,
        grid=(nn, n_tok // tm),
        in_specs=[br, br, br, gate(0), gate(1), gate(2), pl.BlockSpec((3, BW, tn), lambda j, i: (0, 0, j))],
        out_specs=pl.BlockSpec((tm, tn), lambda j, i: (i, j)),
        out_shape=jax.ShapeDtypeStruct((n_tok, D), BF16),
        compiler_params=_cp("arbitrary", "arbitrary"),
        name="merge_branches",
    )(br_rwkv, br_conv, br_attn, p, p, p, w_branch)


def _outproj_kernel(m_ref, w_ref, x_ref, gate_ref, o_ref):
    o_ref[...] = x_ref[...] + gate_ref[0] * _dot(m_ref[...], w_ref[...])


def _outproj_call(m, w_out, x2, mod3, mod_row, tm):
    n_tok = x2.shape[0]
    return pl.pallas_call(
        _outproj_kernel,
        grid=(n_tok // tm,),
        in_specs=[pl.BlockSpec((tm, D), lambda i: (i, 0)),
                  pl.BlockSpec((D, D), lambda i: (0, 0)),
                  pl.BlockSpec((tm, D), lambda i: (i, 0)),
                  pl.BlockSpec((1, 1, D), lambda i: (mod_row(i) * 6 + 2, 0, 0))],
        out_specs=pl.BlockSpec((tm, D), lambda i: (i, 0)),
        out_shape=jax.ShapeDtypeStruct((n_tok, D), F32),
        compiler_params=_cp("arbitrary"),
        name="out_proj",
    )(m, w_out, x2, mod3)


def _ffn_prep_kernel(x_ref, gain_ref, shift_ref, scale_ref, wr_ref, h_ref, aff_ref):
    h = _norm_mod(x_ref[...], gain_ref[...], shift_ref[0], scale_ref[0])
    h_hi = h.astype(BF16)
    h_ref[...] = h_hi
    h_lo = (h - h_hi.astype(F32)).astype(BF16)
    wr = wr_ref[...]
    w_hi = wr.astype(BF16)
    w_lo = (wr - w_hi.astype(F32)).astype(BF16)
    logits = _dot(h_hi, w_hi) + (_dot(h_hi, w_lo) + _dot(h_lo, w_hi))
    lane = lax.broadcasted_iota(jnp.int32, logits.shape, 1)
    logits = jnp.where(lane < N_EXPERTS, logits, NEG_INF)
    m = jnp.max(logits, axis=-1, keepdims=True)
    e = jnp.exp(logits - m)
    aff_ref[...] = e / jnp.sum(e, axis=-1, keepdims=True)


def _ffn_prep_call(x2, gain, mod3, mod_row, wr_pad, tm):
    n_tok = x2.shape[0]
    return pl.pallas_call(
        _ffn_prep_kernel,
        grid=(n_tok // tm,),
        in_specs=[pl.BlockSpec((tm, D), lambda i: (i, 0)),
                  pl.BlockSpec((1, D), lambda i: (0, 0)),
                  pl.BlockSpec((1, 1, D), lambda i: (mod_row(i) * 6 + 3, 0, 0)),
                  pl.BlockSpec((1, 1, D), lambda i: (mod_row(i) * 6 + 4, 0, 0)),
                  pl.BlockSpec((D, LANE), lambda i: (0, 0))],
        out_specs=[pl.BlockSpec((tm, D), lambda i: (i, 0)), pl.BlockSpec((tm, LANE), lambda i: (i, 0))],
        out_shape=[jax.ShapeDtypeStruct((n_tok, D), BF16), jax.ShapeDtypeStruct((n_tok, LANE), F32)],
        compiler_params=_cp("arbitrary"),
        name="ffn_prep",
    )(x2, gain.reshape(1, D), mod3, mod3, wr_pad)


def _select_kernel(aff_ref, slot_ref, *, cap):
    a = aff_ref[0]
    n = a.shape[1]
    bits = lax.bitcast_convert_type(a, jnp.int32)

    def body(i, t):
        cand = t | jnp.left_shift(jnp.int32(1), 30 - i)
        cnt = jnp.sum((bits >= cand).astype(jnp.int32), axis=-1, keepdims=True)
        return jnp.where(cnt >= cap, cand, t)

    thr = lax.fori_loop(0, 31, body, jnp.zeros((a.shape[0], 1), jnp.int32))
    gt = bits > thr
    eq = bits == thr
    n_gt = jnp.sum(gt.astype(jnp.int32), axis=-1, keepdims=True)
    tc = min(n, 512)

    def prefix(mask_bf16):
        cols = []
        for j0 in range(0, n, tc):
            ri = lax.broadcasted_iota(jnp.int32, (n, tc), 0)
            ci = lax.broadcasted_iota(jnp.int32, (n, tc), 1) + j0
            cols.append(_dot(mask_bf16, (ri <= ci).astype(BF16)))
        return jnp.concatenate(cols, axis=1)

    eq_f = eq.astype(BF16)
    excl_eq = prefix(eq_f) - eq_f.astype(F32)
    sel = gt | (eq & (excl_eq < (cap - n_gt).astype(F32)))
    pos = prefix(sel.astype(BF16))
    slot_ref[0] = jnp.where(sel, pos.astype(jnp.int32) - 1, -1)


def _select_call(aff_t, cap):
    bsz, ne, n = aff_t.shape
    return pl.pallas_call(
        functools.partial(_select_kernel, cap=cap),
        grid=(bsz,),
        in_specs=[pl.BlockSpec((1, ne, n), lambda b: (b, 0, 0))],
        out_specs=pl.BlockSpec((1, ne, n), lambda b: (b, 0, 0)),
        out_shape=jax.ShapeDtypeStruct((bsz, ne, n), jnp.int32),
        compiler_params=_cp("arbitrary"),
        name="expert_select",
    )(aff_t)


def _gather_kernel(slot_ref, aff_ref, h_ref, xs_ref, g_ref, *, cap):
    slot = slot_ref[0]
    n = slot.shape[1]
    onehot = lax.broadcasted_iota(jnp.int32, (cap, n), 0) == slot
    xs_ref[0] = _dot(onehot.astype(BF16), h_ref[...]).astype(BF16)
    g_ref[0] = jnp.sum(jnp.where(onehot, aff_ref[0], 0.0), axis=-1, keepdims=True)


def _gather_call(slot, aff_t, h2, cap):
    bsz, ne, n = slot.shape
    return pl.pallas_call(
        functools.partial(_gather_kernel, cap=cap),
        grid=(bsz, ne),
        in_specs=[pl.BlockSpec((1, 1, n), lambda b, e: (b * ne + e, 0, 0)),
                  pl.BlockSpec((1, 1, n), lambda b, e: (b * ne + e, 0, 0)),
                  pl.BlockSpec((n, D), lambda b, e: (b, 0))],
        out_specs=[pl.BlockSpec((1, cap, D), lambda b, e: (e, b, 0)),
                   pl.BlockSpec((1, cap, 1), lambda b, e: (e, b, 0))],
        out_shape=[jax.ShapeDtypeStruct((ne, bsz * cap, D), BF16),
                   jax.ShapeDtypeStruct((ne, bsz * cap, 1), F32)],
        compiler_params=_cp("arbitrary", "arbitrary"),
        name="expert_gather",
    )(slot.reshape(bsz * ne, 1, n), aff_t.reshape(bsz * ne, 1, n), h2)


def _expert_kernel(x_ref, wg_ref, wu_ref, wd_ref, g_ref, o_ref, acc_ref):
    f = pl.program_id(2)

    @pl.when(f == 0)
    def _():
        acc_ref[...] = jnp.zeros_like(acc_ref)

    x = x_ref[0]
    gate = _dot(x, wg_ref[0, 0].astype(BF16))
    hid = gate * _sigmoid(gate) * _dot(x, wu_ref[0, 0].astype(BF16))
    acc_ref[...] += _dot(hid.astype(BF16), wd_ref[0, 0].astype(BF16))

    @pl.when(f == pl.num_programs(2) - 1)
    def _():
        o_ref[0] = (acc_ref[...] * g_ref[0]).astype(BF16)


def _expert_call(xs, g, wg, wu, wd, layer):
    ne, rows, _ = xs.shape
    tm = min(rows, 1024)
    tf = 256
    return pl.pallas_call(
        _expert_kernel,
        grid=(ne, rows // tm, EXPERT_FF // tf),
        in_specs=[pl.BlockSpec((1, tm, D), lambda e, i, f: (e, i, 0)),
                  pl.BlockSpec((1, 1, D, tf), lambda e, i, f: (layer, e, 0, f)),
                  pl.BlockSpec((1, 1, D, tf), lambda e, i, f: (layer, e, 0, f)),
                  pl.BlockSpec((1, 1, tf, D), lambda e, i, f: (layer, e, f, 0)),
                  pl.BlockSpec((1, tm, 1), lambda e, i, f: (e, i, 0))],
        out_specs=pl.BlockSpec((1, tm, D), lambda e, i, f: (e, i, 0)),
        out_shape=jax.ShapeDtypeStruct((ne, rows, D), BF16),
        scratch_shapes=[pltpu.VMEM((tm, D), F32)],
        compiler_params=_cp("arbitrary", "arbitrary", "arbitrary"),
        name="expert_mlp",
    )(xs, wg, wu, wd, g)


def _scatter_kernel(*refs, cap, last):
    if last:
        slot_ref, ys_ref, x_ref, gate_ref, gain_ref, o_ref = refs
    else:
        slot_ref, ys_ref, x_ref, gate_ref, gain_ref, shift_ref, scale_ref, o_ref, h_ref = refs
    slot_t = slot_ref[0]
    tq = slot_t.shape[0]
    lane = lax.broadcasted_iota(jnp.int32, (tq, cap), 1)
    acc = jnp.zeros(x_ref.shape, F32)
    for e in range(N_EXPERTS):
        onehot = (slot_t[:, e:e + 1] == lane).astype(BF16)
        acc += _dot(onehot, ys_ref[e])
    x = x_ref[...] + gate_ref[0] * acc
    if last:
        ms = jnp.mean(x * x, axis=-1, keepdims=True)
        o_ref[...] = (x * lax.rsqrt(ms + NORM_EPS)) * gain_ref[...]
    else:
        o_ref[...] = x
        h_ref[...] = _norm_mod(x, gain_ref[...], shift_ref[0], scale_ref[0]).astype(BF16)


def _scatter_call(slot_t, ys, x2, mod3, mod_row_b, cap, gain, next_mod3=None):
    bsz, n, ne = slot_t.shape
    tq = min(n, 256)
    nt = n // tq
    last = next_mod3 is None
    row = pl.BlockSpec((tq, D), lambda b, t: (b * nt + t, 0))
    mod = lambda c: pl.BlockSpec((1, 1, D), lambda b, t: (mod_row_b(b) * 6 + c, 0, 0))
    in_specs = [pl.BlockSpec((1, tq, ne), lambda b, t: (b, t, 0)),
                pl.BlockSpec((ne, cap, D), lambda b, t: (0, b, 0)),
                row, mod(5), pl.BlockSpec((1, D), lambda b, t: (0, 0))]
    args = [slot_t, ys, x2, mod3, gain.reshape(1, D)]
    out_specs, out_shape = row, jax.ShapeDtypeStruct(x2.shape, F32)
    if not last:
        in_specs += [mod(0), mod(1)]
        args += [next_mod3, next_mod3]
        out_specs, out_shape = [row, row], [out_shape, jax.ShapeDtypeStruct(x2.shape, BF16)]
    return pl.pallas_call(
        functools.partial(_scatter_kernel, cap=cap, last=last),
        grid=(bsz, nt),
        in_specs=in_specs,
        out_specs=out_specs,
        out_shape=out_shape,
        compiler_params=_cp("arbitrary", "arbitrary"),
        name="expert_scatter",
    )(*args)


def _ffn(x2, bsz, n, lp, mod3, mod_row, mod_row_b, tm, gain, next_mod3=None):
    cap = CAPACITY_FACTOR * n // N_EXPERTS
    h2, aff = _ffn_prep_call(x2, lp["norm_ffn"], mod3, mod_row, lp["w_router"], tm)
    aff_t = jnp.swapaxes(aff[:, :N_EXPERTS].reshape(bsz, n, N_EXPERTS), 1, 2)
    slot = _select_call(aff_t, cap)
    xs, g = _gather_call(slot, aff_t, h2, cap)
    ys = _expert_call(xs, g, lp["w_exp_gate"], lp["w_exp_up"], lp["w_exp_down"], lp["layer"])
    return _scatter_call(jnp.swapaxes(slot, 1, 2), ys, x2, mod3, mod_row_b, cap, gain, next_mod3)


def _pad_cols(parts):
    out = []
    for a, width in parts:
        out.append(a)
        if a.shape[-1] < width:
            out.append(jnp.zeros(a.shape[:-1] + (width - a.shape[-1],), a.dtype))
    return jnp.concatenate(out, axis=-1)


def _rwkv_col_parts(w):
    o = 3 * BW
    parts = [(w[..., :o], o)]
    for _ in range(2):
        parts.append((w[..., o:o + DECAY_RANK], LANE))
        o += DECAY_RANK
    for _ in range(2):
        parts.append((w[..., o:o + ICLR_RANK], LANE))
        o += ICLR_RANK
    parts.append((w[..., o:o + GATE_RANK], RW_PAD - C_GD))
    return parts, o + GATE_RANK


RW_END = 3 * BW + 2 * DECAY_RANK + 2 * ICLR_RANK + GATE_RANK
CODE_W = RW_PAD - 3 * BW


def _wrelayout_kernel(w_ref, misc_ref, o_ref):
    j = pl.program_id(1)
    is_code = (j >= 3 * BW // LANE) & (j < RW_PAD // LANE)
    o_ref[0] = jnp.where(is_code, misc_ref[0], w_ref[0]).astype(BF16)


def _wrelayout_call(w_in):
    depth = w_in.shape[0]
    code_parts, _ = _rwkv_col_parts(w_in[:, :, :RW_END])
    misc = _pad_cols(code_parts[1:])
    nb = NP // LANE
    kv_src = RW_END // LANE

    def src(j):
        shifted = jnp.where(j >= C_AK // LANE, j - C_AK // LANE + kv_src, j + (RW_END + 2 * KVH * HD - C_Q) // LANE)
        return jnp.where(j < RW_PAD // LANE, jnp.minimum(j, 3 * BW // LANE - 1), shifted)

    def misc_blk(j):
        return jnp.clip(j - 3 * BW // LANE, 0, CODE_W // LANE - 1)

    return pl.pallas_call(
        _wrelayout_kernel,
        grid=(depth, nb),
        in_specs=[pl.BlockSpec((1, D, LANE), lambda l, j: (l, 0, src(j))),
                  pl.BlockSpec((1, D, LANE), lambda l, j: (l, 0, misc_blk(j)))],
        out_specs=pl.BlockSpec((1, D, LANE), lambda l, j: (l, 0, j)),
        out_shape=jax.ShapeDtypeStruct((depth, D, NP), BF16),
        compiler_params=_cp("arbitrary", "arbitrary"),
        name="w_in_relayout",
    )(w_in, misc)


def _prep_layer(l, depth, shift_mu, decay_up, decay_bias, iclr_up, iclr_bias, gate_up, vres_down, vres_up,
                vres_bias, k_k, k_a, r_k, gn_w, gn_b, conv_w, attn_sink, w_branch, w_out, w_router,
                w_exp_gate, w_exp_up, w_exp_down, norm_mix, norm_ffn, eblk):
    mu_parts, _ = _rwkv_col_parts(shift_mu[l][None, :])
    pad_rank = lambda u: jnp.pad(u, ((0, 0), (0, LANE - u.shape[1]), (0, 0))).astype(BF16)
    lp = dict(
        mu=_pad_cols(mu_parts),
        decay_up=pad_rank(decay_up[l]), decay_bias=decay_bias[l].reshape(2, 1, BW),
        iclr_up=pad_rank(iclr_up[l]), iclr_bias=iclr_bias[l].reshape(2, 1, BW),
        gate_up=gate_up[l].astype(BF16), k_k=k_k[l].reshape(1, BW), k_a=k_a[l].reshape(1, BW),
        r_k=r_k[l].reshape(1, BW), gn_w=gn_w[l].reshape(1, BW), gn_b=gn_b[l].reshape(1, BW),
        conv_w=conv_w[l], sink=attn_sink[l].reshape(1, NH),
        w_branch=w_branch[l].astype(BF16), w_out=w_out[l].astype(BF16),
        w_router=jnp.pad(w_router[l], ((0, 0), (0, LANE - N_EXPERTS))),
        layer=l, w_exp_gate=w_exp_gate, w_exp_up=w_exp_up, w_exp_down=w_exp_down,
        norm_ffn=norm_ffn[l], eblk=eblk)
    if l > 0:
        lp["vres_down"] = jnp.pad(vres_down[l - 1], ((0, 0), (0, LANE - VRES_RANK))).astype(BF16)
        lp["vres_up"] = jnp.pad(vres_up[l - 1], ((0, LANE - VRES_RANK), (0, 0))).astype(BF16)
        lp["vres_bias"] = vres_bias[l - 1].reshape(1, BW)
    return lp


def _rope_tables(seq_len):
    quarter = HD // 4
    inv = ROPE_BASE ** (-jnp.arange(quarter, dtype=F32) / quarter)
    pos = jnp.arange(seq_len)
    ang_r = (pos // GRID_W).astype(F32)[:, None] * inv[None, :]
    ang_c = (pos % GRID_W).astype(F32)[:, None] * inv[None, :]
    cos = jnp.concatenate([jnp.cos(ang_r)] * 2 + [jnp.cos(ang_c)] * 2, axis=1)
    sin = jnp.concatenate([-jnp.sin(ang_r), jnp.sin(ang_r), -jnp.sin(ang_c), jnp.sin(ang_c)], axis=1)
    return jnp.tile(cos, (1, NH)), jnp.tile(sin, (1, NH))


def kernel(x, c, ctx, c_ctx, w_mod, b_mod, norm_mix, norm_ffn, w_in, shift_mu, decay_up, decay_bias, iclr_up,
           iclr_bias, gate_up, vres_down, vres_up, vres_bias, k_k, k_a, r_k, gn_w, gn_b, conv_w, attn_sink,
           w_branch, w_out, w_router, w_exp_gate, w_exp_up, w_exp_down, norm_final):
    bsz, seq_len, _ = x.shape
    ctx_len = ctx.shape[1]
    depth = w_in.shape[0]
    mod_rows = -(-(bsz + 1) // 8) * 8
    cc = jnp.concatenate([c, c_ctx[None, :], jnp.zeros((mod_rows - bsz - 1, D), F32)], axis=0)
    mod_all = _mod_call(cc, w_mod, b_mod)

    hd_i = jnp.arange(BW) // HD
    eblk = (hd_i[:, None] == jnp.arange(LANE)[None, :]).astype(BF16)
    cos, sin = _rope_tables(seq_len)

    row_c = lambda i: bsz
    tm2 = 512
    row_l2 = lambda i: i // (seq_len // tm2)
    mod3_of = lambda l: mod_all[l].reshape(mod_rows * 6, 1, D)

    x_l = x.reshape(bsz * seq_len, D)
    x_c = ctx.reshape(bsz * ctx_len, D)
    vf_l = vf_c = None
    zero_state = jnp.zeros((2, bsz, HD, BW), F32)

    w_p = _wrelayout_call(w_in)
    h_l = _normmod_call(x_l, norm_mix[0], mod3_of(0), row_l2, tm2)
    h_c = _normmod_call(x_c, norm_mix[0], mod3_of(0), row_c, tm2)

    for l in range(depth):
        last = l == depth - 1
        lp = _prep_layer(l, depth, shift_mu, decay_up, decay_bias, iclr_up, iclr_bias, gate_up, vres_down,
                         vres_up, vres_bias, k_k, k_a, r_k, gn_w, gn_b, conv_w, attn_sink, w_branch, w_out,
                         w_router, w_exp_gate, w_exp_up, w_exp_down, norm_mix, norm_ffn, eblk)
        mod3 = mod_all[l].reshape(mod_rows * 6, 1, D)

        p_l = _inproj_call(h_l, w_p, l)
        p_c = _inproj_call(h_c, w_p, l, rwkv_kv_only=last)
        nxt = (norm_final, None) if last else (norm_mix[l + 1], mod3_of(l + 1))
        kv_blk_c = (RW_PAD if last else C_AK) // (KVH * HD)

        st_c = _streams_call(p_c, ctx_len, lp, vf_c)
        st_l = _streams_call(p_l, seq_len, lp, vf_l)
        if l == 0:
            vf_c, vf_l = st_c[2], st_l[2]
        state_c, y_c = _scan_call(st_c, lp["k_a"], zero_state, bsz, ctx_len, not last)
        _, y_l = _scan_call(st_l, lp["k_a"], state_c, bsz, seq_len, True)
        br_rwkv_l = _rwkv_out_call(y_l, st_l, lp)

        br_attn_l = _attn_lat_call(p_l, p_c, kv_blk_c, cos, sin, lp["sink"], bsz, seq_len, ctx_len)
        br_conv_l = _conv_call(p_l, seq_len, lp["conv_w"])

        m_l = _merge_call(br_rwkv_l, br_conv_l, br_attn_l, p_l, lp["w_branch"])
        x_l = _outproj_call(m_l, lp["w_out"], x_l, mod3, row_l2, tm2)
        res = _ffn(x_l, bsz, seq_len, lp, mod3, row_l2, lambda b: b, tm2, *nxt)
        x_l, h_l = (res, None) if last else res

        if not last:
            br_rwkv_c = _rwkv_out_call(y_c, st_c, lp)
            br_attn_c = _attn_ctx_call(p_c, lp["sink"], bsz, ctx_len)
            br_conv_c = _conv_call(p_c, ctx_len, lp["conv_w"])
            m_c = _merge_call(br_rwkv_c, br_conv_c, br_attn_c, p_c, lp["w_branch"])
            x_c = _outproj_call(m_c, lp["w_out"], x_c, mod3, row_c, tm2)
            x_c, h_c = _ffn(x_c, bsz, ctx_len, lp, mod3, row_c, lambda b: bsz, tm2, *nxt)

    return x_l.reshape(bsz, seq_len, D)
```

```python
import functools

import jax
import jax.numpy as jnp
from jax import lax
from jax.experimental import pallas as pl
from jax.experimental.pallas import tpu as pltpu

F32 = jnp.float32
BF16 = jnp.bfloat16
LANE = 128

D = 2048
HD = 64
BW = 1024
NH = BW // HD
KVH = 4
GQ = NH // KVH
DECAY_RANK = 96
ICLR_RANK = 96
GATE_RANK = 256
VRES_RANK = 64
GN_EPS = 64e-5
NORM_EPS = 1e-6
WINDOW = 128
BLK = 128
assert WINDOW == BLK
GRID_W = 64
ROPE_BASE = 10000.0
NEG_INF = -1e30
N_EXPERTS = 16
EXPERT_FF = 2048
CAPACITY_FACTOR = 2
CHUNK = 64

C_R, C_K, C_V = 0, 1024, 2048
C_WD = (3072, 3200)
C_AD = (3328, 3456)
C_GD = 3584
RW_PAD = 4096
C_Q = 4096
C_CONV = 5120
C_GATE = 8192
C_AK = 14336
C_AV = 14592
NP = 14848
TN_IN = 512

VMEM_LIMIT = 56 * 1024 * 1024


def _cp(*sem, vmem=VMEM_LIMIT):
    return pltpu.CompilerParams(dimension_semantics=tuple(sem), vmem_limit_bytes=vmem)


def _dot(a, b):
    return jnp.dot(a, b, preferred_element_type=F32)


def _dot_nt(a, b):
    return lax.dot_general(a, b, (((1,), (1,)), ((), ())), preferred_element_type=F32)


def _dot_tn(a, b):
    return lax.dot_general(a, b, (((0,), (0,)), ((), ())), preferred_element_type=F32)


def _sigmoid(x):
    return 0.5 * jnp.tanh(0.5 * x) + 0.5


def _seg_sum(x, e_ref):
    e = e_ref[...]

    def split(z):
        hi = z.astype(BF16)
        return hi, (z - hi.astype(F32)).astype(BF16)

    hi, lo = split(x)
    hi2, lo2 = split(_dot(hi, e) + _dot(lo, e))
    return _dot_nt(hi2, e) + _dot_nt(lo2, e)


def _mod_kernel(c_ref, w_ref, b_ref, o_ref):
    c = c_ref[...]
    sc = c * _sigmoid(c)
    o_ref[0] = _dot(sc.astype(BF16), w_ref[0].astype(BF16)) + b_ref[0]


def _mod_call(cc, w_mod, b_mod):
    depth, _, n6 = w_mod.shape
    rows = cc.shape[0]
    tn = 1024
    return pl.pallas_call(
        _mod_kernel,
        grid=(depth, n6 // tn),
        in_specs=[pl.BlockSpec((rows, D), lambda l, j: (0, 0)),
                  pl.BlockSpec((1, D, tn), lambda l, j: (l, 0, j)),
                  pl.BlockSpec((1, 1, tn), lambda l, j: (l, 0, j))],
        out_specs=pl.BlockSpec((1, rows, tn), lambda l, j: (l, 0, j)),
        out_shape=jax.ShapeDtypeStruct((depth, rows, n6), F32),
        compiler_params=_cp("arbitrary", "arbitrary"),
        name="mod_proj",
    )(cc, w_mod, b_mod.reshape(depth, 1, n6))


def _norm_mod(x, gain, shift, scale):
    ms = jnp.mean(x * x, axis=-1, keepdims=True)
    y = x * lax.rsqrt(ms + NORM_EPS)
    return (y * gain) * (1.0 + scale) + shift


def _normmod_kernel(x_ref, gain_ref, shift_ref, scale_ref, h_ref):
    h_ref[...] = _norm_mod(x_ref[...], gain_ref[...], shift_ref[0], scale_ref[0]).astype(BF16)


def _normmod_call(x2, gain, mod3, mod_row, tm):
    n_tok = x2.shape[0]
    return pl.pallas_call(
        _normmod_kernel,
        grid=(n_tok // tm,),
        in_specs=[pl.BlockSpec((tm, D), lambda i: (i, 0)),
                  pl.BlockSpec((1, D), lambda i: (0, 0)),
                  pl.BlockSpec((1, 1, D), lambda i: (mod_row(i) * 6 + 0, 0, 0)),
                  pl.BlockSpec((1, 1, D), lambda i: (mod_row(i) * 6 + 1, 0, 0))],
        out_specs=pl.BlockSpec((tm, D), lambda i: (i, 0)),
        out_shape=jax.ShapeDtypeStruct((n_tok, D), BF16),
        compiler_params=_cp("arbitrary"),
        name="norm_mod",
    )(x2, gain.reshape(1, D), mod3, mod3)


def _inproj_kernel(h_ref, w_ref, o_ref):
    o_ref[...] = _dot(h_ref[...], w_ref[0]).astype(BF16)


def _inproj_call(h, w_p, layer, rwkv_kv_only=False):
    n_tok = h.shape[0]
    tm = min(2048, n_tok)
    n_rw = RW_PAD // TN_IN
    if rwkv_kv_only:
        ncols = RW_PAD + NP - C_AK
        col = lambda j: jnp.where(j < n_rw, j, j - n_rw + C_AK // TN_IN)
    else:
        ncols = NP
        col = lambda j: j
    return pl.pallas_call(
        _inproj_kernel,
        grid=(n_tok // tm, ncols // TN_IN),
        in_specs=[pl.BlockSpec((tm, D), lambda i, j: (i, 0)),
                  pl.BlockSpec((1, D, TN_IN), lambda i, j: (layer, 0, col(j)))],
        out_specs=pl.BlockSpec((tm, TN_IN), lambda i, j: (i, j)),
        out_shape=jax.ShapeDtypeStruct((n_tok, ncols), BF16),
        compiler_params=_cp("arbitrary", "arbitrary"),
        name="in_proj",
    )(h, w_p)


HALO = 16


def _shifted(cur, prev_blk, next_blk, first, last):
    tt = cur.shape[0]
    row = lax.broadcasted_iota(jnp.int32, cur.shape, 0)
    p_row = jnp.where(first, 0.0, prev_blk[HALO - 1:HALO, :].astype(F32))
    n_row = jnp.where(last, 0.0, next_blk[0:1, :].astype(F32))
    prev = jnp.where(row == 0, p_row, pltpu.roll(cur, 1, axis=0))
    nxt = jnp.where(row == tt - 1, n_row, pltpu.roll(cur, tt - 1, axis=0))
    return prev, nxt


def _halo_specs(tt, width, col_blk, n_tok):
    nb = n_tok // HALO
    r = tt // HALO
    return [pl.BlockSpec((tt, width), lambda i: (i, col_blk)),
            pl.BlockSpec((HALO, width), lambda i: (jnp.maximum(i * r - 1, 0), col_blk)),
            pl.BlockSpec((HALO, width), lambda i: (jnp.minimum((i + 1) * r, nb - 1), col_blk))]


def _streams_kernel(*refs, tiles_per_seq, has_vres):
    if has_vres:
        (p_ref, pp_ref, pn_ref, mu_ref, dup_ref, dbias_ref, iup_ref, ibias_ref, gup_ref, kk_ref_, e_ref,
         vd_ref, vu_ref, vb_ref, vf_ref,
         r_o, k_o, v_o, kk_o, g_o, a_o, lw_o) = refs
    else:
        (p_ref, pp_ref, pn_ref, mu_ref, dup_ref, dbias_ref, iup_ref, ibias_ref, gup_ref, kk_ref_, e_ref,
         r_o, k_o, v_o, kk_o, g_o, a_o, lw_o) = refs
    j = pl.program_id(0) % tiles_per_seq
    used = C_GD + GATE_RANK
    cur_b = p_ref[:, :used]
    tt = cur_b.shape[0]
    ext = jnp.concatenate([pp_ref[:, :used], cur_b, pn_ref[:, :used]], axis=0)
    t_i = lax.broadcasted_iota(jnp.int32, (tt, tt + 2 * HALO), 0)
    e_i = lax.broadcasted_iota(jnp.int32, (tt, tt + 2 * HALO), 1) - HALO
    tap = ((e_i == t_i - 1) & ((e_i >= 0) | (j > 0))) | ((e_i == t_i + 1) & ((e_i < tt) | (j < tiles_per_seq - 1)))
    avg = _dot(jnp.where(tap, 0.5, 0.0).astype(BF16), ext)
    cur = cur_b.astype(F32)
    ps = cur + mu_ref[:, :used] * (avg - cur)
    r = ps[:, C_R:C_R + BW]
    k = ps[:, C_K:C_K + BW]
    v = ps[:, C_V:C_V + BW]
    gd = ps[:, C_GD:C_GD + GATE_RANK]
    if has_vres:
        low = _dot(v.astype(BF16), vd_ref[...])
        mix = _sigmoid(vb_ref[...] + _dot(low.astype(BF16), vu_ref[...]))
        v = v + (vf_ref[...].astype(F32) - v) * mix
    for d in range(2):
        wd = ps[:, C_WD[d]:C_WD[d] + LANE]
        ad = ps[:, C_AD[d]:C_AD[d] + LANE]
        w_logit = dbias_ref[d] + _dot(jnp.tanh(wd).astype(BF16), dup_ref[d])
        lw_o[d] = -jnp.exp(-0.5) * _sigmoid(w_logit)
        a_o[d] = _sigmoid(ibias_ref[d] + _dot(ad.astype(BF16), iup_ref[d])).astype(BF16)
    kh = k * kk_ref_[...]
    ss = _seg_sum(kh * kh, e_ref)
    kk = kh * lax.rsqrt(jnp.maximum(ss, 1e-24))
    r_o[...] = r.astype(BF16)
    k_o[...] = k.astype(BF16)
    v_o[...] = v.astype(BF16)
    kk_o[...] = kk.astype(BF16)
    g_o[...] = _dot(_sigmoid(gd).astype(BF16), gup_ref[...]).astype(BF16)


def _streams_call(p, seq_len, lp, v_first):
    n_tok = p.shape[0]
    tt = 256
    has_vres = v_first is not None
    full = lambda *s: pl.BlockSpec(s, lambda i: (0,) * len(s))
    tok = pl.BlockSpec((tt, BW), lambda i: (i, 0))
    tok2 = pl.BlockSpec((2, tt, BW), lambda i: (0, i, 0))
    in_specs = _halo_specs(tt, C_GD + GATE_RANK, 0, n_tok) + [
        full(1, RW_PAD), full(2, LANE, BW), full(2, 1, BW), full(2, LANE, BW), full(2, 1, BW),
        full(GATE_RANK, BW), full(1, BW), full(BW, LANE)]
    args = [p, p, p, lp["mu"], lp["decay_up"], lp["decay_bias"], lp["iclr_up"], lp["iclr_bias"],
            lp["gate_up"], lp["k_k"], lp["eblk"]]
    if has_vres:
        in_specs += [full(BW, LANE), full(LANE, BW), full(1, BW), tok]
        args += [lp["vres_down"], lp["vres_up"], lp["vres_bias"], v_first]
    sd = lambda dt: jax.ShapeDtypeStruct((n_tok, BW), dt)
    sd2 = lambda dt: jax.ShapeDtypeStruct((2, n_tok, BW), dt)
    return pl.pallas_call(
        functools.partial(_streams_kernel, tiles_per_seq=seq_len // tt, has_vres=has_vres),
        grid=(n_tok // tt,),
        in_specs=in_specs,
        out_specs=[tok, tok, tok, tok, tok, tok2, tok2],
        out_shape=[sd(BF16), sd(BF16), sd(BF16), sd(BF16), sd(BF16), sd2(BF16), sd2(F32)],
        compiler_params=_cp("arbitrary"),
        name="rwkv_streams",
    )(*args)


def _scan_prologue(d, r, k, v, kk, a, lw, ka):
    c = CHUNK
    ri = lax.broadcasted_iota(jnp.int32, (c, 2 * c), 0)
    ci = lax.broadcasted_iota(jnp.int32, (c, 2 * c), 1) % c
    diff = (ci - ri) if d else (ri - ci)
    incl = diff >= 0
    strict = diff > 0
    tri = incl[:, :c].astype(BF16)
    lw_hi = lw.astype(BF16)
    lw_lo = (lw - lw_hi.astype(F32)).astype(BF16)
    b = _dot(tri, lw_hi) + _dot(tri, lw_lo)
    b_tot = b[0:1, :] if d else b[c - 1:c, :]
    a = a.astype(F32)
    kk = kk.astype(F32)
    kd = k.astype(F32) * (1.0 + (a - 1.0) * ka)
    kka = kk * a
    enb = jnp.exp(-b)
    etail = jnp.exp(b_tot - b)
    return dict(
        incl=incl, strict=strict,
        rt=(r.astype(F32) * jnp.exp(b)).astype(BF16),
        bt=(-kk * jnp.exp(b - lw)).astype(BF16),
        kt=(kd * enb).astype(BF16), at=(kka * enb).astype(BF16),
        kh=(kd * etail).astype(BF16), ah=(kka * etail).astype(BF16),
        vv=v, e_tot=jnp.exp(b_tot))


def _scan_kernel(*refs, with_output, sb):
    ins = refs[:14]
    if with_output:
        yf_ref, yb_ref, sout_ref, st_ref = refs[14:]
    else:
        sout_ref, st_ref = refs[14:]
    ka_ref, s0_ref = ins[12], ins[13]
    s = pl.program_id(1)
    c = CHUNK

    @pl.when(s == 0)
    def _():
        st_ref[...] = s0_ref[...]

    streams = [(d, j) for d in range(2) for j in range(sb)]
    ka = ka_ref[...]
    pro = [_scan_prologue(d, *[ref[j] for ref in ins[6 * d:6 * d + 4]], ins[6 * d + 4][0, j], ins[6 * d + 5][0, j], ka)
           for d, j in streams]
    st_all = [st_ref[d, j] for d, j in streams]

    npair = NH // 2
    units = [(si, slice(p * 2 * HD, (p + 1) * 2 * HD)) for si in range(len(streams)) for p in range(npair)]
    us = range(len(units))
    col = lambda name: [pro[d][name][:, sl] for d, sl in units]
    even1 = lax.broadcasted_iota(jnp.int32, (c, 2 * HD), 1) < HD

    def bd(z):
        zero = jnp.zeros_like(z)
        return jnp.concatenate([jnp.where(even1, z, zero), jnp.where(even1, zero, z)], axis=0)

    strict = [pro[d]["strict"] for d, _ in units]
    incl = [pro[d]["incl"] for d, _ in units]
    st = [st_all[d][:, sl] for d, sl in units]
    vbd = [bd(z) for z in col("vv")]
    rb = [jnp.concatenate(p, axis=0) for p in zip(col("rt"), col("bt"))]
    gk = [_dot_nt(rb[u], bd(z)) for u, z in zip(us, col("kt"))]
    ga = [_dot_nt(rb[u], bd(z)) for u, z in zip(us, col("at"))]
    rbs = [_dot_nt(rb[u], bd(st[u].astype(BF16))) for u in us]
    x = [jnp.where(strict[u], ga[u][c:], 0.0).astype(BF16) for u in us]
    a_bk = [jnp.where(strict[u], gk[u][c:], 0.0) for u in us]
    if with_output:
        a_rk = [jnp.where(incl[u], gk[u][:c], 0.0) for u in us]
        a_ra = [jnp.where(incl[u], ga[u][:c], 0.0).astype(BF16) for u in us]
        av = [_dot(jnp.concatenate([a_rk[u], a_bk[u]], axis=0).astype(BF16), vbd[u]) for u in us]
        w = [rbs[u][c:] + av[u][c:] for u in us]
    else:
        w = [rbs[u][c:] + _dot(a_bk[u].astype(BF16), vbd[u]) for u in us]
    for rnd in range(6):
        if rnd < 5:
            m = [_dot(x[u], jnp.concatenate([bd(x[u]), bd(w[u].astype(BF16))], axis=1)) for u in us]
            x = [m[u][:, :2 * c].astype(BF16) for u in us]
            w = [w[u] + m[u][:, 2 * c:] for u in us]
        else:
            w = [w[u] + _dot(x[u], bd(w[u].astype(BF16))) for u in us]
    ub = [w[u].astype(BF16) for u in us]
    if with_output:
        ys = [rbs[u][:c] + av[u][:c] + _dot(a_ra[u], bd(ub[u])) for u in us]
        for si, (d, j) in enumerate(streams):
            (yb_ref if d else yf_ref)[j] = jnp.concatenate(ys[si * npair:(si + 1) * npair], axis=1)
    full = [_dot_tn(jnp.concatenate([z, ub[u]], axis=0), jnp.concatenate(p, axis=0))
            for u, z, p in zip(us, col("vv"), zip(col("kh"), col("ah")))]
    upd = [jnp.where(even1, f[:HD], f[HD:]) for f in full]
    for si, (d, j) in enumerate(streams):
        st_ref[d, j] = st_all[si] * pro[si]["e_tot"] + jnp.concatenate(upd[si * npair:(si + 1) * npair], axis=1)

    @pl.when(s == pl.num_programs(1) - 1)
    def _():
        sout_ref[...] = st_ref[...]


def _scan_call(streams, k_a, s0, bsz, seq_len, with_output):
    n_tok = streams[0].shape[0]
    nc = seq_len // CHUNK
    sb = 2 if bsz % 2 == 0 else 1
    r, k, v, kk = (z.reshape(bsz, seq_len, BW) for z in streams[:4])
    a2, lw2 = (z.reshape(2, bsz, seq_len, BW) for z in streams[5:])

    chunk = (lambda s: s, lambda s: nc - 1 - s)
    tok = [pl.BlockSpec((sb, CHUNK, BW), lambda b, s, f=f: (b, f(s), 0)) for f in chunk]
    tok2 = [pl.BlockSpec((1, sb, CHUNK, BW), lambda b, s, f=f, d=d: (d, b, f(s), 0)) for d, f in enumerate(chunk)]
    st_spec = pl.BlockSpec((2, sb, HD, BW), lambda b, s: (0, b, 0, 0))
    in_specs, args = [], []
    for d in range(2):
        in_specs += [tok[d]] * 4 + [tok2[d]] * 2
        args += [r, k, v, kk, a2, lw2]
    in_specs += [pl.BlockSpec((1, BW), lambda b, s: (0, 0)), st_spec]
    out_specs = [st_spec]
    out_shape = [jax.ShapeDtypeStruct((2, bsz, HD, BW), F32)]
    if with_output:
        out_specs = tok + out_specs
        out_shape = [jax.ShapeDtypeStruct((bsz, seq_len, BW), F32)] * 2 + out_shape
    res = pl.pallas_call(
        functools.partial(_scan_kernel, with_output=with_output, sb=sb),
        grid=(bsz // sb, nc),
        in_specs=in_specs,
        out_specs=out_specs,
        out_shape=out_shape,
        scratch_shapes=[pltpu.VMEM((2, sb, HD, BW), F32)],
        compiler_params=_cp("arbitrary", "arbitrary"),
        name="rwkv_scan",
    )(*args, k_a, s0)
    if with_output:
        return res[2], (res[0].reshape(n_tok, BW), res[1].reshape(n_tok, BW))
    return res[0], None


def _rwkv_out_kernel(yf_ref, yb_ref, r_ref, k_ref, v_ref, a_ref, g_ref, ka_ref, rk_ref, gnw_ref, gnb_ref, e_ref,
                     o_ref):
    y = yf_ref[...] + yb_ref[...]
    mean = _seg_sum(y, e_ref) * (1.0 / HD)
    yc = y - mean
    var = _seg_sum(yc * yc, e_ref) * (1.0 / HD)
    yn = yc * lax.rsqrt(var + GN_EPS) * gnw_ref[...] + gnb_ref[...]
    r = r_ref[...].astype(F32)
    k = k_ref[...].astype(F32)
    asum = a_ref[0].astype(F32) + a_ref[1].astype(F32)
    kd_sum = k * (2.0 + (asum - 2.0) * ka_ref[...])
    bonus = _seg_sum(r * kd_sum * rk_ref[...], e_ref) * v_ref[...].astype(F32)
    o_ref[...] = ((yn + bonus) * g_ref[...].astype(F32)).astype(BF16)


def _rwkv_out_call(y2, streams, lp):
    r, k, v, _, g, a2, _ = streams
    n_tok = r.shape[0]
    tt = 256
    tok = pl.BlockSpec((tt, BW), lambda i: (i, 0))
    tok2 = pl.BlockSpec((2, tt, BW), lambda i: (0, i, 0))
    vec = pl.BlockSpec((1, BW), lambda i: (0, 0))
    return pl.pallas_call(
        _rwkv_out_kernel,
        grid=(n_tok // tt,),
        in_specs=[tok, tok, tok, tok, tok, tok2, tok, vec, vec, vec, vec, pl.BlockSpec((BW, LANE), lambda i: (0, 0))],
        out_specs=tok,
        out_shape=jax.ShapeDtypeStruct((n_tok, BW), BF16),
        compiler_params=_cp("arbitrary"),
        name="rwkv_out",
    )(y2[0], y2[1], r, k, v, a2, g, lp["k_a"], lp["r_k"], lp["gn_w"], lp["gn_b"], lp["eblk"])


def _conv_kernel(b_ref, c_ref, cp_ref, cn_ref, u_ref, up_ref, un_ref, w_ref, o_ref, *, tiles_per_seq):
    j = pl.program_id(0) % tiles_per_seq
    cu = c_ref[...].astype(F32) * u_ref[...].astype(F32)
    cu_p = cp_ref[...].astype(F32) * up_ref[...].astype(F32)
    cu_n = cn_ref[...].astype(F32) * un_ref[...].astype(F32)
    prev, nxt = _shifted(cu, cu_p, cu_n, j == 0, j == tiles_per_seq - 1)
    w = w_ref[...]
    conv = w[0:1] * prev + w[1:2] * cu + w[2:3] * nxt
    o_ref[...] = (b_ref[...].astype(F32) * conv).astype(BF16)


def _conv_call(p, seq_len, conv_w):
    n_tok = p.shape[0]
    tt = 256
    cb = C_CONV // BW
    return pl.pallas_call(
        functools.partial(_conv_kernel, tiles_per_seq=seq_len // tt),
        grid=(n_tok // tt,),
        in_specs=[pl.BlockSpec((tt, BW), lambda i: (i, cb))] + _halo_specs(tt, BW, cb + 1, n_tok)
        + _halo_specs(tt, BW, cb + 2, n_tok) + [pl.BlockSpec((3, BW), lambda i: (0, 0))],
        out_specs=pl.BlockSpec((tt, BW), lambda i: (i, 0)),
        out_shape=jax.ShapeDtypeStruct((n_tok, BW), BF16),
        compiler_params=_cp("arbitrary"),
        name="short_conv",
    )(p, p, p, p, p, p, p, conv_w)


def _rope_all(blocks, cos, sin):
    ri = lax.broadcasted_iota(jnp.int32, (LANE, LANE), 0)
    ci = lax.broadcasted_iota(jnp.int32, (LANE, LANE), 1)
    perm = (ri == jnp.where((ci % 32) < 16, ci + 16, ci - 16)).astype(BF16)
    slices = [b[:, j:j + LANE] for b in blocks for j in range(0, b.shape[1], LANE)]
    moved = _dot(jnp.concatenate(slices, axis=0), perm)
    outs, row = [], 0
    for b, c, s in zip(blocks, cos, sin):
        cols = []
        for j in range(0, b.shape[1], LANE):
            cols.append(moved[row:row + b.shape[0]])
            row += b.shape[0]
        outs.append(b.astype(F32) * c + jnp.concatenate(cols, axis=1) * s)
    return outs


LOG2E = 1.4426950408889634
Q_SCALE = HD ** -0.5 * LOG2E


def _softmax_pv(s2, sink2_col, v):
    m = jnp.maximum(jnp.max(s2, axis=-1, keepdims=True), sink2_col)
    p = jnp.exp2(s2 - m)
    den = jnp.sum(p, axis=-1, keepdims=True) + jnp.exp2(sink2_col - m)
    return _dot(p.astype(BF16), v) / den


def _attn_lat_kernel(q_ref, kp_ref, kc_ref, kn_ref, vp_ref, vc_ref, vn_ref, kx_ref, vx_ref,
                     cos_ref, cosp_ref, cosn_ref, sin_ref, sinp_ref, sinn_ref, sink_ref, band_ref, o_ref, *, nblk):
    n = pl.program_id(1)
    kvw = KVH * HD
    q, kp, kc, kn = _rope_all(
        [q_ref[...], kp_ref[...], kc_ref[...], kn_ref[...]],
        [cos_ref[...], cosp_ref[:, :kvw], cos_ref[:, :kvw], cosn_ref[:, :kvw]],
        [sin_ref[...], sinp_ref[:, :kvw], sin_ref[:, :kvw], sinn_ref[:, :kvw]])
    q = (q * Q_SCALE).astype(BF16)
    k_all = jnp.concatenate([kp.astype(BF16), kc.astype(BF16), kn.astype(BF16), kx_ref[...]], axis=0)
    v_all = jnp.concatenate([vp_ref[...], vc_ref[...], vn_ref[...], vx_ref[...]], axis=0)
    bias_p = jnp.where(n > 0, band_ref[0, :, :2 * BLK], NEG_INF)
    bias_n = jnp.where(n < nblk - 1, band_ref[1, :, :2 * BLK], NEG_INF)
    sink = sink_ref[...] * LOG2E
    units = [(gi, gi * GQ + 2 * t) for gi in range(KVH) for t in range(GQ // 2)]
    us = range(len(units))
    even = lax.broadcasted_iota(jnp.int32, (BLK, 2 * HD), 1) < HD
    zero = jnp.zeros((BLK, 2 * HD), BF16)
    eye = (lax.broadcasted_iota(jnp.int32, (2 * HD, 2 * HD), 0)
           == lax.broadcasted_iota(jnp.int32, (2 * HD, 2 * HD), 1)).astype(BF16)

    def pair_rows(h):
        qp = q[:, h * HD:(h + 2) * HD]
        return jnp.concatenate([jnp.where(even, qp, zero), jnp.where(even, zero, qp)], axis=0)

    qg = [pair_rows(h) for _, h in units]
    kd = [jnp.concatenate([k_all[:, gi * HD:(gi + 1) * HD]] * 2, axis=1) for gi in range(KVH)]
    vt = [_dot_nt(eye[:HD, :HD], v_all[:, gi * HD:(gi + 1) * HD]).astype(BF16) for gi in range(KVH)]
    ones = jnp.ones((16, vt[0].shape[1]), BF16)
    vt1 = [jnp.concatenate([z, ones], axis=0) for z in vt]
    sk = [jnp.concatenate([jnp.broadcast_to(sink[:, h + t:h + t + 1], (1, BLK)) for t in range(2)], axis=1)
          for _, h in units]
    s = [_dot_nt(kd[units[u][0]], qg[u]) for u in us]
    s = [jnp.concatenate([z[:BLK] + bias_p, z[BLK:2 * BLK], z[2 * BLK:3 * BLK] + bias_n, z[3 * BLK:]], axis=0)
         for z in s]
    m = [jnp.maximum(jnp.max(s[u], axis=0, keepdims=True), sk[u]) for u in us]
    p = [jnp.exp2(s[u] - m[u]).astype(BF16) for u in us]
    pv = [_dot(vt1[units[u][0]], p[u]) for u in us]
    og = [(pv[u][:HD] / (pv[u][HD:HD + 1] + jnp.exp2(sk[u] - m[u]))).astype(BF16) for u in us]
    o_ref[...] = jnp.concatenate(
        [_dot_nt(eye, jnp.concatenate([og[u][:, :BLK], og[u][:, BLK:]], axis=0)) for u in us],
        axis=1).astype(BF16)


def _attn_lat_call(p_l, p_c, kv_blk_c, cos, sin, sink, bsz, seq_len, ctx_len):
    nblk = seq_len // BLK
    kvw = KVH * HD
    kb, vb = C_AK // kvw, C_AV // kvw
    rowq = lambda b, n: b * nblk + n
    rowp = lambda b, n: b * nblk + jnp.maximum(n - 1, 0)
    rown = lambda b, n: b * nblk + jnp.minimum(n + 1, nblk - 1)
    tabp = lambda b, n: (jnp.maximum(n - 1, 0), 0)
    tabn = lambda b, n: (jnp.minimum(n + 1, nblk - 1), 0)
    kv = lambda rf, cb: pl.BlockSpec((BLK, kvw), lambda b, n: (rf(b, n), cb))
    tab = lambda f: pl.BlockSpec((BLK, BW), f)
    kidx = jnp.arange(BLK)[:, None]
    qpos = jnp.arange(GQ * BLK)[None, :] % BLK
    band = jnp.where(jnp.stack([kidx >= qpos, kidx <= qpos]), 0.0, NEG_INF).astype(F32)
    return pl.pallas_call(
        functools.partial(_attn_lat_kernel, nblk=nblk),
        grid=(bsz, nblk),
        in_specs=[pl.BlockSpec((BLK, BW), lambda b, n: (rowq(b, n), C_Q // BW)),
                  kv(rowp, kb), kv(rowq, kb), kv(rown, kb), kv(rowp, vb), kv(rowq, vb), kv(rown, vb),
                  pl.BlockSpec((ctx_len, kvw), lambda b, n: (b, kv_blk_c)),
                  pl.BlockSpec((ctx_len, kvw), lambda b, n: (b, kv_blk_c + 1)),
                  tab(lambda b, n: (n, 0)), tab(tabp), tab(tabn),
                  tab(lambda b, n: (n, 0)), tab(tabp), tab(tabn),
                  pl.BlockSpec((1, NH), lambda b, n: (0, 0)),
                  pl.BlockSpec((2, BLK, GQ * BLK), lambda b, n: (0, 0, 0))],
        out_specs=pl.BlockSpec((BLK, BW), lambda b, n: (rowq(b, n), 0)),
        out_shape=jax.ShapeDtypeStruct((bsz * seq_len, BW), BF16),
        compiler_params=_cp("arbitrary", "arbitrary"),
        name="attn_latent",
    )(p_l, p_l, p_l, p_l, p_l, p_l, p_l, p_c, p_c, cos, cos, cos, sin, sin, sin, sink, band)


def _attn_ctx_kernel(q_ref, kx_ref, vx_ref, sink_ref, o_ref):
    q = (q_ref[...].astype(F32) * Q_SCALE).astype(BF16)
    kx = kx_ref[...]
    vx = vx_ref[...]
    sink = sink_ref[...] * LOG2E
    tq = q.shape[0]
    outs = []
    for gi in range(KVH):
        qg = jnp.concatenate([q[:, (gi * GQ + t) * HD:(gi * GQ + t + 1) * HD] for t in range(GQ)], axis=0)
        sk = jnp.concatenate([jnp.broadcast_to(sink[:, gi * GQ + t:gi * GQ + t + 1], (tq, 1))
                              for t in range(GQ)], axis=0)
        s = _dot_nt(qg, kx[:, gi * HD:(gi + 1) * HD])
        og = _softmax_pv(s, sk, vx[:, gi * HD:(gi + 1) * HD])
        outs += [og[t * tq:(t + 1) * tq] for t in range(GQ)]
    o_ref[...] = jnp.concatenate(outs, axis=1).astype(BF16)


def _attn_ctx_call(p_c, sink, bsz, ctx_len):
    kvw = KVH * HD
    tq = 128
    nq = ctx_len // tq
    return pl.pallas_call(
        _attn_ctx_kernel,
        grid=(bsz, nq),
        in_specs=[pl.BlockSpec((tq, BW), lambda b, n: (b * nq + n, C_Q // BW)),
                  pl.BlockSpec((ctx_len, kvw), lambda b, n: (b, C_AK // kvw)),
                  pl.BlockSpec((ctx_len, kvw), lambda b, n: (b, C_AV // kvw)),
                  pl.BlockSpec((1, NH), lambda b, n: (0, 0))],
        out_specs=pl.BlockSpec((tq, BW), lambda b, n: (b * nq + n, 0)),
        out_shape=jax.ShapeDtypeStruct((bsz * ctx_len, BW), BF16),
        compiler_params=_cp("arbitrary", "arbitrary"),
        name="attn_context",
    )(p_c, p_c, p_c, sink)


def _merge_kernel(b0_ref, b1_ref, b2_ref, g0_ref, g1_ref, g2_ref, w_ref, o_ref):
    acc = _sigmoid(g0_ref[...].astype(F32)) * _dot(b0_ref[...], w_ref[0])
    acc += _sigmoid(g1_ref[...].astype(F32)) * _dot(b1_ref[...], w_ref[1])
    acc += _sigmoid(g2_ref[...].astype(F32)) * _dot(b2_ref[...], w_ref[2])
    o_ref[...] = acc.astype(BF16)


def _merge_call(br_rwkv, br_conv, br_attn, p, w_branch):
    n_tok = p.shape[0]
    tm, tn = 512, 1024
    nn = D // tn
    br = pl.BlockSpec((tm, BW), lambda j, i: (i, 0))
    gate = lambda t: pl.BlockSpec((tm, tn), lambda j, i: (i, (C_GATE + t * D) // tn + j))
    return pl.pallas_call(
        _merge_kernel,
        grid=(nn, n_tok // tm),
        in_specs=[br, br, br, gate(0), gate(1), gate(2), pl.BlockSpec((3, BW, tn), lambda j, i: (0, 0, j))],
        out_specs=pl.BlockSpec((tm, tn), lambda j, i: (i, j)),
        out_shape=jax.ShapeDtypeStruct((n_tok, D), BF16),
        compiler_params=_cp("arbitrary", "arbitrary"),
        name="merge_branches",
    )(br_rwkv, br_conv, br_attn, p, p, p, w_branch)


def _outproj_kernel(m_ref, w_ref, x_ref, gate_ref, o_ref):
    o_ref[...] = x_ref[...] + gate_ref[0] * _dot(m_ref[...], w_ref[...])


def _outproj_call(m, w_out, x2, mod3, mod_row, tm):
    n_tok = x2.shape[0]
    return pl.pallas_call(
        _outproj_kernel,
        grid=(n_tok // tm,),
        in_specs=[pl.BlockSpec((tm, D), lambda i: (i, 0)),
                  pl.BlockSpec((D, D), lambda i: (0, 0)),
                  pl.BlockSpec((tm, D), lambda i: (i, 0)),
                  pl.BlockSpec((1, 1, D), lambda i: (mod_row(i) * 6 + 2, 0, 0))],
        out_specs=pl.BlockSpec((tm, D), lambda i: (i, 0)),
        out_shape=jax.ShapeDtypeStruct((n_tok, D), F32),
        compiler_params=_cp("arbitrary"),
        name="out_proj",
    )(m, w_out, x2, mod3)


def _ffn_prep_kernel(x_ref, gain_ref, shift_ref, scale_ref, wr_ref, h_ref, aff_ref):
    h = _norm_mod(x_ref[...], gain_ref[...], shift_ref[0], scale_ref[0])
    h_hi = h.astype(BF16)
    h_ref[...] = h_hi
    h_lo = (h - h_hi.astype(F32)).astype(BF16)
    wr = wr_ref[...]
    w_hi = wr.astype(BF16)
    w_lo = (wr - w_hi.astype(F32)).astype(BF16)
    logits = _dot(h_hi, w_hi) + (_dot(h_hi, w_lo) + _dot(h_lo, w_hi))
    lane = lax.broadcasted_iota(jnp.int32, logits.shape, 1)
    logits = jnp.where(lane < N_EXPERTS, logits, NEG_INF)
    m = jnp.max(logits, axis=-1, keepdims=True)
    e = jnp.exp(logits - m)
    aff_ref[...] = e / jnp.sum(e, axis=-1, keepdims=True)


def _ffn_prep_call(x2, gain, mod3, mod_row, wr_pad, tm):
    n_tok = x2.shape[0]
    return pl.pallas_call(
        _ffn_prep_kernel,
        grid=(n_tok // tm,),
        in_specs=[pl.BlockSpec((tm, D), lambda i: (i, 0)),
                  pl.BlockSpec((1, D), lambda i: (0, 0)),
                  pl.BlockSpec((1, 1, D), lambda i: (mod_row(i) * 6 + 3, 0, 0)),
                  pl.BlockSpec((1, 1, D), lambda i: (mod_row(i) * 6 + 4, 0, 0)),
                  pl.BlockSpec((D, LANE), lambda i: (0, 0))],
        out_specs=[pl.BlockSpec((tm, D), lambda i: (i, 0)), pl.BlockSpec((tm, LANE), lambda i: (i, 0))],
        out_shape=[jax.ShapeDtypeStruct((n_tok, D), BF16), jax.ShapeDtypeStruct((n_tok, LANE), F32)],
        compiler_params=_cp("arbitrary"),
        name="ffn_prep",
    )(x2, gain.reshape(1, D), mod3, mod3, wr_pad)


def _select_kernel(aff_ref, slot_ref, *, cap):
    a = aff_ref[0]
    n = a.shape[1]
    bits = lax.bitcast_convert_type(a, jnp.int32)

    def body(i, t):
        cand = t | jnp.left_shift(jnp.int32(1), 30 - i)
        cnt = jnp.sum((bits >= cand).astype(jnp.int32), axis=-1, keepdims=True)
        return jnp.where(cnt >= cap, cand, t)

    thr = lax.fori_loop(0, 31, body, jnp.zeros((a.shape[0], 1), jnp.int32))
    gt = bits > thr
    eq = bits == thr
    n_gt = jnp.sum(gt.astype(jnp.int32), axis=-1, keepdims=True)
    tc = min(n, 512)

    def prefix(mask_bf16):
        cols = []
        for j0 in range(0, n, tc):
            ri = lax.broadcasted_iota(jnp.int32, (n, tc), 0)
            ci = lax.broadcasted_iota(jnp.int32, (n, tc), 1) + j0
            cols.append(_dot(mask_bf16, (ri <= ci).astype(BF16)))
        return jnp.concatenate(cols, axis=1)

    eq_f = eq.astype(BF16)
    excl_eq = prefix(eq_f) - eq_f.astype(F32)
    sel = gt | (eq & (excl_eq < (cap - n_gt).astype(F32)))
    pos = prefix(sel.astype(BF16))
    slot_ref[0] = jnp.where(sel, pos.astype(jnp.int32) - 1, -1)


def _select_call(aff_t, cap):
    bsz, ne, n = aff_t.shape
    return pl.pallas_call(
        functools.partial(_select_kernel, cap=cap),
        grid=(bsz,),
        in_specs=[pl.BlockSpec((1, ne, n), lambda b: (b, 0, 0))],
        out_specs=pl.BlockSpec((1, ne, n), lambda b: (b, 0, 0)),
        out_shape=jax.ShapeDtypeStruct((bsz, ne, n), jnp.int32),
        compiler_params=_cp("arbitrary"),
        name="expert_select",
    )(aff_t)


def _gather_kernel(slot_ref, aff_ref, h_ref, xs_ref, g_ref, *, cap):
    slot = slot_ref[0]
    n = slot.shape[1]
    onehot = lax.broadcasted_iota(jnp.int32, (cap, n), 0) == slot
    xs_ref[0] = _dot(onehot.astype(BF16), h_ref[...]).astype(BF16)
    g_ref[0] = jnp.sum(jnp.where(onehot, aff_ref[0], 0.0), axis=-1, keepdims=True)


def _gather_call(slot, aff_t, h2, cap):
    bsz, ne, n = slot.shape
    return pl.pallas_call(
        functools.partial(_gather_kernel, cap=cap),
        grid=(bsz, ne),
        in_specs=[pl.BlockSpec((1, 1, n), lambda b, e: (b * ne + e, 0, 0)),
                  pl.BlockSpec((1, 1, n), lambda b, e: (b * ne + e, 0, 0)),
                  pl.BlockSpec((n, D), lambda b, e: (b, 0))],
        out_specs=[pl.BlockSpec((1, cap, D), lambda b, e: (e, b, 0)),
                   pl.BlockSpec((1, cap, 1), lambda b, e: (e, b, 0))],
        out_shape=[jax.ShapeDtypeStruct((ne, bsz * cap, D), BF16),
                   jax.ShapeDtypeStruct((ne, bsz * cap, 1), F32)],
        compiler_params=_cp("arbitrary", "arbitrary"),
        name="expert_gather",
    )(slot.reshape(bsz * ne, 1, n), aff_t.reshape(bsz * ne, 1, n), h2)


def _expert_kernel(x_ref, wg_ref, wu_ref, wd_ref, g_ref, o_ref, acc_ref):
    f = pl.program_id(2)

    @pl.when(f == 0)
    def _():
        acc_ref[...] = jnp.zeros_like(acc_ref)

    x = x_ref[0]
    gate = _dot(x, wg_ref[0, 0].astype(BF16))
    hid = gate * _sigmoid(gate) * _dot(x, wu_ref[0, 0].astype(BF16))
    acc_ref[...] += _dot(hid.astype(BF16), wd_ref[0, 0].astype(BF16))

    @pl.when(f == pl.num_programs(2) - 1)
    def _():
        o_ref[0] = (acc_ref[...] * g_ref[0]).astype(BF16)


def _expert_call(xs, g, wg, wu, wd, layer):
    ne, rows, _ = xs.shape
    tm = min(rows, 1024)
    tf = 256
    return pl.pallas_call(
        _expert_kernel,
        grid=(ne, rows // tm, EXPERT_FF // tf),
        in_specs=[pl.BlockSpec((1, tm, D), lambda e, i, f: (e, i, 0)),
                  pl.BlockSpec((1, 1, D, tf), lambda e, i, f: (layer, e, 0, f)),
                  pl.BlockSpec((1, 1, D, tf), lambda e, i, f: (layer, e, 0, f)),
                  pl.BlockSpec((1, 1, tf, D), lambda e, i, f: (layer, e, f, 0)),
                  pl.BlockSpec((1, tm, 1), lambda e, i, f: (e, i, 0))],
        out_specs=pl.BlockSpec((1, tm, D), lambda e, i, f: (e, i, 0)),
        out_shape=jax.ShapeDtypeStruct((ne, rows, D), BF16),
        scratch_shapes=[pltpu.VMEM((tm, D), F32)],
        compiler_params=_cp("arbitrary", "arbitrary", "arbitrary"),
        name="expert_mlp",
    )(xs, wg, wu, wd, g)


def _scatter_kernel(*refs, cap, last):
    if last:
        slot_ref, ys_ref, x_ref, gate_ref, gain_ref, o_ref = refs
    else:
        slot_ref, ys_ref, x_ref, gate_ref, gain_ref, shift_ref, scale_ref, o_ref, h_ref = refs
    slot_t = slot_ref[0]
    tq = slot_t.shape[0]
    lane = lax.broadcasted_iota(jnp.int32, (tq, cap), 1)
    acc = jnp.zeros(x_ref.shape, F32)
    for e in range(N_EXPERTS):
        onehot = (slot_t[:, e:e + 1] == lane).astype(BF16)
        acc += _dot(onehot, ys_ref[e])
    x = x_ref[...] + gate_ref[0] * acc
    if last:
        ms = jnp.mean(x * x, axis=-1, keepdims=True)
        o_ref[...] = (x * lax.rsqrt(ms + NORM_EPS)) * gain_ref[...]
    else:
        o_ref[...] = x
        h_ref[...] = _norm_mod(x, gain_ref[...], shift_ref[0], scale_ref[0]).astype(BF16)


def _scatter_call(slot_t, ys, x2, mod3, mod_row_b, cap, gain, next_mod3=None):
    bsz, n, ne = slot_t.shape
    tq = min(n, 256)
    nt = n // tq
    last = next_mod3 is None
    row = pl.BlockSpec((tq, D), lambda b, t: (b * nt + t, 0))
    mod = lambda c: pl.BlockSpec((1, 1, D), lambda b, t: (mod_row_b(b) * 6 + c, 0, 0))
    in_specs = [pl.BlockSpec((1, tq, ne), lambda b, t: (b, t, 0)),
                pl.BlockSpec((ne, cap, D), lambda b, t: (0, b, 0)),
                row, mod(5), pl.BlockSpec((1, D), lambda b, t: (0, 0))]
    args = [slot_t, ys, x2, mod3, gain.reshape(1, D)]
    out_specs, out_shape = row, jax.ShapeDtypeStruct(x2.shape, F32)
    if not last:
        in_specs += [mod(0), mod(1)]
        args += [next_mod3, next_mod3]
        out_specs, out_shape = [row, row], [out_shape, jax.ShapeDtypeStruct(x2.shape, BF16)]
    return pl.pallas_call(
        functools.partial(_scatter_kernel, cap=cap, last=last),
        grid=(bsz, nt),
        in_specs=in_specs,
        out_specs=out_specs,
        out_shape=out_shape,
        compiler_params=_cp("arbitrary", "arbitrary"),
        name="expert_scatter",
    )(*args)


def _ffn(x2, bsz, n, lp, mod3, mod_row, mod_row_b, tm, gain, next_mod3=None):
    cap = CAPACITY_FACTOR * n // N_EXPERTS
    h2, aff = _ffn_prep_call(x2, lp["norm_ffn"], mod3, mod_row, lp["w_router"], tm)
    aff_t = jnp.swapaxes(aff[:, :N_EXPERTS].reshape(bsz, n, N_EXPERTS), 1, 2)
    slot = _select_call(aff_t, cap)
    xs, g = _gather_call(slot, aff_t, h2, cap)
    ys = _expert_call(xs, g, lp["w_exp_gate"], lp["w_exp_up"], lp["w_exp_down"], lp["layer"])
    return _scatter_call(jnp.swapaxes(slot, 1, 2), ys, x2, mod3, mod_row_b, cap, gain, next_mod3)


def _pad_cols(parts):
    out = []
    for a, width in parts:
        out.append(a)
        if a.shape[-1] < width:
            out.append(jnp.zeros(a.shape[:-1] + (width - a.shape[-1],), a.dtype))
    return jnp.concatenate(out, axis=-1)


def _rwkv_col_parts(w):
    o = 3 * BW
    parts = [(w[..., :o], o)]
    for _ in range(2):
        parts.append((w[..., o:o + DECAY_RANK], LANE))
        o += DECAY_RANK
    for _ in range(2):
        parts.append((w[..., o:o + ICLR_RANK], LANE))
        o += ICLR_RANK
    parts.append((w[..., o:o + GATE_RANK], RW_PAD - C_GD))
    return parts, o + GATE_RANK


RW_END = 3 * BW + 2 * DECAY_RANK + 2 * ICLR_RANK + GATE_RANK
CODE_W = RW_PAD - 3 * BW


def _wrelayout_kernel(w_ref, misc_ref, o_ref):
    j = pl.program_id(1)
    is_code = (j >= 3 * BW // LANE) & (j < RW_PAD // LANE)
    o_ref[0] = jnp.where(is_code, misc_ref[0], w_ref[0]).astype(BF16)


def _wrelayout_call(w_in):
    depth = w_in.shape[0]
    code_parts, _ = _rwkv_col_parts(w_in[:, :, :RW_END])
    misc = _pad_cols(code_parts[1:])
    nb = NP // LANE
    kv_src = RW_END // LANE

    def src(j):
        shifted = jnp.where(j >= C_AK // LANE, j - C_AK // LANE + kv_src, j + (RW_END + 2 * KVH * HD - C_Q) // LANE)
        return jnp.where(j < RW_PAD // LANE, jnp.minimum(j, 3 * BW // LANE - 1), shifted)

    def misc_blk(j):
        return jnp.clip(j - 3 * BW // LANE, 0, CODE_W // LANE - 1)

    return pl.pallas_call(
        _wrelayout_kernel,
        grid=(depth, nb),
        in_specs=[pl.BlockSpec((1, D, LANE), lambda l, j: (l, 0, src(j))),
                  pl.BlockSpec((1, D, LANE), lambda l, j: (l, 0, misc_blk(j)))],
        out_specs=pl.BlockSpec((1, D, LANE), lambda l, j: (l, 0, j)),
        out_shape=jax.ShapeDtypeStruct((depth, D, NP), BF16),
        compiler_params=_cp("arbitrary", "arbitrary"),
        name="w_in_relayout",
    )(w_in, misc)


def _prep_layer(l, depth, shift_mu, decay_up, decay_bias, iclr_up, iclr_bias, gate_up, vres_down, vres_up,
                vres_bias, k_k, k_a, r_k, gn_w, gn_b, conv_w, attn_sink, w_branch, w_out, w_router,
                w_exp_gate, w_exp_up, w_exp_down, norm_mix, norm_ffn, eblk):
    mu_parts, _ = _rwkv_col_parts(shift_mu[l][None, :])
    pad_rank = lambda u: jnp.pad(u, ((0, 0), (0, LANE - u.shape[1]), (0, 0))).astype(BF16)
    lp = dict(
        mu=_pad_cols(mu_parts),
        decay_up=pad_rank(decay_up[l]), decay_bias=decay_bias[l].reshape(2, 1, BW),
        iclr_up=pad_rank(iclr_up[l]), iclr_bias=iclr_bias[l].reshape(2, 1, BW),
        gate_up=gate_up[l].astype(BF16), k_k=k_k[l].reshape(1, BW), k_a=k_a[l].reshape(1, BW),
        r_k=r_k[l].reshape(1, BW), gn_w=gn_w[l].reshape(1, BW), gn_b=gn_b[l].reshape(1, BW),
        conv_w=conv_w[l], sink=attn_sink[l].reshape(1, NH),
        w_branch=w_branch[l].astype(BF16), w_out=w_out[l].astype(BF16),
        w_router=jnp.pad(w_router[l], ((0, 0), (0, LANE - N_EXPERTS))),
        layer=l, w_exp_gate=w_exp_gate, w_exp_up=w_exp_up, w_exp_down=w_exp_down,
        norm_ffn=norm_ffn[l], eblk=eblk)
    if l > 0:
        lp["vres_down"] = jnp.pad(vres_down[l - 1], ((0, 0), (0, LANE - VRES_RANK))).astype(BF16)
        lp["vres_up"] = jnp.pad(vres_up[l - 1], ((0, LANE - VRES_RANK), (0, 0))).astype(BF16)
        lp["vres_bias"] = vres_bias[l - 1].reshape(1, BW)
    return lp


def _rope_tables(seq_len):
    quarter = HD // 4
    inv = ROPE_BASE ** (-jnp.arange(quarter, dtype=F32) / quarter)
    pos = jnp.arange(seq_len)
    ang_r = (pos // GRID_W).astype(F32)[:, None] * inv[None, :]
    ang_c = (pos % GRID_W).astype(F32)[:, None] * inv[None, :]
    cos = jnp.concatenate([jnp.cos(ang_r)] * 2 + [jnp.cos(ang_c)] * 2, axis=1)
    sin = jnp.concatenate([-jnp.sin(ang_r), jnp.sin(ang_r), -jnp.sin(ang_c), jnp.sin(ang_c)], axis=1)
    return jnp.tile(cos, (1, NH)), jnp.tile(sin, (1, NH))


def kernel(x, c, ctx, c_ctx, w_mod, b_mod, norm_mix, norm_ffn, w_in, shift_mu, decay_up, decay_bias, iclr_up,
           iclr_bias, gate_up, vres_down, vres_up, vres_bias, k_k, k_a, r_k, gn_w, gn_b, conv_w, attn_sink,
           w_branch, w_out, w_router, w_exp_gate, w_exp_up, w_exp_down, norm_final):
    bsz, seq_len, _ = x.shape
    ctx_len = ctx.shape[1]
    depth = w_in.shape[0]
    mod_rows = -(-(bsz + 1) // 8) * 8
    cc = jnp.concatenate([c, c_ctx[None, :], jnp.zeros((mod_rows - bsz - 1, D), F32)], axis=0)
    mod_all = _mod_call(cc, w_mod, b_mod)

    hd_i = jnp.arange(BW) // HD
    eblk = (hd_i[:, None] == jnp.arange(LANE)[None, :]).astype(BF16)
    cos, sin = _rope_tables(seq_len)

    row_c = lambda i: bsz
    tm2 = 512
    row_l2 = lambda i: i // (seq_len // tm2)
    mod3_of = lambda l: mod_all[l].reshape(mod_rows * 6, 1, D)

    x_l = x.reshape(bsz * seq_len, D)
    x_c = ctx.reshape(bsz * ctx_len, D)
    vf_l = vf_c = None
    zero_state = jnp.zeros((2, bsz, HD, BW), F32)

    w_p = _wrelayout_call(w_in)
    h_l = _normmod_call(x_l, norm_mix[0], mod3_of(0), row_l2, tm2)
    h_c = _normmod_call(x_c, norm_mix[0], mod3_of(0), row_c, tm2)

    for l in range(depth):
        last = l == depth - 1
        lp = _prep_layer(l, depth, shift_mu, decay_up, decay_bias, iclr_up, iclr_bias, gate_up, vres_down,
                         vres_up, vres_bias, k_k, k_a, r_k, gn_w, gn_b, conv_w, attn_sink, w_branch, w_out,
                         w_router, w_exp_gate, w_exp_up, w_exp_down, norm_mix, norm_ffn, eblk)
        mod3 = mod_all[l].reshape(mod_rows * 6, 1, D)

        p_l = _inproj_call(h_l, w_p, l)
        p_c = _inproj_call(h_c, w_p, l, rwkv_kv_only=last)
        nxt = (norm_final, None) if last else (norm_mix[l + 1], mod3_of(l + 1))
        kv_blk_c = (RW_PAD if last else C_AK) // (KVH * HD)

        st_c = _streams_call(p_c, ctx_len, lp, vf_c)
        st_l = _streams_call(p_l, seq_len, lp, vf_l)
        if l == 0:
            vf_c, vf_l = st_c[2], st_l[2]
        state_c, y_c = _scan_call(st_c, lp["k_a"], zero_state, bsz, ctx_len, not last)
        _, y_l = _scan_call(st_l, lp["k_a"], state_c, bsz, seq_len, True)
        br_rwkv_l = _rwkv_out_call(y_l, st_l, lp)

        br_attn_l = _attn_lat_call(p_l, p_c, kv_blk_c, cos, sin, lp["sink"], bsz, seq_len, ctx_len)
        br_conv_l = _conv_call(p_l, seq_len, lp["conv_w"])

        m_l = _merge_call(br_rwkv_l, br_conv_l, br_attn_l, p_l, lp["w_branch"])
        x_l = _outproj_call(m_l, lp["w_out"], x_l, mod3, row_l2, tm2)
        res = _ffn(x_l, bsz, seq_len, lp, mod3, row_l2, lambda b: b, tm2, *nxt)
        x_l, h_l = (res, None) if last else res

        if not last:
            br_rwkv_c = _rwkv_out_call(y_c, st_c, lp)
            br_attn_c = _attn_ctx_call(p_c, lp["sink"], bsz, ctx_len)
            br_conv_c = _conv_call(p_c, ctx_len, lp["conv_w"])
            m_c = _merge_call(br_rwkv_c, br_conv_c, br_attn_c, p_c, lp["w_branch"])
            x_c = _outproj_call(m_c, lp["w_out"], x_c, mod3, row_c, tm2)
            x_c, h_c = _ffn(x_c, bsz, ctx_len, lp, mod3, row_c, lambda b: bsz, tm2, *nxt)

    return x_l.reshape(bsz, seq_len, D)
```
